```python
import math
import jax, jax.numpy as jnp
from jax import lax
import numpy as np

D_MODEL = 1024
BATCH = 16
SEQ = 2048
DEPTH = 2

N_META = 16
BLOCK = 128
WINDOW = 128
A_HEADS = 8
A_KV_HEADS = 2
A_HEAD_DIM = 64
A_WIDTH = A_HEADS * A_HEAD_DIM
B_HEADS = 8
B_NOPE_DIM = 64
B_ROPE_DIM = 32
B_V_DIM = 64
B_WIDTH = B_HEADS * B_V_DIM
Q_LORA_RANK = 256
KV_LORA_RANK = 128
MIX_WIDTH = A_WIDTH + B_WIDTH
IN_SIZES = (A_WIDTH, A_KV_HEADS * A_HEAD_DIM, A_KV_HEADS * A_HEAD_DIM, A_WIDTH,
            Q_LORA_RANK, KV_LORA_RANK, B_ROPE_DIM, B_WIDTH)
IN_WIDTH = 2 * A_WIDTH + 2 * A_KV_HEADS * A_HEAD_DIM + Q_LORA_RANK + KV_LORA_RANK + B_ROPE_DIM + B_WIDTH
N_BUCKETS = 32
MAX_DISTANCE = 128
ROPE_THETA = 10000.0
EPS = 1e-6

kernel_name = "hymba_swa_mla_hybrid_encoder"


def rms_norm(x, g):
    xf = x.astype(jnp.float32)
    y = xf * lax.rsqrt(jnp.mean(xf * xf, axis=-1, keepdims=True) + EPS)
    return (y * g.astype(jnp.float32)).astype(x.dtype)


def t5_bucket(rel):
    nb = N_BUCKETS // 2
    max_exact = nb // 2
    ret = jnp.where(rel > 0, nb, 0)
    n = jnp.abs(rel)
    nf = jnp.maximum(n, 1).astype(jnp.float32)
    large = max_exact + (jnp.log(nf / max_exact) / math.log(MAX_DISTANCE / max_exact)
                         * (nb - max_exact)).astype(jnp.int32)
    large = jnp.minimum(large, nb - 1)
    return ret + jnp.where(n < max_exact, n, large)


def rel_bias(table, q_pos, k_pos):
    b = t5_bucket(k_pos[..., None, :] - q_pos[..., :, None])
    return jnp.moveaxis(table.astype(jnp.float32)[b], -1, -3)


def softmax_with_sink(logits, sink, mask):
    logits = jnp.where(mask, logits, -jnp.inf)
    s = sink.astype(jnp.float32)[:, None, None]
    m = jnp.maximum(jnp.max(logits, axis=-1, keepdims=True), s)
    e = jnp.exp(logits - m)
    return e / (jnp.sum(e, axis=-1, keepdims=True) + jnp.exp(s - m))


def apply_rope(x, cos, sin):
    x1, x2 = jnp.split(x.astype(jnp.float32), 2, axis=-1)
    return jnp.concatenate([x1 * cos - x2 * sin, x2 * cos + x1 * sin], axis=-1).astype(x.dtype)


def window_attention(q, k, v, sink, table):
    B, L = q.shape[0], q.shape[1]
    S = L - N_META
    nb = S // BLOCK
    g = A_HEADS // A_KV_HEADS
    dh = A_HEAD_DIM
    scale = dh ** -0.5
    q = q.reshape(B, L, A_KV_HEADS, g, dh)
    qm, qr = q[:, :N_META], q[:, N_META:]
    km, kr = k[:, :N_META], k[:, N_META:]
    vm, vr = v[:, :N_META], v[:, N_META:]

    qb = qr.reshape(B, nb, BLOCK, A_KV_HEADS, g, dh)
    pad = ((0, 0), (BLOCK, BLOCK), (0, 0), (0, 0))

    def band(t, tm):
        tp = jnp.pad(t, pad).reshape(B, nb + 2, BLOCK, A_KV_HEADS, dh)
        tb = jnp.concatenate([tp[:, :-2], tp[:, 1:-1], tp[:, 2:]], axis=2)
        tmb = jnp.broadcast_to(tm[:, None], (B, nb, N_META, A_KV_HEADS, dh))
        return jnp.concatenate([tmb, tb], axis=2)

    kb, vb = band(kr, km), band(vr, vm)
    blk = jnp.arange(nb)[:, None]
    q_idx = blk * BLOCK + jnp.arange(BLOCK)[None, :]
    k_idx = (blk - 1) * BLOCK + jnp.arange(3 * BLOCK)[None, :]
    valid = ((k_idx[:, None, :] >= 0) & (k_idx[:, None, :] < S)
             & (jnp.abs(q_idx[:, :, None] - k_idx[:, None, :]) <= WINDOW))
    mask = jnp.concatenate([jnp.ones((nb, BLOCK, N_META), bool), valid], axis=-1)
    q_pos = N_META + q_idx
    k_pos = jnp.concatenate([jnp.broadcast_to(jnp.arange(N_META)[None], (nb, N_META)), N_META + k_idx], axis=-1)
    bias = rel_bias(table, q_pos, k_pos)
    logits = jnp.einsum('bnqhgd,bnkhd->bnhgqk', qb, kb).astype(jnp.float32) * scale
    logits = logits.reshape(B, nb, A_HEADS, BLOCK, N_META + 3 * BLOCK) + bias[None]
    p = softmax_with_sink(logits, sink, mask[None, :, None]).astype(v.dtype)
    p = p.reshape(B, nb, A_KV_HEADS, g, BLOCK, N_META + 3 * BLOCK)
    out_r = jnp.einsum('bnhgqk,bnkhd->bnqhgd', p, vb).reshape(B, S, A_WIDTH)

    km2 = jnp.concatenate([km, kr[:, :BLOCK]], axis=1)
    vm2 = jnp.concatenate([vm, vr[:, :BLOCK]], axis=1)
    qp = jnp.arange(N_META)
    kp = jnp.arange(N_META + BLOCK)
    mmask = jnp.abs(kp[None, :] - qp[:, None]) <= WINDOW
    mbias = rel_bias(table, qp, kp)
    ml = jnp.einsum('bqhgd,bkhd->bhgqk', qm, km2).astype(jnp.float32) * scale
    ml = ml.reshape(B, A_HEADS, N_META, N_META + BLOCK) + mbias[None]
    mp = softmax_with_sink(ml, sink, mmask).astype(v.dtype).reshape(B, A_KV_HEADS, g, N_META, N_META + BLOCK)
    out_m = jnp.einsum('bhgqk,bkhd->bqhgd', mp, vm2).reshape(B, N_META, A_WIDTH)
    return jnp.concatenate([out_m, out_r], axis=1)


def mla_attention(q_nope, q_rope, k_nope, k_rope, v):
    B, L = q_nope.shape[0], q_nope.shape[1]
    S = L - N_META
    nb = S // BLOCK
    scale = (B_NOPE_DIM + B_ROPE_DIM) ** -0.5

    def attend(qn, qr):
        logits = (jnp.einsum('bqhd,bkhd->bhqk', qn, k_nope)
                  + jnp.einsum('bqhd,bkd->bhqk', qr, k_rope)).astype(jnp.float32) * scale
        p = jax.nn.softmax(logits, axis=-1).astype(v.dtype)
        return jnp.einsum('bhqk,bkhd->bqhd', p, v)

    out_m = attend(q_nope[:, :N_META], q_rope[:, :N_META]).reshape(B, N_META, B_WIDTH)
    qn_b = jnp.moveaxis(q_nope[:, N_META:].reshape(B, nb, BLOCK, B_HEADS, B_NOPE_DIM), 1, 0)
    qr_b = jnp.moveaxis(q_rope[:, N_META:].reshape(B, nb, BLOCK, B_HEADS, B_ROPE_DIM), 1, 0)
    out_r = lax.map(lambda a: attend(a[0], a[1]), (qn_b, qr_b))
    out_r = jnp.moveaxis(out_r, 0, 1).reshape(B, S, B_WIDTH)
    return jnp.concatenate([out_m, out_r], axis=1)


def hybrid_layer(h, norm_in, w_in, sink_a, norm_q_lat, w_uq, norm_kv_lat, w_ukv,
                 norm_out_a, norm_out_b, w_out, rel_table, cos, sin):
    B, L = h.shape[0], h.shape[1]
    u = rms_norm(h, norm_in)
    proj = jnp.einsum('bld,de->ble', u, w_in)
    offsets = np.cumsum(IN_SIZES)[:-1].tolist()
    qa, ka, va, ga, cq, ckv, kr, gb = jnp.split(proj, offsets, axis=-1)

    ya = window_attention(qa.reshape(B, L, A_HEADS, A_HEAD_DIM),
                          ka.reshape(B, L, A_KV_HEADS, A_HEAD_DIM),
                          va.reshape(B, L, A_KV_HEADS, A_HEAD_DIM), sink_a, rel_table)

    q = jnp.einsum('blr,re->ble', rms_norm(cq, norm_q_lat), w_uq).reshape(B, L, B_HEADS, B_NOPE_DIM + B_ROPE_DIM)
    q_nope, q_rope = q[..., :B_NOPE_DIM], q[..., B_NOPE_DIM:]
    q_rope = apply_rope(q_rope, cos[:, None], sin[:, None])
    kv = jnp.einsum('blr,re->ble', rms_norm(ckv, norm_kv_lat), w_ukv).reshape(B, L, B_HEADS, B_NOPE_DIM + B_V_DIM)
    k_nope, vb = kv[..., :B_NOPE_DIM], kv[..., B_NOPE_DIM:]
    k_rope = apply_rope(kr, cos, sin)
    yb = mla_attention(q_nope, q_rope, k_nope, k_rope, vb)

    y = jnp.concatenate([rms_norm(ya, norm_out_a) * jax.nn.silu(ga),
                         rms_norm(yb, norm_out_b) * jax.nn.silu(gb)], axis=-1)
    return h + jnp.einsum('ble,ed->bld', y, w_out)


def setup_inputs(seed: int = 0) -> dict:
    key = jax.random.key(seed)
    ks = jax.random.split(key, 16)
    f32 = jnp.float32
    nrm = lambda k, s, sc: jax.random.normal(k, s, f32) * sc
    gain = lambda k, s: 1.0 + 0.1 * jax.random.normal(k, s, f32)
    return {
        "x": nrm(ks[0], (BATCH, SEQ, D_MODEL), 1.0),
        "meta_tokens": nrm(ks[1], (N_META, D_MODEL), 1.0),
        "rel_bias_table": nrm(ks[2], (N_BUCKETS, A_HEADS), 0.5),
        "norm_in": gain(ks[3], (DEPTH, D_MODEL)),
        "w_in": nrm(ks[4], (DEPTH, D_MODEL, IN_WIDTH), D_MODEL ** -0.5),
        "sink_a": nrm(ks[5], (DEPTH, A_HEADS), 0.5),
        "norm_q_lat": gain(ks[6], (DEPTH, Q_LORA_RANK)),
        "w_uq": nrm(ks[7], (DEPTH, Q_LORA_RANK, B_HEADS * (B_NOPE_DIM + B_ROPE_DIM)), Q_LORA_RANK ** -0.5),
        "norm_kv_lat": gain(ks[8], (DEPTH, KV_LORA_RANK)),
        "w_ukv": nrm(ks[9], (DEPTH, KV_LORA_RANK, B_HEADS * (B_NOPE_DIM + B_V_DIM)), KV_LORA_RANK ** -0.5),
        "norm_out_a": gain(ks[10], (DEPTH, A_WIDTH)),
        "norm_out_b": gain(ks[11], (DEPTH, B_WIDTH)),
        "w_out": nrm(ks[12], (DEPTH, MIX_WIDTH, D_MODEL), MIX_WIDTH ** -0.5),
        "norm_final": gain(ks[13], (D_MODEL,)),
    }


def reference(x, meta_tokens, rel_bias_table, norm_in, w_in, sink_a, norm_q_lat, w_uq,
              norm_kv_lat, w_ukv, norm_out_a, norm_out_b, w_out, norm_final):
    B = x.shape[0]
    meta = jnp.broadcast_to(meta_tokens.astype(x.dtype)[None], (B, N_META, D_MODEL))
    h = jnp.concatenate([meta, x], axis=1)
    L = h.shape[1]
    half = B_ROPE_DIM // 2
    freqs = ROPE_THETA ** (-jnp.arange(half, dtype=jnp.float32) / half)
    ang = jnp.arange(L, dtype=jnp.float32)[:, None] * freqs[None, :]
    cos, sin = jnp.cos(ang), jnp.sin(ang)
    for i in range(DEPTH):
        h = hybrid_layer(h, norm_in[i], w_in[i], sink_a[i], norm_q_lat[i], w_uq[i], norm_kv_lat[i],
                         w_ukv[i], norm_out_a[i], norm_out_b[i], w_out[i], rel_bias_table, cos, sin)
    return rms_norm(h[:, N_META:], norm_final)
```

```python
import functools
import math

import jax
import jax.numpy as jnp
from jax import lax
from jax.experimental import pallas as pl
from jax.experimental.pallas import tpu as pltpu

D_MODEL = 1024
N_META = 16
BLOCK = 128
WINDOW = 128
A_HEADS = 8
A_KV_HEADS = 2
A_GROUP = A_HEADS // A_KV_HEADS
A_HEAD_DIM = 64
A_WIDTH = A_HEADS * A_HEAD_DIM
B_HEADS = 8
B_NOPE_DIM = 64
B_ROPE_DIM = 32
B_V_DIM = 64
B_WIDTH = B_HEADS * B_V_DIM
B_QK_PAD = 128
Q_LORA_RANK = 256
KV_LORA_RANK = 128
N_BUCKETS = 32
MAX_DISTANCE = 128
ROPE_THETA = 10000.0
EPS = 1e-6

LOG2E = math.log2(math.e)
NEG = -1e30
ROW_TILE = 256
MLA_Q_TILE = 256
VMEM_LIMIT_BYTES = 56 * 1024 * 1024

_C_QA = 0
_C_KV = _C_QA + A_WIDTH
_C_GA = _C_KV + 2 * A_KV_HEADS * A_HEAD_DIM
_C_CQ = _C_GA + A_WIDTH
_C_CKV = _C_CQ + Q_LORA_RANK
_C_KRA = _C_CKV + KV_LORA_RANK
_C_KRB = _C_KRA + B_QK_PAD
_C_GB = _C_KRB + B_QK_PAD
_C_END = _C_GB + B_WIDTH

_NT = (((1,), (1,)), ((), ()))

bf16 = jnp.bfloat16
f32 = jnp.float32


def _cparams(*sem):
    return pltpu.CompilerParams(dimension_semantics=sem, vmem_limit_bytes=VMEM_LIMIT_BYTES)


def _rms(x, gain):
    return x * lax.rsqrt(jnp.mean(x * x, axis=-1, keepdims=True) + EPS) * gain


def _silu(x):
    return x / (1.0 + jnp.exp(-x))


def _proj_kernel(h_ref, tab_ref, gin_ref, w1_ref, gq_ref, wqa_ref, wqb_ref, gkv_ref, wk_ref, wvt_ref,
                 qa_ref, ka_ref, va_ref, g_ref, q_ref, k_ref, vt_ref):
    u = _rms(h_ref[...], gin_ref[...]).astype(bf16)

    def mm(lo, hi):
        return jnp.dot(u, w1_ref[:, lo:hi], preferred_element_type=f32)

    qa = mm(_C_QA, _C_KV) * (A_HEAD_DIM ** -0.5 * LOG2E)
    for h in range(A_HEADS):
        qa_ref[h] = qa[:, h * A_HEAD_DIM:(h + 1) * A_HEAD_DIM].astype(bf16)
    kv = mm(_C_KV, _C_GA)
    for j in range(A_KV_HEADS):
        ka_ref[j] = kv[:, j * A_HEAD_DIM:(j + 1) * A_HEAD_DIM].astype(bf16)
        va_ref[j] = kv[:, (A_KV_HEADS + j) * A_HEAD_DIM:(A_KV_HEADS + j + 1) * A_HEAD_DIM].astype(bf16)
    g_ref[:, :A_WIDTH] = _silu(mm(_C_GA, _C_CQ)).astype(bf16)
    g_ref[:, A_WIDTH:] = _silu(mm(_C_GB, _C_END)).astype(bf16)

    lat = mm(_C_CQ, _C_GB)
    cq = lat[:, :Q_LORA_RANK]
    ckv = lat[:, Q_LORA_RANK:Q_LORA_RANK + KV_LORA_RANK]
    kra = lat[:, _C_KRA - _C_CQ:_C_KRB - _C_CQ]
    krb = lat[:, _C_KRB - _C_CQ:_C_GB - _C_CQ]
    cos_q = tab_ref[:, 0 * B_QK_PAD:1 * B_QK_PAD]
    sin_q = tab_ref[:, 1 * B_QK_PAD:2 * B_QK_PAD]
    cos_k = tab_ref[:, 2 * B_QK_PAD:3 * B_QK_PAD]
    sin_k = tab_ref[:, 3 * B_QK_PAD:4 * B_QK_PAD]

    cqn = _rms(cq, gq_ref[...]).astype(bf16)
    qa_part = jnp.dot(cqn, wqa_ref[...], preferred_element_type=f32)
    qb_part = jnp.dot(cqn, wqb_ref[...], preferred_element_type=f32)
    for h in range(B_HEADS):
        sl = slice(h * B_QK_PAD, (h + 1) * B_QK_PAD)
        q_ref[h] = (qa_part[:, sl] * cos_q + qb_part[:, sl] * sin_q).astype(bf16)

    ckvn = _rms(ckv, gkv_ref[...]).astype(bf16)
    k_nope = jnp.dot(ckvn, wk_ref[...], preferred_element_type=f32)
    k_rope = kra * cos_k + krb * sin_k
    for h in range(B_HEADS):
        k_ref[h] = (k_nope[:, h * B_QK_PAD:(h + 1) * B_QK_PAD] + k_rope).astype(bf16)
    vt = lax.dot_general(wvt_ref[...], ckvn, _NT, preferred_element_type=f32)
    vt_ref[...] = vt.reshape(B_HEADS, B_V_DIM, vt.shape[-1]).astype(bf16)


def _proj(h, tab, tab_blocks, lw):
    rows = h.shape[0]
    t = ROW_TILE
    steps = rows // t
    const = lambda shape: pl.BlockSpec(shape, lambda i: (0,) * len(shape))
    return pl.pallas_call(
        _proj_kernel,
        grid=(steps,),
        in_specs=[
            pl.BlockSpec((t, D_MODEL), lambda i: (i, 0)),
            pl.BlockSpec((t, 4 * B_QK_PAD), lambda i: (i % tab_blocks, 0)),
            const((1, D_MODEL)),
            const((D_MODEL, _C_END)),
            const((1, Q_LORA_RANK)),
            const((Q_LORA_RANK, B_HEADS * B_QK_PAD)),
            const((Q_LORA_RANK, B_HEADS * B_QK_PAD)),
            const((1, KV_LORA_RANK)),
            const((KV_LORA_RANK, B_HEADS * B_QK_PAD)),
            const((B_WIDTH, KV_LORA_RANK)),
        ],
        out_specs=[
            pl.BlockSpec((A_HEADS, t, A_HEAD_DIM), lambda i: (0, i, 0)),
            pl.BlockSpec((A_KV_HEADS, t, A_HEAD_DIM), lambda i: (0, i, 0)),
            pl.BlockSpec((A_KV_HEADS, t, A_HEAD_DIM), lambda i: (0, i, 0)),
            pl.BlockSpec((t, A_WIDTH + B_WIDTH), lambda i: (i, 0)),
            pl.BlockSpec((B_HEADS, t, B_QK_PAD), lambda i: (0, i, 0)),
            pl.BlockSpec((B_HEADS, t, B_QK_PAD), lambda i: (0, i, 0)),
            pl.BlockSpec((B_HEADS, B_V_DIM, t), lambda i: (0, 0, i)),
        ],
        out_shape=[
            jax.ShapeDtypeStruct((A_HEADS, rows, A_HEAD_DIM), bf16),
            jax.ShapeDtypeStruct((A_KV_HEADS, rows, A_HEAD_DIM), bf16),
            jax.ShapeDtypeStruct((A_KV_HEADS, rows, A_HEAD_DIM), bf16),
            jax.ShapeDtypeStruct((rows, A_WIDTH + B_WIDTH), bf16),
            jax.ShapeDtypeStruct((B_HEADS, rows, B_QK_PAD), bf16),
            jax.ShapeDtypeStruct((B_HEADS, rows, B_QK_PAD), bf16),
            jax.ShapeDtypeStruct((B_HEADS, B_V_DIM, rows), bf16),
        ],
        compiler_params=_cparams("arbitrary"),
        name="proj",
    )(h, tab, lw["gin"], lw["w1"], lw["gq"], lw["wqa"], lw["wqb"], lw["gkv"], lw["wk"], lw["wvt"])


def _out_kernel(h_ref, ya_ref, yb_ref, g_ref, na_ref, nb_ref, w_ref, *rest, final):
    if final:
        nf_ref, o_ref = rest
    else:
        (o_ref,) = rest
    ya = _rms(ya_ref[...].astype(f32), na_ref[...]) * g_ref[:, :A_WIDTH].astype(f32)
    yb = _rms(yb_ref[...].astype(f32), nb_ref[...]) * g_ref[:, A_WIDTH:].astype(f32)
    hn = h_ref[...]
    hn = hn + jnp.dot(ya.astype(bf16), w_ref[:A_WIDTH, :], preferred_element_type=f32)
    hn = hn + jnp.dot(yb.astype(bf16), w_ref[A_WIDTH:, :], preferred_element_type=f32)
    if final:
        hn = _rms(hn, nf_ref[...])
    o_ref[...] = hn


def _out(h, ya, yb, g, lw, norm_final=None):
    rows = h.shape[0]
    t = ROW_TILE
    row = lambda width: pl.BlockSpec((t, width), lambda i: (i, 0))
    const = lambda shape: pl.BlockSpec(shape, lambda i: (0,) * len(shape))
    final = norm_final is not None
    in_specs = [row(D_MODEL), row(A_WIDTH), row(B_WIDTH), row(A_WIDTH + B_WIDTH),
                const((1, A_WIDTH)), const((1, B_WIDTH)), const((A_WIDTH + B_WIDTH, D_MODEL))]
    args = [h, ya, yb, g, lw["na"], lw["nb"], lw["wout"]]
    if final:
        in_specs.append(const((1, D_MODEL)))
        args.append(norm_final)
    return pl.pallas_call(
        functools.partial(_out_kernel, final=final),
        grid=(rows // t,),
        in_specs=in_specs,
        out_specs=row(D_MODEL),
        out_shape=jax.ShapeDtypeStruct((rows, D_MODEL), f32),
        compiler_params=_cparams("arbitrary"),
        name="outproj",
    )(*args)


def _bias_kernel(tab_ref, idx_ref, o_ref):
    idx = idx_ref[...]
    for h in range(A_HEADS):
        acc = jnp.where(idx < 0, NEG, 0.0).astype(f32)
        for b in range(N_BUCKETS):
            acc = jnp.where(idx == b, tab_ref[b * A_HEADS + h], acc)
        o_ref[h] = acc


def _bias_lookup(table_flat, idx):
    return pl.pallas_call(
        _bias_kernel,
        in_specs=[pl.BlockSpec(memory_space=pltpu.SMEM), pl.BlockSpec(idx.shape, lambda: (0, 0))],
        out_specs=pl.BlockSpec((A_HEADS,) + idx.shape, lambda: (0, 0, 0)),
        out_shape=jax.ShapeDtypeStruct((A_HEADS,) + idx.shape, f32),
        compiler_params=pltpu.CompilerParams(vmem_limit_bytes=VMEM_LIMIT_BYTES),
        name="relbias",
    )(table_flat, idx)


def _t5_bucket(rel):
    nb = N_BUCKETS // 2
    max_exact = nb // 2
    ret = jnp.where(rel > 0, nb, 0)
    n = jnp.abs(rel)
    nf = jnp.maximum(n, 1).astype(jnp.float32)
    large = max_exact + (jnp.log(nf / max_exact) / math.log(MAX_DISTANCE / max_exact)
                         * (nb - max_exact)).astype(jnp.int32)
    large = jnp.minimum(large, nb - 1)
    return ret + jnp.where(n < max_exact, n, large)


def _bias_tables(rel_bias_table, nblk):
    table_flat = (rel_bias_table.astype(f32) * LOG2E).reshape(-1)
    i = jnp.arange(BLOCK, dtype=jnp.int32)[:, None]
    j = jnp.arange(3 * BLOCK, dtype=jnp.int32)[None, :]
    rel = j - i - BLOCK
    band = jnp.where(jnp.abs(rel) <= WINDOW, _t5_bucket(rel), -1)
    first = jnp.where(j < BLOCK, -1, band)
    last = jnp.where(j >= 2 * BLOCK, -1, band)
    idx_band = jnp.concatenate([first, band, last], axis=0)
    bias_band = _bias_lookup(table_flat, idx_band).reshape(A_HEADS, 3, BLOCK, 3 * BLOCK).transpose(1, 0, 2, 3)

    n = jnp.arange(nblk, dtype=jnp.int32)[:, None, None]
    k = jnp.arange(N_META, dtype=jnp.int32)[None, :, None]
    qi = jnp.arange(BLOCK, dtype=jnp.int32)[None, None, :]
    rel_m = k - (N_META + n * BLOCK + qi)
    idx_m = _t5_bucket(rel_m).reshape(nblk * N_META, BLOCK)
    bias_meta = (_bias_lookup(table_flat, idx_m).reshape(A_HEADS, nblk, N_META, BLOCK)
                 .transpose(1, 0, 3, 2))

    qp = jnp.arange(N_META, dtype=jnp.int32)[:, None]
    kp = jnp.arange(N_META + BLOCK, dtype=jnp.int32)[None, :]
    rel_q = kp - qp
    idx_q = jnp.where(jnp.abs(rel_q) <= WINDOW, _t5_bucket(rel_q), -1)
    bias_q = _bias_lookup(table_flat, idx_q)
    return bias_band, bias_meta, bias_q[:, :, :N_META], bias_q[:, :, N_META:]


def _softmax_sink_parts(s_list, sink):
    m = sink
    for s in s_list:
        m = jnp.maximum(m, jnp.max(s, axis=-1, keepdims=True))
    es = [jnp.exp2(s - m) for s in s_list]
    den = jnp.exp2(sink - m)
    for e in es:
        den = den + jnp.sum(e, axis=-1, keepdims=True)
    return es, 1.0 / den


def _win_kernel(sink_ref, q_ref, ka_ref, va_ref, kam_ref, vam_ref, bias_ref, biasm_ref, o_ref, *, nblk):
    n = pl.program_id(1)
    lo = pl.multiple_of(jnp.maximum(n - 1, 0) * BLOCK, BLOCK)
    mid = pl.multiple_of(n * BLOCK, BLOCK)
    hi = pl.multiple_of(jnp.minimum(n + 1, nblk - 1) * BLOCK, BLOCK)
    for kvh in range(A_KV_HEADS):
        band = lambda ref: jnp.concatenate(
            [ref[kvh, pl.ds(lo, BLOCK), :], ref[kvh, pl.ds(mid, BLOCK), :], ref[kvh, pl.ds(hi, BLOCK), :]], axis=0)
        kb, vb = band(ka_ref), band(va_ref)
        q4 = q_ref[A_GROUP * kvh:A_GROUP * (kvh + 1)].reshape(A_GROUP * BLOCK, A_HEAD_DIM)
        s = lax.dot_general(q4, kb, _NT, preferred_element_type=f32)
        sm = lax.dot_general(q4, kam_ref[kvh], _NT, preferred_element_type=f32)
        ps, pms, rs = [], [], []
        for g in range(A_GROUP):
            h = A_GROUP * kvh + g
            rows = slice(g * BLOCK, (g + 1) * BLOCK)
            (e, em), r = _softmax_sink_parts([s[rows] + bias_ref[0, h], sm[rows] + biasm_ref[0, h]], sink_ref[h])
            ps.append(e.astype(bf16))
            pms.append(em.astype(bf16))
            rs.append(r)
        o = (jnp.dot(jnp.concatenate(ps, axis=0), vb, preferred_element_type=f32)
             + jnp.dot(jnp.concatenate(pms, axis=0), vam_ref[kvh], preferred_element_type=f32))
        for g in range(A_GROUP):
            h = A_GROUP * kvh + g
            o_ref[:, h * A_HEAD_DIM:(h + 1) * A_HEAD_DIM] = (o[g * BLOCK:(g + 1) * BLOCK] * rs[g]).astype(o_ref.dtype)


def _win(sink, qa, ka, va, kam, vam, bias_band, bias_meta, batch, seq):
    nblk = seq // BLOCK
    variant = lambda n: jnp.where(n == 0, 0, jnp.where(n == nblk - 1, 2, 1))
    return pl.pallas_call(
        functools.partial(_win_kernel, nblk=nblk),
        grid=(batch, nblk),
        in_specs=[
            pl.BlockSpec(memory_space=pltpu.SMEM),
            pl.BlockSpec((A_HEADS, BLOCK, A_HEAD_DIM), lambda b, n: (0, b * nblk + n, 0)),
            pl.BlockSpec((A_KV_HEADS, seq, A_HEAD_DIM), lambda b, n: (0, b, 0)),
            pl.BlockSpec((A_KV_HEADS, seq, A_HEAD_DIM), lambda b, n: (0, b, 0)),
            pl.BlockSpec((A_KV_HEADS, N_META, A_HEAD_DIM), lambda b, n: (0, b, 0)),
            pl.BlockSpec((A_KV_HEADS, N_META, A_HEAD_DIM), lambda b, n: (0, b, 0)),
            pl.BlockSpec((1, A_HEADS, BLOCK, 3 * BLOCK), lambda b, n: (variant(n), 0, 0, 0)),
            pl.BlockSpec((1, A_HEADS, BLOCK, N_META), lambda b, n: (n, 0, 0, 0)),
        ],
        out_specs=pl.BlockSpec((BLOCK, A_WIDTH), lambda b, n: (b * nblk + n, 0)),
        out_shape=jax.ShapeDtypeStruct((batch * seq, A_WIDTH), bf16),
        compiler_params=_cparams("arbitrary", "arbitrary"),
        name="win",
    )(sink, qa, ka, va, kam, vam, bias_band, bias_meta)


def _win_meta_kernel(sink_ref, q_ref, ka_ref, va_ref, kam_ref, vam_ref, biasm_ref, biasr_ref, o_ref):
    for kvh in range(A_KV_HEADS):
        q4 = q_ref[A_GROUP * kvh:A_GROUP * (kvh + 1)].reshape(A_GROUP * N_META, A_HEAD_DIM)
        sm = lax.dot_general(q4, kam_ref[kvh], _NT, preferred_element_type=f32)
        sr = lax.dot_general(q4, ka_ref[kvh], _NT, preferred_element_type=f32)
        pms, prs, rs = [], [], []
        for g in range(A_GROUP):
            h = A_GROUP * kvh + g
            rows = slice(g * N_META, (g + 1) * N_META)
            (em, er), r = _softmax_sink_parts([sm[rows] + biasm_ref[h], sr[rows] + biasr_ref[h]], sink_ref[h])
            pms.append(em.astype(bf16))
            prs.append(er.astype(bf16))
            rs.append(r)
        o = (jnp.dot(jnp.concatenate(pms, axis=0), vam_ref[kvh], preferred_element_type=f32)
             + jnp.dot(jnp.concatenate(prs, axis=0), va_ref[kvh], preferred_element_type=f32))
        for g in range(A_GROUP):
            h = A_GROUP * kvh + g
            o_ref[:, h * A_HEAD_DIM:(h + 1) * A_HEAD_DIM] = (o[g * N_META:(g + 1) * N_META] * rs[g]).astype(o_ref.dtype)


def _win_meta(sink, qam, ka, va, kam, vam, bias_mm, bias_mr, batch, seq):
    nblk = seq // BLOCK
    const3 = lambda shape: pl.BlockSpec(shape, lambda b: (0, 0, 0))
    return pl.pallas_call(
        _win_meta_kernel,
        grid=(batch,),
        in_specs=[
            pl.BlockSpec(memory_space=pltpu.SMEM),
            pl.BlockSpec((A_HEADS, N_META, A_HEAD_DIM), lambda b: (0, b, 0)),
            pl.BlockSpec((A_KV_HEADS, BLOCK, A_HEAD_DIM), lambda b: (0, b * nblk, 0)),
            pl.BlockSpec((A_KV_HEADS, BLOCK, A_HEAD_DIM), lambda b: (0, b * nblk, 0)),
            pl.BlockSpec((A_KV_HEADS, N_META, A_HEAD_DIM), lambda b: (0, b, 0)),
            pl.BlockSpec((A_KV_HEADS, N_META, A_HEAD_DIM), lambda b: (0, b, 0)),
            const3((A_HEADS, N_META, N_META)),
            const3((A_HEADS, N_META, BLOCK)),
        ],
        out_specs=pl.BlockSpec((N_META, A_WIDTH), lambda b: (b, 0)),
        out_shape=jax.ShapeDtypeStruct((batch * N_META, A_WIDTH), bf16),
        compiler_params=_cparams("arbitrary"),
        name="win_meta",
    )(sink, qam, ka, va, kam, vam, bias_mm, bias_mr)


def _mla_kernel(q_ref, k_ref, km_ref, vt_ref, vtm_ref, o_ref, acc_ref):
    def head(h, carry):
        q = q_ref[h]
        s = lax.dot_general(k_ref[h], q, _NT, preferred_element_type=f32)
        sm = lax.dot_general(km_ref[h], q, _NT, preferred_element_type=f32)
        m = jnp.maximum(jnp.max(s, axis=0, keepdims=True), jnp.max(sm, axis=0, keepdims=True))
        p = jnp.exp2(s - m)
        pm = jnp.exp2(sm - m)
        den = jnp.sum(p, axis=0, keepdims=True) + jnp.sum(pm, axis=0, keepdims=True)
        acc = (jnp.dot(vt_ref[h], p.astype(bf16), preferred_element_type=f32)
               + jnp.dot(vtm_ref[0, h], pm.astype(bf16), preferred_element_type=f32))
        acc_ref[h] = acc * (1.0 / den)
        return carry

    lax.fori_loop(0, B_HEADS, head, 0)
    o_ref[...] = acc_ref[...].reshape(B_WIDTH, acc_ref.shape[-1]).T.astype(o_ref.dtype)


def _mla(q, k, km, vt, vtm, batch, seq):
    tq = MLA_Q_TILE
    nq = seq // tq
    return pl.pallas_call(
        _mla_kernel,
        grid=(batch, nq),
        in_specs=[
            pl.BlockSpec((B_HEADS, tq, B_QK_PAD), lambda b, i: (0, b * nq + i, 0)),
            pl.BlockSpec((B_HEADS, seq, B_QK_PAD), lambda b, i: (0, b, 0)),
            pl.BlockSpec((B_HEADS, N_META, B_QK_PAD), lambda b, i: (0, b, 0)),
            pl.BlockSpec((B_HEADS, B_V_DIM, seq), lambda b, i: (0, 0, b)),
            pl.BlockSpec((1, B_HEADS, B_V_DIM, N_META), lambda b, i: (b, 0, 0, 0)),
        ],
        out_specs=pl.BlockSpec((tq, B_WIDTH), lambda b, i: (b * nq + i, 0)),
        out_shape=jax.ShapeDtypeStruct((batch * seq, B_WIDTH), bf16),
        scratch_shapes=[pltpu.VMEM((B_HEADS, B_V_DIM, tq), f32)],
        compiler_params=_cparams("arbitrary", "arbitrary"),
        name="mla",
    )(q, k, km, vt, vtm)


def _mla_meta_kernel(q_ref, k_ref, km_ref, vt_ref, vtm_ref, o_ref):
    for h in range(B_HEADS):
        q = q_ref[h]
        s = lax.dot_general(q, k_ref[h], _NT, preferred_element_type=f32)
        sm = lax.dot_general(q, km_ref[h], _NT, preferred_element_type=f32)
        m = jnp.maximum(jnp.max(s, axis=-1, keepdims=True), jnp.max(sm, axis=-1, keepdims=True))
        p = jnp.exp2(s - m)
        pm = jnp.exp2(sm - m)
        den = jnp.sum(p, axis=-1, keepdims=True) + jnp.sum(pm, axis=-1, keepdims=True)
        o = (lax.dot_general(p.astype(bf16), vt_ref[h], _NT, preferred_element_type=f32)
             + lax.dot_general(pm.astype(bf16), vtm_ref[0, h], _NT, preferred_element_type=f32))
        o_ref[:, h * B_V_DIM:(h + 1) * B_V_DIM] = (o * (1.0 / den)).astype(o_ref.dtype)


def _mla_meta(qm, k, km, vt, vtm, batch, seq):
    return pl.pallas_call(
        _mla_meta_kernel,
        grid=(batch,),
        in_specs=[
            pl.BlockSpec((B_HEADS, N_META, B_QK_PAD), lambda b: (0, b, 0)),
            pl.BlockSpec((B_HEADS, seq, B_QK_PAD), lambda b: (0, b, 0)),
            pl.BlockSpec((B_HEADS, N_META, B_QK_PAD), lambda b: (0, b, 0)),
            pl.BlockSpec((B_HEADS, B_V_DIM, seq), lambda b: (0, 0, b)),
            pl.BlockSpec((1, B_HEADS, B_V_DIM, N_META), lambda b: (b, 0, 0, 0)),
        ],
        out_specs=pl.BlockSpec((N_META, B_WIDTH), lambda b: (b, 0)),
        out_shape=jax.ShapeDtypeStruct((batch * N_META, B_WIDTH), bf16),
        compiler_params=_cparams("arbitrary"),
        name="mla_meta",
    )(qm, k, km, vt, vtm)


def _rope_tables(pos):
    half = B_ROPE_DIM // 2
    freqs = ROPE_THETA ** (-jnp.arange(half, dtype=jnp.float32) / half)
    ang = pos.astype(jnp.float32)[:, None] * freqs[None, :]
    cos, sin = jnp.cos(ang), jnp.sin(ang)
    rows = pos.shape[0]
    c_mla = (B_NOPE_DIM + B_ROPE_DIM) ** -0.5 * LOG2E
    z = lambda w: jnp.zeros((rows, w), f32)
    tail = B_QK_PAD - B_NOPE_DIM - B_ROPE_DIM
    cos_q = jnp.concatenate([jnp.full((rows, B_NOPE_DIM), c_mla, f32), c_mla * cos, c_mla * cos, z(tail)], axis=1)
    sin_q = jnp.concatenate([z(B_NOPE_DIM), c_mla * sin, c_mla * sin, z(tail)], axis=1)
    cos_k = jnp.concatenate([z(B_NOPE_DIM), cos, cos, z(tail)], axis=1)
    sin_k = jnp.concatenate([z(B_NOPE_DIM), sin, sin, z(tail)], axis=1)
    return jnp.concatenate([cos_q, sin_q, cos_k, sin_k], axis=1)


def _rot_cols(w):
    half = w.shape[-1] // 2
    return jnp.concatenate([-w[..., half:], w[..., :half]], axis=-1)


def _layer_weights(norm_in, w_in, norm_q_lat, w_uq, norm_kv_lat, w_ukv, norm_out_a, norm_out_b, w_out):
    d = w_in.shape[0]
    offs = [0, A_WIDTH, A_WIDTH + 128, A_WIDTH + 256, 2 * A_WIDTH + 256,
            2 * A_WIDTH + 256 + Q_LORA_RANK, 2 * A_WIDTH + 256 + Q_LORA_RANK + KV_LORA_RANK,
            2 * A_WIDTH + 256 + Q_LORA_RANK + KV_LORA_RANK + B_ROPE_DIM]
    qa, ka, va, ga, cq, ckv, kr = (w_in[:, offs[i]:offs[i + 1]] for i in range(7))
    gb = w_in[:, offs[7]:]
    tail = B_QK_PAD - B_NOPE_DIM - B_ROPE_DIM
    pad_k = lambda w: jnp.concatenate([jnp.zeros((d, B_NOPE_DIM), w.dtype), w, jnp.zeros((d, tail), w.dtype)], axis=1)
    w1 = jnp.concatenate([qa, ka, va, ga, cq, ckv, pad_k(kr), pad_k(_rot_cols(kr)), gb], axis=1).astype(bf16)

    r = w_uq.shape[0]
    uq = w_uq.reshape(r, B_HEADS, B_NOPE_DIM + B_ROPE_DIM)
    nope, rope = uq[..., :B_NOPE_DIM], uq[..., B_NOPE_DIM:]
    wqa = jnp.concatenate([nope, rope, jnp.zeros((r, B_HEADS, tail), uq.dtype)], axis=-1)
    wqb = jnp.concatenate([jnp.zeros_like(nope), _rot_cols(rope), jnp.zeros((r, B_HEADS, tail), uq.dtype)], axis=-1)

    rk = w_ukv.shape[0]
    ukv = w_ukv.reshape(rk, B_HEADS, B_NOPE_DIM + B_V_DIM)
    k_nope, v = ukv[..., :B_NOPE_DIM], ukv[..., B_NOPE_DIM:]
    wk = jnp.concatenate([k_nope, jnp.zeros((rk, B_HEADS, B_QK_PAD - B_NOPE_DIM), ukv.dtype)], axis=-1)
    return {
        "gin": norm_in.astype(f32)[None, :],
        "w1": w1,
        "gq": norm_q_lat.astype(f32)[None, :],
        "wqa": wqa.reshape(r, B_HEADS * B_QK_PAD).astype(bf16),
        "wqb": wqb.reshape(r, B_HEADS * B_QK_PAD).astype(bf16),
        "gkv": norm_kv_lat.astype(f32)[None, :],
        "wk": wk.reshape(rk, B_HEADS * B_QK_PAD).astype(bf16),
        "wvt": v.reshape(rk, B_WIDTH).T.astype(bf16),
        "na": norm_out_a.astype(f32)[None, :],
        "nb": norm_out_b.astype(f32)[None, :],
        "wout": w_out.astype(bf16),
    }


def kernel(x, meta_tokens, rel_bias_table, norm_in, w_in, sink_a, norm_q_lat, w_uq, norm_kv_lat, w_ukv,
           norm_out_a, norm_out_b, w_out, norm_final):
    batch, seq, d = x.shape
    depth = w_in.shape[0]
    assert d == D_MODEL and seq % MLA_Q_TILE == 0 and (batch * N_META) % ROW_TILE == 0

    h_real = x.reshape(batch * seq, d).astype(f32)
    h_meta = jnp.tile(meta_tokens.astype(f32), (batch, 1))
    tab_real = _rope_tables(N_META + jnp.arange(seq))
    tab_meta = _rope_tables(jnp.arange(batch * N_META) % N_META)
    bias_band, bias_meta, bias_mm, bias_mr = _bias_tables(rel_bias_table, seq // BLOCK)

    out = None
    for i in range(depth):
        last = i == depth - 1
        lw = _layer_weights(norm_in[i], w_in[i], norm_q_lat[i], w_uq[i], norm_kv_lat[i], w_ukv[i],
                            norm_out_a[i], norm_out_b[i], w_out[i])
        sink = sink_a[i].astype(f32) * LOG2E
        qa, ka, va, g, q, k, vt = _proj(h_real, tab_real, seq // ROW_TILE, lw)
        qam, kam, vam, gm, qm, km, vtm = _proj(h_meta, tab_meta, 1, lw)
        vtm_b = vtm.reshape(B_HEADS, B_V_DIM, batch, N_META).transpose(2, 0, 1, 3)

        ya = _win(sink, qa, ka, va, kam, vam, bias_band, bias_meta, batch, seq)
        yb = _mla(q, k, km, vt, vtm_b, batch, seq)
        if last:
            out = _out(h_real, ya, yb, g, lw, norm_final.astype(f32)[None, :])
        else:
            yam = _win_meta(sink, qam, ka, va, kam, vam, bias_mm, bias_mr, batch, seq)
            ybm = _mla_meta(qm, k, km, vt, vtm_b, batch, seq)
            h_real = _out(h_real, ya, yb, g, lw)
            h_meta = _out(h_meta, yam, ybm, gm, lw)
    return out.reshape(batch, seq, d).astype(x.dtype)
```

```python
import functools
import math

import jax
import jax.numpy as jnp
from jax import lax
from jax.experimental import pallas as pl
from jax.experimental.pallas import tpu as pltpu

D_MODEL = 1024
N_META = 16
BLOCK = 128
WINDOW = 128
A_HEADS = 8
A_KV_HEADS = 2
A_GROUP = A_HEADS // A_KV_HEADS
A_HEAD_DIM = 64
A_WIDTH = A_HEADS * A_HEAD_DIM
B_HEADS = 8
B_NOPE_DIM = 64
B_ROPE_DIM = 32
B_V_DIM = 64
B_WIDTH = B_HEADS * B_V_DIM
B_QK_PAD = 128
Q_LORA_RANK = 256
KV_LORA_RANK = 128
N_BUCKETS = 32
MAX_DISTANCE = 128
ROPE_THETA = 10000.0
EPS = 1e-6

LOG2E = math.log2(math.e)
NEG = -1e30
ROW_TILE = 256
MLA_Q_TILE = 256
VMEM_LIMIT_BYTES = 56 * 1024 * 1024

_C_QA = 0
_C_KV = _C_QA + A_WIDTH
_C_GA = _C_KV + 2 * A_KV_HEADS * A_HEAD_DIM
_C_CQ = _C_GA + A_WIDTH
_C_CKV = _C_CQ + Q_LORA_RANK
_C_KRA = _C_CKV + KV_LORA_RANK
_C_KRB = _C_KRA + B_QK_PAD
_C_GB = _C_KRB + B_QK_PAD
_C_END = _C_GB + B_WIDTH

_NT = (((1,), (1,)), ((), ()))

bf16 = jnp.bfloat16
f32 = jnp.float32


def _cparams(*sem):
    return pltpu.CompilerParams(dimension_semantics=sem, vmem_limit_bytes=VMEM_LIMIT_BYTES)


def _rms(x, gain):
    return x * lax.rsqrt(jnp.mean(x * x, axis=-1, keepdims=True) + EPS) * gain


def _silu(x):
    return x / (1.0 + jnp.exp(-x))


def _proj_kernel(h_ref, tab_ref, gin_ref, w1_ref, gq_ref, wqa_ref, wqb_ref, gkv_ref, wk_ref, wvt_ref,
                 qa_ref, ka_ref, va_ref, g_ref, q_ref, k_ref, vt_ref):
    u = _rms(h_ref[...], gin_ref[...]).astype(bf16)

    def mm(lo, hi):
        return jnp.dot(u, w1_ref[:, lo:hi], preferred_element_type=f32)

    qa = mm(_C_QA, _C_KV) * (A_HEAD_DIM ** -0.5 * LOG2E)
    for h in range(A_HEADS):
        qa_ref[h] = qa[:, h * A_HEAD_DIM:(h + 1) * A_HEAD_DIM].astype(bf16)
    kv = mm(_C_KV, _C_GA)
    for j in range(A_KV_HEADS):
        ka_ref[j] = kv[:, j * A_HEAD_DIM:(j + 1) * A_HEAD_DIM].astype(bf16)
        va_ref[j] = kv[:, (A_KV_HEADS + j) * A_HEAD_DIM:(A_KV_HEADS + j + 1) * A_HEAD_DIM].astype(bf16)
    g_ref[:, :A_WIDTH] = _silu(mm(_C_GA, _C_CQ)).astype(bf16)
    g_ref[:, A_WIDTH:] = _silu(mm(_C_GB, _C_END)).astype(bf16)

    lat = mm(_C_CQ, _C_GB)
    cq = lat[:, :Q_LORA_RANK]
    ckv = lat[:, Q_LORA_RANK:Q_LORA_RANK + KV_LORA_RANK]
    kra = lat[:, _C_KRA - _C_CQ:_C_KRB - _C_CQ]
    krb = lat[:, _C_KRB - _C_CQ:_C_GB - _C_CQ]
    cos_q = tab_ref[:, 0 * B_QK_PAD:1 * B_QK_PAD]
    sin_q = tab_ref[:, 1 * B_QK_PAD:2 * B_QK_PAD]
    cos_k = tab_ref[:, 2 * B_QK_PAD:3 * B_QK_PAD]
    sin_k = tab_ref[:, 3 * B_QK_PAD:4 * B_QK_PAD]

    cqn = _rms(cq, gq_ref[...]).astype(bf16)
    qa_part = jnp.dot(cqn, wqa_ref[...], preferred_element_type=f32)
    qb_part = jnp.dot(cqn, wqb_ref[...], preferred_element_type=f32)
    for h in range(B_HEADS):
        sl = slice(h * B_QK_PAD, (h + 1) * B_QK_PAD)
        q_ref[h] = (qa_part[:, sl] * cos_q + qb_part[:, sl] * sin_q).astype(bf16)

    ckvn = _rms(ckv, gkv_ref[...]).astype(bf16)
    k_nope = jnp.dot(ckvn, wk_ref[...], preferred_element_type=f32)
    k_rope = kra * cos_k + krb * sin_k
    for h in range(B_HEADS):
        k_ref[h] = (k_nope[:, h * B_QK_PAD:(h + 1) * B_QK_PAD] + k_rope).astype(bf16)
    vt = lax.dot_general(wvt_ref[...], ckvn, _NT, preferred_element_type=f32)
    vt_ref[...] = vt.reshape(B_HEADS, B_V_DIM, vt.shape[-1]).astype(bf16)


def _proj(h, tab, tab_blocks, lw):
    rows = h.shape[0]
    t = ROW_TILE
    steps = rows // t
    const = lambda shape: pl.BlockSpec(shape, lambda i: (0,) * len(shape))
    return pl.pallas_call(
        _proj_kernel,
        grid=(steps,),
        in_specs=[
            pl.BlockSpec((t, D_MODEL), lambda i: (i, 0)),
            pl.BlockSpec((t, 4 * B_QK_PAD), lambda i: (i % tab_blocks, 0)),
            const((1, D_MODEL)),
            const((D_MODEL, _C_END)),
            const((1, Q_LORA_RANK)),
            const((Q_LORA_RANK, B_HEADS * B_QK_PAD)),
            const((Q_LORA_RANK, B_HEADS * B_QK_PAD)),
            const((1, KV_LORA_RANK)),
            const((KV_LORA_RANK, B_HEADS * B_QK_PAD)),
            const((B_WIDTH, KV_LORA_RANK)),
        ],
        out_specs=[
            pl.BlockSpec((A_HEADS, t, A_HEAD_DIM), lambda i: (0, i, 0)),
            pl.BlockSpec((A_KV_HEADS, t, A_HEAD_DIM), lambda i: (0, i, 0)),
            pl.BlockSpec((A_KV_HEADS, t, A_HEAD_DIM), lambda i: (0, i, 0)),
            pl.BlockSpec((t, A_WIDTH + B_WIDTH), lambda i: (i, 0)),
            pl.BlockSpec((B_HEADS, t, B_QK_PAD), lambda i: (0, i, 0)),
            pl.BlockSpec((B_HEADS, t, B_QK_PAD), lambda i: (0, i, 0)),
            pl.BlockSpec((B_HEADS, B_V_DIM, t), lambda i: (0, 0, i)),
        ],
        out_shape=[
            jax.ShapeDtypeStruct((A_HEADS, rows, A_HEAD_DIM), bf16),
            jax.ShapeDtypeStruct((A_KV_HEADS, rows, A_HEAD_DIM), bf16),
            jax.ShapeDtypeStruct((A_KV_HEADS, rows, A_HEAD_DIM), bf16),
            jax.ShapeDtypeStruct((rows, A_WIDTH + B_WIDTH), bf16),
            jax.ShapeDtypeStruct((B_HEADS, rows, B_QK_PAD), bf16),
            jax.ShapeDtypeStruct((B_HEADS, rows, B_QK_PAD), bf16),
            jax.ShapeDtypeStruct((B_HEADS, B_V_DIM, rows), bf16),
        ],
        compiler_params=_cparams("arbitrary"),
        name="proj",
    )(h, tab, lw["gin"], lw["w1"], lw["gq"], lw["wqa"], lw["wqb"], lw["gkv"], lw["wk"], lw["wvt"])


def _out_kernel(h_ref, ya_ref, yb_ref, g_ref, na_ref, nb_ref, w_ref, *rest, final):
    if final:
        nf_ref, o_ref = rest
    else:
        (o_ref,) = rest
    ya = _rms(ya_ref[...].astype(f32), na_ref[...]) * g_ref[:, :A_WIDTH].astype(f32)
    yb = _rms(yb_ref[...].astype(f32), nb_ref[...]) * g_ref[:, A_WIDTH:].astype(f32)
    hn = h_ref[...]
    hn = hn + jnp.dot(ya.astype(bf16), w_ref[:A_WIDTH, :], preferred_element_type=f32)
    hn = hn + jnp.dot(yb.astype(bf16), w_ref[A_WIDTH:, :], preferred_element_type=f32)
    if final:
        hn = _rms(hn, nf_ref[...])
    o_ref[...] = hn


def _out(h, ya, yb, g, lw, norm_final=None):
    rows = h.shape[0]
    t = ROW_TILE
    row = lambda width: pl.BlockSpec((t, width), lambda i: (i, 0))
    const = lambda shape: pl.BlockSpec(shape, lambda i: (0,) * len(shape))
    final = norm_final is not None
    in_specs = [row(D_MODEL), row(A_WIDTH), row(B_WIDTH), row(A_WIDTH + B_WIDTH),
                const((1, A_WIDTH)), const((1, B_WIDTH)), const((A_WIDTH + B_WIDTH, D_MODEL))]
    args = [h, ya, yb, g, lw["na"], lw["nb"], lw["wout"]]
    if final:
        in_specs.append(const((1, D_MODEL)))
        args.append(norm_final)
    return pl.pallas_call(
        functools.partial(_out_kernel, final=final),
        grid=(rows // t,),
        in_specs=in_specs,
        out_specs=row(D_MODEL),
        out_shape=jax.ShapeDtypeStruct((rows, D_MODEL), f32),
        compiler_params=_cparams("arbitrary"),
        name="outproj",
    )(*args)


def _bias_kernel(tab_ref, idx_ref, o_ref):
    idx = idx_ref[...]
    for h in range(A_HEADS):
        acc = jnp.where(idx < 0, NEG, 0.0).astype(f32)
        for b in range(N_BUCKETS):
            acc = jnp.where(idx == b, tab_ref[b * A_HEADS + h], acc)
        o_ref[h] = acc


def _bias_lookup(table_flat, idx):
    return pl.pallas_call(
        _bias_kernel,
        in_specs=[pl.BlockSpec(memory_space=pltpu.SMEM), pl.BlockSpec(idx.shape, lambda: (0, 0))],
        out_specs=pl.BlockSpec((A_HEADS,) + idx.shape, lambda: (0, 0, 0)),
        out_shape=jax.ShapeDtypeStruct((A_HEADS,) + idx.shape, f32),
        compiler_params=pltpu.CompilerParams(vmem_limit_bytes=VMEM_LIMIT_BYTES),
        name="relbias",
    )(table_flat, idx)


def _t5_bucket(rel):
    nb = N_BUCKETS // 2
    max_exact = nb // 2
    ret = jnp.where(rel > 0, nb, 0)
    n = jnp.abs(rel)
    nf = jnp.maximum(n, 1).astype(jnp.float32)
    large = max_exact + (jnp.log(nf / max_exact) / math.log(MAX_DISTANCE / max_exact)
                         * (nb - max_exact)).astype(jnp.int32)
    large = jnp.minimum(large, nb - 1)
    return ret + jnp.where(n < max_exact, n, large)


def _bias_tables(rel_bias_table, nblk):
    table_flat = (rel_bias_table.astype(f32) * LOG2E).reshape(-1)
    i = jnp.arange(BLOCK, dtype=jnp.int32)[:, None]
    j = jnp.arange(3 * BLOCK, dtype=jnp.int32)[None, :]
    rel = j - i - BLOCK
    band = jnp.where(jnp.abs(rel) <= WINDOW, _t5_bucket(rel), -1)
    first = jnp.where(j < BLOCK, -1, band)
    last = jnp.where(j >= 2 * BLOCK, -1, band)
    idx_band = jnp.concatenate([first, band, last], axis=0)
    bias_band = _bias_lookup(table_flat, idx_band).reshape(A_HEADS, 3, BLOCK, 3 * BLOCK).transpose(1, 0, 2, 3)

    n = jnp.arange(nblk, dtype=jnp.int32)[:, None, None]
    k = jnp.arange(N_META, dtype=jnp.int32)[None, :, None]
    qi = jnp.arange(BLOCK, dtype=jnp.int32)[None, None, :]
    rel_m = k - (N_META + n * BLOCK + qi)
    idx_m = _t5_bucket(rel_m).reshape(nblk * N_META, BLOCK)
    bias_meta = (_bias_lookup(table_flat, idx_m).reshape(A_HEADS, nblk, N_META, BLOCK)
                 .transpose(1, 0, 3, 2))

    qp = jnp.arange(N_META, dtype=jnp.int32)[:, None]
    kp = jnp.arange(N_META + BLOCK, dtype=jnp.int32)[None, :]
    rel_q = kp - qp
    idx_q = jnp.where(jnp.abs(rel_q) <= WINDOW, _t5_bucket(rel_q), -1)
    bias_q = _bias_lookup(table_flat, idx_q)
    return bias_band, bias_meta, bias_q[:, :, :N_META], bias_q[:, :, N_META:]


def _softmax_sink_parts(s_list, sink):
    m = sink
    for s in s_list:
        m = jnp.maximum(m, jnp.max(s, axis=-1, keepdims=True))
    es = [jnp.exp2(s - m) for s in s_list]
    den = jnp.exp2(sink - m)
    for e in es:
        den = den + jnp.sum(e, axis=-1, keepdims=True)
    return es, 1.0 / den


def _win_kernel(sink_ref, q_ref, ka_ref, va_ref, kam_ref, vam_ref, bias_ref, biasm_ref, o_ref, *, nblk):
    n = pl.program_id(1)
    lo = pl.multiple_of(jnp.maximum(n - 1, 0) * BLOCK, BLOCK)
    mid = pl.multiple_of(n * BLOCK, BLOCK)
    hi = pl.multiple_of(jnp.minimum(n + 1, nblk - 1) * BLOCK, BLOCK)
    for kvh in range(A_KV_HEADS):
        band = lambda ref: jnp.concatenate(
            [ref[kvh, pl.ds(lo, BLOCK), :], ref[kvh, pl.ds(mid, BLOCK), :], ref[kvh, pl.ds(hi, BLOCK), :]], axis=0)
        kb, vb = band(ka_ref), band(va_ref)
        q4 = q_ref[A_GROUP * kvh:A_GROUP * (kvh + 1)].reshape(A_GROUP * BLOCK, A_HEAD_DIM)
        s = lax.dot_general(q4, kb, _NT, preferred_element_type=f32)
        sm = lax.dot_general(q4, kam_ref[kvh], _NT, preferred_element_type=f32)
        ps, pms, rs = [], [], []
        for g in range(A_GROUP):
            h = A_GROUP * kvh + g
            rows = slice(g * BLOCK, (g + 1) * BLOCK)
            (e, em), r = _softmax_sink_parts([s[rows] + bias_ref[0, h], sm[rows] + biasm_ref[0, h]], sink_ref[h])
            ps.append(e.astype(bf16))
            pms.append(em.astype(bf16))
            rs.append(r)
        o = (jnp.dot(jnp.concatenate(ps, axis=0), vb, preferred_element_type=f32)
             + jnp.dot(jnp.concatenate(pms, axis=0), vam_ref[kvh], preferred_element_type=f32))
        for g in range(A_GROUP):
            h = A_GROUP * kvh + g
            o_ref[:, h * A_HEAD_DIM:(h + 1) * A_HEAD_DIM] = (o[g * BLOCK:(g + 1) * BLOCK] * rs[g]).astype(o_ref.dtype)


def _win(sink, qa, ka, va, kam, vam, bias_band, bias_meta, batch, seq):
    nblk = seq // BLOCK
    variant = lambda n: jnp.where(n == 0, 0, jnp.where(n == nblk - 1, 2, 1))
    return pl.pallas_call(
        functools.partial(_win_kernel, nblk=nblk),
        grid=(batch, nblk),
        in_specs=[
            pl.BlockSpec(memory_space=pltpu.SMEM),
            pl.BlockSpec((A_HEADS, BLOCK, A_HEAD_DIM), lambda b, n: (0, b * nblk + n, 0)),
            pl.BlockSpec((A_KV_HEADS, seq, A_HEAD_DIM), lambda b, n: (0, b, 0)),
            pl.BlockSpec((A_KV_HEADS, seq, A_HEAD_DIM), lambda b, n: (0, b, 0)),
            pl.BlockSpec((A_KV_HEADS, N_META, A_HEAD_DIM), lambda b, n: (0, b, 0)),
            pl.BlockSpec((A_KV_HEADS, N_META, A_HEAD_DIM), lambda b, n: (0, b, 0)),
            pl.BlockSpec((1, A_HEADS, BLOCK, 3 * BLOCK), lambda b, n: (variant(n), 0, 0, 0)),
            pl.BlockSpec((1, A_HEADS, BLOCK, N_META), lambda b, n: (n, 0, 0, 0)),
        ],
        out_specs=pl.BlockSpec((BLOCK, A_WIDTH), lambda b, n: (b * nblk + n, 0)),
        out_shape=jax.ShapeDtypeStruct((batch * seq, A_WIDTH), bf16),
        compiler_params=_cparams("arbitrary", "arbitrary"),
        name="win",
    )(sink, qa, ka, va, kam, vam, bias_band, bias_meta)


def _win_meta_kernel(sink_ref, q_ref, ka_ref, va_ref, kam_ref, vam_ref, biasm_ref, biasr_ref, o_ref):
    for kvh in range(A_KV_HEADS):
        q4 = q_ref[A_GROUP * kvh:A_GROUP * (kvh + 1)].reshape(A_GROUP * N_META, A_HEAD_DIM)
        sm = lax.dot_general(q4, kam_ref[kvh], _NT, preferred_element_type=f32)
        sr = lax.dot_general(q4, ka_ref[kvh], _NT, preferred_element_type=f32)
        pms, prs, rs = [], [], []
        for g in range(A_GROUP):
            h = A_GROUP * kvh + g
            rows = slice(g * N_META, (g + 1) * N_META)
            (em, er), r = _softmax_sink_parts([sm[rows] + biasm_ref[h], sr[rows] + biasr_ref[h]], sink_ref[h])
            pms.append(em.astype(bf16))
            prs.append(er.astype(bf16))
            rs.append(r)
        o = (jnp.dot(jnp.concatenate(pms, axis=0), vam_ref[kvh], preferred_element_type=f32)
             + jnp.dot(jnp.concatenate(prs, axis=0), va_ref[kvh], preferred_element_type=f32))
        for g in range(A_GROUP):
            h = A_GROUP * kvh + g
            o_ref[:, h * A_HEAD_DIM:(h + 1) * A_HEAD_DIM] = (o[g * N_META:(g + 1) * N_META] * rs[g]).astype(o_ref.dtype)


def _win_meta(sink, qam, ka, va, kam, vam, bias_mm, bias_mr, batch, seq):
    nblk = seq // BLOCK
    const3 = lambda shape: pl.BlockSpec(shape, lambda b: (0, 0, 0))
    return pl.pallas_call(
        _win_meta_kernel,
        grid=(batch,),
        in_specs=[
            pl.BlockSpec(memory_space=pltpu.SMEM),
            pl.BlockSpec((A_HEADS, N_META, A_HEAD_DIM), lambda b: (0, b, 0)),
            pl.BlockSpec((A_KV_HEADS, BLOCK, A_HEAD_DIM), lambda b: (0, b * nblk, 0)),
            pl.BlockSpec((A_KV_HEADS, BLOCK, A_HEAD_DIM), lambda b: (0, b * nblk, 0)),
            pl.BlockSpec((A_KV_HEADS, N_META, A_HEAD_DIM), lambda b: (0, b, 0)),
            pl.BlockSpec((A_KV_HEADS, N_META, A_HEAD_DIM), lambda b: (0, b, 0)),
            const3((A_HEADS, N_META, N_META)),
            const3((A_HEADS, N_META, BLOCK)),
        ],
        out_specs=pl.BlockSpec((N_META, A_WIDTH), lambda b: (b, 0)),
        out_shape=jax.ShapeDtypeStruct((batch * N_META, A_WIDTH), bf16),
        compiler_params=_cparams("arbitrary"),
        name="win_meta",
    )(sink, qam, ka, va, kam, vam, bias_mm, bias_mr)


def _mla_kernel(q_ref, k_ref, km_ref, vt_ref, vtm_ref, o_ref, acc_ref, s0_ref, s1_ref, sm0_ref, sm1_ref):
    bufs = ((s0_ref, sm0_ref), (s1_ref, sm1_ref))

    def scores(h, slot):
        q = q_ref[h]
        s = lax.dot_general(k_ref[h], q, _NT, preferred_element_type=f32)
        sm = lax.dot_general(km_ref[h], q, _NT, preferred_element_type=f32)
        bufs[slot][0][...] = s
        bufs[slot][1][...] = sm
        return jnp.maximum(jnp.max(s, axis=0, keepdims=True), jnp.max(sm, axis=0, keepdims=True))

    def attend(h, slot, m):
        p = jnp.exp2(bufs[slot][0][...] - m)
        pm = jnp.exp2(bufs[slot][1][...] - m)
        den = jnp.sum(p, axis=0, keepdims=True) + jnp.sum(pm, axis=0, keepdims=True)
        acc = (jnp.dot(vt_ref[h], p.astype(bf16), preferred_element_type=f32)
               + jnp.dot(vtm_ref[0, h], pm.astype(bf16), preferred_element_type=f32))
        acc_ref[h] = acc * (1.0 / den)

    def pair(j, m_even):
        m_odd = scores(2 * j + 1, 1)
        attend(2 * j, 0, m_even)
        m_even_next = scores(2 * j + 2, 0)
        attend(2 * j + 1, 1, m_odd)
        return m_even_next

    m_even = lax.fori_loop(0, B_HEADS // 2 - 1, pair, scores(0, 0))
    m_odd = scores(B_HEADS - 1, 1)
    attend(B_HEADS - 2, 0, m_even)
    attend(B_HEADS - 1, 1, m_odd)
    o_ref[...] = acc_ref[...].reshape(B_WIDTH, acc_ref.shape[-1]).T.astype(o_ref.dtype)


def _mla(q, k, km, vt, vtm, batch, seq):
    tq = MLA_Q_TILE
    nq = seq // tq
    return pl.pallas_call(
        _mla_kernel,
        grid=(batch, nq),
        in_specs=[
            pl.BlockSpec((B_HEADS, tq, B_QK_PAD), lambda b, i: (0, b * nq + i, 0)),
            pl.BlockSpec((B_HEADS, seq, B_QK_PAD), lambda b, i: (0, b, 0)),
            pl.BlockSpec((B_HEADS, N_META, B_QK_PAD), lambda b, i: (0, b, 0)),
            pl.BlockSpec((B_HEADS, B_V_DIM, seq), lambda b, i: (0, 0, b)),
            pl.BlockSpec((1, B_HEADS, B_V_DIM, N_META), lambda b, i: (b, 0, 0, 0)),
        ],
        out_specs=pl.BlockSpec((tq, B_WIDTH), lambda b, i: (b * nq + i, 0)),
        out_shape=jax.ShapeDtypeStruct((batch * seq, B_WIDTH), bf16),
        scratch_shapes=[pltpu.VMEM((B_HEADS, B_V_DIM, tq), f32),
                        pltpu.VMEM((seq, tq), f32), pltpu.VMEM((seq, tq), f32),
                        pltpu.VMEM((N_META, tq), f32), pltpu.VMEM((N_META, tq), f32)],
        compiler_params=_cparams("arbitrary", "arbitrary"),
        name="mla",
    )(q, k, km, vt, vtm)


def _mla_meta_kernel(q_ref, k_ref, km_ref, vt_ref, vtm_ref, o_ref):
    for h in range(B_HEADS):
        q = q_ref[h]
        s = lax.dot_general(q, k_ref[h], _NT, preferred_element_type=f32)
        sm = lax.dot_general(q, km_ref[h], _NT, preferred_element_type=f32)
        m = jnp.maximum(jnp.max(s, axis=-1, keepdims=True), jnp.max(sm, axis=-1, keepdims=True))
        p = jnp.exp2(s - m)
        pm = jnp.exp2(sm - m)
        den = jnp.sum(p, axis=-1, keepdims=True) + jnp.sum(pm, axis=-1, keepdims=True)
        o = (lax.dot_general(p.astype(bf16), vt_ref[h], _NT, preferred_element_type=f32)
             + lax.dot_general(pm.astype(bf16), vtm_ref[0, h], _NT, preferred_element_type=f32))
        o_ref[:, h * B_V_DIM:(h + 1) * B_V_DIM] = (o * (1.0 / den)).astype(o_ref.dtype)


def _mla_meta(qm, k, km, vt, vtm, batch, seq):
    return pl.pallas_call(
        _mla_meta_kernel,
        grid=(batch,),
        in_specs=[
            pl.BlockSpec((B_HEADS, N_META, B_QK_PAD), lambda b: (0, b, 0)),
            pl.BlockSpec((B_HEADS, seq, B_QK_PAD), lambda b: (0, b, 0)),
            pl.BlockSpec((B_HEADS, N_META, B_QK_PAD), lambda b: (0, b, 0)),
            pl.BlockSpec((B_HEADS, B_V_DIM, seq), lambda b: (0, 0, b)),
            pl.BlockSpec((1, B_HEADS, B_V_DIM, N_META), lambda b: (b, 0, 0, 0)),
        ],
        out_specs=pl.BlockSpec((N_META, B_WIDTH), lambda b: (b, 0)),
        out_shape=jax.ShapeDtypeStruct((batch * N_META, B_WIDTH), bf16),
        compiler_params=_cparams("arbitrary"),
        name="mla_meta",
    )(qm, k, km, vt, vtm)


def _rope_tables(pos):
    half = B_ROPE_DIM // 2
    freqs = ROPE_THETA ** (-jnp.arange(half, dtype=jnp.float32) / half)
    ang = pos.astype(jnp.float32)[:, None] * freqs[None, :]
    cos, sin = jnp.cos(ang), jnp.sin(ang)
    rows = pos.shape[0]
    c_mla = (B_NOPE_DIM + B_ROPE_DIM) ** -0.5 * LOG2E
    z = lambda w: jnp.zeros((rows, w), f32)
    tail = B_QK_PAD - B_NOPE_DIM - B_ROPE_DIM
    cos_q = jnp.concatenate([jnp.full((rows, B_NOPE_DIM), c_mla, f32), c_mla * cos, c_mla * cos, z(tail)], axis=1)
    sin_q = jnp.concatenate([z(B_NOPE_DIM), c_mla * sin, c_mla * sin, z(tail)], axis=1)
    cos_k = jnp.concatenate([z(B_NOPE_DIM), cos, cos, z(tail)], axis=1)
    sin_k = jnp.concatenate([z(B_NOPE_DIM), sin, sin, z(tail)], axis=1)
    return jnp.concatenate([cos_q, sin_q, cos_k, sin_k], axis=1)


def _rot_cols(w):
    half = w.shape[-1] // 2
    return jnp.concatenate([-w[..., half:], w[..., :half]], axis=-1)


def _layer_weights(norm_in, w_in, norm_q_lat, w_uq, norm_kv_lat, w_ukv, norm_out_a, norm_out_b, w_out):
    d = w_in.shape[0]
    offs = [0, A_WIDTH, A_WIDTH + 128, A_WIDTH + 256, 2 * A_WIDTH + 256,
            2 * A_WIDTH + 256 + Q_LORA_RANK, 2 * A_WIDTH + 256 + Q_LORA_RANK + KV_LORA_RANK,
            2 * A_WIDTH + 256 + Q_LORA_RANK + KV_LORA_RANK + B_ROPE_DIM]
    qa, ka, va, ga, cq, ckv, kr = (w_in[:, offs[i]:offs[i + 1]] for i in range(7))
    gb = w_in[:, offs[7]:]
    tail = B_QK_PAD - B_NOPE_DIM - B_ROPE_DIM
    pad_k = lambda w: jnp.concatenate([jnp.zeros((d, B_NOPE_DIM), w.dtype), w, jnp.zeros((d, tail), w.dtype)], axis=1)
    w1 = jnp.concatenate([qa, ka, va, ga, cq, ckv, pad_k(kr), pad_k(_rot_cols(kr)), gb], axis=1).astype(bf16)

    r = w_uq.shape[0]
    uq = w_uq.reshape(r, B_HEADS, B_NOPE_DIM + B_ROPE_DIM)
    nope, rope = uq[..., :B_NOPE_DIM], uq[..., B_NOPE_DIM:]
    wqa = jnp.concatenate([nope, rope, jnp.zeros((r, B_HEADS, tail), uq.dtype)], axis=-1)
    wqb = jnp.concatenate([jnp.zeros_like(nope), _rot_cols(rope), jnp.zeros((r, B_HEADS, tail), uq.dtype)], axis=-1)

    rk = w_ukv.shape[0]
    ukv = w_ukv.reshape(rk, B_HEADS, B_NOPE_DIM + B_V_DIM)
    k_nope, v = ukv[..., :B_NOPE_DIM], ukv[..., B_NOPE_DIM:]
    wk = jnp.concatenate([k_nope, jnp.zeros((rk, B_HEADS, B_QK_PAD - B_NOPE_DIM), ukv.dtype)], axis=-1)
    return {
        "gin": norm_in.astype(f32)[None, :],
        "w1": w1,
        "gq": norm_q_lat.astype(f32)[None, :],
        "wqa": wqa.reshape(r, B_HEADS * B_QK_PAD).astype(bf16),
        "wqb": wqb.reshape(r, B_HEADS * B_QK_PAD).astype(bf16),
        "gkv": norm_kv_lat.astype(f32)[None, :],
        "wk": wk.reshape(rk, B_HEADS * B_QK_PAD).astype(bf16),
        "wvt": v.reshape(rk, B_WIDTH).T.astype(bf16),
        "na": norm_out_a.astype(f32)[None, :],
        "nb": norm_out_b.astype(f32)[None, :],
        "wout": w_out.astype(bf16),
    }


def kernel(x, meta_tokens, rel_bias_table, norm_in, w_in, sink_a, norm_q_lat, w_uq, norm_kv_lat, w_ukv,
           norm_out_a, norm_out_b, w_out, norm_final):
    batch, seq, d = x.shape
    depth = w_in.shape[0]
    assert d == D_MODEL and seq % MLA_Q_TILE == 0 and (batch * N_META) % ROW_TILE == 0

    h_real = x.reshape(batch * seq, d).astype(f32)
    h_meta = jnp.tile(meta_tokens.astype(f32), (batch, 1))
    tab_real = _rope_tables(N_META + jnp.arange(seq))
    tab_meta = _rope_tables(jnp.arange(batch * N_META) % N_META)
    bias_band, bias_meta, bias_mm, bias_mr = _bias_tables(rel_bias_table, seq // BLOCK)

    out = None
    for i in range(depth):
        last = i == depth - 1
        lw = _layer_weights(norm_in[i], w_in[i], norm_q_lat[i], w_uq[i], norm_kv_lat[i], w_ukv[i],
                            norm_out_a[i], norm_out_b[i], w_out[i])
        sink = sink_a[i].astype(f32) * LOG2E
        qa, ka, va, g, q, k, vt = _proj(h_real, tab_real, seq // ROW_TILE, lw)
        qam, kam, vam, gm, qm, km, vtm = _proj(h_meta, tab_meta, 1, lw)
        vtm_b = vtm.reshape(B_HEADS, B_V_DIM, batch, N_META).transpose(2, 0, 1, 3)

        ya = _win(sink, qa, ka, va, kam, vam, bias_band, bias_meta, batch, seq)
        yb = _mla(q, k, km, vt, vtm_b, batch, seq)
        if last:
            out = _out(h_real, ya, yb, g, lw, norm_final.astype(f32)[None, :])
        else:
            yam = _win_meta(sink, qam, ka, va, kam, vam, bias_mm, bias_mr, batch, seq)
            ybm = _mla_meta(qm, k, km, vt, vtm_b, batch, seq)
            h_real = _out(h_real, ya, yb, g, lw)
            h_meta = _out(h_meta, yam, ybm, gm, lw)
    return out.reshape(batch, seq, d).astype(x.dtype)
```

```python
import functools
import math

import jax
import jax.numpy as jnp
from jax import lax
from jax.experimental import pallas as pl
from jax.experimental.pallas import tpu as pltpu

D_MODEL = 1024
N_META = 16
BLOCK = 128
WINDOW = 128
A_HEADS = 8
A_KV_HEADS = 2
A_GROUP = A_HEADS // A_KV_HEADS
A_HEAD_DIM = 64
A_WIDTH = A_HEADS * A_HEAD_DIM
B_HEADS = 8
B_NOPE_DIM = 64
B_ROPE_DIM = 32
B_V_DIM = 64
B_WIDTH = B_HEADS * B_V_DIM
B_QK_PAD = 128
BF16_SUBLANES = 16
B_V_AUG = B_V_DIM + BF16_SUBLANES
A_V_AUG = A_HEAD_DIM + BF16_SUBLANES
Q_LORA_RANK = 256
KV_LORA_RANK = 128
N_BUCKETS = 32
MAX_DISTANCE = 128
ROPE_THETA = 10000.0
EPS = 1e-6

LOG2E = math.log2(math.e)
NEG = -1e30
ROW_TILE = 256
MLA_Q_TILE = 256
MLA_SUBTILES = 4
WIN_BLOCKS = 4
VMEM_LIMIT_BYTES = 56 * 1024 * 1024

_C_QA = 0
_C_KA = _C_QA + A_WIDTH
_C_GA = _C_KA + A_KV_HEADS * A_HEAD_DIM
_C_CQ = _C_GA + A_WIDTH
_C_CKV = _C_CQ + Q_LORA_RANK
_C_KRA = _C_CKV + KV_LORA_RANK
_C_KRB = _C_KRA + B_QK_PAD
_C_GB = _C_KRB + B_QK_PAD
_C_END = _C_GB + B_WIDTH

_NT = (((1,), (1,)), ((), ()))

bf16 = jnp.bfloat16
f32 = jnp.float32


def _cparams(*sem):
    return pltpu.CompilerParams(dimension_semantics=sem, vmem_limit_bytes=VMEM_LIMIT_BYTES)


def _rms(x, gain):
    return x * lax.rsqrt(jnp.mean(x * x, axis=-1, keepdims=True) + EPS) * gain


def _silu(x):
    return x / (1.0 + jnp.exp(-x))


def _proj_kernel(h_ref, tab_ref, gin_ref, w1_ref, wvat_ref, gq_ref, wqa_ref, wqb_ref, gkv_ref, wk_ref, wvt_ref,
                 qa_ref, ka_ref, vat_ref, g_ref, q_ref, k_ref, vt_ref):
    u = _rms(h_ref[...], gin_ref[...]).astype(bf16)
    t = u.shape[0]

    def mm(lo, hi):
        return jnp.dot(u, w1_ref[:, lo:hi], preferred_element_type=f32)

    qa = mm(_C_QA, _C_KA) * (A_HEAD_DIM ** -0.5 * LOG2E)
    for h in range(A_HEADS):
        qa_ref[h] = qa[:, h * A_HEAD_DIM:(h + 1) * A_HEAD_DIM].astype(bf16)
    ka = mm(_C_KA, _C_GA)
    for j in range(A_KV_HEADS):
        ka_ref[j] = ka[:, j * A_HEAD_DIM:(j + 1) * A_HEAD_DIM].astype(bf16)
    vat = lax.dot_general(wvat_ref[...], u, _NT, preferred_element_type=f32)
    vat_ref[:, :A_HEAD_DIM, :] = vat.reshape(A_KV_HEADS, A_HEAD_DIM, t).astype(bf16)
    vat_ref[:, A_HEAD_DIM:, :] = jnp.ones((A_KV_HEADS, A_V_AUG - A_HEAD_DIM, t), bf16)
    g_ref[:, :A_WIDTH] = _silu(mm(_C_GA, _C_CQ)).astype(bf16)
    g_ref[:, A_WIDTH:] = _silu(mm(_C_GB, _C_END)).astype(bf16)

    lat = mm(_C_CQ, _C_GB)
    cq = lat[:, :Q_LORA_RANK]
    ckv = lat[:, Q_LORA_RANK:Q_LORA_RANK + KV_LORA_RANK]
    kra = lat[:, _C_KRA - _C_CQ:_C_KRB - _C_CQ]
    krb = lat[:, _C_KRB - _C_CQ:_C_GB - _C_CQ]
    cos_q = tab_ref[:, 0 * B_QK_PAD:1 * B_QK_PAD]
    sin_q = tab_ref[:, 1 * B_QK_PAD:2 * B_QK_PAD]
    cos_k = tab_ref[:, 2 * B_QK_PAD:3 * B_QK_PAD]
    sin_k = tab_ref[:, 3 * B_QK_PAD:4 * B_QK_PAD]

    cqn = _rms(cq, gq_ref[...]).astype(bf16)
    qa_part = jnp.dot(cqn, wqa_ref[...], preferred_element_type=f32)
    qb_part = jnp.dot(cqn, wqb_ref[...], preferred_element_type=f32)
    for h in range(B_HEADS):
        sl = slice(h * B_QK_PAD, (h + 1) * B_QK_PAD)
        q_ref[h] = (qa_part[:, sl] * cos_q + qb_part[:, sl] * sin_q).astype(bf16)

    ckvn = _rms(ckv, gkv_ref[...]).astype(bf16)
    k_nope = jnp.dot(ckvn, wk_ref[...], preferred_element_type=f32)
    k_rope = kra * cos_k + krb * sin_k
    for h in range(B_HEADS):
        k_ref[h] = (k_nope[:, h * B_QK_PAD:(h + 1) * B_QK_PAD] + k_rope).astype(bf16)
    vt = lax.dot_general(wvt_ref[...], ckvn, _NT, preferred_element_type=f32)
    vt_ref[:, :B_V_DIM, :] = vt.reshape(B_HEADS, B_V_DIM, t).astype(bf16)
    vt_ref[:, B_V_DIM:, :] = jnp.ones((B_HEADS, B_V_AUG - B_V_DIM, t), bf16)


def _proj(h, tab, tab_blocks, lw):
    rows = h.shape[0]
    t = ROW_TILE
    steps = rows // t
    const = lambda shape: pl.BlockSpec(shape, lambda i: (0,) * len(shape))
    return pl.pallas_call(
        _proj_kernel,
        grid=(steps,),
        in_specs=[
            pl.BlockSpec((t, D_MODEL), lambda i: (i, 0)),
            pl.BlockSpec((t, 4 * B_QK_PAD), lambda i: (i % tab_blocks, 0)),
            const((1, D_MODEL)),
            const((D_MODEL, _C_END)),
            const((A_KV_HEADS * A_HEAD_DIM, D_MODEL)),
            const((1, Q_LORA_RANK)),
            const((Q_LORA_RANK, B_HEADS * B_QK_PAD)),
            const((Q_LORA_RANK, B_HEADS * B_QK_PAD)),
            const((1, KV_LORA_RANK)),
            const((KV_LORA_RANK, B_HEADS * B_QK_PAD)),
            const((B_WIDTH, KV_LORA_RANK)),
        ],
        out_specs=[
            pl.BlockSpec((A_HEADS, t, A_HEAD_DIM), lambda i: (0, i, 0)),
            pl.BlockSpec((A_KV_HEADS, t, A_HEAD_DIM), lambda i: (0, i, 0)),
            pl.BlockSpec((A_KV_HEADS, A_V_AUG, t), lambda i: (0, 0, i)),
            pl.BlockSpec((t, A_WIDTH + B_WIDTH), lambda i: (i, 0)),
            pl.BlockSpec((B_HEADS, t, B_QK_PAD), lambda i: (0, i, 0)),
            pl.BlockSpec((B_HEADS, t, B_QK_PAD), lambda i: (0, i, 0)),
            pl.BlockSpec((B_HEADS, B_V_AUG, t), lambda i: (0, 0, i)),
        ],
        out_shape=[
            jax.ShapeDtypeStruct((A_HEADS, rows, A_HEAD_DIM), bf16),
            jax.ShapeDtypeStruct((A_KV_HEADS, rows, A_HEAD_DIM), bf16),
            jax.ShapeDtypeStruct((A_KV_HEADS, A_V_AUG, rows), bf16),
            jax.ShapeDtypeStruct((rows, A_WIDTH + B_WIDTH), bf16),
            jax.ShapeDtypeStruct((B_HEADS, rows, B_QK_PAD), bf16),
            jax.ShapeDtypeStruct((B_HEADS, rows, B_QK_PAD), bf16),
            jax.ShapeDtypeStruct((B_HEADS, B_V_AUG, rows), bf16),
        ],
        compiler_params=_cparams("arbitrary"),
        name="proj",
    )(h, tab, lw["gin"], lw["w1"], lw["wvat"], lw["gq"], lw["wqa"], lw["wqb"], lw["gkv"], lw["wk"], lw["wvt"])


def _out_kernel(h_ref, ya_ref, yb_ref, g_ref, na_ref, nb_ref, w_ref, *rest, final):
    if final:
        nf_ref, o_ref = rest
    else:
        (o_ref,) = rest
    ya = _rms(ya_ref[...].astype(f32), na_ref[...]) * g_ref[:, :A_WIDTH].astype(f32)
    yb = _rms(yb_ref[...].astype(f32), nb_ref[...]) * g_ref[:, A_WIDTH:].astype(f32)
    hn = h_ref[...]
    hn = hn + jnp.dot(ya.astype(bf16), w_ref[:A_WIDTH, :], preferred_element_type=f32)
    hn = hn + jnp.dot(yb.astype(bf16), w_ref[A_WIDTH:, :], preferred_element_type=f32)
    if final:
        hn = _rms(hn, nf_ref[...])
    o_ref[...] = hn


def _out(h, ya, yb, g, lw, norm_final=None):
    rows = h.shape[0]
    t = ROW_TILE
    row = lambda width: pl.BlockSpec((t, width), lambda i: (i, 0))
    const = lambda shape: pl.BlockSpec(shape, lambda i: (0,) * len(shape))
    final = norm_final is not None
    in_specs = [row(D_MODEL), row(A_WIDTH), row(B_WIDTH), row(A_WIDTH + B_WIDTH),
                const((1, A_WIDTH)), const((1, B_WIDTH)), const((A_WIDTH + B_WIDTH, D_MODEL))]
    args = [h, ya, yb, g, lw["na"], lw["nb"], lw["wout"]]
    if final:
        in_specs.append(const((1, D_MODEL)))
        args.append(norm_final)
    return pl.pallas_call(
        functools.partial(_out_kernel, final=final),
        grid=(rows // t,),
        in_specs=in_specs,
        out_specs=row(D_MODEL),
        out_shape=jax.ShapeDtypeStruct((rows, D_MODEL), f32),
        compiler_params=_cparams("arbitrary"),
        name="outproj",
    )(*args)


def _bias_kernel(tab_ref, idx_ref, o_ref):
    idx = idx_ref[...]
    for h in range(A_HEADS):
        acc = jnp.where(idx < 0, NEG, 0.0).astype(f32)
        for b in range(N_BUCKETS):
            acc = jnp.where(idx == b, tab_ref[b * A_HEADS + h], acc)
        o_ref[h] = acc


def _bias_lookup(table_flat, idx):
    return pl.pallas_call(
        _bias_kernel,
        in_specs=[pl.BlockSpec(memory_space=pltpu.SMEM), pl.BlockSpec(idx.shape, lambda: (0, 0))],
        out_specs=pl.BlockSpec((A_HEADS,) + idx.shape, lambda: (0, 0, 0)),
        out_shape=jax.ShapeDtypeStruct((A_HEADS,) + idx.shape, f32),
        compiler_params=pltpu.CompilerParams(vmem_limit_bytes=VMEM_LIMIT_BYTES),
        name="relbias",
    )(table_flat, idx)


def _t5_bucket(rel):
    nb = N_BUCKETS // 2
    max_exact = nb // 2
    ret = jnp.where(rel > 0, nb, 0)
    n = jnp.abs(rel)
    nf = jnp.maximum(n, 1).astype(jnp.float32)
    large = max_exact + (jnp.log(nf / max_exact) / math.log(MAX_DISTANCE / max_exact)
                         * (nb - max_exact)).astype(jnp.int32)
    large = jnp.minimum(large, nb - 1)
    return ret + jnp.where(n < max_exact, n, large)


def _bias_tables(rel_bias_table, nblk):
    table_flat = (rel_bias_table.astype(f32) * LOG2E).reshape(-1)
    i = jnp.arange(BLOCK, dtype=jnp.int32)[:, None]
    j = jnp.arange(3 * BLOCK, dtype=jnp.int32)[None, :]
    rel = j - i - BLOCK
    band = jnp.where(jnp.abs(rel) <= WINDOW, _t5_bucket(rel), -1)
    first = jnp.where(j < BLOCK, -1, band)
    last = jnp.where(j >= 2 * BLOCK, -1, band)
    idx_band = jnp.concatenate([first.T, band.T, last.T], axis=0)
    bias_band = (_bias_lookup(table_flat, idx_band)
                 .reshape(A_KV_HEADS, A_GROUP, 3, 3 * BLOCK, BLOCK)
                 .transpose(2, 0, 3, 1, 4).reshape(3, A_KV_HEADS, 3 * BLOCK, A_GROUP * BLOCK))

    n = jnp.arange(nblk, dtype=jnp.int32)[:, None, None]
    k = jnp.arange(N_META, dtype=jnp.int32)[None, :, None]
    qi = jnp.arange(BLOCK, dtype=jnp.int32)[None, None, :]
    rel_m = k - (N_META + n * BLOCK + qi)
    idx_m = _t5_bucket(rel_m).reshape(nblk * N_META, BLOCK)
    bias_meta = (_bias_lookup(table_flat, idx_m)
                 .reshape(A_KV_HEADS, A_GROUP, nblk, N_META, BLOCK)
                 .transpose(2, 0, 3, 1, 4).reshape(nblk, A_KV_HEADS, N_META, A_GROUP * BLOCK))

    qp = jnp.arange(N_META, dtype=jnp.int32)[:, None]
    kp = jnp.arange(N_META + BLOCK, dtype=jnp.int32)[None, :]
    rel_q = kp - qp
    idx_q = jnp.where(jnp.abs(rel_q) <= WINDOW, _t5_bucket(rel_q), -1)
    bias_q = _bias_lookup(table_flat, idx_q)
    return bias_band, bias_meta, bias_q[:, :, :N_META], bias_q[:, :, N_META:]


def _win_kernel(sink_ref, q_ref, kp_ref, kc_ref, kn_ref, vp_ref, vc_ref, vn_ref, kam_ref, vam_ref,
                bias_ref, bm_ref, o_ref):
    w = WIN_BLOCKS
    i, ntile = pl.program_id(1), pl.num_programs(1)

    def qrows(blk):
        return slice(blk * BLOCK, (blk + 1) * BLOCK)

    def band(blk, prev_ref, cur_ref, next_ref, kvh, axis):
        take = lambda ref, lo, hi: ref[kvh, lo:hi, :] if axis == 0 else ref[kvh, :, lo:hi]
        parts = [take(cur_ref, max(blk - 1, 0) * BLOCK, min(blk + 2, w) * BLOCK)]
        if blk == 0:
            parts.insert(0, take(prev_ref, (w - 1) * BLOCK, w * BLOCK))
        if blk == w - 1:
            parts.append(take(next_ref, 0, BLOCK))
        return jnp.concatenate(parts, axis=axis)

    def variant(blk):
        v = 1
        if blk == 0:
            v = jnp.where(i == 0, 0, v)
        if blk == w - 1:
            v = jnp.where(i == ntile - 1, 2, v)
        return v

    def scores(blk, kvh):
        kb = band(blk, kp_ref, kc_ref, kn_ref, kvh, 0)
        q4 = q_ref[A_GROUP * kvh:A_GROUP * (kvh + 1), qrows(blk), :].reshape(A_GROUP * BLOCK, A_HEAD_DIM)
        s = lax.dot_general(kb, q4, _NT, preferred_element_type=f32) + bias_ref[variant(blk), kvh]
        sm = lax.dot_general(kam_ref[kvh], q4, _NT, preferred_element_type=f32) + bm_ref[blk, kvh]
        sink = sink_ref[kvh]
        m = jnp.maximum(jnp.maximum(jnp.max(s, axis=0, keepdims=True), jnp.max(sm, axis=0, keepdims=True)), sink)
        return s, sm, m

    def attend(blk, kvh, s, sm, m):
        vb = band(blk, vp_ref, vc_ref, vn_ref, kvh, 1)
        p = jnp.exp2(s - m).astype(bf16)
        pm = jnp.exp2(sm - m).astype(bf16)
        acc = (jnp.dot(vb, p, preferred_element_type=f32)
               + jnp.dot(vam_ref[0, kvh], pm, preferred_element_type=f32))
        den = acc[A_HEAD_DIM:A_HEAD_DIM + 1] + jnp.exp2(sink_ref[kvh] - m)
        return acc[:A_HEAD_DIM] * (1.0 / den)

    units = [(blk, kvh) for blk in range(w) for kvh in range(A_KV_HEADS)]
    outs = {}
    pending = scores(*units[0])
    for u, unit in enumerate(units):
        nxt = scores(*units[u + 1]) if u + 1 < len(units) else None
        outs[unit] = attend(*unit, *pending)
        pending = nxt
        blk, kvh = unit
        if kvh == A_KV_HEADS - 1:
            ot = jnp.concatenate([outs[(blk, j)][:, g * BLOCK:(g + 1) * BLOCK]
                                  for j in range(A_KV_HEADS) for g in range(A_GROUP)], axis=0)
            o_ref[qrows(blk), :] = ot.T.astype(o_ref.dtype)


def _win(sink_rows, qa, ka, vat, kam, vatm, bias_band, bias_meta, batch, seq):
    tb = WIN_BLOCKS * BLOCK
    ntile = seq // tb
    prev_ = lambda b, i: b * ntile + jnp.maximum(i - 1, 0)
    cur_ = lambda b, i: b * ntile + i
    next_ = lambda b, i: b * ntile + jnp.minimum(i + 1, ntile - 1)
    kspec = lambda f: pl.BlockSpec((A_KV_HEADS, tb, A_HEAD_DIM), lambda b, i: (0, f(b, i), 0))
    vspec = lambda f: pl.BlockSpec((A_KV_HEADS, A_V_AUG, tb), lambda b, i: (0, 0, f(b, i)))
    return pl.pallas_call(
        _win_kernel,
        grid=(batch, ntile),
        in_specs=[
            pl.BlockSpec((A_KV_HEADS, 1, A_GROUP * BLOCK), lambda b, i: (0, 0, 0)),
            pl.BlockSpec((A_HEADS, tb, A_HEAD_DIM), lambda b, i: (0, cur_(b, i), 0)),
            kspec(prev_), kspec(cur_), kspec(next_),
            vspec(prev_), vspec(cur_), vspec(next_),
            pl.BlockSpec((A_KV_HEADS, N_META, A_HEAD_DIM), lambda b, i: (0, b, 0)),
            pl.BlockSpec((1, A_KV_HEADS, A_V_AUG, N_META), lambda b, i: (b, 0, 0, 0)),
            pl.BlockSpec((3, A_KV_HEADS, 3 * BLOCK, A_GROUP * BLOCK), lambda b, i: (0, 0, 0, 0)),
            pl.BlockSpec((WIN_BLOCKS, A_KV_HEADS, N_META, A_GROUP * BLOCK), lambda b, i: (i, 0, 0, 0)),
        ],
        out_specs=pl.BlockSpec((tb, A_WIDTH), lambda b, i: (cur_(b, i), 0)),
        out_shape=jax.ShapeDtypeStruct((batch * seq, A_WIDTH), bf16),
        compiler_params=_cparams("arbitrary", "arbitrary"),
        name="win",
    )(sink_rows, qa, ka, ka, ka, vat, vat, vat, kam, vatm, bias_band, bias_meta)


def _win_meta_kernel(sink_ref, q_ref, ka_ref, vat_ref, kam_ref, vatm_ref, biasm_ref, biasr_ref, o_ref):
    for kvh in range(A_KV_HEADS):
        q4 = q_ref[A_GROUP * kvh:A_GROUP * (kvh + 1)].reshape(A_GROUP * N_META, A_HEAD_DIM)
        sm = lax.dot_general(q4, kam_ref[kvh], _NT, preferred_element_type=f32)
        sr = lax.dot_general(q4, ka_ref[kvh], _NT, preferred_element_type=f32)
        for g in range(A_GROUP):
            h = A_GROUP * kvh + g
            rows = slice(g * N_META, (g + 1) * N_META)
            smg = sm[rows] + biasm_ref[h]
            srg = sr[rows] + biasr_ref[h]
            sink = sink_ref[h]
            m = jnp.maximum(jnp.maximum(jnp.max(smg, axis=-1, keepdims=True),
                                        jnp.max(srg, axis=-1, keepdims=True)), sink)
            pm = jnp.exp2(smg - m).astype(bf16)
            pr = jnp.exp2(srg - m).astype(bf16)
            o = (lax.dot_general(pm, vatm_ref[0, kvh], _NT, preferred_element_type=f32)
                 + lax.dot_general(pr, vat_ref[kvh], _NT, preferred_element_type=f32))
            den = o[:, A_HEAD_DIM:A_HEAD_DIM + 1] + jnp.exp2(sink - m)
            o_ref[:, h * A_HEAD_DIM:(h + 1) * A_HEAD_DIM] = (o[:, :A_HEAD_DIM] * (1.0 / den)).astype(o_ref.dtype)


def _win_meta(sink, qam, ka, vat, kam, vatm, bias_mm, bias_mr, batch, seq):
    nblk = seq // BLOCK
    const3 = lambda shape: pl.BlockSpec(shape, lambda b: (0, 0, 0))
    return pl.pallas_call(
        _win_meta_kernel,
        grid=(batch,),
        in_specs=[
            pl.BlockSpec(memory_space=pltpu.SMEM),
            pl.BlockSpec((A_HEADS, N_META, A_HEAD_DIM), lambda b: (0, b, 0)),
            pl.BlockSpec((A_KV_HEADS, BLOCK, A_HEAD_DIM), lambda b: (0, b * nblk, 0)),
            pl.BlockSpec((A_KV_HEADS, A_V_AUG, BLOCK), lambda b: (0, 0, b * nblk)),
            pl.BlockSpec((A_KV_HEADS, N_META, A_HEAD_DIM), lambda b: (0, b, 0)),
            pl.BlockSpec((1, A_KV_HEADS, A_V_AUG, N_META), lambda b: (b, 0, 0, 0)),
            const3((A_HEADS, N_META, N_META)),
            const3((A_HEADS, N_META, BLOCK)),
        ],
        out_specs=pl.BlockSpec((N_META, A_WIDTH), lambda b: (b, 0)),
        out_shape=jax.ShapeDtypeStruct((batch * N_META, A_WIDTH), bf16),
        compiler_params=_cparams("arbitrary"),
        name="win_meta",
    )(sink, qam, ka, vat, kam, vatm, bias_mm, bias_mr)


def _mla_kernel(q_ref, k_ref, km_ref, vt_ref, vtm_ref, o_ref, acc_ref, s0_ref, s1_ref, sm0_ref, sm1_ref):
    bufs = ((s0_ref, sm0_ref), (s1_ref, sm1_ref))
    nsub, _, _, tq = acc_ref.shape
    units = nsub * B_HEADS

    def scores(u, slot):
        h, sub = u % B_HEADS, u // B_HEADS
        q = q_ref[h, pl.ds(pl.multiple_of(sub * tq, tq), tq), :]
        s = lax.dot_general(k_ref[h], q, _NT, preferred_element_type=f32)
        sm = lax.dot_general(km_ref[h], q, _NT, preferred_element_type=f32)
        bufs[slot][0][...] = s
        bufs[slot][1][...] = sm
        return jnp.maximum(jnp.max(s, axis=0, keepdims=True), jnp.max(sm, axis=0, keepdims=True))

    def attend(u, slot, m):
        h, sub = u % B_HEADS, u // B_HEADS
        p = jnp.exp2(bufs[slot][0][...] - m).astype(bf16)
        pm = jnp.exp2(bufs[slot][1][...] - m).astype(bf16)
        acc = (jnp.dot(vt_ref[h], p, preferred_element_type=f32)
               + jnp.dot(vtm_ref[0, h], pm, preferred_element_type=f32))
        acc_ref[sub, h] = acc[:B_V_DIM] * (1.0 / acc[B_V_DIM:B_V_DIM + 1])

    def pair(j, m_even):
        m_odd = scores(2 * j + 1, 1)
        attend(2 * j, 0, m_even)
        m_even_next = scores(2 * j + 2, 0)
        attend(2 * j + 1, 1, m_odd)
        return m_even_next

    m_even = lax.fori_loop(0, units // 2 - 1, pair, scores(0, 0))
    m_odd = scores(units - 1, 1)
    attend(units - 2, 0, m_even)
    attend(units - 1, 1, m_odd)
    for sub in range(nsub):
        o_ref[sub * tq:(sub + 1) * tq, :] = acc_ref[sub].reshape(B_WIDTH, tq).T.astype(o_ref.dtype)


def _mla(q, k, km, vt, vtm, batch, seq):
    tq = MLA_Q_TILE
    tb = MLA_Q_TILE * MLA_SUBTILES
    nq = seq // tb
    return pl.pallas_call(
        _mla_kernel,
        grid=(batch, nq),
        in_specs=[
            pl.BlockSpec((B_HEADS, tb, B_QK_PAD), lambda b, i: (0, b * nq + i, 0)),
            pl.BlockSpec((B_HEADS, seq, B_QK_PAD), lambda b, i: (0, b, 0)),
            pl.BlockSpec((B_HEADS, N_META, B_QK_PAD), lambda b, i: (0, b, 0)),
            pl.BlockSpec((B_HEADS, B_V_AUG, seq), lambda b, i: (0, 0, b)),
            pl.BlockSpec((1, B_HEADS, B_V_AUG, N_META), lambda b, i: (b, 0, 0, 0)),
        ],
        out_specs=pl.BlockSpec((tb, B_WIDTH), lambda b, i: (b * nq + i, 0)),
        out_shape=jax.ShapeDtypeStruct((batch * seq, B_WIDTH), bf16),
        scratch_shapes=[pltpu.VMEM((MLA_SUBTILES, B_HEADS, B_V_DIM, tq), f32),
                        pltpu.VMEM((seq, tq), f32), pltpu.VMEM((seq, tq), f32),
                        pltpu.VMEM((N_META, tq), f32), pltpu.VMEM((N_META, tq), f32)],
        compiler_params=_cparams("arbitrary", "arbitrary"),
        name="mla",
    )(q, k, km, vt, vtm)


def _mla_meta_kernel(q_ref, k_ref, km_ref, vt_ref, vtm_ref, o_ref):
    for h in range(B_HEADS):
        q = q_ref[h]
        s = lax.dot_general(q, k_ref[h], _NT, preferred_element_type=f32)
        sm = lax.dot_general(q, km_ref[h], _NT, preferred_element_type=f32)
        m = jnp.maximum(jnp.max(s, axis=-1, keepdims=True), jnp.max(sm, axis=-1, keepdims=True))
        p = jnp.exp2(s - m).astype(bf16)
        pm = jnp.exp2(sm - m).astype(bf16)
        o = (lax.dot_general(p, vt_ref[h], _NT, preferred_element_type=f32)
             + lax.dot_general(pm, vtm_ref[0, h], _NT, preferred_element_type=f32))
        o_ref[:, h * B_V_DIM:(h + 1) * B_V_DIM] = (
            o[:, :B_V_DIM] * (1.0 / o[:, B_V_DIM:B_V_DIM + 1])).astype(o_ref.dtype)


def _mla_meta(qm, k, km, vt, vtm, batch, seq):
    return pl.pallas_call(
        _mla_meta_kernel,
        grid=(batch,),
        in_specs=[
            pl.BlockSpec((B_HEADS, N_META, B_QK_PAD), lambda b: (0, b, 0)),
            pl.BlockSpec((B_HEADS, seq, B_QK_PAD), lambda b: (0, b, 0)),
            pl.BlockSpec((B_HEADS, N_META, B_QK_PAD), lambda b: (0, b, 0)),
            pl.BlockSpec((B_HEADS, B_V_AUG, seq), lambda b: (0, 0, b)),
            pl.BlockSpec((1, B_HEADS, B_V_AUG, N_META), lambda b: (b, 0, 0, 0)),
        ],
        out_specs=pl.BlockSpec((N_META, B_WIDTH), lambda b: (b, 0)),
        out_shape=jax.ShapeDtypeStruct((batch * N_META, B_WIDTH), bf16),
        compiler_params=_cparams("arbitrary"),
        name="mla_meta",
    )(qm, k, km, vt, vtm)


def _rope_tables(pos):
    half = B_ROPE_DIM // 2
    freqs = ROPE_THETA ** (-jnp.arange(half, dtype=jnp.float32) / half)
    ang = pos.astype(jnp.float32)[:, None] * freqs[None, :]
    cos, sin = jnp.cos(ang), jnp.sin(ang)
    rows = pos.shape[0]
    c_mla = (B_NOPE_DIM + B_ROPE_DIM) ** -0.5 * LOG2E
    z = lambda w: jnp.zeros((rows, w), f32)
    tail = B_QK_PAD - B_NOPE_DIM - B_ROPE_DIM
    cos_q = jnp.concatenate([jnp.full((rows, B_NOPE_DIM), c_mla, f32), c_mla * cos, c_mla * cos, z(tail)], axis=1)
    sin_q = jnp.concatenate([z(B_NOPE_DIM), c_mla * sin, c_mla * sin, z(tail)], axis=1)
    cos_k = jnp.concatenate([z(B_NOPE_DIM), cos, cos, z(tail)], axis=1)
    sin_k = jnp.concatenate([z(B_NOPE_DIM), sin, sin, z(tail)], axis=1)
    return jnp.concatenate([cos_q, sin_q, cos_k, sin_k], axis=1)


def _rot_cols(w):
    half = w.shape[-1] // 2
    return jnp.concatenate([-w[..., half:], w[..., :half]], axis=-1)


def _layer_weights(norm_in, w_in, norm_q_lat, w_uq, norm_kv_lat, w_ukv, norm_out_a, norm_out_b, w_out):
    d = w_in.shape[0]
    offs = [0, A_WIDTH, A_WIDTH + 128, A_WIDTH + 256, 2 * A_WIDTH + 256,
            2 * A_WIDTH + 256 + Q_LORA_RANK, 2 * A_WIDTH + 256 + Q_LORA_RANK + KV_LORA_RANK,
            2 * A_WIDTH + 256 + Q_LORA_RANK + KV_LORA_RANK + B_ROPE_DIM]
    qa, ka, va, ga, cq, ckv, kr = (w_in[:, offs[i]:offs[i + 1]] for i in range(7))
    gb = w_in[:, offs[7]:]
    tail = B_QK_PAD - B_NOPE_DIM - B_ROPE_DIM
    pad_k = lambda w: jnp.concatenate([jnp.zeros((d, B_NOPE_DIM), w.dtype), w, jnp.zeros((d, tail), w.dtype)], axis=1)
    w1 = jnp.concatenate([qa, ka, ga, cq, ckv, pad_k(kr), pad_k(_rot_cols(kr)), gb], axis=1).astype(bf16)

    r = w_uq.shape[0]
    uq = w_uq.reshape(r, B_HEADS, B_NOPE_DIM + B_ROPE_DIM)
    nope, rope = uq[..., :B_NOPE_DIM], uq[..., B_NOPE_DIM:]
    wqa = jnp.concatenate([nope, rope, jnp.zeros((r, B_HEADS, tail), uq.dtype)], axis=-1)
    wqb = jnp.concatenate([jnp.zeros_like(nope), _rot_cols(rope), jnp.zeros((r, B_HEADS, tail), uq.dtype)], axis=-1)

    rk = w_ukv.shape[0]
    ukv = w_ukv.reshape(rk, B_HEADS, B_NOPE_DIM + B_V_DIM)
    k_nope, v = ukv[..., :B_NOPE_DIM], ukv[..., B_NOPE_DIM:]
    wk = jnp.concatenate([k_nope, jnp.zeros((rk, B_HEADS, B_QK_PAD - B_NOPE_DIM), ukv.dtype)], axis=-1)
    return {
        "gin": norm_in.astype(f32)[None, :],
        "w1": w1,
        "wvat": va.T.astype(bf16),
        "gq": norm_q_lat.astype(f32)[None, :],
        "wqa": wqa.reshape(r, B_HEADS * B_QK_PAD).astype(bf16),
        "wqb": wqb.reshape(r, B_HEADS * B_QK_PAD).astype(bf16),
        "gkv": norm_kv_lat.astype(f32)[None, :],
        "wk": wk.reshape(rk, B_HEADS * B_QK_PAD).astype(bf16),
        "wvt": v.reshape(rk, B_WIDTH).T.astype(bf16),
        "na": norm_out_a.astype(f32)[None, :],
        "nb": norm_out_b.astype(f32)[None, :],
        "wout": w_out.astype(bf16),
    }


def kernel(x, meta_tokens, rel_bias_table, norm_in, w_in, sink_a, norm_q_lat, w_uq, norm_kv_lat, w_ukv,
           norm_out_a, norm_out_b, w_out, norm_final):
    batch, seq, d = x.shape
    depth = w_in.shape[0]
    assert d == D_MODEL and seq % (MLA_Q_TILE * MLA_SUBTILES) == 0 and (batch * N_META) % ROW_TILE == 0

    h_real = x.reshape(batch * seq, d).astype(f32)
    h_meta = jnp.tile(meta_tokens.astype(f32), (batch, 1))
    tab_real = _rope_tables(N_META + jnp.arange(seq))
    tab_meta = _rope_tables(jnp.arange(batch * N_META) % N_META)
    bias_band, bias_meta, bias_mm, bias_mr = _bias_tables(rel_bias_table, seq // BLOCK)

    out = None
    for i in range(depth):
        last = i == depth - 1
        lw = _layer_weights(norm_in[i], w_in[i], norm_q_lat[i], w_uq[i], norm_kv_lat[i], w_ukv[i],
                            norm_out_a[i], norm_out_b[i], w_out[i])
        sink = sink_a[i].astype(f32) * LOG2E
        sink_rows = jnp.repeat(sink.reshape(A_KV_HEADS, A_GROUP), BLOCK, axis=1)[:, None, :]
        qa, ka, vat, g, q, k, vt = _proj(h_real, tab_real, seq // ROW_TILE, lw)
        qam, kam, vatm, gm, qm, km, vtm = _proj(h_meta, tab_meta, 1, lw)
        vtm_b = vtm.reshape(B_HEADS, B_V_AUG, batch, N_META).transpose(2, 0, 1, 3)
        vatm_b = vatm.reshape(A_KV_HEADS, A_V_AUG, batch, N_META).transpose(2, 0, 1, 3)

        ya = _win(sink_rows, qa, ka, vat, kam, vatm_b, bias_band, bias_meta, batch, seq)
        yb = _mla(q, k, km, vt, vtm_b, batch, seq)
        if last:
            out = _out(h_real, ya, yb, g, lw, norm_final.astype(f32)[None, :])
        else:
            yam = _win_meta(sink, qam, ka, vat, kam, vatm_b, bias_mm, bias_mr, batch, seq)
            ybm = _mla_meta(qm, k, km, vt, vtm_b, batch, seq)
            h_real = _out(h_real, ya, yb, g, lw)
            h_meta = _out(h_meta, yam, ybm, gm, lw)
    return out.reshape(batch, seq, d).astype(x.dtype)
```

```python
import functools
import math

import jax
import jax.numpy as jnp
from jax import lax
from jax.experimental import pallas as pl
from jax.experimental.pallas import tpu as pltpu

D_MODEL = 1024
N_META = 16
BLOCK = 128
WINDOW = 128
A_HEADS = 8
A_KV_HEADS = 2
A_GROUP = A_HEADS // A_KV_HEADS
A_HEAD_DIM = 64
A_WIDTH = A_HEADS * A_HEAD_DIM
B_HEADS = 8
B_NOPE_DIM = 64
B_ROPE_DIM = 32
B_V_DIM = 64
B_WIDTH = B_HEADS * B_V_DIM
B_QK_PAD = 128
BF16_SUBLANES = 16
B_V_AUG = B_V_DIM + BF16_SUBLANES
A_V_AUG = A_HEAD_DIM + BF16_SUBLANES
Q_LORA_RANK = 256
KV_LORA_RANK = 128
N_BUCKETS = 32
MAX_DISTANCE = 128
ROPE_THETA = 10000.0
EPS = 1e-6

LOG2E = math.log2(math.e)
NEG = -1e30
ROW_TILE = 256
MLA_Q_TILE = 256
MLA_SUBTILES = 4
MLA_KEY_CHUNK = 1024
WIN_BLOCKS = 4
VMEM_LIMIT_BYTES = 56 * 1024 * 1024

_C_QA = 0
_C_CQ = _C_QA + A_WIDTH
_C_CKV = _C_CQ + Q_LORA_RANK
_C_KA = _C_CKV + KV_LORA_RANK
_C_KRA = _C_KA + A_KV_HEADS * A_HEAD_DIM
_C_KRB = _C_KRA + B_QK_PAD
_C_END = _C_KRB + B_QK_PAD

_NT = (((1,), (1,)), ((), ()))

bf16 = jnp.bfloat16
f32 = jnp.float32


def _cparams(*sem, flags=None):
    return pltpu.CompilerParams(dimension_semantics=sem, vmem_limit_bytes=VMEM_LIMIT_BYTES, flags=flags)


def _rms(x, gain):
    return x * lax.rsqrt(jnp.mean(x * x, axis=-1, keepdims=True) + EPS) * gain


def _silu(x):
    return x / (1.0 + jnp.exp(-x))


def _proj_kernel(h_ref, tab_ref, gin_ref, w1_ref, wvat_ref, gq_ref, wqa_ref, wqb_ref, gkv_ref, wk_ref, wvt_ref,
                 qa_ref, ka_ref, vat_ref, q_ref, k_ref, vt_ref):
    u = _rms(h_ref[...], gin_ref[...]).astype(bf16)
    t = u.shape[0]

    def mm(lo, hi):
        return jnp.dot(u, w1_ref[:, lo:hi], preferred_element_type=f32)

    qa = mm(_C_QA, _C_CQ) * (A_HEAD_DIM ** -0.5 * LOG2E)
    for h in range(A_HEADS):
        qa_ref[h] = qa[:, h * A_HEAD_DIM:(h + 1) * A_HEAD_DIM].astype(bf16)
    vat = lax.dot_general(wvat_ref[...], u, _NT, preferred_element_type=f32)
    vat_ref[:, :A_HEAD_DIM, :] = vat.reshape(A_KV_HEADS, A_HEAD_DIM, t).astype(bf16)
    vat_ref[:, A_HEAD_DIM:, :] = jnp.ones((A_KV_HEADS, A_V_AUG - A_HEAD_DIM, t), bf16)
    mixed = mm(_C_CQ, _C_END)
    cq = mixed[:, :_C_CKV - _C_CQ]
    ckv = mixed[:, _C_CKV - _C_CQ:_C_KA - _C_CQ]
    ka = mixed[:, _C_KA - _C_CQ:_C_KRA - _C_CQ]
    kra = mixed[:, _C_KRA - _C_CQ:_C_KRB - _C_CQ]
    krb = mixed[:, _C_KRB - _C_CQ:]
    for j in range(A_KV_HEADS):
        ka_ref[j] = ka[:, j * A_HEAD_DIM:(j + 1) * A_HEAD_DIM].astype(bf16)
    cos_q = tab_ref[:, 0 * B_QK_PAD:1 * B_QK_PAD]
    sin_q = tab_ref[:, 1 * B_QK_PAD:2 * B_QK_PAD]
    cos_k = tab_ref[:, 2 * B_QK_PAD:3 * B_QK_PAD]
    sin_k = tab_ref[:, 3 * B_QK_PAD:4 * B_QK_PAD]

    cqn = _rms(cq, gq_ref[...]).astype(bf16)
    qa_part = jnp.dot(cqn, wqa_ref[...], preferred_element_type=f32)
    qb_part = jnp.dot(cqn, wqb_ref[...], preferred_element_type=f32)
    for h in range(B_HEADS):
        sl = slice(h * B_QK_PAD, (h + 1) * B_QK_PAD)
        q_ref[h] = (qa_part[:, sl] * cos_q + qb_part[:, sl] * sin_q).astype(bf16)

    ckvn = _rms(ckv, gkv_ref[...]).astype(bf16)
    k_nope = jnp.dot(ckvn, wk_ref[...], preferred_element_type=f32)
    k_rope = kra * cos_k + krb * sin_k
    for h in range(B_HEADS):
        k_ref[h] = (k_nope[:, h * B_QK_PAD:(h + 1) * B_QK_PAD] + k_rope).astype(bf16)
    vt = lax.dot_general(wvt_ref[...], ckvn, _NT, preferred_element_type=f32)
    vt_ref[:, :B_V_DIM, :] = vt.reshape(B_HEADS, B_V_DIM, t).astype(bf16)
    vt_ref[:, B_V_DIM:, :] = jnp.ones((B_HEADS, B_V_AUG - B_V_DIM, t), bf16)


def _proj(h, tab, tab_blocks, lw):
    rows = h.shape[0]
    t = ROW_TILE
    steps = rows // t
    const = lambda shape: pl.BlockSpec(shape, lambda i: (0,) * len(shape))
    return pl.pallas_call(
        _proj_kernel,
        grid=(steps,),
        in_specs=[
            pl.BlockSpec((t, D_MODEL), lambda i: (i, 0)),
            pl.BlockSpec((t, 4 * B_QK_PAD), lambda i: (i % tab_blocks, 0)),
            const((1, D_MODEL)),
            const((D_MODEL, _C_END)),
            const((A_KV_HEADS * A_HEAD_DIM, D_MODEL)),
            const((1, Q_LORA_RANK)),
            const((Q_LORA_RANK, B_HEADS * B_QK_PAD)),
            const((Q_LORA_RANK, B_HEADS * B_QK_PAD)),
            const((1, KV_LORA_RANK)),
            const((KV_LORA_RANK, B_HEADS * B_QK_PAD)),
            const((B_WIDTH, KV_LORA_RANK)),
        ],
        out_specs=[
            pl.BlockSpec((A_HEADS, t, A_HEAD_DIM), lambda i: (0, i, 0)),
            pl.BlockSpec((A_KV_HEADS, t, A_HEAD_DIM), lambda i: (0, i, 0)),
            pl.BlockSpec((A_KV_HEADS, A_V_AUG, t), lambda i: (0, 0, i)),
            pl.BlockSpec((B_HEADS, t, B_QK_PAD), lambda i: (0, i, 0)),
            pl.BlockSpec((B_HEADS, t, B_QK_PAD), lambda i: (0, i, 0)),
            pl.BlockSpec((B_HEADS, B_V_AUG, t), lambda i: (0, 0, i)),
        ],
        out_shape=[
            jax.ShapeDtypeStruct((A_HEADS, rows, A_HEAD_DIM), bf16),
            jax.ShapeDtypeStruct((A_KV_HEADS, rows, A_HEAD_DIM), bf16),
            jax.ShapeDtypeStruct((A_KV_HEADS, A_V_AUG, rows), bf16),
            jax.ShapeDtypeStruct((B_HEADS, rows, B_QK_PAD), bf16),
            jax.ShapeDtypeStruct((B_HEADS, rows, B_QK_PAD), bf16),
            jax.ShapeDtypeStruct((B_HEADS, B_V_AUG, rows), bf16),
        ],
        compiler_params=_cparams("arbitrary"),
        name="proj",
    )(h, tab, lw["gin"], lw["w1"], lw["wvat"], lw["gq"], lw["wqa"], lw["wqb"], lw["gkv"], lw["wk"], lw["wvt"])


def _out_kernel(h_ref, ya_ref, yb_ref, gin_ref, wg_ref, na_ref, nb_ref, w_ref, *rest, final):
    if final:
        nf_ref, o_ref = rest
    else:
        (o_ref,) = rest
    hn = h_ref[...]
    u = _rms(hn, gin_ref[...]).astype(bf16)
    ga = _silu(jnp.dot(u, wg_ref[:, :A_WIDTH], preferred_element_type=f32))
    gb = _silu(jnp.dot(u, wg_ref[:, A_WIDTH:], preferred_element_type=f32))
    ya = _rms(ya_ref[...].astype(f32), na_ref[...]) * ga
    yb = _rms(yb_ref[...].astype(f32), nb_ref[...]) * gb
    hn = hn + jnp.dot(ya.astype(bf16), w_ref[:A_WIDTH, :], preferred_element_type=f32)
    hn = hn + jnp.dot(yb.astype(bf16), w_ref[A_WIDTH:, :], preferred_element_type=f32)
    if final:
        hn = _rms(hn, nf_ref[...])
    o_ref[...] = hn


def _out(h, ya, yb, lw, norm_final=None):
    rows = h.shape[0]
    t = ROW_TILE
    row = lambda width: pl.BlockSpec((t, width), lambda i: (i, 0))
    const = lambda shape: pl.BlockSpec(shape, lambda i: (0,) * len(shape))
    final = norm_final is not None
    in_specs = [row(D_MODEL), row(A_WIDTH), row(B_WIDTH),
                const((1, D_MODEL)), const((D_MODEL, A_WIDTH + B_WIDTH)),
                const((1, A_WIDTH)), const((1, B_WIDTH)), const((A_WIDTH + B_WIDTH, D_MODEL))]
    args = [h, ya, yb, lw["gin"], lw["wg"], lw["na"], lw["nb"], lw["wout"]]
    if final:
        in_specs.append(const((1, D_MODEL)))
        args.append(norm_final)
    return pl.pallas_call(
        functools.partial(_out_kernel, final=final),
        grid=(rows // t,),
        in_specs=in_specs,
        out_specs=row(D_MODEL),
        out_shape=jax.ShapeDtypeStruct((rows, D_MODEL), f32),
        compiler_params=_cparams("arbitrary"),
        name="outproj",
    )(*args)


def _bias_kernel(tab_ref, idx_ref, o_ref):
    idx = idx_ref[...]
    for h in range(A_HEADS):
        acc = jnp.where(idx < 0, NEG, 0.0).astype(f32)
        for b in range(N_BUCKETS):
            acc = jnp.where(idx == b, tab_ref[b * A_HEADS + h], acc)
        o_ref[h] = acc


def _bias_lookup(table_flat, idx):
    return pl.pallas_call(
        _bias_kernel,
        in_specs=[pl.BlockSpec(memory_space=pltpu.SMEM), pl.BlockSpec(idx.shape, lambda: (0, 0))],
        out_specs=pl.BlockSpec((A_HEADS,) + idx.shape, lambda: (0, 0, 0)),
        out_shape=jax.ShapeDtypeStruct((A_HEADS,) + idx.shape, f32),
        compiler_params=pltpu.CompilerParams(vmem_limit_bytes=VMEM_LIMIT_BYTES),
        name="relbias",
    )(table_flat, idx)


def _t5_bucket(rel):
    nb = N_BUCKETS // 2
    max_exact = nb // 2
    ret = jnp.where(rel > 0, nb, 0)
    n = jnp.abs(rel)
    nf = jnp.maximum(n, 1).astype(jnp.float32)
    large = max_exact + (jnp.log(nf / max_exact) / math.log(MAX_DISTANCE / max_exact)
                         * (nb - max_exact)).astype(jnp.int32)
    large = jnp.minimum(large, nb - 1)
    bucket = ret + jnp.where(n < max_exact, n, large)
    return jnp.bitwise_and(bucket, N_BUCKETS - 1)


def _bias_tables(rel_bias_table, nblk):
    table_flat = (rel_bias_table.astype(f32) * LOG2E).reshape(-1)
    i = jnp.arange(BLOCK, dtype=jnp.int32)[:, None]
    j = jnp.arange(3 * BLOCK, dtype=jnp.int32)[None, :]
    rel = j - i - BLOCK
    band = jnp.where(jnp.abs(rel) <= WINDOW, _t5_bucket(rel), -1)
    first = jnp.where(j < BLOCK, -1, band)
    last = jnp.where(j >= 2 * BLOCK, -1, band)
    idx_band = jnp.concatenate([first.T, band.T, last.T], axis=0)
    bias_band = (_bias_lookup(table_flat, idx_band)
                 .reshape(A_KV_HEADS, A_GROUP, 3, 3 * BLOCK, BLOCK)
                 .transpose(2, 0, 3, 1, 4).reshape(3, A_KV_HEADS, 3 * BLOCK, A_GROUP * BLOCK))

    n = jnp.arange(nblk, dtype=jnp.int32)[:, None, None]
    k = jnp.arange(N_META, dtype=jnp.int32)[None, :, None]
    qi = jnp.arange(BLOCK, dtype=jnp.int32)[None, None, :]
    rel_m = k - (N_META + n * BLOCK + qi)
    idx_m = _t5_bucket(rel_m).reshape(nblk * N_META, BLOCK)
    bias_meta = (_bias_lookup(table_flat, idx_m)
                 .reshape(A_KV_HEADS, A_GROUP, nblk, N_META, BLOCK)
                 .transpose(2, 0, 3, 1, 4).reshape(nblk, A_KV_HEADS, N_META, A_GROUP * BLOCK))

    qp = jnp.arange(N_META, dtype=jnp.int32)[:, None]
    kp = jnp.arange(N_META + BLOCK, dtype=jnp.int32)[None, :]
    rel_q = kp - qp
    idx_q = jnp.where(jnp.abs(rel_q) <= WINDOW, _t5_bucket(rel_q), -1)
    bias_q = _bias_lookup(table_flat, idx_q)
    return bias_band, bias_meta, bias_q[:, :, :N_META], bias_q[:, :, N_META:]


def _win_kernel(sink_ref, q_ref, kp_ref, kc_ref, kn_ref, vp_ref, vc_ref, vn_ref, kam_ref, vam_ref,
                bias_ref, bm_ref, o_ref):
    w = WIN_BLOCKS
    i, ntile = pl.program_id(1), pl.num_programs(1)

    def qrows(blk):
        return slice(blk * BLOCK, (blk + 1) * BLOCK)

    def band(blk, prev_ref, cur_ref, next_ref, kvh, axis):
        take = lambda ref, lo, hi: ref[kvh, lo:hi, :] if axis == 0 else ref[kvh, :, lo:hi]
        parts = [take(cur_ref, max(blk - 1, 0) * BLOCK, min(blk + 2, w) * BLOCK)]
        if blk == 0:
            parts.insert(0, take(prev_ref, (w - 1) * BLOCK, w * BLOCK))
        if blk == w - 1:
            parts.append(take(next_ref, 0, BLOCK))
        return jnp.concatenate(parts, axis=axis)

    def variant(blk):
        v = 1
        if blk == 0:
            v = jnp.where(i == 0, 0, v)
        if blk == w - 1:
            v = jnp.where(i == ntile - 1, 2, v)
        return v

    def scores(blk, kvh):
        kb = band(blk, kp_ref, kc_ref, kn_ref, kvh, 0)
        q4 = q_ref[A_GROUP * kvh:A_GROUP * (kvh + 1), qrows(blk), :].reshape(A_GROUP * BLOCK, A_HEAD_DIM)
        s = lax.dot_general(kb, q4, _NT, preferred_element_type=f32) + bias_ref[variant(blk), kvh]
        sm = lax.dot_general(kam_ref[kvh], q4, _NT, preferred_element_type=f32) + bm_ref[blk, kvh]
        sink = sink_ref[kvh]
        m = jnp.maximum(jnp.maximum(jnp.max(s, axis=0, keepdims=True), jnp.max(sm, axis=0, keepdims=True)), sink)
        return s, sm, m

    def attend(blk, kvh, s, sm, m):
        vb = band(blk, vp_ref, vc_ref, vn_ref, kvh, 1)
        p = jnp.exp2(s - m).astype(bf16)
        pm = jnp.exp2(sm - m).astype(bf16)
        acc = (jnp.dot(vb, p, preferred_element_type=f32)
               + jnp.dot(vam_ref[0, kvh], pm, preferred_element_type=f32))
        den = acc[A_HEAD_DIM:A_HEAD_DIM + 1] + jnp.exp2(sink_ref[kvh] - m)
        return acc[:A_HEAD_DIM] * (1.0 / den)

    units = [(blk, kvh) for blk in range(w) for kvh in range(A_KV_HEADS)]
    outs = {}
    pending = scores(*units[0])
    for u, unit in enumerate(units):
        nxt = scores(*units[u + 1]) if u + 1 < len(units) else None
        outs[unit] = attend(*unit, *pending)
        pending = nxt
        blk, kvh = unit
        if kvh == A_KV_HEADS - 1:
            ot = jnp.concatenate([outs[(blk, j)][:, g * BLOCK:(g + 1) * BLOCK]
                                  for j in range(A_KV_HEADS) for g in range(A_GROUP)], axis=0)
            o_ref[qrows(blk), :] = ot.T.astype(o_ref.dtype)


def _win(sink_rows, qa, ka, vat, kam, vatm, bias_band, bias_meta, batch, seq):
    tb = WIN_BLOCKS * BLOCK
    ntile = seq // tb
    prev_ = lambda b, i: b * ntile + jnp.maximum(i - 1, 0)
    cur_ = lambda b, i: b * ntile + i
    next_ = lambda b, i: b * ntile + jnp.minimum(i + 1, ntile - 1)
    kspec = lambda f: pl.BlockSpec((A_KV_HEADS, tb, A_HEAD_DIM), lambda b, i: (0, f(b, i), 0))
    vspec = lambda f: pl.BlockSpec((A_KV_HEADS, A_V_AUG, tb), lambda b, i: (0, 0, f(b, i)))
    return pl.pallas_call(
        _win_kernel,
        grid=(batch, ntile),
        in_specs=[
            pl.BlockSpec((A_KV_HEADS, 1, A_GROUP * BLOCK), lambda b, i: (0, 0, 0)),
            pl.BlockSpec((A_HEADS, tb, A_HEAD_DIM), lambda b, i: (0, cur_(b, i), 0)),
            kspec(prev_), kspec(cur_), kspec(next_),
            vspec(prev_), vspec(cur_), vspec(next_),
            pl.BlockSpec((A_KV_HEADS, N_META, A_HEAD_DIM), lambda b, i: (0, b, 0)),
            pl.BlockSpec((1, A_KV_HEADS, A_V_AUG, N_META), lambda b, i: (b, 0, 0, 0)),
            pl.BlockSpec((3, A_KV_HEADS, 3 * BLOCK, A_GROUP * BLOCK), lambda b, i: (0, 0, 0, 0)),
            pl.BlockSpec((WIN_BLOCKS, A_KV_HEADS, N_META, A_GROUP * BLOCK), lambda b, i: (i, 0, 0, 0)),
        ],
        out_specs=pl.BlockSpec((tb, A_WIDTH), lambda b, i: (cur_(b, i), 0)),
        out_shape=jax.ShapeDtypeStruct((batch * seq, A_WIDTH), bf16),
        compiler_params=_cparams("arbitrary", "arbitrary"),
        name="win",
    )(sink_rows, qa, ka, ka, ka, vat, vat, vat, kam, vatm, bias_band, bias_meta)


def _win_meta_kernel(sink_ref, q_ref, ka_ref, vat_ref, kam_ref, vatm_ref, biasm_ref, biasr_ref, o_ref):
    for kvh in range(A_KV_HEADS):
        q4 = q_ref[A_GROUP * kvh:A_GROUP * (kvh + 1)].reshape(A_GROUP * N_META, A_HEAD_DIM)
        sm = lax.dot_general(q4, kam_ref[kvh], _NT, preferred_element_type=f32)
        sr = lax.dot_general(q4, ka_ref[kvh], _NT, preferred_element_type=f32)
        for g in range(A_GROUP):
            h = A_GROUP * kvh + g
            rows = slice(g * N_META, (g + 1) * N_META)
            smg = sm[rows] + biasm_ref[h]
            srg = sr[rows] + biasr_ref[h]
            sink = sink_ref[h]
            m = jnp.maximum(jnp.maximum(jnp.max(smg, axis=-1, keepdims=True),
                                        jnp.max(srg, axis=-1, keepdims=True)), sink)
            pm = jnp.exp2(smg - m).astype(bf16)
            pr = jnp.exp2(srg - m).astype(bf16)
            o = (lax.dot_general(pm, vatm_ref[0, kvh], _NT, preferred_element_type=f32)
                 + lax.dot_general(pr, vat_ref[kvh], _NT, preferred_element_type=f32))
            den = o[:, A_HEAD_DIM:A_HEAD_DIM + 1] + jnp.exp2(sink - m)
            o_ref[:, h * A_HEAD_DIM:(h + 1) * A_HEAD_DIM] = (o[:, :A_HEAD_DIM] * (1.0 / den)).astype(o_ref.dtype)


def _win_meta(sink, qam, ka, vat, kam, vatm, bias_mm, bias_mr, batch, seq):
    nblk = seq // BLOCK
    const3 = lambda shape: pl.BlockSpec(shape, lambda b: (0, 0, 0))
    return pl.pallas_call(
        _win_meta_kernel,
        grid=(batch,),
        in_specs=[
            pl.BlockSpec(memory_space=pltpu.SMEM),
            pl.BlockSpec((A_HEADS, N_META, A_HEAD_DIM), lambda b: (0, b, 0)),
            pl.BlockSpec((A_KV_HEADS, BLOCK, A_HEAD_DIM), lambda b: (0, b * nblk, 0)),
            pl.BlockSpec((A_KV_HEADS, A_V_AUG, BLOCK), lambda b: (0, 0, b * nblk)),
            pl.BlockSpec((A_KV_HEADS, N_META, A_HEAD_DIM), lambda b: (0, b, 0)),
            pl.BlockSpec((1, A_KV_HEADS, A_V_AUG, N_META), lambda b: (b, 0, 0, 0)),
            const3((A_HEADS, N_META, N_META)),
            const3((A_HEADS, N_META, BLOCK)),
        ],
        out_specs=pl.BlockSpec((N_META, A_WIDTH), lambda b: (b, 0)),
        out_shape=jax.ShapeDtypeStruct((batch * N_META, A_WIDTH), bf16),
        compiler_params=_cparams("arbitrary"),
        name="win_meta",
    )(sink, qam, ka, vat, kam, vatm, bias_mm, bias_mr)


def _mla_kernel(q_ref, k_ref, km_ref, vt_ref, vtm_ref, o_ref, acc_ref, s0_ref, s1_ref, sm0_ref, sm1_ref):
    s_bufs, sm_bufs = (s0_ref, s1_ref), (sm0_ref, sm1_ref)
    nsub, _, _, tq = acc_ref.shape
    units = nsub * B_HEADS
    seq = k_ref.shape[1]
    chunks = [(c * MLA_KEY_CHUNK, (c + 1) * MLA_KEY_CHUNK) for c in range(seq // MLA_KEY_CHUNK)]

    def unit(u):
        return u % B_HEADS, u // B_HEADS

    def step(t, slot, m, *, score_next=True, attend_cur=True):
        if score_next:
            hn, subn = unit(t + 1)
            q = q_ref[hn, pl.ds(pl.multiple_of(subn * tq, tq), tq), :]
            sm = lax.dot_general(km_ref[hn], q, _NT, preferred_element_type=f32)
            sm_bufs[1 - slot][...] = sm
            m_next = jnp.max(sm, axis=0, keepdims=True)
        if attend_cur:
            h, sub = unit(t)
            pm = jnp.exp2(sm_bufs[slot][...] - m).astype(bf16)
            acc = jnp.dot(vtm_ref[0, h], pm, preferred_element_type=f32)
        for lo, hi in chunks:
            if score_next:
                s = lax.dot_general(k_ref[hn, lo:hi, :], q, _NT, preferred_element_type=f32)
                s_bufs[1 - slot][lo:hi, :] = s
                m_next = jnp.maximum(m_next, jnp.max(s, axis=0, keepdims=True))
            if attend_cur:
                p = jnp.exp2(s_bufs[slot][lo:hi, :] - m).astype(bf16)
                acc = acc + jnp.dot(vt_ref[h, :, lo:hi], p, preferred_element_type=f32)
        if attend_cur:
            acc_ref[sub, h] = acc[:B_V_DIM] * (1.0 / acc[B_V_DIM:B_V_DIM + 1])
        return m_next if score_next else None

    def two_steps(j, m_even):
        m_odd = step(2 * j, 0, m_even)
        return step(2 * j + 1, 1, m_odd)

    m_first = step(-1, 1, None, attend_cur=False)
    m_even = lax.fori_loop(0, units // 2 - 1, two_steps, m_first)
    m_odd = step(units - 2, 0, m_even)
    step(units - 1, 1, m_odd, score_next=False)
    for sub in range(nsub):
        o_ref[sub * tq:(sub + 1) * tq, :] = acc_ref[sub].reshape(B_WIDTH, tq).T.astype(o_ref.dtype)


def _mla(q, k, km, vt, vtm, batch, seq):
    tq = MLA_Q_TILE
    tb = MLA_Q_TILE * MLA_SUBTILES
    nq = seq // tb
    return pl.pallas_call(
        _mla_kernel,
        grid=(batch, nq),
        in_specs=[
            pl.BlockSpec((B_HEADS, tb, B_QK_PAD), lambda b, i: (0, b * nq + i, 0)),
            pl.BlockSpec((B_HEADS, seq, B_QK_PAD), lambda b, i: (0, b, 0)),
            pl.BlockSpec((B_HEADS, N_META, B_QK_PAD), lambda b, i: (0, b, 0)),
            pl.BlockSpec((B_HEADS, B_V_AUG, seq), lambda b, i: (0, 0, b)),
            pl.BlockSpec((1, B_HEADS, B_V_AUG, N_META), lambda b, i: (b, 0, 0, 0)),
        ],
        out_specs=pl.BlockSpec((tb, B_WIDTH), lambda b, i: (b * nq + i, 0)),
        out_shape=jax.ShapeDtypeStruct((batch * seq, B_WIDTH), bf16),
        scratch_shapes=([pltpu.VMEM((MLA_SUBTILES, B_HEADS, B_V_DIM, tq), f32)]
                        + [pltpu.VMEM((seq, tq), f32)] * 2 + [pltpu.VMEM((N_META, tq), f32)] * 2),
        compiler_params=_cparams("arbitrary", "arbitrary"),
        name="mla",
    )(q, k, km, vt, vtm)


def _mla_meta_kernel(q_ref, k_ref, km_ref, vt_ref, vtm_ref, o_ref):
    for h in range(B_HEADS):
        q = q_ref[h]
        s = lax.dot_general(q, k_ref[h], _NT, preferred_element_type=f32)
        sm = lax.dot_general(q, km_ref[h], _NT, preferred_element_type=f32)
        m = jnp.maximum(jnp.max(s, axis=-1, keepdims=True), jnp.max(sm, axis=-1, keepdims=True))
        p = jnp.exp2(s - m).astype(bf16)
        pm = jnp.exp2(sm - m).astype(bf16)
        o = (lax.dot_general(p, vt_ref[h], _NT, preferred_element_type=f32)
             + lax.dot_general(pm, vtm_ref[0, h], _NT, preferred_element_type=f32))
        o_ref[:, h * B_V_DIM:(h + 1) * B_V_DIM] = (
            o[:, :B_V_DIM] * (1.0 / o[:, B_V_DIM:B_V_DIM + 1])).astype(o_ref.dtype)


def _mla_meta(qm, k, km, vt, vtm, batch, seq):
    return pl.pallas_call(
        _mla_meta_kernel,
        grid=(batch,),
        in_specs=[
            pl.BlockSpec((B_HEADS, N_META, B_QK_PAD), lambda b: (0, b, 0)),
            pl.BlockSpec((B_HEADS, seq, B_QK_PAD), lambda b: (0, b, 0)),
            pl.BlockSpec((B_HEADS, N_META, B_QK_PAD), lambda b: (0, b, 0)),
            pl.BlockSpec((B_HEADS, B_V_AUG, seq), lambda b: (0, 0, b)),
            pl.BlockSpec((1, B_HEADS, B_V_AUG, N_META), lambda b: (b, 0, 0, 0)),
        ],
        out_specs=pl.BlockSpec((N_META, B_WIDTH), lambda b: (b, 0)),
        out_shape=jax.ShapeDtypeStruct((batch * N_META, B_WIDTH), bf16),
        compiler_params=_cparams("arbitrary"),
        name="mla_meta",
    )(qm, k, km, vt, vtm)


def _rope_tables(pos):
    half = B_ROPE_DIM // 2
    freqs = ROPE_THETA ** (-jnp.arange(half, dtype=jnp.float32) / half)
    ang = pos.astype(jnp.float32)[:, None] * freqs[None, :]
    cos, sin = jnp.cos(ang), jnp.sin(ang)
    rows = pos.shape[0]
    c_mla = (B_NOPE_DIM + B_ROPE_DIM) ** -0.5 * LOG2E
    z = lambda w: jnp.zeros((rows, w), f32)
    tail = B_QK_PAD - B_NOPE_DIM - B_ROPE_DIM
    cos_q = jnp.concatenate([jnp.full((rows, B_NOPE_DIM), c_mla, f32), c_mla * cos, c_mla * cos, z(tail)], axis=1)
    sin_q = jnp.concatenate([z(B_NOPE_DIM), c_mla * sin, c_mla * sin, z(tail)], axis=1)
    cos_k = jnp.concatenate([z(B_NOPE_DIM), cos, cos, z(tail)], axis=1)
    sin_k = jnp.concatenate([z(B_NOPE_DIM), sin, sin, z(tail)], axis=1)
    return jnp.concatenate([cos_q, sin_q, cos_k, sin_k], axis=1)


def _rot_cols(w):
    half = w.shape[-1] // 2
    return jnp.concatenate([-w[..., half:], w[..., :half]], axis=-1)


def _layer_weights(norm_in, w_in, norm_q_lat, w_uq, norm_kv_lat, w_ukv, norm_out_a, norm_out_b, w_out):
    d = w_in.shape[0]
    offs = [0, A_WIDTH, A_WIDTH + 128, A_WIDTH + 256, 2 * A_WIDTH + 256,
            2 * A_WIDTH + 256 + Q_LORA_RANK, 2 * A_WIDTH + 256 + Q_LORA_RANK + KV_LORA_RANK,
            2 * A_WIDTH + 256 + Q_LORA_RANK + KV_LORA_RANK + B_ROPE_DIM]
    qa, ka, va, ga, cq, ckv, kr = (w_in[:, offs[i]:offs[i + 1]] for i in range(7))
    gb = w_in[:, offs[7]:]
    tail = B_QK_PAD - B_NOPE_DIM - B_ROPE_DIM
    pad_k = lambda w: jnp.concatenate([jnp.zeros((d, B_NOPE_DIM), w.dtype), w, jnp.zeros((d, tail), w.dtype)], axis=1)
    w1 = jnp.concatenate([qa, cq, ckv, ka, pad_k(kr), pad_k(_rot_cols(kr))], axis=1).astype(bf16)

    r = w_uq.shape[0]
    uq = w_uq.reshape(r, B_HEADS, B_NOPE_DIM + B_ROPE_DIM)
    nope, rope = uq[..., :B_NOPE_DIM], uq[..., B_NOPE_DIM:]
    wqa = jnp.concatenate([nope, rope, jnp.zeros((r, B_HEADS, tail), uq.dtype)], axis=-1)
    wqb = jnp.concatenate([jnp.zeros_like(nope), _rot_cols(rope), jnp.zeros((r, B_HEADS, tail), uq.dtype)], axis=-1)

    rk = w_ukv.shape[0]
    ukv = w_ukv.reshape(rk, B_HEADS, B_NOPE_DIM + B_V_DIM)
    k_nope, v = ukv[..., :B_NOPE_DIM], ukv[..., B_NOPE_DIM:]
    wk = jnp.concatenate([k_nope, jnp.zeros((rk, B_HEADS, B_QK_PAD - B_NOPE_DIM), ukv.dtype)], axis=-1)
    return {
        "gin": norm_in.astype(f32)[None, :],
        "w1": w1,
        "wg": jnp.concatenate([ga, gb], axis=1).astype(bf16),
        "wvat": va.T.astype(bf16),
        "gq": norm_q_lat.astype(f32)[None, :],
        "wqa": wqa.reshape(r, B_HEADS * B_QK_PAD).astype(bf16),
        "wqb": wqb.reshape(r, B_HEADS * B_QK_PAD).astype(bf16),
        "gkv": norm_kv_lat.astype(f32)[None, :],
        "wk": wk.reshape(rk, B_HEADS * B_QK_PAD).astype(bf16),
        "wvt": v.reshape(rk, B_WIDTH).T.astype(bf16),
        "na": norm_out_a.astype(f32)[None, :],
        "nb": norm_out_b.astype(f32)[None, :],
        "wout": w_out.astype(bf16),
    }


def kernel(x, meta_tokens, rel_bias_table, norm_in, w_in, sink_a, norm_q_lat, w_uq, norm_kv_lat, w_ukv,
           norm_out_a, norm_out_b, w_out, norm_final):
    batch, seq, d = x.shape
    depth = w_in.shape[0]
    assert d == D_MODEL and seq % (MLA_Q_TILE * MLA_SUBTILES) == 0 and (batch * N_META) % ROW_TILE == 0

    h_real = x.reshape(batch * seq, d).astype(f32)
    h_meta = jnp.tile(meta_tokens.astype(f32), (batch, 1))
    tab_real = _rope_tables(N_META + jnp.arange(seq))
    tab_meta = _rope_tables(jnp.arange(batch * N_META) % N_META)
    bias_band, bias_meta, bias_mm, bias_mr = _bias_tables(rel_bias_table, seq // BLOCK)

    out = None
    for i in range(depth):
        last = i == depth - 1
        lw = _layer_weights(norm_in[i], w_in[i], norm_q_lat[i], w_uq[i], norm_kv_lat[i], w_ukv[i],
                            norm_out_a[i], norm_out_b[i], w_out[i])
        sink = sink_a[i].astype(f32) * LOG2E
        sink_rows = jnp.repeat(sink.reshape(A_KV_HEADS, A_GROUP), BLOCK, axis=1)[:, None, :]
        qa, ka, vat, q, k, vt = _proj(h_real, tab_real, seq // ROW_TILE, lw)
        qam, kam, vatm, qm, km, vtm = _proj(h_meta, tab_meta, 1, lw)
        vtm_b = vtm.reshape(B_HEADS, B_V_AUG, batch, N_META).transpose(2, 0, 1, 3)
        vatm_b = vatm.reshape(A_KV_HEADS, A_V_AUG, batch, N_META).transpose(2, 0, 1, 3)

        ya = _win(sink_rows, qa, ka, vat, kam, vatm_b, bias_band, bias_meta, batch, seq)
        yb = _mla(q, k, km, vt, vtm_b, batch, seq)
        if last:
            out = _out(h_real, ya, yb, lw, norm_final.astype(f32)[None, :])
        else:
            yam = _win_meta(sink, qam, ka, vat, kam, vatm_b, bias_mm, bias_mr, batch, seq)
            ybm = _mla_meta(qm, k, km, vt, vtm_b, batch, seq)
            h_real = _out(h_real, ya, yb, lw)
            h_meta = _out(h_meta, yam, ybm, lw)
    return out.reshape(batch, seq, d).astype(x.dtype)
```

```python
import functools
import math

import jax
import jax.numpy as jnp
from jax import lax
from jax.experimental import pallas as pl
from jax.experimental.pallas import tpu as pltpu

D_MODEL = 1024
N_META = 16
BLOCK = 128
WINDOW = 128
A_HEADS = 8
A_KV_HEADS = 2
A_GROUP = A_HEADS // A_KV_HEADS
A_HEAD_DIM = 64
A_WIDTH = A_HEADS * A_HEAD_DIM
B_HEADS = 8
B_NOPE_DIM = 64
B_ROPE_DIM = 32
B_V_DIM = 64
B_WIDTH = B_HEADS * B_V_DIM
B_QK_PAD = 128
BF16_SUBLANES = 16
B_V_AUG = B_V_DIM + BF16_SUBLANES
A_V_AUG = A_HEAD_DIM + BF16_SUBLANES
Q_LORA_RANK = 256
KV_LORA_RANK = 128
N_BUCKETS = 32
MAX_DISTANCE = 128
ROPE_THETA = 10000.0
EPS = 1e-6

LOG2E = math.log2(math.e)
NEG = -1e30
ROW_TILE = 256
MLA_Q_TILE = 256
MLA_SUBTILES = 8
MLA_STEPS_PER_ITER = 4
MLA_KEY_CHUNK = 1024
WIN_BLOCKS = 4
BIAS_ROW_CHUNK = 32
VMEM_LIMIT_BYTES = 56 * 1024 * 1024

_C_QA = 0
_C_CQ = _C_QA + A_WIDTH
_C_CKV = _C_CQ + Q_LORA_RANK
_C_KA = _C_CKV + KV_LORA_RANK
_C_KRA = _C_KA + A_KV_HEADS * A_HEAD_DIM
_C_KRB = _C_KRA + B_QK_PAD
_C_END = _C_KRB + B_QK_PAD

_NT = (((1,), (1,)), ((), ()))

bf16 = jnp.bfloat16
f32 = jnp.float32


def _cparams(*sem, flags=None):
    return pltpu.CompilerParams(dimension_semantics=sem, vmem_limit_bytes=VMEM_LIMIT_BYTES, flags=flags)


def _rms(x, gain):
    return x * lax.rsqrt(jnp.mean(x * x, axis=-1, keepdims=True) + EPS) * gain


def _silu(x):
    return x / (1.0 + jnp.exp(-x))


def _proj_kernel(h_ref, tab_ref, gin_ref, w1_ref, wvat_ref, gq_ref, wqa_ref, wqb_ref, gkv_ref, wk_ref, wvt_ref,
                 qa_ref, ka_ref, vat_ref, q_ref, k_ref, vt_ref):
    u = _rms(h_ref[...], gin_ref[...]).astype(bf16)
    t = u.shape[0]

    def mm(lo, hi):
        return jnp.dot(u, w1_ref[:, lo:hi], preferred_element_type=f32)

    qa = mm(_C_QA, _C_CQ) * (A_HEAD_DIM ** -0.5 * LOG2E)
    for h in range(A_HEADS):
        qa_ref[h] = qa[:, h * A_HEAD_DIM:(h + 1) * A_HEAD_DIM].astype(bf16)
    vat = lax.dot_general(wvat_ref[...], u, _NT, preferred_element_type=f32)
    vat_ref[:, :A_HEAD_DIM, :] = vat.reshape(A_KV_HEADS, A_HEAD_DIM, t).astype(bf16)
    vat_ref[:, A_HEAD_DIM:, :] = jnp.ones((A_KV_HEADS, A_V_AUG - A_HEAD_DIM, t), bf16)
    mixed = mm(_C_CQ, _C_END)
    cq = mixed[:, :_C_CKV - _C_CQ]
    ckv = mixed[:, _C_CKV - _C_CQ:_C_KA - _C_CQ]
    ka = mixed[:, _C_KA - _C_CQ:_C_KRA - _C_CQ]
    kra = mixed[:, _C_KRA - _C_CQ:_C_KRB - _C_CQ]
    krb = mixed[:, _C_KRB - _C_CQ:]
    for j in range(A_KV_HEADS):
        ka_ref[j] = ka[:, j * A_HEAD_DIM:(j + 1) * A_HEAD_DIM].astype(bf16)
    cos_q = tab_ref[:, 0 * B_QK_PAD:1 * B_QK_PAD]
    sin_q = tab_ref[:, 1 * B_QK_PAD:2 * B_QK_PAD]
    cos_k = tab_ref[:, 2 * B_QK_PAD:3 * B_QK_PAD]
    sin_k = tab_ref[:, 3 * B_QK_PAD:4 * B_QK_PAD]

    cqn = _rms(cq, gq_ref[...]).astype(bf16)
    qa_part = jnp.dot(cqn, wqa_ref[...], preferred_element_type=f32)
    qb_part = jnp.dot(cqn, wqb_ref[...], preferred_element_type=f32)
    for h in range(B_HEADS):
        sl = slice(h * B_QK_PAD, (h + 1) * B_QK_PAD)
        q_ref[h] = (qa_part[:, sl] * cos_q + qb_part[:, sl] * sin_q).astype(bf16)

    ckvn = _rms(ckv, gkv_ref[...]).astype(bf16)
    k_nope = jnp.dot(ckvn, wk_ref[...], preferred_element_type=f32)
    k_rope = kra * cos_k + krb * sin_k
    for h in range(B_HEADS):
        k_ref[h] = (k_nope[:, h * B_QK_PAD:(h + 1) * B_QK_PAD] + k_rope).astype(bf16)
    vt = lax.dot_general(wvt_ref[...], ckvn, _NT, preferred_element_type=f32)
    vt_ref[:, :B_V_DIM, :] = vt.reshape(B_HEADS, B_V_DIM, t).astype(bf16)
    vt_ref[:, B_V_DIM:, :] = jnp.ones((B_HEADS, B_V_AUG - B_V_DIM, t), bf16)


def _layer_spec(layer, shape):
    return pl.BlockSpec((None,) + tuple(shape), lambda i: (layer,) + (0,) * len(shape))


def _proj(h, tab, tab_blocks, lw, layer):
    rows = h.shape[0]
    t = ROW_TILE
    steps = rows // t
    const = functools.partial(_layer_spec, layer)
    return pl.pallas_call(
        _proj_kernel,
        grid=(steps,),
        in_specs=[
            pl.BlockSpec((t, D_MODEL), lambda i: (i, 0)),
            pl.BlockSpec((t, 4 * B_QK_PAD), lambda i: (i % tab_blocks, 0)),
            const((1, D_MODEL)),
            const((D_MODEL, _C_END)),
            const((A_KV_HEADS * A_HEAD_DIM, D_MODEL)),
            const((1, Q_LORA_RANK)),
            const((Q_LORA_RANK, B_HEADS * B_QK_PAD)),
            const((Q_LORA_RANK, B_HEADS * B_QK_PAD)),
            const((1, KV_LORA_RANK)),
            const((KV_LORA_RANK, B_HEADS * B_QK_PAD)),
            const((B_WIDTH, KV_LORA_RANK)),
        ],
        out_specs=[
            pl.BlockSpec((A_HEADS, t, A_HEAD_DIM), lambda i: (0, i, 0)),
            pl.BlockSpec((A_KV_HEADS, t, A_HEAD_DIM), lambda i: (0, i, 0)),
            pl.BlockSpec((A_KV_HEADS, A_V_AUG, t), lambda i: (0, 0, i)),
            pl.BlockSpec((B_HEADS, t, B_QK_PAD), lambda i: (0, i, 0)),
            pl.BlockSpec((B_HEADS, t, B_QK_PAD), lambda i: (0, i, 0)),
            pl.BlockSpec((B_HEADS, B_V_AUG, t), lambda i: (0, 0, i)),
        ],
        out_shape=[
            jax.ShapeDtypeStruct((A_HEADS, rows, A_HEAD_DIM), bf16),
            jax.ShapeDtypeStruct((A_KV_HEADS, rows, A_HEAD_DIM), bf16),
            jax.ShapeDtypeStruct((A_KV_HEADS, A_V_AUG, rows), bf16),
            jax.ShapeDtypeStruct((B_HEADS, rows, B_QK_PAD), bf16),
            jax.ShapeDtypeStruct((B_HEADS, rows, B_QK_PAD), bf16),
            jax.ShapeDtypeStruct((B_HEADS, B_V_AUG, rows), bf16),
        ],
        compiler_params=_cparams("arbitrary"),
        name="proj",
    )(h, tab, lw["gin"], lw["w1"], lw["wvat"], lw["gq"], lw["wqa"], lw["wqb"], lw["gkv"], lw["wk"], lw["wvt"])


def _out_kernel(h_ref, ya_ref, yb_ref, gin_ref, wg_ref, na_ref, nb_ref, w_ref, *rest, final):
    if final:
        nf_ref, o_ref = rest
    else:
        (o_ref,) = rest
    hn = h_ref[...]
    u = _rms(hn, gin_ref[...]).astype(bf16)
    ga = _silu(jnp.dot(u, wg_ref[:, :A_WIDTH], preferred_element_type=f32))
    gb = _silu(jnp.dot(u, wg_ref[:, A_WIDTH:], preferred_element_type=f32))
    ya = _rms(ya_ref[...].astype(f32), na_ref[...]) * ga
    yb = _rms(yb_ref[...].astype(f32), nb_ref[...]) * gb
    hn = hn + jnp.dot(ya.astype(bf16), w_ref[:A_WIDTH, :], preferred_element_type=f32)
    hn = hn + jnp.dot(yb.astype(bf16), w_ref[A_WIDTH:, :], preferred_element_type=f32)
    if final:
        hn = _rms(hn, nf_ref[...])
    o_ref[...] = hn


def _out(h, ya, yb, lw, layer, norm_final=None):
    rows = h.shape[0]
    t = ROW_TILE
    row = lambda width: pl.BlockSpec((t, width), lambda i: (i, 0))
    const = functools.partial(_layer_spec, layer)
    final = norm_final is not None
    in_specs = [row(D_MODEL), row(A_WIDTH), row(B_WIDTH),
                const((1, D_MODEL)), const((D_MODEL, A_WIDTH + B_WIDTH)),
                const((1, A_WIDTH)), const((1, B_WIDTH)), const((A_WIDTH + B_WIDTH, D_MODEL))]
    args = [h, ya, yb, lw["gin"], lw["wg"], lw["na"], lw["nb"], lw["wout"]]
    if final:
        in_specs.append(pl.BlockSpec((1, D_MODEL), lambda i: (0, 0)))
        args.append(norm_final)
    return pl.pallas_call(
        functools.partial(_out_kernel, final=final),
        grid=(rows // t,),
        in_specs=in_specs,
        out_specs=row(D_MODEL),
        out_shape=jax.ShapeDtypeStruct((rows, D_MODEL), f32),
        compiler_params=_cparams("arbitrary"),
        name="outproj",
    )(*args)


def _bias_kernel(tab_ref, idx_ref, o_ref):
    rows = idx_ref.shape[0]
    chunk = math.gcd(rows, BIAS_ROW_CHUNK)

    def body(r, carry):
        sl = pl.ds(pl.multiple_of(r * chunk, chunk), chunk)
        idx = idx_ref[sl, :]
        accs = [jnp.where(idx < 0, NEG, 0.0).astype(f32)] * A_HEADS
        for b in range(N_BUCKETS):
            hit = idx == b
            accs = [jnp.where(hit, tab_ref[b * A_HEADS + h], accs[h]) for h in range(A_HEADS)]
        for h in range(A_HEADS):
            o_ref[h, sl, :] = accs[h]
        return carry

    lax.fori_loop(0, rows // chunk, body, 0)


def _bias_lookup(table_flat, idx):
    return pl.pallas_call(
        _bias_kernel,
        in_specs=[pl.BlockSpec(memory_space=pltpu.SMEM), pl.BlockSpec(idx.shape, lambda: (0, 0))],
        out_specs=pl.BlockSpec((A_HEADS,) + idx.shape, lambda: (0, 0, 0)),
        out_shape=jax.ShapeDtypeStruct((A_HEADS,) + idx.shape, f32),
        compiler_params=pltpu.CompilerParams(vmem_limit_bytes=VMEM_LIMIT_BYTES),
        name="relbias",
    )(table_flat, idx)


def _t5_bucket(rel):
    nb = N_BUCKETS // 2
    max_exact = nb // 2
    ret = jnp.where(rel > 0, nb, 0)
    n = jnp.abs(rel)
    nf = jnp.maximum(n, 1).astype(jnp.float32)
    large = max_exact + (jnp.log(nf / max_exact) / math.log(MAX_DISTANCE / max_exact)
                         * (nb - max_exact)).astype(jnp.int32)
    large = jnp.minimum(large, nb - 1)
    bucket = ret + jnp.where(n < max_exact, n, large)
    return jnp.bitwise_and(bucket, N_BUCKETS - 1)


def _bias_tables(rel_bias_table, nblk):
    table_flat = (rel_bias_table.astype(f32) * LOG2E).reshape(-1)
    i = jnp.arange(BLOCK, dtype=jnp.int32)[:, None]
    j = jnp.arange(3 * BLOCK, dtype=jnp.int32)[None, :]
    rel = j - i - BLOCK
    band = jnp.where(jnp.abs(rel) <= WINDOW, _t5_bucket(rel), -1)
    interior = (_bias_lookup(table_flat, band.T)
                .reshape(A_KV_HEADS, A_GROUP, 3 * BLOCK, BLOCK)
                .transpose(0, 2, 1, 3).reshape(A_KV_HEADS, 3 * BLOCK, A_GROUP * BLOCK))
    key = jnp.arange(3 * BLOCK, dtype=jnp.int32)[None, :, None]
    first = jnp.where(key < BLOCK, NEG, interior)
    last = jnp.where(key >= 2 * BLOCK, NEG, interior)
    bias_band = jnp.stack([first, interior, last])

    n = jnp.arange(nblk, dtype=jnp.int32)[:, None, None]
    k = jnp.arange(N_META, dtype=jnp.int32)[None, :, None]
    qi = jnp.arange(BLOCK, dtype=jnp.int32)[None, None, :]
    rel_m = k - (N_META + n * BLOCK + qi)
    idx_m = _t5_bucket(rel_m).reshape(nblk * N_META, BLOCK)
    bias_meta = (_bias_lookup(table_flat, idx_m)
                 .reshape(A_KV_HEADS, A_GROUP, nblk, N_META, BLOCK)
                 .transpose(2, 0, 3, 1, 4).reshape(nblk, A_KV_HEADS, N_META, A_GROUP * BLOCK))

    qp = jnp.arange(N_META, dtype=jnp.int32)[:, None]
    kp = jnp.arange(N_META + BLOCK, dtype=jnp.int32)[None, :]
    rel_q = kp - qp
    idx_q = jnp.where(jnp.abs(rel_q) <= WINDOW, _t5_bucket(rel_q), -1)
    bias_q = _bias_lookup(table_flat, idx_q)
    return bias_band, bias_meta, bias_q[:, :, :N_META], bias_q[:, :, N_META:]


def _win_kernel(sink_ref, q_ref, kp_ref, kc_ref, kn_ref, vp_ref, vc_ref, vn_ref, kam_ref, vam_ref,
                bias_ref, bm_ref, o_ref):
    w = WIN_BLOCKS
    i, ntile = pl.program_id(1), pl.num_programs(1)

    def qrows(blk):
        return slice(blk * BLOCK, (blk + 1) * BLOCK)

    def band(blk, prev_ref, cur_ref, next_ref, kvh, axis):
        take = lambda ref, lo, hi: ref[kvh, lo:hi, :] if axis == 0 else ref[kvh, :, lo:hi]
        parts = [take(cur_ref, max(blk - 1, 0) * BLOCK, min(blk + 2, w) * BLOCK)]
        if blk == 0:
            parts.insert(0, take(prev_ref, (w - 1) * BLOCK, w * BLOCK))
        if blk == w - 1:
            parts.append(take(next_ref, 0, BLOCK))
        return jnp.concatenate(parts, axis=axis)

    def variant(blk):
        v = 1
        if blk == 0:
            v = jnp.where(i == 0, 0, v)
        if blk == w - 1:
            v = jnp.where(i == ntile - 1, 2, v)
        return v

    def scores(blk, kvh):
        kb = band(blk, kp_ref, kc_ref, kn_ref, kvh, 0)
        q4 = q_ref[A_GROUP * kvh:A_GROUP * (kvh + 1), qrows(blk), :].reshape(A_GROUP * BLOCK, A_HEAD_DIM)
        s = lax.dot_general(kb, q4, _NT, preferred_element_type=f32) + bias_ref[variant(blk), kvh]
        sm = lax.dot_general(kam_ref[kvh], q4, _NT, preferred_element_type=f32) + bm_ref[blk, kvh]
        sink = sink_ref[kvh]
        m = jnp.maximum(jnp.maximum(jnp.max(s, axis=0, keepdims=True), jnp.max(sm, axis=0, keepdims=True)), sink)
        return s, sm, m

    def attend(blk, kvh, s, sm, m):
        vb = band(blk, vp_ref, vc_ref, vn_ref, kvh, 1)
        p = jnp.exp2(s - m).astype(bf16)
        pm = jnp.exp2(sm - m).astype(bf16)
        acc = (jnp.dot(vb, p, preferred_element_type=f32)
               + jnp.dot(vam_ref[0, kvh], pm, preferred_element_type=f32))
        den = acc[A_HEAD_DIM:A_HEAD_DIM + 1] + jnp.exp2(sink_ref[kvh] - m)
        return acc[:A_HEAD_DIM] * (1.0 / den)

    units = [(blk, kvh) for blk in range(w) for kvh in range(A_KV_HEADS)]
    outs = {}
    pending = scores(*units[0])
    for u, unit in enumerate(units):
        nxt = scores(*units[u + 1]) if u + 1 < len(units) else None
        outs[unit] = attend(*unit, *pending)
        pending = nxt
        blk, kvh = unit
        if kvh == A_KV_HEADS - 1:
            ot = jnp.concatenate([outs[(blk, j)][:, g * BLOCK:(g + 1) * BLOCK]
                                  for j in range(A_KV_HEADS) for g in range(A_GROUP)], axis=0)
            o_ref[qrows(blk), :] = ot.T.astype(o_ref.dtype)


def _win(sink_rows, qa, ka, vat, kam, vatm, bias_band, bias_meta, batch, seq):
    tb = WIN_BLOCKS * BLOCK
    ntile = seq // tb
    prev_ = lambda b, i: b * ntile + jnp.maximum(i - 1, 0)
    cur_ = lambda b, i: b * ntile + i
    next_ = lambda b, i: b * ntile + jnp.minimum(i + 1, ntile - 1)
    kspec = lambda f: pl.BlockSpec((A_KV_HEADS, tb, A_HEAD_DIM), lambda b, i: (0, f(b, i), 0))
    vspec = lambda f: pl.BlockSpec((A_KV_HEADS, A_V_AUG, tb), lambda b, i: (0, 0, f(b, i)))
    return pl.pallas_call(
        _win_kernel,
        grid=(batch, ntile),
        in_specs=[
            pl.BlockSpec((A_KV_HEADS, 1, A_GROUP * BLOCK), lambda b, i: (0, 0, 0)),
            pl.BlockSpec((A_HEADS, tb, A_HEAD_DIM), lambda b, i: (0, cur_(b, i), 0)),
            kspec(prev_), kspec(cur_), kspec(next_),
            vspec(prev_), vspec(cur_), vspec(next_),
            pl.BlockSpec((A_KV_HEADS, N_META, A_HEAD_DIM), lambda b, i: (0, b, 0)),
            pl.BlockSpec((1, A_KV_HEADS, A_V_AUG, N_META), lambda b, i: (b, 0, 0, 0)),
            pl.BlockSpec((3, A_KV_HEADS, 3 * BLOCK, A_GROUP * BLOCK), lambda b, i: (0, 0, 0, 0)),
            pl.BlockSpec((WIN_BLOCKS, A_KV_HEADS, N_META, A_GROUP * BLOCK), lambda b, i: (i, 0, 0, 0)),
        ],
        out_specs=pl.BlockSpec((tb, A_WIDTH), lambda b, i: (cur_(b, i), 0)),
        out_shape=jax.ShapeDtypeStruct((batch * seq, A_WIDTH), bf16),
        compiler_params=_cparams("arbitrary", "arbitrary"),
        name="win",
    )(sink_rows, qa, ka, ka, ka, vat, vat, vat, kam, vatm, bias_band, bias_meta)


def _win_meta_kernel(sink_ref, q_ref, ka_ref, vat_ref, kam_ref, vatm_ref, biasm_ref, biasr_ref, o_ref):
    for kvh in range(A_KV_HEADS):
        q4 = q_ref[A_GROUP * kvh:A_GROUP * (kvh + 1)].reshape(A_GROUP * N_META, A_HEAD_DIM)
        sm = lax.dot_general(q4, kam_ref[kvh], _NT, preferred_element_type=f32)
        sr = lax.dot_general(q4, ka_ref[kvh], _NT, preferred_element_type=f32)
        for g in range(A_GROUP):
            h = A_GROUP * kvh + g
            rows = slice(g * N_META, (g + 1) * N_META)
            smg = sm[rows] + biasm_ref[h]
            srg = sr[rows] + biasr_ref[h]
            sink = sink_ref[h]
            m = jnp.maximum(jnp.maximum(jnp.max(smg, axis=-1, keepdims=True),
                                        jnp.max(srg, axis=-1, keepdims=True)), sink)
            pm = jnp.exp2(smg - m).astype(bf16)
            pr = jnp.exp2(srg - m).astype(bf16)
            o = (lax.dot_general(pm, vatm_ref[0, kvh], _NT, preferred_element_type=f32)
                 + lax.dot_general(pr, vat_ref[kvh], _NT, preferred_element_type=f32))
            den = o[:, A_HEAD_DIM:A_HEAD_DIM + 1] + jnp.exp2(sink - m)
            o_ref[:, h * A_HEAD_DIM:(h + 1) * A_HEAD_DIM] = (o[:, :A_HEAD_DIM] * (1.0 / den)).astype(o_ref.dtype)


def _win_meta(sink, qam, ka, vat, kam, vatm, bias_mm, bias_mr, batch, seq):
    nblk = seq // BLOCK
    const3 = lambda shape: pl.BlockSpec(shape, lambda b: (0, 0, 0))
    return pl.pallas_call(
        _win_meta_kernel,
        grid=(batch,),
        in_specs=[
            pl.BlockSpec(memory_space=pltpu.SMEM),
            pl.BlockSpec((A_HEADS, N_META, A_HEAD_DIM), lambda b: (0, b, 0)),
            pl.BlockSpec((A_KV_HEADS, BLOCK, A_HEAD_DIM), lambda b: (0, b * nblk, 0)),
            pl.BlockSpec((A_KV_HEADS, A_V_AUG, BLOCK), lambda b: (0, 0, b * nblk)),
            pl.BlockSpec((A_KV_HEADS, N_META, A_HEAD_DIM), lambda b: (0, b, 0)),
            pl.BlockSpec((1, A_KV_HEADS, A_V_AUG, N_META), lambda b: (b, 0, 0, 0)),
            const3((A_HEADS, N_META, N_META)),
            const3((A_HEADS, N_META, BLOCK)),
        ],
        out_specs=pl.BlockSpec((N_META, A_WIDTH), lambda b: (b, 0)),
        out_shape=jax.ShapeDtypeStruct((batch * N_META, A_WIDTH), bf16),
        compiler_params=_cparams("arbitrary"),
        name="win_meta",
    )(sink, qam, ka, vat, kam, vatm, bias_mm, bias_mr)


def _mla_kernel(q_ref, k_ref, km_ref, vt_ref, vtm_ref, o_ref, acc_ref, s0_ref, s1_ref, sm0_ref, sm1_ref):
    s_bufs, sm_bufs = (s0_ref, s1_ref), (sm0_ref, sm1_ref)
    nsub, _, _, tq = acc_ref.shape
    units = nsub * B_HEADS
    seq = k_ref.shape[1]
    chunks = [(c * MLA_KEY_CHUNK, (c + 1) * MLA_KEY_CHUNK) for c in range(seq // MLA_KEY_CHUNK)]

    def unit(u):
        return u % B_HEADS, u // B_HEADS

    def step(t, slot, m, *, score_next=True, attend_cur=True):
        if score_next:
            hn, subn = unit(t + 1)
            q = q_ref[hn, pl.ds(pl.multiple_of(subn * tq, tq), tq), :]
            sm = lax.dot_general(km_ref[hn], q, _NT, preferred_element_type=f32)
            sm_bufs[1 - slot][...] = sm
            m_next = jnp.max(sm, axis=0, keepdims=True)
        if attend_cur:
            h, sub = unit(t)
            pm = jnp.exp2(sm_bufs[slot][...] - m).astype(bf16)
            acc = jnp.dot(vtm_ref[0, h], pm, preferred_element_type=f32)
        for lo, hi in chunks:
            if score_next:
                s = lax.dot_general(k_ref[hn, lo:hi, :], q, _NT, preferred_element_type=f32)
                s_bufs[1 - slot][lo:hi, :] = s
                m_next = jnp.maximum(m_next, jnp.max(s, axis=0, keepdims=True))
            if attend_cur:
                p = jnp.exp2(s_bufs[slot][lo:hi, :] - m).astype(bf16)
                acc = acc + jnp.dot(vt_ref[h, :, lo:hi], p, preferred_element_type=f32)
        if attend_cur:
            acc_ref[sub, h] = acc[:B_V_DIM] * (1.0 / acc[B_V_DIM:B_V_DIM + 1])
        return m_next if score_next else None

    def steps(j, m):
        for i in range(MLA_STEPS_PER_ITER):
            m = step(MLA_STEPS_PER_ITER * j + i, i % 2, m)
        return m

    assert MLA_STEPS_PER_ITER % 2 == 0 and units % 2 == 0
    iters = (units - 1) // MLA_STEPS_PER_ITER
    m = step(-1, 1, None, attend_cur=False)
    m = lax.fori_loop(0, iters, steps, m)
    for t in range(iters * MLA_STEPS_PER_ITER, units - 1):
        m = step(t, t % 2, m)
    step(units - 1, 1, m, score_next=False)
    for sub in range(nsub):
        o_ref[sub * tq:(sub + 1) * tq, :] = acc_ref[sub].reshape(B_WIDTH, tq).T.astype(o_ref.dtype)


def _mla(q, k, km, vt, vtm, batch, seq):
    tq = MLA_Q_TILE
    tb = MLA_Q_TILE * MLA_SUBTILES
    nq = seq // tb
    return pl.pallas_call(
        _mla_kernel,
        grid=(batch, nq),
        in_specs=[
            pl.BlockSpec((B_HEADS, tb, B_QK_PAD), lambda b, i: (0, b * nq + i, 0)),
            pl.BlockSpec((B_HEADS, seq, B_QK_PAD), lambda b, i: (0, b, 0)),
            pl.BlockSpec((B_HEADS, N_META, B_QK_PAD), lambda b, i: (0, b, 0)),
            pl.BlockSpec((B_HEADS, B_V_AUG, seq), lambda b, i: (0, 0, b)),
            pl.BlockSpec((1, B_HEADS, B_V_AUG, N_META), lambda b, i: (b, 0, 0, 0)),
        ],
        out_specs=pl.BlockSpec((tb, B_WIDTH), lambda b, i: (b * nq + i, 0)),
        out_shape=jax.ShapeDtypeStruct((batch * seq, B_WIDTH), bf16),
        scratch_shapes=([pltpu.VMEM((MLA_SUBTILES, B_HEADS, B_V_DIM, tq), f32)]
                        + [pltpu.VMEM((seq, tq), f32)] * 2 + [pltpu.VMEM((N_META, tq), f32)] * 2),
        compiler_params=_cparams("arbitrary", "arbitrary"),
        name="mla",
    )(q, k, km, vt, vtm)


def _mla_meta_kernel(q_ref, k_ref, km_ref, vt_ref, vtm_ref, o_ref):
    for h in range(B_HEADS):
        q = q_ref[h]
        s = lax.dot_general(q, k_ref[h], _NT, preferred_element_type=f32)
        sm = lax.dot_general(q, km_ref[h], _NT, preferred_element_type=f32)
        m = jnp.maximum(jnp.max(s, axis=-1, keepdims=True), jnp.max(sm, axis=-1, keepdims=True))
        p = jnp.exp2(s - m).astype(bf16)
        pm = jnp.exp2(sm - m).astype(bf16)
        o = (lax.dot_general(p, vt_ref[h], _NT, preferred_element_type=f32)
             + lax.dot_general(pm, vtm_ref[0, h], _NT, preferred_element_type=f32))
        o_ref[:, h * B_V_DIM:(h + 1) * B_V_DIM] = (
            o[:, :B_V_DIM] * (1.0 / o[:, B_V_DIM:B_V_DIM + 1])).astype(o_ref.dtype)


def _mla_meta(qm, k, km, vt, vtm, batch, seq):
    return pl.pallas_call(
        _mla_meta_kernel,
        grid=(batch,),
        in_specs=[
            pl.BlockSpec((B_HEADS, N_META, B_QK_PAD), lambda b: (0, b, 0)),
            pl.BlockSpec((B_HEADS, seq, B_QK_PAD), lambda b: (0, b, 0)),
            pl.BlockSpec((B_HEADS, N_META, B_QK_PAD), lambda b: (0, b, 0)),
            pl.BlockSpec((B_HEADS, B_V_AUG, seq), lambda b: (0, 0, b)),
            pl.BlockSpec((1, B_HEADS, B_V_AUG, N_META), lambda b: (b, 0, 0, 0)),
        ],
        out_specs=pl.BlockSpec((N_META, B_WIDTH), lambda b: (b, 0)),
        out_shape=jax.ShapeDtypeStruct((batch * N_META, B_WIDTH), bf16),
        compiler_params=_cparams("arbitrary"),
        name="mla_meta",
    )(qm, k, km, vt, vtm)


def _rope_tables(pos):
    half = B_ROPE_DIM // 2
    freqs = ROPE_THETA ** (-jnp.arange(half, dtype=jnp.float32) / half)
    ang = pos.astype(jnp.float32)[:, None] * freqs[None, :]
    cos, sin = jnp.cos(ang), jnp.sin(ang)
    rows = pos.shape[0]
    c_mla = (B_NOPE_DIM + B_ROPE_DIM) ** -0.5 * LOG2E
    z = lambda w: jnp.zeros((rows, w), f32)
    tail = B_QK_PAD - B_NOPE_DIM - B_ROPE_DIM
    cos_q = jnp.concatenate([jnp.full((rows, B_NOPE_DIM), c_mla, f32), c_mla * cos, c_mla * cos, z(tail)], axis=1)
    sin_q = jnp.concatenate([z(B_NOPE_DIM), c_mla * sin, c_mla * sin, z(tail)], axis=1)
    cos_k = jnp.concatenate([z(B_NOPE_DIM), cos, cos, z(tail)], axis=1)
    sin_k = jnp.concatenate([z(B_NOPE_DIM), sin, sin, z(tail)], axis=1)
    return jnp.concatenate([cos_q, sin_q, cos_k, sin_k], axis=1)


def _rot_cols(w):
    half = w.shape[-1] // 2
    return jnp.concatenate([-w[..., half:], w[..., :half]], axis=-1)


def _stacked_weights(norm_in, w_in, norm_q_lat, w_uq, norm_kv_lat, w_ukv, norm_out_a, norm_out_b, w_out):
    depth, d, _ = w_in.shape
    sizes = (A_WIDTH, A_KV_HEADS * A_HEAD_DIM, A_KV_HEADS * A_HEAD_DIM, A_WIDTH,
             Q_LORA_RANK, KV_LORA_RANK, B_ROPE_DIM, B_WIDTH)
    offs = [sum(sizes[:i]) for i in range(len(sizes) + 1)]
    qa, ka, va, ga, cq, ckv, kr, gb = (w_in[..., offs[i]:offs[i + 1]] for i in range(len(sizes)))
    tail = B_QK_PAD - B_NOPE_DIM - B_ROPE_DIM
    zeros = lambda *shape: jnp.zeros(shape, w_in.dtype)
    pad_k = lambda w: jnp.concatenate([zeros(depth, d, B_NOPE_DIM), w, zeros(depth, d, tail)], axis=-1)
    w1 = jnp.concatenate([qa, cq, ckv, ka, pad_k(kr), pad_k(_rot_cols(kr))], axis=-1).astype(bf16)

    r = w_uq.shape[1]
    uq = w_uq.reshape(depth, r, B_HEADS, B_NOPE_DIM + B_ROPE_DIM)
    nope, rope = uq[..., :B_NOPE_DIM], uq[..., B_NOPE_DIM:]
    wqa = jnp.concatenate([nope, rope, zeros(depth, r, B_HEADS, tail)], axis=-1)
    wqb = jnp.concatenate([jnp.zeros_like(nope), _rot_cols(rope), zeros(depth, r, B_HEADS, tail)], axis=-1)

    rk = w_ukv.shape[1]
    ukv = w_ukv.reshape(depth, rk, B_HEADS, B_NOPE_DIM + B_V_DIM)
    k_nope, v = ukv[..., :B_NOPE_DIM], ukv[..., B_NOPE_DIM:]
    wk = jnp.concatenate([k_nope, zeros(depth, rk, B_HEADS, B_QK_PAD - B_NOPE_DIM)], axis=-1)
    row = lambda g: g.astype(f32)[:, None, :]
    return {
        "gin": row(norm_in),
        "w1": w1,
        "wg": jnp.concatenate([ga, gb], axis=-1).astype(bf16),
        "wvat": jnp.swapaxes(va, 1, 2).astype(bf16),
        "gq": row(norm_q_lat),
        "wqa": wqa.reshape(depth, r, B_HEADS * B_QK_PAD).astype(bf16),
        "wqb": wqb.reshape(depth, r, B_HEADS * B_QK_PAD).astype(bf16),
        "gkv": row(norm_kv_lat),
        "wk": wk.reshape(depth, rk, B_HEADS * B_QK_PAD).astype(bf16),
        "wvt": jnp.swapaxes(v.reshape(depth, rk, B_WIDTH), 1, 2).astype(bf16),
        "na": row(norm_out_a),
        "nb": row(norm_out_b),
        "wout": w_out.astype(bf16),
    }


def kernel(x, meta_tokens, rel_bias_table, norm_in, w_in, sink_a, norm_q_lat, w_uq, norm_kv_lat, w_ukv,
           norm_out_a, norm_out_b, w_out, norm_final):
    batch, seq, d = x.shape
    depth = w_in.shape[0]
    assert d == D_MODEL and seq % (MLA_Q_TILE * MLA_SUBTILES) == 0 and (batch * N_META) % ROW_TILE == 0

    h_real = x.reshape(batch * seq, d).astype(f32)
    h_meta = jnp.tile(meta_tokens.astype(f32), (batch, 1))
    tab_real = _rope_tables(N_META + jnp.arange(seq))
    tab_meta = _rope_tables(jnp.arange(batch * N_META) % N_META)
    bias_band, bias_meta, bias_mm, bias_mr = _bias_tables(rel_bias_table, seq // BLOCK)

    lw = _stacked_weights(norm_in, w_in, norm_q_lat, w_uq, norm_kv_lat, w_ukv, norm_out_a, norm_out_b, w_out)
    sinks = sink_a.astype(f32) * LOG2E
    sinks_rows = jnp.repeat(sinks.reshape(depth, A_KV_HEADS, A_GROUP), BLOCK, axis=2)[:, :, None, :]

    out = None
    for i in range(depth):
        last = i == depth - 1
        sink, sink_rows = sinks[i], sinks_rows[i]
        qa, ka, vat, q, k, vt = _proj(h_real, tab_real, seq // ROW_TILE, lw, i)
        qam, kam, vatm, qm, km, vtm = _proj(h_meta, tab_meta, 1, lw, i)
        vtm_b = vtm.reshape(B_HEADS, B_V_AUG, batch, N_META).transpose(2, 0, 1, 3)
        vatm_b = vatm.reshape(A_KV_HEADS, A_V_AUG, batch, N_META).transpose(2, 0, 1, 3)

        ya = _win(sink_rows, qa, ka, vat, kam, vatm_b, bias_band, bias_meta, batch, seq)
        yb = _mla(q, k, km, vt, vtm_b, batch, seq)
        if last:
            out = _out(h_real, ya, yb, lw, i, norm_final.astype(f32)[None, :])
        else:
            yam = _win_meta(sink, qam, ka, vat, kam, vatm_b, bias_mm, bias_mr, batch, seq)
            ybm = _mla_meta(qm, k, km, vt, vtm_b, batch, seq)
            h_real = _out(h_real, ya, yb, lw, i)
            h_meta = _out(h_meta, yam, ybm, lw, i)
    return out.reshape(batch, seq, d).astype(x.dtype)
```

```python
import functools
import math

import jax
import jax.numpy as jnp
from jax import lax
from jax.experimental import pallas as pl
from jax.experimental.pallas import tpu as pltpu

D_MODEL = 1024
N_META = 16
BLOCK = 128
WINDOW = 128
A_HEADS = 8
A_KV_HEADS = 2
A_GROUP = A_HEADS // A_KV_HEADS
A_HEAD_DIM = 64
A_WIDTH = A_HEADS * A_HEAD_DIM
B_HEADS = 8
B_NOPE_DIM = 64
B_ROPE_DIM = 32
B_V_DIM = 64
B_WIDTH = B_HEADS * B_V_DIM
B_QK_PAD = 128
BF16_SUBLANES = 16
B_V_AUG = B_V_DIM + BF16_SUBLANES
A_V_AUG = A_HEAD_DIM + BF16_SUBLANES
Q_LORA_RANK = 256
KV_LORA_RANK = 128
N_BUCKETS = 32
MAX_DISTANCE = 128
ROPE_THETA = 10000.0
EPS = 1e-6

LOG2E = math.log2(math.e)
NEG = -1e30
ROW_TILE = 512
MLA_Q_TILE = 256
MLA_SUBTILES = 8
MLA_STEPS_PER_ITER = 4
MLA_KEY_CHUNK = 1024
WIN_BLOCKS = 8
BIAS_ROW_CHUNK = 32
VMEM_LIMIT_BYTES = 56 * 1024 * 1024

_C_QA = 0
_C_CQ = _C_QA + A_WIDTH
_C_CKV = _C_CQ + Q_LORA_RANK
_C_KA = _C_CKV + KV_LORA_RANK
_C_KRA = _C_KA + A_KV_HEADS * A_HEAD_DIM
_C_KRB = _C_KRA + B_QK_PAD
_C_END = _C_KRB + B_QK_PAD

_NT = (((1,), (1,)), ((), ()))

bf16 = jnp.bfloat16
f32 = jnp.float32


def _cparams(*sem, flags=None):
    return pltpu.CompilerParams(dimension_semantics=sem, vmem_limit_bytes=VMEM_LIMIT_BYTES, flags=flags)


def _rms(x, gain):
    return x * lax.rsqrt(jnp.mean(x * x, axis=-1, keepdims=True) + EPS) * gain


def _silu(x):
    return x / (1.0 + jnp.exp(-x))


def _proj_kernel(h_ref, tab_ref, gin_ref, w1_ref, wvat_ref, gq_ref, wqa_ref, wqb_ref, gkv_ref, wk_ref, wvt_ref,
                 qa_ref, ka_ref, vat_ref, q_ref, k_ref, vt_ref):
    u = _rms(h_ref[...], gin_ref[...]).astype(bf16)
    t = u.shape[0]

    def mm(lo, hi):
        return jnp.dot(u, w1_ref[:, lo:hi], preferred_element_type=f32)

    qa = mm(_C_QA, _C_CQ) * (A_HEAD_DIM ** -0.5 * LOG2E)
    for h in range(A_HEADS):
        qa_ref[h] = qa[:, h * A_HEAD_DIM:(h + 1) * A_HEAD_DIM].astype(bf16)
    vat = lax.dot_general(wvat_ref[...], u, _NT, preferred_element_type=f32)
    vat_ref[:, :A_HEAD_DIM, :] = vat.reshape(A_KV_HEADS, A_HEAD_DIM, t).astype(bf16)
    vat_ref[:, A_HEAD_DIM:, :] = jnp.ones((A_KV_HEADS, A_V_AUG - A_HEAD_DIM, t), bf16)
    mixed = mm(_C_CQ, _C_END)
    cq = mixed[:, :_C_CKV - _C_CQ]
    ckv = mixed[:, _C_CKV - _C_CQ:_C_KA - _C_CQ]
    ka = mixed[:, _C_KA - _C_CQ:_C_KRA - _C_CQ]
    kra = mixed[:, _C_KRA - _C_CQ:_C_KRB - _C_CQ]
    krb = mixed[:, _C_KRB - _C_CQ:]
    for j in range(A_KV_HEADS):
        ka_ref[j] = ka[:, j * A_HEAD_DIM:(j + 1) * A_HEAD_DIM].astype(bf16)
    tab_rows = pl.ds(pl.multiple_of((pl.program_id(0) % (tab_ref.shape[0] // t)) * t, t), t)
    cos_q = tab_ref[tab_rows, 0 * B_QK_PAD:1 * B_QK_PAD]
    sin_q = tab_ref[tab_rows, 1 * B_QK_PAD:2 * B_QK_PAD]
    cos_k = tab_ref[tab_rows, 2 * B_QK_PAD:3 * B_QK_PAD]
    sin_k = tab_ref[tab_rows, 3 * B_QK_PAD:4 * B_QK_PAD]

    cqn = _rms(cq, gq_ref[...]).astype(bf16)
    qa_part = jnp.dot(cqn, wqa_ref[...], preferred_element_type=f32)
    qb_part = jnp.dot(cqn, wqb_ref[...], preferred_element_type=f32)
    for h in range(B_HEADS):
        sl = slice(h * B_QK_PAD, (h + 1) * B_QK_PAD)
        q_ref[h] = (qa_part[:, sl] * cos_q + qb_part[:, sl] * sin_q).astype(bf16)

    ckvn = _rms(ckv, gkv_ref[...]).astype(bf16)
    k_nope = jnp.dot(ckvn, wk_ref[...], preferred_element_type=f32)
    k_rope = kra * cos_k + krb * sin_k
    for h in range(B_HEADS):
        k_ref[h] = (k_nope[:, h * B_QK_PAD:(h + 1) * B_QK_PAD] + k_rope).astype(bf16)
    vt = lax.dot_general(wvt_ref[...], ckvn, _NT, preferred_element_type=f32)
    vt_ref[:, :B_V_DIM, :] = vt.reshape(B_HEADS, B_V_DIM, t).astype(bf16)
    vt_ref[:, B_V_DIM:, :] = jnp.ones((B_HEADS, B_V_AUG - B_V_DIM, t), bf16)


def _layer_spec(layer, shape):
    return pl.BlockSpec((None,) + tuple(shape), lambda i: (layer,) + (0,) * len(shape))


def _proj(h, tab, lw, layer):
    rows = h.shape[0]
    t = min(ROW_TILE, rows)
    steps = rows // t
    assert tab.shape[0] % t == 0
    const = functools.partial(_layer_spec, layer)
    return pl.pallas_call(
        _proj_kernel,
        grid=(steps,),
        in_specs=[
            pl.BlockSpec((t, D_MODEL), lambda i: (i, 0)),
            pl.BlockSpec(tab.shape, lambda i: (0, 0)),
            const((1, D_MODEL)),
            const((D_MODEL, _C_END)),
            const((A_KV_HEADS * A_HEAD_DIM, D_MODEL)),
            const((1, Q_LORA_RANK)),
            const((Q_LORA_RANK, B_HEADS * B_QK_PAD)),
            const((Q_LORA_RANK, B_HEADS * B_QK_PAD)),
            const((1, KV_LORA_RANK)),
            const((KV_LORA_RANK, B_HEADS * B_QK_PAD)),
            const((B_WIDTH, KV_LORA_RANK)),
        ],
        out_specs=[
            pl.BlockSpec((A_HEADS, t, A_HEAD_DIM), lambda i: (0, i, 0)),
            pl.BlockSpec((A_KV_HEADS, t, A_HEAD_DIM), lambda i: (0, i, 0)),
            pl.BlockSpec((A_KV_HEADS, A_V_AUG, t), lambda i: (0, 0, i)),
            pl.BlockSpec((B_HEADS, t, B_QK_PAD), lambda i: (0, i, 0)),
            pl.BlockSpec((B_HEADS, t, B_QK_PAD), lambda i: (0, i, 0)),
            pl.BlockSpec((B_HEADS, B_V_AUG, t), lambda i: (0, 0, i)),
        ],
        out_shape=[
            jax.ShapeDtypeStruct((A_HEADS, rows, A_HEAD_DIM), bf16),
            jax.ShapeDtypeStruct((A_KV_HEADS, rows, A_HEAD_DIM), bf16),
            jax.ShapeDtypeStruct((A_KV_HEADS, A_V_AUG, rows), bf16),
            jax.ShapeDtypeStruct((B_HEADS, rows, B_QK_PAD), bf16),
            jax.ShapeDtypeStruct((B_HEADS, rows, B_QK_PAD), bf16),
            jax.ShapeDtypeStruct((B_HEADS, B_V_AUG, rows), bf16),
        ],
        compiler_params=_cparams("arbitrary"),
        name="proj",
    )(h, tab, lw["gin"], lw["w1"], lw["wvat"], lw["gq"], lw["wqa"], lw["wqb"], lw["gkv"], lw["wk"], lw["wvt"])


def _out_kernel(h_ref, ya_ref, yb_ref, gin_ref, wg_ref, na_ref, nb_ref, w_ref, *rest, final):
    if final:
        nf_ref, o_ref = rest
    else:
        (o_ref,) = rest
    hn = h_ref[...]
    u = _rms(hn, gin_ref[...]).astype(bf16)
    ga = _silu(jnp.dot(u, wg_ref[:, :A_WIDTH], preferred_element_type=f32))
    gb = _silu(jnp.dot(u, wg_ref[:, A_WIDTH:], preferred_element_type=f32))
    ya = _rms(ya_ref[...].astype(f32), na_ref[...]) * ga
    yb = _rms(yb_ref[...].astype(f32), nb_ref[...]) * gb
    hn = hn + jnp.dot(ya.astype(bf16), w_ref[:A_WIDTH, :], preferred_element_type=f32)
    hn = hn + jnp.dot(yb.astype(bf16), w_ref[A_WIDTH:, :], preferred_element_type=f32)
    if final:
        hn = _rms(hn, nf_ref[...])
    o_ref[...] = hn


def _out(h, ya, yb, lw, layer, norm_final=None):
    rows = h.shape[0]
    t = min(ROW_TILE, rows)
    row = lambda width: pl.BlockSpec((t, width), lambda i: (i, 0))
    const = functools.partial(_layer_spec, layer)
    final = norm_final is not None
    in_specs = [row(D_MODEL), row(A_WIDTH), row(B_WIDTH),
                const((1, D_MODEL)), const((D_MODEL, A_WIDTH + B_WIDTH)),
                const((1, A_WIDTH)), const((1, B_WIDTH)), const((A_WIDTH + B_WIDTH, D_MODEL))]
    args = [h, ya, yb, lw["gin"], lw["wg"], lw["na"], lw["nb"], lw["wout"]]
    if final:
        in_specs.append(pl.BlockSpec((1, D_MODEL), lambda i: (0, 0)))
        args.append(norm_final)
    return pl.pallas_call(
        functools.partial(_out_kernel, final=final),
        grid=(rows // t,),
        in_specs=in_specs,
        out_specs=row(D_MODEL),
        out_shape=jax.ShapeDtypeStruct((rows, D_MODEL), f32),
        compiler_params=_cparams("arbitrary"),
        name="outproj",
    )(*args)


def _bias_kernel(tab_ref, idx_ref, o_ref):
    rows = idx_ref.shape[0]
    chunk = math.gcd(rows, BIAS_ROW_CHUNK)

    def body(r, carry):
        sl = pl.ds(pl.multiple_of(r * chunk, chunk), chunk)
        idx = idx_ref[sl, :]
        accs = [jnp.where(idx < 0, NEG, 0.0).astype(f32)] * A_HEADS
        for b in range(N_BUCKETS):
            hit = idx == b
            accs = [jnp.where(hit, tab_ref[b * A_HEADS + h], accs[h]) for h in range(A_HEADS)]
        for h in range(A_HEADS):
            o_ref[h, sl, :] = accs[h]
        return carry

    lax.fori_loop(0, rows // chunk, body, 0)


def _bias_lookup(table_flat, idx):
    return pl.pallas_call(
        _bias_kernel,
        in_specs=[pl.BlockSpec(memory_space=pltpu.SMEM), pl.BlockSpec(idx.shape, lambda: (0, 0))],
        out_specs=pl.BlockSpec((A_HEADS,) + idx.shape, lambda: (0, 0, 0)),
        out_shape=jax.ShapeDtypeStruct((A_HEADS,) + idx.shape, f32),
        compiler_params=pltpu.CompilerParams(vmem_limit_bytes=VMEM_LIMIT_BYTES),
        name="relbias",
    )(table_flat, idx)


def _t5_bucket(rel):
    nb = N_BUCKETS // 2
    max_exact = nb // 2
    ret = jnp.where(rel > 0, nb, 0)
    n = jnp.abs(rel)
    nf = jnp.maximum(n, 1).astype(jnp.float32)
    large = max_exact + (jnp.log(nf / max_exact) / math.log(MAX_DISTANCE / max_exact)
                         * (nb - max_exact)).astype(jnp.int32)
    large = jnp.minimum(large, nb - 1)
    bucket = ret + jnp.where(n < max_exact, n, large)
    return jnp.bitwise_and(bucket, N_BUCKETS - 1)


def _bias_tables(rel_bias_table, nblk):
    table_flat = (rel_bias_table.astype(f32) * LOG2E).reshape(-1)
    i = jnp.arange(BLOCK, dtype=jnp.int32)[:, None]
    j = jnp.arange(3 * BLOCK, dtype=jnp.int32)[None, :]
    rel = j - i - BLOCK
    band = jnp.where(jnp.abs(rel) <= WINDOW, _t5_bucket(rel), -1)
    interior = (_bias_lookup(table_flat, band.T)
                .reshape(A_KV_HEADS, A_GROUP, 3 * BLOCK, BLOCK)
                .transpose(0, 2, 1, 3).reshape(A_KV_HEADS, 3 * BLOCK, A_GROUP * BLOCK))
    key = jnp.arange(3 * BLOCK, dtype=jnp.int32)[None, :, None]
    first = jnp.where(key < BLOCK, NEG, interior)
    last = jnp.where(key >= 2 * BLOCK, NEG, interior)
    bias_band = jnp.stack([first, interior, last])

    n = jnp.arange(nblk, dtype=jnp.int32)[:, None, None]
    k = jnp.arange(N_META, dtype=jnp.int32)[None, :, None]
    qi = jnp.arange(BLOCK, dtype=jnp.int32)[None, None, :]
    rel_m = k - (N_META + n * BLOCK + qi)
    idx_m = _t5_bucket(rel_m).reshape(nblk * N_META, BLOCK)
    bias_meta = (_bias_lookup(table_flat, idx_m)
                 .reshape(A_KV_HEADS, A_GROUP, nblk, N_META, BLOCK)
                 .transpose(2, 0, 3, 1, 4).reshape(nblk, A_KV_HEADS, N_META, A_GROUP * BLOCK))

    qp = jnp.arange(N_META, dtype=jnp.int32)[:, None]
    kp = jnp.arange(N_META + BLOCK, dtype=jnp.int32)[None, :]
    rel_q = kp - qp
    idx_q = jnp.where(jnp.abs(rel_q) <= WINDOW, _t5_bucket(rel_q), -1)
    bias_q = _bias_lookup(table_flat, idx_q)
    return bias_band, bias_meta, bias_q[:, :, :N_META], bias_q[:, :, N_META:]


def _win_kernel(sink_ref, q_ref, kp_ref, kc_ref, kn_ref, vp_ref, vc_ref, vn_ref, kam_ref, vam_ref,
                bias_ref, bm_ref, o_ref):
    w = WIN_BLOCKS
    i, ntile = pl.program_id(1), pl.num_programs(1)

    def qrows(blk):
        return slice(blk * BLOCK, (blk + 1) * BLOCK)

    def band(blk, prev_ref, cur_ref, next_ref, kvh, axis):
        take = lambda ref, lo, hi: ref[kvh, lo:hi, :] if axis == 0 else ref[kvh, :, lo:hi]
        parts = [take(cur_ref, max(blk - 1, 0) * BLOCK, min(blk + 2, w) * BLOCK)]
        if blk == 0:
            parts.insert(0, take(prev_ref, (w - 1) * BLOCK, w * BLOCK))
        if blk == w - 1:
            parts.append(take(next_ref, 0, BLOCK))
        return jnp.concatenate(parts, axis=axis)

    def variant(blk):
        v = 1
        if blk == 0:
            v = jnp.where(i == 0, 0, v)
        if blk == w - 1:
            v = jnp.where(i == ntile - 1, 2, v)
        return v

    def scores(blk, kvh):
        kb = band(blk, kp_ref, kc_ref, kn_ref, kvh, 0)
        q4 = q_ref[A_GROUP * kvh:A_GROUP * (kvh + 1), qrows(blk), :].reshape(A_GROUP * BLOCK, A_HEAD_DIM)
        s = lax.dot_general(kb, q4, _NT, preferred_element_type=f32) + bias_ref[variant(blk), kvh]
        sm = lax.dot_general(kam_ref[kvh], q4, _NT, preferred_element_type=f32) + bm_ref[blk, kvh]
        sink = sink_ref[kvh]
        m = jnp.maximum(jnp.maximum(jnp.max(s, axis=0, keepdims=True), jnp.max(sm, axis=0, keepdims=True)), sink)
        return s, sm, m

    def attend(blk, kvh, s, sm, m):
        vb = band(blk, vp_ref, vc_ref, vn_ref, kvh, 1)
        p = jnp.exp2(s - m).astype(bf16)
        pm = jnp.exp2(sm - m).astype(bf16)
        acc = (jnp.dot(vb, p, preferred_element_type=f32)
               + jnp.dot(vam_ref[0, kvh], pm, preferred_element_type=f32))
        den = acc[A_HEAD_DIM:A_HEAD_DIM + 1] + jnp.exp2(sink_ref[kvh] - m)
        return acc[:A_HEAD_DIM] * (1.0 / den)

    units = [(blk, kvh) for blk in range(w) for kvh in range(A_KV_HEADS)]
    outs = {}
    pending = scores(*units[0])
    for u, unit in enumerate(units):
        nxt = scores(*units[u + 1]) if u + 1 < len(units) else None
        outs[unit] = attend(*unit, *pending)
        pending = nxt
        blk, kvh = unit
        if kvh == A_KV_HEADS - 1:
            ot = jnp.concatenate([outs[(blk, j)][:, g * BLOCK:(g + 1) * BLOCK]
                                  for j in range(A_KV_HEADS) for g in range(A_GROUP)], axis=0)
            o_ref[qrows(blk), :] = ot.T.astype(o_ref.dtype)


def _win(sink_rows, qa, ka, vat, kam, vatm, bias_band, bias_meta, batch, seq):
    tb = WIN_BLOCKS * BLOCK
    ntile = seq // tb
    prev_ = lambda b, i: b * ntile + jnp.maximum(i - 1, 0)
    cur_ = lambda b, i: b * ntile + i
    next_ = lambda b, i: b * ntile + jnp.minimum(i + 1, ntile - 1)
    kspec = lambda f: pl.BlockSpec((A_KV_HEADS, tb, A_HEAD_DIM), lambda b, i: (0, f(b, i), 0))
    vspec = lambda f: pl.BlockSpec((A_KV_HEADS, A_V_AUG, tb), lambda b, i: (0, 0, f(b, i)))
    return pl.pallas_call(
        _win_kernel,
        grid=(batch, ntile),
        in_specs=[
            pl.BlockSpec((A_KV_HEADS, 1, A_GROUP * BLOCK), lambda b, i: (0, 0, 0)),
            pl.BlockSpec((A_HEADS, tb, A_HEAD_DIM), lambda b, i: (0, cur_(b, i), 0)),
            kspec(prev_), kspec(cur_), kspec(next_),
            vspec(prev_), vspec(cur_), vspec(next_),
            pl.BlockSpec((A_KV_HEADS, N_META, A_HEAD_DIM), lambda b, i: (0, b, 0)),
            pl.BlockSpec((1, A_KV_HEADS, A_V_AUG, N_META), lambda b, i: (b, 0, 0, 0)),
            pl.BlockSpec((3, A_KV_HEADS, 3 * BLOCK, A_GROUP * BLOCK), lambda b, i: (0, 0, 0, 0)),
            pl.BlockSpec((WIN_BLOCKS, A_KV_HEADS, N_META, A_GROUP * BLOCK), lambda b, i: (i, 0, 0, 0)),
        ],
        out_specs=pl.BlockSpec((tb, A_WIDTH), lambda b, i: (cur_(b, i), 0)),
        out_shape=jax.ShapeDtypeStruct((batch * seq, A_WIDTH), bf16),
        compiler_params=_cparams("arbitrary", "arbitrary"),
        name="win",
    )(sink_rows, qa, ka, ka, ka, vat, vat, vat, kam, vatm, bias_band, bias_meta)


def _win_meta_kernel(sink_ref, q_ref, ka_ref, vat_ref, kam_ref, vatm_ref, biasm_ref, biasr_ref, o_ref):
    for kvh in range(A_KV_HEADS):
        q4 = q_ref[A_GROUP * kvh:A_GROUP * (kvh + 1)].reshape(A_GROUP * N_META, A_HEAD_DIM)
        sm = lax.dot_general(q4, kam_ref[kvh], _NT, preferred_element_type=f32)
        sr = lax.dot_general(q4, ka_ref[kvh], _NT, preferred_element_type=f32)
        for g in range(A_GROUP):
            h = A_GROUP * kvh + g
            rows = slice(g * N_META, (g + 1) * N_META)
            smg = sm[rows] + biasm_ref[h]
            srg = sr[rows] + biasr_ref[h]
            sink = sink_ref[h]
            m = jnp.maximum(jnp.maximum(jnp.max(smg, axis=-1, keepdims=True),
                                        jnp.max(srg, axis=-1, keepdims=True)), sink)
            pm = jnp.exp2(smg - m).astype(bf16)
            pr = jnp.exp2(srg - m).astype(bf16)
            o = (lax.dot_general(pm, vatm_ref[0, kvh], _NT, preferred_element_type=f32)
                 + lax.dot_general(pr, vat_ref[kvh], _NT, preferred_element_type=f32))
            den = o[:, A_HEAD_DIM:A_HEAD_DIM + 1] + jnp.exp2(sink - m)
            o_ref[:, h * A_HEAD_DIM:(h + 1) * A_HEAD_DIM] = (o[:, :A_HEAD_DIM] * (1.0 / den)).astype(o_ref.dtype)


def _win_meta(sink, qam, ka, vat, kam, vatm, bias_mm, bias_mr, batch, seq):
    nblk = seq // BLOCK
    const3 = lambda shape: pl.BlockSpec(shape, lambda b: (0, 0, 0))
    return pl.pallas_call(
        _win_meta_kernel,
        grid=(batch,),
        in_specs=[
            pl.BlockSpec(memory_space=pltpu.SMEM),
            pl.BlockSpec((A_HEADS, N_META, A_HEAD_DIM), lambda b: (0, b, 0)),
            pl.BlockSpec((A_KV_HEADS, BLOCK, A_HEAD_DIM), lambda b: (0, b * nblk, 0)),
            pl.BlockSpec((A_KV_HEADS, A_V_AUG, BLOCK), lambda b: (0, 0, b * nblk)),
            pl.BlockSpec((A_KV_HEADS, N_META, A_HEAD_DIM), lambda b: (0, b, 0)),
            pl.BlockSpec((1, A_KV_HEADS, A_V_AUG, N_META), lambda b: (b, 0, 0, 0)),
            const3((A_HEADS, N_META, N_META)),
            const3((A_HEADS, N_META, BLOCK)),
        ],
        out_specs=pl.BlockSpec((N_META, A_WIDTH), lambda b: (b, 0)),
        out_shape=jax.ShapeDtypeStruct((batch * N_META, A_WIDTH), bf16),
        compiler_params=_cparams("arbitrary"),
        name="win_meta",
    )(sink, qam, ka, vat, kam, vatm, bias_mm, bias_mr)


def _mla_kernel(q_ref, k_ref, km_ref, vt_ref, vtm_ref, o_ref, acc_ref, s0_ref, s1_ref, sm0_ref, sm1_ref):
    s_bufs, sm_bufs = (s0_ref, s1_ref), (sm0_ref, sm1_ref)
    nsub, _, _, tq = acc_ref.shape
    units = nsub * B_HEADS
    seq = k_ref.shape[1]
    chunks = [(c * MLA_KEY_CHUNK, (c + 1) * MLA_KEY_CHUNK) for c in range(seq // MLA_KEY_CHUNK)]

    def unit(u):
        return u % B_HEADS, u // B_HEADS

    def step(t, slot, m, *, score_next=True, attend_cur=True):
        if score_next:
            hn, subn = unit(t + 1)
            q = q_ref[hn, pl.ds(pl.multiple_of(subn * tq, tq), tq), :]
            sm = lax.dot_general(km_ref[hn], q, _NT, preferred_element_type=f32)
            sm_bufs[1 - slot][...] = sm
            m_next = jnp.max(sm, axis=0, keepdims=True)
        if attend_cur:
            h, sub = unit(t)
            pm = jnp.exp2(sm_bufs[slot][...] - m).astype(bf16)
            acc = jnp.dot(vtm_ref[0, h], pm, preferred_element_type=f32)
        for lo, hi in chunks:
            if score_next:
                s = lax.dot_general(k_ref[hn, lo:hi, :], q, _NT, preferred_element_type=f32)
                s_bufs[1 - slot][lo:hi, :] = s
                m_next = jnp.maximum(m_next, jnp.max(s, axis=0, keepdims=True))
            if attend_cur:
                p = jnp.exp2(s_bufs[slot][lo:hi, :] - m).astype(bf16)
                acc = acc + jnp.dot(vt_ref[h, :, lo:hi], p, preferred_element_type=f32)
        if attend_cur:
            acc_ref[sub, h] = acc[:B_V_DIM] * (1.0 / acc[B_V_DIM:B_V_DIM + 1])
        return m_next if score_next else None

    def steps(j, m):
        for i in range(MLA_STEPS_PER_ITER):
            m = step(MLA_STEPS_PER_ITER * j + i, i % 2, m)
        return m

    assert MLA_STEPS_PER_ITER % 2 == 0 and units % 2 == 0
    iters = (units - 1) // MLA_STEPS_PER_ITER
    m = step(-1, 1, None, attend_cur=False)
    m = lax.fori_loop(0, iters, steps, m)
    for t in range(iters * MLA_STEPS_PER_ITER, units - 1):
        m = step(t, t % 2, m)
    step(units - 1, 1, m, score_next=False)
    for sub in range(nsub):
        o_ref[sub * tq:(sub + 1) * tq, :] = acc_ref[sub].reshape(B_WIDTH, tq).T.astype(o_ref.dtype)


def _mla(q, k, km, vt, vtm, batch, seq):
    tq = MLA_Q_TILE
    tb = MLA_Q_TILE * MLA_SUBTILES
    nq = seq // tb
    return pl.pallas_call(
        _mla_kernel,
        grid=(batch, nq),
        in_specs=[
            pl.BlockSpec((B_HEADS, tb, B_QK_PAD), lambda b, i: (0, b * nq + i, 0)),
            pl.BlockSpec((B_HEADS, seq, B_QK_PAD), lambda b, i: (0, b, 0)),
            pl.BlockSpec((B_HEADS, N_META, B_QK_PAD), lambda b, i: (0, b, 0)),
            pl.BlockSpec((B_HEADS, B_V_AUG, seq), lambda b, i: (0, 0, b)),
            pl.BlockSpec((1, B_HEADS, B_V_AUG, N_META), lambda b, i: (b, 0, 0, 0)),
        ],
        out_specs=pl.BlockSpec((tb, B_WIDTH), lambda b, i: (b * nq + i, 0)),
        out_shape=jax.ShapeDtypeStruct((batch * seq, B_WIDTH), bf16),
        scratch_shapes=([pltpu.VMEM((MLA_SUBTILES, B_HEADS, B_V_DIM, tq), f32)]
                        + [pltpu.VMEM((seq, tq), f32)] * 2 + [pltpu.VMEM((N_META, tq), f32)] * 2),
        compiler_params=_cparams("arbitrary", "arbitrary"),
        name="mla",
    )(q, k, km, vt, vtm)


def _mla_meta_kernel(q_ref, k_ref, km_ref, vt_ref, vtm_ref, o_ref):
    for h in range(B_HEADS):
        q = q_ref[h]
        s = lax.dot_general(q, k_ref[h], _NT, preferred_element_type=f32)
        sm = lax.dot_general(q, km_ref[h], _NT, preferred_element_type=f32)
        m = jnp.maximum(jnp.max(s, axis=-1, keepdims=True), jnp.max(sm, axis=-1, keepdims=True))
        p = jnp.exp2(s - m).astype(bf16)
        pm = jnp.exp2(sm - m).astype(bf16)
        o = (lax.dot_general(p, vt_ref[h], _NT, preferred_element_type=f32)
             + lax.dot_general(pm, vtm_ref[0, h], _NT, preferred_element_type=f32))
        o_ref[:, h * B_V_DIM:(h + 1) * B_V_DIM] = (
            o[:, :B_V_DIM] * (1.0 / o[:, B_V_DIM:B_V_DIM + 1])).astype(o_ref.dtype)


def _mla_meta(qm, k, km, vt, vtm, batch, seq):
    return pl.pallas_call(
        _mla_meta_kernel,
        grid=(batch,),
        in_specs=[
            pl.BlockSpec((B_HEADS, N_META, B_QK_PAD), lambda b: (0, b, 0)),
            pl.BlockSpec((B_HEADS, seq, B_QK_PAD), lambda b: (0, b, 0)),
            pl.BlockSpec((B_HEADS, N_META, B_QK_PAD), lambda b: (0, b, 0)),
            pl.BlockSpec((B_HEADS, B_V_AUG, seq), lambda b: (0, 0, b)),
            pl.BlockSpec((1, B_HEADS, B_V_AUG, N_META), lambda b: (b, 0, 0, 0)),
        ],
        out_specs=pl.BlockSpec((N_META, B_WIDTH), lambda b: (b, 0)),
        out_shape=jax.ShapeDtypeStruct((batch * N_META, B_WIDTH), bf16),
        compiler_params=_cparams("arbitrary"),
        name="mla_meta",
    )(qm, k, km, vt, vtm)


def _rope_tables(pos):
    half = B_ROPE_DIM // 2
    freqs = ROPE_THETA ** (-jnp.arange(half, dtype=jnp.float32) / half)
    ang = pos.astype(jnp.float32)[:, None] * freqs[None, :]
    cos, sin = jnp.cos(ang), jnp.sin(ang)
    rows = pos.shape[0]
    c_mla = (B_NOPE_DIM + B_ROPE_DIM) ** -0.5 * LOG2E
    z = lambda w: jnp.zeros((rows, w), f32)
    tail = B_QK_PAD - B_NOPE_DIM - B_ROPE_DIM
    cos_q = jnp.concatenate([jnp.full((rows, B_NOPE_DIM), c_mla, f32), c_mla * cos, c_mla * cos, z(tail)], axis=1)
    sin_q = jnp.concatenate([z(B_NOPE_DIM), c_mla * sin, c_mla * sin, z(tail)], axis=1)
    cos_k = jnp.concatenate([z(B_NOPE_DIM), cos, cos, z(tail)], axis=1)
    sin_k = jnp.concatenate([z(B_NOPE_DIM), sin, sin, z(tail)], axis=1)
    return jnp.concatenate([cos_q, sin_q, cos_k, sin_k], axis=1)


def _rot_cols(w):
    half = w.shape[-1] // 2
    return jnp.concatenate([-w[..., half:], w[..., :half]], axis=-1)


def _stacked_weights(norm_in, w_in, norm_q_lat, w_uq, norm_kv_lat, w_ukv, norm_out_a, norm_out_b, w_out):
    depth, d, _ = w_in.shape
    sizes = (A_WIDTH, A_KV_HEADS * A_HEAD_DIM, A_KV_HEADS * A_HEAD_DIM, A_WIDTH,
             Q_LORA_RANK, KV_LORA_RANK, B_ROPE_DIM, B_WIDTH)
    offs = [sum(sizes[:i]) for i in range(len(sizes) + 1)]
    qa, ka, va, ga, cq, ckv, kr, gb = (w_in[..., offs[i]:offs[i + 1]] for i in range(len(sizes)))
    tail = B_QK_PAD - B_NOPE_DIM - B_ROPE_DIM
    zeros = lambda *shape: jnp.zeros(shape, w_in.dtype)
    pad_k = lambda w: jnp.concatenate([zeros(depth, d, B_NOPE_DIM), w, zeros(depth, d, tail)], axis=-1)
    w1 = jnp.concatenate([qa, cq, ckv, ka, pad_k(kr), pad_k(_rot_cols(kr))], axis=-1).astype(bf16)

    r = w_uq.shape[1]
    uq = w_uq.reshape(depth, r, B_HEADS, B_NOPE_DIM + B_ROPE_DIM)
    nope, rope = uq[..., :B_NOPE_DIM], uq[..., B_NOPE_DIM:]
    wqa = jnp.concatenate([nope, rope, zeros(depth, r, B_HEADS, tail)], axis=-1)
    wqb = jnp.concatenate([jnp.zeros_like(nope), _rot_cols(rope), zeros(depth, r, B_HEADS, tail)], axis=-1)

    rk = w_ukv.shape[1]
    ukv = w_ukv.reshape(depth, rk, B_HEADS, B_NOPE_DIM + B_V_DIM)
    k_nope, v = ukv[..., :B_NOPE_DIM], ukv[..., B_NOPE_DIM:]
    wk = jnp.concatenate([k_nope, zeros(depth, rk, B_HEADS, B_QK_PAD - B_NOPE_DIM)], axis=-1)
    row = lambda g: g.astype(f32)[:, None, :]
    return {
        "gin": row(norm_in),
        "w1": w1,
        "wg": jnp.concatenate([ga, gb], axis=-1).astype(bf16),
        "wvat": jnp.swapaxes(va, 1, 2).astype(bf16),
        "gq": row(norm_q_lat),
        "wqa": wqa.reshape(depth, r, B_HEADS * B_QK_PAD).astype(bf16),
        "wqb": wqb.reshape(depth, r, B_HEADS * B_QK_PAD).astype(bf16),
        "gkv": row(norm_kv_lat),
        "wk": wk.reshape(depth, rk, B_HEADS * B_QK_PAD).astype(bf16),
        "wvt": jnp.swapaxes(v.reshape(depth, rk, B_WIDTH), 1, 2).astype(bf16),
        "na": row(norm_out_a),
        "nb": row(norm_out_b),
        "wout": w_out.astype(bf16),
    }


def kernel(x, meta_tokens, rel_bias_table, norm_in, w_in, sink_a, norm_q_lat, w_uq, norm_kv_lat, w_ukv,
           norm_out_a, norm_out_b, w_out, norm_final):
    batch, seq, d = x.shape
    depth = w_in.shape[0]
    assert d == D_MODEL and seq % (MLA_Q_TILE * MLA_SUBTILES) == 0 and seq % ROW_TILE == 0

    h_real = x.reshape(batch * seq, d).astype(f32)
    h_meta = jnp.tile(meta_tokens.astype(f32), (batch, 1))
    tab_real = _rope_tables(N_META + jnp.arange(seq))
    tab_meta = _rope_tables(jnp.arange(batch * N_META) % N_META)
    bias_band, bias_meta, bias_mm, bias_mr = _bias_tables(rel_bias_table, seq // BLOCK)

    lw = _stacked_weights(norm_in, w_in, norm_q_lat, w_uq, norm_kv_lat, w_ukv, norm_out_a, norm_out_b, w_out)
    sinks = sink_a.astype(f32) * LOG2E
    sinks_rows = jnp.repeat(sinks.reshape(depth, A_KV_HEADS, A_GROUP), BLOCK, axis=2)[:, :, None, :]

    out = None
    for i in range(depth):
        last = i == depth - 1
        sink, sink_rows = sinks[i], sinks_rows[i]
        qa, ka, vat, q, k, vt = _proj(h_real, tab_real, lw, i)
        qam, kam, vatm, qm, km, vtm = _proj(h_meta, tab_meta, lw, i)
        vtm_b = vtm.reshape(B_HEADS, B_V_AUG, batch, N_META).transpose(2, 0, 1, 3)
        vatm_b = vatm.reshape(A_KV_HEADS, A_V_AUG, batch, N_META).transpose(2, 0, 1, 3)

        ya = _win(sink_rows, qa, ka, vat, kam, vatm_b, bias_band, bias_meta, batch, seq)
        yb = _mla(q, k, km, vt, vtm_b, batch, seq)
        if last:
            out = _out(h_real, ya, yb, lw, i, norm_final.astype(f32)[None, :])
        else:
            yam = _win_meta(sink, qam, ka, vat, kam, vatm_b, bias_mm, bias_mr, batch, seq)
            ybm = _mla_meta(qm, k, km, vt, vtm_b, batch, seq)
            h_real = _out(h_real, ya, yb, lw, i)
            h_meta = _out(h_meta, yam, ybm, lw, i)
    return out.reshape(batch, seq, d).astype(x.dtype)
```

```python
import functools
import math

import jax
import jax.numpy as jnp
from jax import lax
from jax.experimental import pallas as pl
from jax.experimental.pallas import tpu as pltpu

D_MODEL = 1024
N_META = 16
BLOCK = 128
WINDOW = 128
A_HEADS = 8
A_KV_HEADS = 2
A_GROUP = A_HEADS // A_KV_HEADS
A_HEAD_DIM = 64
A_WIDTH = A_HEADS * A_HEAD_DIM
B_HEADS = 8
B_NOPE_DIM = 64
B_ROPE_DIM = 32
B_V_DIM = 64
B_WIDTH = B_HEADS * B_V_DIM
B_QK_PAD = 128
BF16_SUBLANES = 16
B_V_AUG = B_V_DIM + BF16_SUBLANES
A_V_AUG = A_HEAD_DIM + BF16_SUBLANES
Q_LORA_RANK = 256
KV_LORA_RANK = 128
N_BUCKETS = 32
MAX_DISTANCE = 128
ROPE_THETA = 10000.0
EPS = 1e-6

LOG2E = math.log2(math.e)
NEG = -1e30
ROW_TILE = 1024
MLA_Q_TILE = 256
MLA_SUBTILES = 8
MLA_STEPS_PER_ITER = 4
MLA_KEY_CHUNK = 1024
WIN_BLOCKS = 8
BIAS_ROW_CHUNK = 32
VMEM_LIMIT_BYTES = 56 * 1024 * 1024

_C_QA = 0
_C_CQ = _C_QA + A_WIDTH
_C_CKV = _C_CQ + Q_LORA_RANK
_C_KA = _C_CKV + KV_LORA_RANK
_C_KRA = _C_KA + A_KV_HEADS * A_HEAD_DIM
_C_KRB = _C_KRA + B_QK_PAD
_C_END = _C_KRB + B_QK_PAD

_NT = (((1,), (1,)), ((), ()))

bf16 = jnp.bfloat16
f32 = jnp.float32


def _cparams(*sem, flags=None):
    return pltpu.CompilerParams(dimension_semantics=sem, vmem_limit_bytes=VMEM_LIMIT_BYTES, flags=flags)


def _rms(x, gain):
    return x * lax.rsqrt(jnp.mean(x * x, axis=-1, keepdims=True) + EPS) * gain


def _silu(x):
    return x / (1.0 + jnp.exp(-x))


def _proj_kernel(h_ref, tab_ref, gin_ref, w1_ref, wvat_ref, gq_ref, wqa_ref, wqb_ref, gkv_ref, wk_ref, wvt_ref,
                 qa_ref, ka_ref, vat_ref, q_ref, k_ref, vt_ref):
    u = _rms(h_ref[...], gin_ref[...]).astype(bf16)
    t = u.shape[0]

    def mm(lo, hi):
        return jnp.dot(u, w1_ref[:, lo:hi], preferred_element_type=f32)

    qa = mm(_C_QA, _C_CQ) * (A_HEAD_DIM ** -0.5 * LOG2E)
    for h in range(A_HEADS):
        qa_ref[h] = qa[:, h * A_HEAD_DIM:(h + 1) * A_HEAD_DIM].astype(bf16)
    vat = lax.dot_general(wvat_ref[...], u, _NT, preferred_element_type=f32)
    vat_ref[:, :A_HEAD_DIM, :] = vat.reshape(A_KV_HEADS, A_HEAD_DIM, t).astype(bf16)
    vat_ref[:, A_HEAD_DIM:, :] = jnp.ones((A_KV_HEADS, A_V_AUG - A_HEAD_DIM, t), bf16)
    mixed = mm(_C_CQ, _C_END)
    cq = mixed[:, :_C_CKV - _C_CQ]
    ckv = mixed[:, _C_CKV - _C_CQ:_C_KA - _C_CQ]
    ka = mixed[:, _C_KA - _C_CQ:_C_KRA - _C_CQ]
    kra = mixed[:, _C_KRA - _C_CQ:_C_KRB - _C_CQ]
    krb = mixed[:, _C_KRB - _C_CQ:]
    for j in range(A_KV_HEADS):
        ka_ref[j] = ka[:, j * A_HEAD_DIM:(j + 1) * A_HEAD_DIM].astype(bf16)
    tab_rows = pl.ds(pl.multiple_of((pl.program_id(0) % (tab_ref.shape[0] // t)) * t, t), t)
    cos_q = tab_ref[tab_rows, 0 * B_QK_PAD:1 * B_QK_PAD]
    sin_q = tab_ref[tab_rows, 1 * B_QK_PAD:2 * B_QK_PAD]
    cos_k = tab_ref[tab_rows, 2 * B_QK_PAD:3 * B_QK_PAD]
    sin_k = tab_ref[tab_rows, 3 * B_QK_PAD:4 * B_QK_PAD]

    cqn = _rms(cq, gq_ref[...]).astype(bf16)
    qa_part = jnp.dot(cqn, wqa_ref[...], preferred_element_type=f32)
    qb_part = jnp.dot(cqn, wqb_ref[...], preferred_element_type=f32)
    for h in range(B_HEADS):
        sl = slice(h * B_QK_PAD, (h + 1) * B_QK_PAD)
        q_ref[h] = (qa_part[:, sl] * cos_q + qb_part[:, sl] * sin_q).astype(bf16)

    ckvn = _rms(ckv, gkv_ref[...]).astype(bf16)
    k_nope = jnp.dot(ckvn, wk_ref[...], preferred_element_type=f32)
    k_rope = kra * cos_k + krb * sin_k
    for h in range(B_HEADS):
        k_ref[h] = (k_nope[:, h * B_QK_PAD:(h + 1) * B_QK_PAD] + k_rope).astype(bf16)
    vt = lax.dot_general(wvt_ref[...], ckvn, _NT, preferred_element_type=f32)
    vt_ref[:, :B_V_DIM, :] = vt.reshape(B_HEADS, B_V_DIM, t).astype(bf16)
    vt_ref[:, B_V_DIM:, :] = jnp.ones((B_HEADS, B_V_AUG - B_V_DIM, t), bf16)


def _layer_spec(layer, shape):
    return pl.BlockSpec((None,) + tuple(shape), lambda i: (layer,) + (0,) * len(shape))


def _proj(h, tab, lw, layer):
    rows = h.shape[0]
    t = min(ROW_TILE, rows)
    steps = rows // t
    assert tab.shape[0] % t == 0
    const = functools.partial(_layer_spec, layer)
    return pl.pallas_call(
        _proj_kernel,
        grid=(steps,),
        in_specs=[
            pl.BlockSpec((t, D_MODEL), lambda i: (i, 0)),
            pl.BlockSpec(tab.shape, lambda i: (0, 0)),
            const((1, D_MODEL)),
            const((D_MODEL, _C_END)),
            const((A_KV_HEADS * A_HEAD_DIM, D_MODEL)),
            const((1, Q_LORA_RANK)),
            const((Q_LORA_RANK, B_HEADS * B_QK_PAD)),
            const((Q_LORA_RANK, B_HEADS * B_QK_PAD)),
            const((1, KV_LORA_RANK)),
            const((KV_LORA_RANK, B_HEADS * B_QK_PAD)),
            const((B_WIDTH, KV_LORA_RANK)),
        ],
        out_specs=[
            pl.BlockSpec((A_HEADS, t, A_HEAD_DIM), lambda i: (0, i, 0)),
            pl.BlockSpec((A_KV_HEADS, t, A_HEAD_DIM), lambda i: (0, i, 0)),
            pl.BlockSpec((A_KV_HEADS, A_V_AUG, t), lambda i: (0, 0, i)),
            pl.BlockSpec((B_HEADS, t, B_QK_PAD), lambda i: (0, i, 0)),
            pl.BlockSpec((B_HEADS, t, B_QK_PAD), lambda i: (0, i, 0)),
            pl.BlockSpec((B_HEADS, B_V_AUG, t), lambda i: (0, 0, i)),
        ],
        out_shape=[
            jax.ShapeDtypeStruct((A_HEADS, rows, A_HEAD_DIM), bf16),
            jax.ShapeDtypeStruct((A_KV_HEADS, rows, A_HEAD_DIM), bf16),
            jax.ShapeDtypeStruct((A_KV_HEADS, A_V_AUG, rows), bf16),
            jax.ShapeDtypeStruct((B_HEADS, rows, B_QK_PAD), bf16),
            jax.ShapeDtypeStruct((B_HEADS, rows, B_QK_PAD), bf16),
            jax.ShapeDtypeStruct((B_HEADS, B_V_AUG, rows), bf16),
        ],
        compiler_params=_cparams("arbitrary"),
        name="proj",
    )(h, tab, lw["gin"], lw["w1"], lw["wvat"], lw["gq"], lw["wqa"], lw["wqb"], lw["gkv"], lw["wk"], lw["wvt"])


def _out_kernel(h_ref, ya_ref, yb_ref, gin_ref, wg_ref, na_ref, nb_ref, w_ref, *rest, final):
    if final:
        nf_ref, o_ref = rest
    else:
        (o_ref,) = rest
    hn = h_ref[...]
    u = _rms(hn, gin_ref[...]).astype(bf16)
    ga = _silu(jnp.dot(u, wg_ref[:, :A_WIDTH], preferred_element_type=f32))
    gb = _silu(jnp.dot(u, wg_ref[:, A_WIDTH:], preferred_element_type=f32))
    ya = _rms(ya_ref[...].astype(f32), na_ref[...]) * ga
    yb = _rms(yb_ref[...].astype(f32), nb_ref[...]) * gb
    hn = hn + jnp.dot(ya.astype(bf16), w_ref[:A_WIDTH, :], preferred_element_type=f32)
    hn = hn + jnp.dot(yb.astype(bf16), w_ref[A_WIDTH:, :], preferred_element_type=f32)
    if final:
        hn = _rms(hn, nf_ref[...])
    o_ref[...] = hn


def _out(h, ya, yb, lw, layer, norm_final=None):
    rows = h.shape[0]
    t = min(ROW_TILE, rows)
    row = lambda width: pl.BlockSpec((t, width), lambda i: (i, 0))
    const = functools.partial(_layer_spec, layer)
    final = norm_final is not None
    in_specs = [row(D_MODEL), row(A_WIDTH), row(B_WIDTH),
                const((1, D_MODEL)), const((D_MODEL, A_WIDTH + B_WIDTH)),
                const((1, A_WIDTH)), const((1, B_WIDTH)), const((A_WIDTH + B_WIDTH, D_MODEL))]
    args = [h, ya, yb, lw["gin"], lw["wg"], lw["na"], lw["nb"], lw["wout"]]
    if final:
        in_specs.append(pl.BlockSpec((1, D_MODEL), lambda i: (0, 0)))
        args.append(norm_final)
    return pl.pallas_call(
        functools.partial(_out_kernel, final=final),
        grid=(rows // t,),
        in_specs=in_specs,
        out_specs=row(D_MODEL),
        out_shape=jax.ShapeDtypeStruct((rows, D_MODEL), f32),
        compiler_params=_cparams("arbitrary"),
        name="outproj",
    )(*args)


def _bias_kernel(tab_ref, idx_ref, o_ref):
    rows = idx_ref.shape[0]
    chunk = math.gcd(rows, BIAS_ROW_CHUNK)

    def body(r, carry):
        sl = pl.ds(pl.multiple_of(r * chunk, chunk), chunk)
        idx = idx_ref[sl, :]
        accs = [jnp.where(idx < 0, NEG, 0.0).astype(f32)] * A_HEADS
        for b in range(N_BUCKETS):
            hit = idx == b
            accs = [jnp.where(hit, tab_ref[b * A_HEADS + h], accs[h]) for h in range(A_HEADS)]
        for h in range(A_HEADS):
            o_ref[h, sl, :] = accs[h]
        return carry

    lax.fori_loop(0, rows // chunk, body, 0)


def _bias_lookup(table_flat, idx):
    return pl.pallas_call(
        _bias_kernel,
        in_specs=[pl.BlockSpec(memory_space=pltpu.SMEM), pl.BlockSpec(idx.shape, lambda: (0, 0))],
        out_specs=pl.BlockSpec((A_HEADS,) + idx.shape, lambda: (0, 0, 0)),
        out_shape=jax.ShapeDtypeStruct((A_HEADS,) + idx.shape, f32),
        compiler_params=pltpu.CompilerParams(vmem_limit_bytes=VMEM_LIMIT_BYTES),
        name="relbias",
    )(table_flat, idx)


def _t5_bucket(rel):
    nb = N_BUCKETS // 2
    max_exact = nb // 2
    ret = jnp.where(rel > 0, nb, 0)
    n = jnp.abs(rel)
    nf = jnp.maximum(n, 1).astype(jnp.float32)
    large = max_exact + (jnp.log(nf / max_exact) / math.log(MAX_DISTANCE / max_exact)
                         * (nb - max_exact)).astype(jnp.int32)
    large = jnp.minimum(large, nb - 1)
    bucket = ret + jnp.where(n < max_exact, n, large)
    return jnp.bitwise_and(bucket, N_BUCKETS - 1)


def _bias_tables(rel_bias_table, nblk):
    table_flat = (rel_bias_table.astype(f32) * LOG2E).reshape(-1)
    i = jnp.arange(BLOCK, dtype=jnp.int32)[:, None]
    j = jnp.arange(3 * BLOCK, dtype=jnp.int32)[None, :]
    rel = j - i - BLOCK
    band = jnp.where(jnp.abs(rel) <= WINDOW, _t5_bucket(rel), -1)
    interior = (_bias_lookup(table_flat, band.T)
                .reshape(A_KV_HEADS, A_GROUP, 3 * BLOCK, BLOCK)
                .transpose(0, 2, 1, 3).reshape(A_KV_HEADS, 3 * BLOCK, A_GROUP * BLOCK))
    key = jnp.arange(3 * BLOCK, dtype=jnp.int32)[None, :, None]
    first = jnp.where(key < BLOCK, NEG, interior)
    last = jnp.where(key >= 2 * BLOCK, NEG, interior)
    bias_band = jnp.stack([first, interior, last])

    n = jnp.arange(nblk, dtype=jnp.int32)[:, None, None]
    k = jnp.arange(N_META, dtype=jnp.int32)[None, :, None]
    qi = jnp.arange(BLOCK, dtype=jnp.int32)[None, None, :]
    rel_m = k - (N_META + n * BLOCK + qi)
    idx_m = _t5_bucket(rel_m).reshape(nblk * N_META, BLOCK)
    bias_meta = (_bias_lookup(table_flat, idx_m)
                 .reshape(A_KV_HEADS, A_GROUP, nblk, N_META, BLOCK)
                 .transpose(2, 0, 3, 1, 4).reshape(nblk, A_KV_HEADS, N_META, A_GROUP * BLOCK))

    qp = jnp.arange(N_META, dtype=jnp.int32)[:, None]
    kp = jnp.arange(N_META + BLOCK, dtype=jnp.int32)[None, :]
    rel_q = kp - qp
    idx_q = jnp.where(jnp.abs(rel_q) <= WINDOW, _t5_bucket(rel_q), -1)
    bias_q = _bias_lookup(table_flat, idx_q)
    return bias_band, bias_meta, bias_q[:, :, :N_META], bias_q[:, :, N_META:]


def _win_kernel(sink_ref, q_ref, kp_ref, kc_ref, kn_ref, vp_ref, vc_ref, vn_ref, kam_ref, vam_ref,
                bias_ref, bm_ref, o_ref):
    w = WIN_BLOCKS
    i, ntile = pl.program_id(1), pl.num_programs(1)

    def qrows(blk):
        return slice(blk * BLOCK, (blk + 1) * BLOCK)

    def band(blk, prev_ref, cur_ref, next_ref, kvh, axis):
        take = lambda ref, lo, hi: ref[kvh, lo:hi, :] if axis == 0 else ref[kvh, :, lo:hi]
        parts = [take(cur_ref, max(blk - 1, 0) * BLOCK, min(blk + 2, w) * BLOCK)]
        if blk == 0:
            parts.insert(0, take(prev_ref, (w - 1) * BLOCK, w * BLOCK))
        if blk == w - 1:
            parts.append(take(next_ref, 0, BLOCK))
        return jnp.concatenate(parts, axis=axis)

    def variant(blk):
        v = 1
        if blk == 0:
            v = jnp.where(i == 0, 0, v)
        if blk == w - 1:
            v = jnp.where(i == ntile - 1, 2, v)
        return v

    def scores(blk, kvh):
        kb = jnp.concatenate([band(blk, kp_ref, kc_ref, kn_ref, kvh, 0), kam_ref[kvh]], axis=0)
        q4 = q_ref[A_GROUP * kvh:A_GROUP * (kvh + 1), qrows(blk), :].reshape(A_GROUP * BLOCK, A_HEAD_DIM)
        bias = jnp.concatenate([bias_ref[variant(blk), kvh], bm_ref[blk, kvh]], axis=0)
        s = lax.dot_general(kb, q4, _NT, preferred_element_type=f32) + bias
        m = jnp.maximum(jnp.max(s, axis=0, keepdims=True), sink_ref[kvh])
        return s, m

    def attend(blk, kvh, s, m):
        vb = jnp.concatenate([band(blk, vp_ref, vc_ref, vn_ref, kvh, 1), vam_ref[0, kvh]], axis=1)
        p = jnp.exp2(s - m).astype(bf16)
        acc = jnp.dot(vb, p, preferred_element_type=f32)
        den = acc[A_HEAD_DIM:A_HEAD_DIM + 1] + jnp.exp2(sink_ref[kvh] - m)
        return acc[:A_HEAD_DIM] * (1.0 / den)

    units = [(blk, kvh) for blk in range(w) for kvh in range(A_KV_HEADS)]
    outs = {}
    pending = scores(*units[0])
    for u, unit in enumerate(units):
        nxt = scores(*units[u + 1]) if u + 1 < len(units) else None
        outs[unit] = attend(*unit, *pending)
        pending = nxt
        blk, kvh = unit
        if kvh == A_KV_HEADS - 1:
            ot = jnp.concatenate([outs[(blk, j)][:, g * BLOCK:(g + 1) * BLOCK]
                                  for j in range(A_KV_HEADS) for g in range(A_GROUP)], axis=0)
            o_ref[qrows(blk), :] = ot.T.astype(o_ref.dtype)


def _win(sink_rows, qa, ka, vat, kam, vatm, bias_band, bias_meta, batch, seq):
    tb = WIN_BLOCKS * BLOCK
    ntile = seq // tb
    prev_ = lambda b, i: b * ntile + jnp.maximum(i - 1, 0)
    cur_ = lambda b, i: b * ntile + i
    next_ = lambda b, i: b * ntile + jnp.minimum(i + 1, ntile - 1)
    kspec = lambda f: pl.BlockSpec((A_KV_HEADS, tb, A_HEAD_DIM), lambda b, i: (0, f(b, i), 0))
    vspec = lambda f: pl.BlockSpec((A_KV_HEADS, A_V_AUG, tb), lambda b, i: (0, 0, f(b, i)))
    return pl.pallas_call(
        _win_kernel,
        grid=(batch, ntile),
        in_specs=[
            pl.BlockSpec((A_KV_HEADS, 1, A_GROUP * BLOCK), lambda b, i: (0, 0, 0)),
            pl.BlockSpec((A_HEADS, tb, A_HEAD_DIM), lambda b, i: (0, cur_(b, i), 0)),
            kspec(prev_), kspec(cur_), kspec(next_),
            vspec(prev_), vspec(cur_), vspec(next_),
            pl.BlockSpec((A_KV_HEADS, N_META, A_HEAD_DIM), lambda b, i: (0, b, 0)),
            pl.BlockSpec((1, A_KV_HEADS, A_V_AUG, N_META), lambda b, i: (b, 0, 0, 0)),
            pl.BlockSpec((3, A_KV_HEADS, 3 * BLOCK, A_GROUP * BLOCK), lambda b, i: (0, 0, 0, 0)),
            pl.BlockSpec((WIN_BLOCKS, A_KV_HEADS, N_META, A_GROUP * BLOCK), lambda b, i: (i, 0, 0, 0)),
        ],
        out_specs=pl.BlockSpec((tb, A_WIDTH), lambda b, i: (cur_(b, i), 0)),
        out_shape=jax.ShapeDtypeStruct((batch * seq, A_WIDTH), bf16),
        compiler_params=_cparams("arbitrary", "arbitrary"),
        name="win",
    )(sink_rows, qa, ka, ka, ka, vat, vat, vat, kam, vatm, bias_band, bias_meta)


def _win_meta_kernel(sink_ref, q_ref, ka_ref, vat_ref, kam_ref, vatm_ref, biasm_ref, biasr_ref, o_ref):
    for kvh in range(A_KV_HEADS):
        q4 = q_ref[A_GROUP * kvh:A_GROUP * (kvh + 1)].reshape(A_GROUP * N_META, A_HEAD_DIM)
        sm = lax.dot_general(q4, kam_ref[kvh], _NT, preferred_element_type=f32)
        sr = lax.dot_general(q4, ka_ref[kvh], _NT, preferred_element_type=f32)
        for g in range(A_GROUP):
            h = A_GROUP * kvh + g
            rows = slice(g * N_META, (g + 1) * N_META)
            smg = sm[rows] + biasm_ref[h]
            srg = sr[rows] + biasr_ref[h]
            sink = sink_ref[h]
            m = jnp.maximum(jnp.maximum(jnp.max(smg, axis=-1, keepdims=True),
                                        jnp.max(srg, axis=-1, keepdims=True)), sink)
            pm = jnp.exp2(smg - m).astype(bf16)
            pr = jnp.exp2(srg - m).astype(bf16)
            o = (lax.dot_general(pm, vatm_ref[0, kvh], _NT, preferred_element_type=f32)
                 + lax.dot_general(pr, vat_ref[kvh], _NT, preferred_element_type=f32))
            den = o[:, A_HEAD_DIM:A_HEAD_DIM + 1] + jnp.exp2(sink - m)
            o_ref[:, h * A_HEAD_DIM:(h + 1) * A_HEAD_DIM] = (o[:, :A_HEAD_DIM] * (1.0 / den)).astype(o_ref.dtype)


def _win_meta(sink, qam, ka, vat, kam, vatm, bias_mm, bias_mr, batch, seq):
    nblk = seq // BLOCK
    const3 = lambda shape: pl.BlockSpec(shape, lambda b: (0, 0, 0))
    return pl.pallas_call(
        _win_meta_kernel,
        grid=(batch,),
        in_specs=[
            pl.BlockSpec(memory_space=pltpu.SMEM),
            pl.BlockSpec((A_HEADS, N_META, A_HEAD_DIM), lambda b: (0, b, 0)),
            pl.BlockSpec((A_KV_HEADS, BLOCK, A_HEAD_DIM), lambda b: (0, b * nblk, 0)),
            pl.BlockSpec((A_KV_HEADS, A_V_AUG, BLOCK), lambda b: (0, 0, b * nblk)),
            pl.BlockSpec((A_KV_HEADS, N_META, A_HEAD_DIM), lambda b: (0, b, 0)),
            pl.BlockSpec((1, A_KV_HEADS, A_V_AUG, N_META), lambda b: (b, 0, 0, 0)),
            const3((A_HEADS, N_META, N_META)),
            const3((A_HEADS, N_META, BLOCK)),
        ],
        out_specs=pl.BlockSpec((N_META, A_WIDTH), lambda b: (b, 0)),
        out_shape=jax.ShapeDtypeStruct((batch * N_META, A_WIDTH), bf16),
        compiler_params=_cparams("arbitrary"),
        name="win_meta",
    )(sink, qam, ka, vat, kam, vatm, bias_mm, bias_mr)


def _mla_kernel(q_ref, k_ref, km_ref, vt_ref, vtm_ref, o_ref, acc_ref, s0_ref, s1_ref, sm0_ref, sm1_ref):
    s_bufs, sm_bufs = (s0_ref, s1_ref), (sm0_ref, sm1_ref)
    nsub, _, _, tq = acc_ref.shape
    units = nsub * B_HEADS
    seq = k_ref.shape[1]
    chunks = [(c * MLA_KEY_CHUNK, (c + 1) * MLA_KEY_CHUNK) for c in range(seq // MLA_KEY_CHUNK)]

    def unit(u):
        return u % B_HEADS, u // B_HEADS

    def step(t, slot, m, *, score_next=True, attend_cur=True):
        if score_next:
            hn, subn = unit(t + 1)
            q = q_ref[hn, pl.ds(pl.multiple_of(subn * tq, tq), tq), :]
            sm = lax.dot_general(km_ref[hn], q, _NT, preferred_element_type=f32)
            sm_bufs[1 - slot][...] = sm
            m_next = jnp.max(sm, axis=0, keepdims=True)
        if attend_cur:
            h, sub = unit(t)
            pm = jnp.exp2(sm_bufs[slot][...] - m).astype(bf16)
            acc = jnp.dot(vtm_ref[0, h], pm, preferred_element_type=f32)
        for lo, hi in chunks:
            if score_next:
                s = lax.dot_general(k_ref[hn, lo:hi, :], q, _NT, preferred_element_type=f32)
                s_bufs[1 - slot][lo:hi, :] = s
                m_next = jnp.maximum(m_next, jnp.max(s, axis=0, keepdims=True))
            if attend_cur:
                p = jnp.exp2(s_bufs[slot][lo:hi, :] - m).astype(bf16)
                acc = acc + jnp.dot(vt_ref[h, :, lo:hi], p, preferred_element_type=f32)
        if attend_cur:
            acc_ref[sub, h] = acc[:B_V_DIM] * (1.0 / acc[B_V_DIM:B_V_DIM + 1])
        return m_next if score_next else None

    def steps(j, m):
        for i in range(MLA_STEPS_PER_ITER):
            m = step(MLA_STEPS_PER_ITER * j + i, i % 2, m)
        return m

    assert MLA_STEPS_PER_ITER % 2 == 0 and units % 2 == 0
    iters = (units - 1) // MLA_STEPS_PER_ITER
    m = step(-1, 1, None, attend_cur=False)
    m = lax.fori_loop(0, iters, steps, m)
    for t in range(iters * MLA_STEPS_PER_ITER, units - 1):
        m = step(t, t % 2, m)
    step(units - 1, 1, m, score_next=False)
    for sub in range(nsub):
        o_ref[sub * tq:(sub + 1) * tq, :] = acc_ref[sub].reshape(B_WIDTH, tq).T.astype(o_ref.dtype)


def _mla(q, k, km, vt, vtm, batch, seq):
    tq = MLA_Q_TILE
    tb = MLA_Q_TILE * MLA_SUBTILES
    nq = seq // tb
    return pl.pallas_call(
        _mla_kernel,
        grid=(batch, nq),
        in_specs=[
            pl.BlockSpec((B_HEADS, tb, B_QK_PAD), lambda b, i: (0, b * nq + i, 0)),
            pl.BlockSpec((B_HEADS, seq, B_QK_PAD), lambda b, i: (0, b, 0)),
            pl.BlockSpec((B_HEADS, N_META, B_QK_PAD), lambda b, i: (0, b, 0)),
            pl.BlockSpec((B_HEADS, B_V_AUG, seq), lambda b, i: (0, 0, b)),
            pl.BlockSpec((1, B_HEADS, B_V_AUG, N_META), lambda b, i: (b, 0, 0, 0)),
        ],
        out_specs=pl.BlockSpec((tb, B_WIDTH), lambda b, i: (b * nq + i, 0)),
        out_shape=jax.ShapeDtypeStruct((batch * seq, B_WIDTH), bf16),
        scratch_shapes=([pltpu.VMEM((MLA_SUBTILES, B_HEADS, B_V_DIM, tq), f32)]
                        + [pltpu.VMEM((seq, tq), f32)] * 2 + [pltpu.VMEM((N_META, tq), f32)] * 2),
        compiler_params=_cparams("arbitrary", "arbitrary"),
        name="mla",
    )(q, k, km, vt, vtm)


def _mla_meta_kernel(q_ref, k_ref, km_ref, vt_ref, vtm_ref, o_ref):
    for h in range(B_HEADS):
        q = q_ref[h]
        s = lax.dot_general(q, k_ref[h], _NT, preferred_element_type=f32)
        sm = lax.dot_general(q, km_ref[h], _NT, preferred_element_type=f32)
        m = jnp.maximum(jnp.max(s, axis=-1, keepdims=True), jnp.max(sm, axis=-1, keepdims=True))
        p = jnp.exp2(s - m).astype(bf16)
        pm = jnp.exp2(sm - m).astype(bf16)
        o = (lax.dot_general(p, vt_ref[h], _NT, preferred_element_type=f32)
             + lax.dot_general(pm, vtm_ref[0, h], _NT, preferred_element_type=f32))
        o_ref[:, h * B_V_DIM:(h + 1) * B_V_DIM] = (
            o[:, :B_V_DIM] * (1.0 / o[:, B_V_DIM:B_V_DIM + 1])).astype(o_ref.dtype)


def _mla_meta(qm, k, km, vt, vtm, batch, seq):
    return pl.pallas_call(
        _mla_meta_kernel,
        grid=(batch,),
        in_specs=[
            pl.BlockSpec((B_HEADS, N_META, B_QK_PAD), lambda b: (0, b, 0)),
            pl.BlockSpec((B_HEADS, seq, B_QK_PAD), lambda b: (0, b, 0)),
            pl.BlockSpec((B_HEADS, N_META, B_QK_PAD), lambda b: (0, b, 0)),
            pl.BlockSpec((B_HEADS, B_V_AUG, seq), lambda b: (0, 0, b)),
            pl.BlockSpec((1, B_HEADS, B_V_AUG, N_META), lambda b: (b, 0, 0, 0)),
        ],
        out_specs=pl.BlockSpec((N_META, B_WIDTH), lambda b: (b, 0)),
        out_shape=jax.ShapeDtypeStruct((batch * N_META, B_WIDTH), bf16),
        compiler_params=_cparams("arbitrary"),
        name="mla_meta",
    )(qm, k, km, vt, vtm)


def _rope_tables(pos):
    half = B_ROPE_DIM // 2
    freqs = ROPE_THETA ** (-jnp.arange(half, dtype=jnp.float32) / half)
    tail = B_QK_PAD - B_NOPE_DIM - B_ROPE_DIM
    lane_freq = jnp.concatenate([jnp.zeros((B_NOPE_DIM,), f32), freqs, freqs, jnp.zeros((tail,), f32)])
    ang = pos.astype(jnp.float32)[:, None] * lane_freq[None, :]
    cos, sin = jnp.cos(ang), jnp.sin(ang)
    lane = jnp.arange(B_QK_PAD)[None, :]
    nope = lane < B_NOPE_DIM
    rope = (lane >= B_NOPE_DIM) & (lane < B_NOPE_DIM + B_ROPE_DIM)
    c_mla = (B_NOPE_DIM + B_ROPE_DIM) ** -0.5 * LOG2E
    cos_q = jnp.where(nope, c_mla, jnp.where(rope, c_mla * cos, 0.0))
    sin_q = jnp.where(rope, c_mla * sin, 0.0)
    cos_k = jnp.where(rope, cos, 0.0)
    sin_k = jnp.where(rope, sin, 0.0)
    return jnp.concatenate([cos_q, sin_q, cos_k, sin_k], axis=1).astype(f32)


def _rot_cols(w):
    half = w.shape[-1] // 2
    return jnp.concatenate([-w[..., half:], w[..., :half]], axis=-1)


def _stacked_weights(norm_in, w_in, norm_q_lat, w_uq, norm_kv_lat, w_ukv, norm_out_a, norm_out_b, w_out):
    depth, d, _ = w_in.shape
    sizes = (A_WIDTH, A_KV_HEADS * A_HEAD_DIM, A_KV_HEADS * A_HEAD_DIM, A_WIDTH,
             Q_LORA_RANK, KV_LORA_RANK, B_ROPE_DIM, B_WIDTH)
    offs = [sum(sizes[:i]) for i in range(len(sizes) + 1)]
    qa, ka, va, ga, cq, ckv, kr, gb = (w_in[..., offs[i]:offs[i + 1]] for i in range(len(sizes)))
    tail = B_QK_PAD - B_NOPE_DIM - B_ROPE_DIM
    zeros = lambda *shape: jnp.zeros(shape, w_in.dtype)
    kr_groups = jnp.concatenate([zeros(depth, d, B_NOPE_DIM), kr, zeros(depth, d, tail + B_NOPE_DIM),
                                 _rot_cols(kr), zeros(depth, d, tail)], axis=-1)
    w1 = jnp.concatenate([qa, cq, ckv, ka, kr_groups], axis=-1).astype(bf16)

    r = w_uq.shape[1]
    uq = w_uq.reshape(depth, r, B_HEADS, B_NOPE_DIM + B_ROPE_DIM)
    nope, rope = uq[..., :B_NOPE_DIM], uq[..., B_NOPE_DIM:]
    wqa = jnp.concatenate([nope, rope, zeros(depth, r, B_HEADS, tail)], axis=-1)
    wqb = jnp.concatenate([jnp.zeros_like(nope), _rot_cols(rope), zeros(depth, r, B_HEADS, tail)], axis=-1)

    rk = w_ukv.shape[1]
    ukv = w_ukv.reshape(depth, rk, B_HEADS, B_NOPE_DIM + B_V_DIM)
    k_nope, v = ukv[..., :B_NOPE_DIM], ukv[..., B_NOPE_DIM:]
    wk = jnp.concatenate([k_nope, zeros(depth, rk, B_HEADS, B_QK_PAD - B_NOPE_DIM)], axis=-1)
    row = lambda g: g.astype(f32)[:, None, :]
    return {
        "gin": row(norm_in),
        "w1": w1,
        "wg": jnp.concatenate([ga, gb], axis=-1).astype(bf16),
        "wvat": jnp.swapaxes(va, 1, 2).astype(bf16),
        "gq": row(norm_q_lat),
        "wqa": wqa.reshape(depth, r, B_HEADS * B_QK_PAD).astype(bf16),
        "wqb": wqb.reshape(depth, r, B_HEADS * B_QK_PAD).astype(bf16),
        "gkv": row(norm_kv_lat),
        "wk": wk.reshape(depth, rk, B_HEADS * B_QK_PAD).astype(bf16),
        "wvt": jnp.swapaxes(v.reshape(depth, rk, B_WIDTH), 1, 2).astype(bf16),
        "na": row(norm_out_a),
        "nb": row(norm_out_b),
        "wout": w_out.astype(bf16),
    }


def kernel(x, meta_tokens, rel_bias_table, norm_in, w_in, sink_a, norm_q_lat, w_uq, norm_kv_lat, w_ukv,
           norm_out_a, norm_out_b, w_out, norm_final):
    batch, seq, d = x.shape
    depth = w_in.shape[0]
    assert d == D_MODEL and seq % (MLA_Q_TILE * MLA_SUBTILES) == 0 and seq % ROW_TILE == 0

    h_real = x.reshape(batch * seq, d).astype(f32)
    h_meta = jnp.tile(meta_tokens.astype(f32), (batch, 1))
    tab_real = _rope_tables(N_META + jnp.arange(seq))
    tab_meta = _rope_tables(jnp.arange(batch * N_META) % N_META)
    bias_band, bias_meta, bias_mm, bias_mr = _bias_tables(rel_bias_table, seq // BLOCK)

    lw = _stacked_weights(norm_in, w_in, norm_q_lat, w_uq, norm_kv_lat, w_ukv, norm_out_a, norm_out_b, w_out)
    sinks = sink_a.astype(f32) * LOG2E
    sinks_rows = jnp.repeat(sinks.reshape(depth, A_KV_HEADS, A_GROUP), BLOCK, axis=2)[:, :, None, :]

    out = None
    for i in range(depth):
        last = i == depth - 1
        sink, sink_rows = sinks[i], sinks_rows[i]
        qa, ka, vat, q, k, vt = _proj(h_real, tab_real, lw, i)
        qam, kam, vatm, qm, km, vtm = _proj(h_meta, tab_meta, lw, i)
        vtm_b = vtm.reshape(B_HEADS, B_V_AUG, batch, N_META).transpose(2, 0, 1, 3)
        vatm_b = vatm.reshape(A_KV_HEADS, A_V_AUG, batch, N_META).transpose(2, 0, 1, 3)

        ya = _win(sink_rows, qa, ka, vat, kam, vatm_b, bias_band, bias_meta, batch, seq)
        yb = _mla(q, k, km, vt, vtm_b, batch, seq)
        if last:
            out = _out(h_real, ya, yb, lw, i, norm_final.astype(f32)[None, :])
        else:
            yam = _win_meta(sink, qam, ka, vat, kam, vatm_b, bias_mm, bias_mr, batch, seq)
            ybm = _mla_meta(qm, k, km, vt, vtm_b, batch, seq)
            h_real = _out(h_real, ya, yb, lw, i)
            h_meta = _out(h_meta, yam, ybm, lw, i)
    return out.reshape(batch, seq, d).astype(x.dtype)
```

```python
import functools
import math

import jax
import jax.numpy as jnp
from jax import lax
from jax.experimental import pallas as pl
from jax.experimental.pallas import tpu as pltpu

D_MODEL = 1024
N_META = 16
BLOCK = 128
WINDOW = 128
A_HEADS = 8
A_KV_HEADS = 2
A_GROUP = A_HEADS // A_KV_HEADS
A_HEAD_DIM = 64
A_WIDTH = A_HEADS * A_HEAD_DIM
B_HEADS = 8
B_NOPE_DIM = 64
B_ROPE_DIM = 32
B_V_DIM = 64
B_WIDTH = B_HEADS * B_V_DIM
B_QK_PAD = 128
BF16_SUBLANES = 16
B_V_AUG = B_V_DIM + BF16_SUBLANES
A_V_AUG = A_HEAD_DIM + BF16_SUBLANES
Q_LORA_RANK = 256
KV_LORA_RANK = 128
LAT_T_ROWS = KV_LORA_RANK + BF16_SUBLANES
N_BUCKETS = 32
MAX_DISTANCE = 128
ROPE_THETA = 10000.0
EPS = 1e-6

LOG2E = math.log2(math.e)
NEG = -1e30
ROW_TILE = 1024
MLA_Q_TILE = 256
MLA_SUBTILES = 8
MLA_STEPS_PER_ITER = 4
MLA_KEY_CHUNK = 1024
WIN_BLOCKS = 8
BIAS_ROW_CHUNK = 32
VMEM_LIMIT_BYTES = 56 * 1024 * 1024

_C_QA = 0
_C_CQ = _C_QA + A_WIDTH
_C_CKV = _C_CQ + Q_LORA_RANK
_C_KA = _C_CKV + KV_LORA_RANK
_C_KRA = _C_KA + A_KV_HEADS * A_HEAD_DIM
_C_KRB = _C_KRA + B_QK_PAD
_C_END = _C_KRB + B_QK_PAD

_NT = (((1,), (1,)), ((), ()))

bf16 = jnp.bfloat16
f32 = jnp.float32


def _cparams(*sem, flags=None):
    return pltpu.CompilerParams(dimension_semantics=sem, vmem_limit_bytes=VMEM_LIMIT_BYTES, flags=flags)


def _rms(x, gain):
    return x * lax.rsqrt(jnp.mean(x * x, axis=-1, keepdims=True) + EPS) * gain


def _silu(x):
    return x / (1.0 + jnp.exp(-x))


def _proj_kernel(h_ref, tab_ref, gin_ref, w1_ref, wvat_ref, gq_ref, wqa_ref, wqb_ref, gkv_ref, wk_ref, wvt_ref,
                 qa_ref, ka_ref, vat_ref, q_ref, k_ref, vt_ref, *latent_refs):
    u = _rms(h_ref[...], gin_ref[...]).astype(bf16)
    t = u.shape[0]

    def mm(lo, hi):
        return jnp.dot(u, w1_ref[:, lo:hi], preferred_element_type=f32)

    qa = mm(_C_QA, _C_CQ) * (A_HEAD_DIM ** -0.5 * LOG2E)
    for h in range(A_HEADS):
        qa_ref[h] = qa[:, h * A_HEAD_DIM:(h + 1) * A_HEAD_DIM].astype(bf16)
    vat = lax.dot_general(wvat_ref[...], u, _NT, preferred_element_type=f32)
    vat_ref[:, :A_HEAD_DIM, :] = vat.reshape(A_KV_HEADS, A_HEAD_DIM, t).astype(bf16)
    vat_ref[:, A_HEAD_DIM:, :] = jnp.ones((A_KV_HEADS, A_V_AUG - A_HEAD_DIM, t), bf16)
    mixed = mm(_C_CQ, _C_END)
    cq = mixed[:, :_C_CKV - _C_CQ]
    ckv = mixed[:, _C_CKV - _C_CQ:_C_KA - _C_CQ]
    ka = mixed[:, _C_KA - _C_CQ:_C_KRA - _C_CQ]
    kra = mixed[:, _C_KRA - _C_CQ:_C_KRB - _C_CQ]
    krb = mixed[:, _C_KRB - _C_CQ:]
    for j in range(A_KV_HEADS):
        ka_ref[j] = ka[:, j * A_HEAD_DIM:(j + 1) * A_HEAD_DIM].astype(bf16)
    tab_rows = pl.ds(pl.multiple_of((pl.program_id(0) % (tab_ref.shape[0] // t)) * t, t), t)
    cos_q = tab_ref[tab_rows, 0 * B_QK_PAD:1 * B_QK_PAD]
    sin_q = tab_ref[tab_rows, 1 * B_QK_PAD:2 * B_QK_PAD]
    cos_k = tab_ref[tab_rows, 2 * B_QK_PAD:3 * B_QK_PAD]
    sin_k = tab_ref[tab_rows, 3 * B_QK_PAD:4 * B_QK_PAD]

    cqn = _rms(cq, gq_ref[...]).astype(bf16)
    qa_part = jnp.dot(cqn, wqa_ref[...], preferred_element_type=f32)
    qb_part = jnp.dot(cqn, wqb_ref[...], preferred_element_type=f32)
    for h in range(B_HEADS):
        sl = slice(h * B_QK_PAD, (h + 1) * B_QK_PAD)
        q_ref[h] = (qa_part[:, sl] * cos_q + qb_part[:, sl] * sin_q).astype(bf16)

    ckvn_f32 = _rms(ckv, gkv_ref[...])
    ckvn = ckvn_f32.astype(bf16)
    k_nope = jnp.dot(ckvn, wk_ref[...], preferred_element_type=f32)
    k_rope = kra * cos_k + krb * sin_k
    for h in range(B_HEADS):
        k_ref[h] = (k_nope[:, h * B_QK_PAD:(h + 1) * B_QK_PAD] + k_rope).astype(bf16)
    vt = lax.dot_general(wvt_ref[...], ckvn, _NT, preferred_element_type=f32)
    vt_ref[:, :B_V_DIM, :] = vt.reshape(B_HEADS, B_V_DIM, t).astype(bf16)
    vt_ref[:, B_V_DIM:, :] = jnp.ones((B_HEADS, B_V_AUG - B_V_DIM, t), bf16)
    if latent_refs:
        kcat_ref, lat_t_ref = latent_refs
        kcat_ref[:, :KV_LORA_RANK] = ckvn
        kcat_ref[:, KV_LORA_RANK:] = k_rope.astype(bf16)
        lat_t_ref[:KV_LORA_RANK, :] = ckvn_f32.T.astype(bf16)
        lat_t_ref[KV_LORA_RANK:, :] = jnp.ones((LAT_T_ROWS - KV_LORA_RANK, t), bf16)


def _layer_spec(layer, shape):
    return pl.BlockSpec((None,) + tuple(shape), lambda i: (layer,) + (0,) * len(shape))


def _proj(h, tab, lw, layer, latent):
    rows = h.shape[0]
    t = min(ROW_TILE, rows)
    steps = rows // t
    assert tab.shape[0] % t == 0
    const = functools.partial(_layer_spec, layer)
    latent_specs = [pl.BlockSpec((t, KV_LORA_RANK + B_QK_PAD), lambda i: (i, 0)),
                    pl.BlockSpec((LAT_T_ROWS, t), lambda i: (0, i))] if latent else []
    latent_shapes = [jax.ShapeDtypeStruct((rows, KV_LORA_RANK + B_QK_PAD), bf16),
                     jax.ShapeDtypeStruct((LAT_T_ROWS, rows), bf16)] if latent else []
    return pl.pallas_call(
        _proj_kernel,
        grid=(steps,),
        in_specs=[
            pl.BlockSpec((t, D_MODEL), lambda i: (i, 0)),
            pl.BlockSpec(tab.shape, lambda i: (0, 0)),
            const((1, D_MODEL)),
            const((D_MODEL, _C_END)),
            const((A_KV_HEADS * A_HEAD_DIM, D_MODEL)),
            const((1, Q_LORA_RANK)),
            const((Q_LORA_RANK, B_HEADS * B_QK_PAD)),
            const((Q_LORA_RANK, B_HEADS * B_QK_PAD)),
            const((1, KV_LORA_RANK)),
            const((KV_LORA_RANK, B_HEADS * B_QK_PAD)),
            const((B_WIDTH, KV_LORA_RANK)),
        ],
        out_specs=[
            pl.BlockSpec((A_HEADS, t, A_HEAD_DIM), lambda i: (0, i, 0)),
            pl.BlockSpec((A_KV_HEADS, t, A_HEAD_DIM), lambda i: (0, i, 0)),
            pl.BlockSpec((A_KV_HEADS, A_V_AUG, t), lambda i: (0, 0, i)),
            pl.BlockSpec((B_HEADS, t, B_QK_PAD), lambda i: (0, i, 0)),
            pl.BlockSpec((B_HEADS, t, B_QK_PAD), lambda i: (0, i, 0)),
            pl.BlockSpec((B_HEADS, B_V_AUG, t), lambda i: (0, 0, i)),
        ] + latent_specs,
        out_shape=[
            jax.ShapeDtypeStruct((A_HEADS, rows, A_HEAD_DIM), bf16),
            jax.ShapeDtypeStruct((A_KV_HEADS, rows, A_HEAD_DIM), bf16),
            jax.ShapeDtypeStruct((A_KV_HEADS, A_V_AUG, rows), bf16),
            jax.ShapeDtypeStruct((B_HEADS, rows, B_QK_PAD), bf16),
            jax.ShapeDtypeStruct((B_HEADS, rows, B_QK_PAD), bf16),
            jax.ShapeDtypeStruct((B_HEADS, B_V_AUG, rows), bf16),
        ] + latent_shapes,
        compiler_params=_cparams("arbitrary"),
        name="proj",
    )(h, tab, lw["gin"], lw["w1"], lw["wvat"], lw["gq"], lw["wqa"], lw["wqb"], lw["gkv"], lw["wk"], lw["wvt"])


def _out_kernel(h_ref, ya_ref, yb_ref, gin_ref, wg_ref, na_ref, nb_ref, w_ref, *rest, final):
    if final:
        nf_ref, o_ref = rest
    else:
        (o_ref,) = rest
    hn = h_ref[...]
    u = _rms(hn, gin_ref[...]).astype(bf16)
    ga = _silu(jnp.dot(u, wg_ref[:, :A_WIDTH], preferred_element_type=f32))
    gb = _silu(jnp.dot(u, wg_ref[:, A_WIDTH:], preferred_element_type=f32))
    ya = _rms(ya_ref[...].astype(f32), na_ref[...]) * ga
    yb = _rms(yb_ref[...].astype(f32), nb_ref[...]) * gb
    hn = hn + jnp.dot(ya.astype(bf16), w_ref[:A_WIDTH, :], preferred_element_type=f32)
    hn = hn + jnp.dot(yb.astype(bf16), w_ref[A_WIDTH:, :], preferred_element_type=f32)
    if final:
        hn = _rms(hn, nf_ref[...])
    o_ref[...] = hn


def _out(h, ya, yb, lw, layer, norm_final=None):
    rows = h.shape[0]
    t = min(ROW_TILE, rows)
    row = lambda width: pl.BlockSpec((t, width), lambda i: (i, 0))
    const = functools.partial(_layer_spec, layer)
    final = norm_final is not None
    in_specs = [row(D_MODEL), row(A_WIDTH), row(B_WIDTH),
                const((1, D_MODEL)), const((D_MODEL, A_WIDTH + B_WIDTH)),
                const((1, A_WIDTH)), const((1, B_WIDTH)), const((A_WIDTH + B_WIDTH, D_MODEL))]
    args = [h, ya, yb, lw["gin"], lw["wg"], lw["na"], lw["nb"], lw["wout"]]
    if final:
        in_specs.append(pl.BlockSpec((1, D_MODEL), lambda i: (0, 0)))
        args.append(norm_final)
    return pl.pallas_call(
        functools.partial(_out_kernel, final=final),
        grid=(rows // t,),
        in_specs=in_specs,
        out_specs=row(D_MODEL),
        out_shape=jax.ShapeDtypeStruct((rows, D_MODEL), f32),
        compiler_params=_cparams("arbitrary"),
        name="outproj",
    )(*args)


def _bias_kernel(tab_ref, idx_ref, o_ref):
    rows = idx_ref.shape[0]
    chunk = math.gcd(rows, BIAS_ROW_CHUNK)

    def body(r, carry):
        sl = pl.ds(pl.multiple_of(r * chunk, chunk), chunk)
        idx = idx_ref[sl, :]
        accs = [jnp.where(idx < 0, NEG, 0.0).astype(f32)] * A_HEADS
        for b in range(N_BUCKETS):
            hit = idx == b
            accs = [jnp.where(hit, tab_ref[b * A_HEADS + h], accs[h]) for h in range(A_HEADS)]
        for h in range(A_HEADS):
            o_ref[h, sl, :] = accs[h]
        return carry

    lax.fori_loop(0, rows // chunk, body, 0)


def _bias_lookup(table_flat, idx):
    return pl.pallas_call(
        _bias_kernel,
        in_specs=[pl.BlockSpec(memory_space=pltpu.SMEM), pl.BlockSpec(idx.shape, lambda: (0, 0))],
        out_specs=pl.BlockSpec((A_HEADS,) + idx.shape, lambda: (0, 0, 0)),
        out_shape=jax.ShapeDtypeStruct((A_HEADS,) + idx.shape, f32),
        compiler_params=pltpu.CompilerParams(vmem_limit_bytes=VMEM_LIMIT_BYTES),
        name="relbias",
    )(table_flat, idx)


def _t5_bucket(rel):
    nb = N_BUCKETS // 2
    max_exact = nb // 2
    ret = jnp.where(rel > 0, nb, 0)
    n = jnp.abs(rel)
    nf = jnp.maximum(n, 1).astype(jnp.float32)
    large = max_exact + (jnp.log(nf / max_exact) / math.log(MAX_DISTANCE / max_exact)
                         * (nb - max_exact)).astype(jnp.int32)
    large = jnp.minimum(large, nb - 1)
    bucket = ret + jnp.where(n < max_exact, n, large)
    return jnp.bitwise_and(bucket, N_BUCKETS - 1)


def _bias_tables(rel_bias_table, nblk):
    table_flat = (rel_bias_table.astype(f32) * LOG2E).reshape(-1)
    i = jnp.arange(BLOCK, dtype=jnp.int32)[:, None]
    j = jnp.arange(3 * BLOCK, dtype=jnp.int32)[None, :]
    rel = j - i - BLOCK
    band = jnp.where(jnp.abs(rel) <= WINDOW, _t5_bucket(rel), -1)
    interior = (_bias_lookup(table_flat, band.T)
                .reshape(A_KV_HEADS, A_GROUP, 3 * BLOCK, BLOCK)
                .transpose(0, 2, 1, 3).reshape(A_KV_HEADS, 3 * BLOCK, A_GROUP * BLOCK))
    key = jnp.arange(3 * BLOCK, dtype=jnp.int32)[None, :, None]
    first = jnp.where(key < BLOCK, NEG, interior)
    last = jnp.where(key >= 2 * BLOCK, NEG, interior)
    bias_band = jnp.stack([first, interior, last])

    n = jnp.arange(nblk, dtype=jnp.int32)[:, None, None]
    k = jnp.arange(N_META, dtype=jnp.int32)[None, :, None]
    qi = jnp.arange(BLOCK, dtype=jnp.int32)[None, None, :]
    rel_m = k - (N_META + n * BLOCK + qi)
    idx_m = _t5_bucket(rel_m).reshape(nblk * N_META, BLOCK)
    bias_meta = (_bias_lookup(table_flat, idx_m)
                 .reshape(A_KV_HEADS, A_GROUP, nblk, N_META, BLOCK)
                 .transpose(2, 0, 3, 1, 4).reshape(nblk, A_KV_HEADS, N_META, A_GROUP * BLOCK))

    qp = jnp.arange(N_META, dtype=jnp.int32)[:, None]
    kp = jnp.arange(N_META + BLOCK, dtype=jnp.int32)[None, :]
    rel_q = kp - qp
    idx_q = jnp.where(jnp.abs(rel_q) <= WINDOW, _t5_bucket(rel_q), -1)
    bias_q = _bias_lookup(table_flat, idx_q)
    return bias_band, bias_meta, bias_q[:, :, :N_META], bias_q[:, :, N_META:]


def _win_kernel(sink_ref, q_ref, kp_ref, kc_ref, kn_ref, vp_ref, vc_ref, vn_ref, kam_ref, vam_ref,
                bias_ref, bm_ref, o_ref):
    w = WIN_BLOCKS
    i, ntile = pl.program_id(1), pl.num_programs(1)

    def qrows(blk):
        return slice(blk * BLOCK, (blk + 1) * BLOCK)

    def band(blk, prev_ref, cur_ref, next_ref, kvh, axis):
        take = lambda ref, lo, hi: ref[kvh, lo:hi, :] if axis == 0 else ref[kvh, :, lo:hi]
        parts = [take(cur_ref, max(blk - 1, 0) * BLOCK, min(blk + 2, w) * BLOCK)]
        if blk == 0:
            parts.insert(0, take(prev_ref, (w - 1) * BLOCK, w * BLOCK))
        if blk == w - 1:
            parts.append(take(next_ref, 0, BLOCK))
        return jnp.concatenate(parts, axis=axis)

    def variant(blk):
        v = 1
        if blk == 0:
            v = jnp.where(i == 0, 0, v)
        if blk == w - 1:
            v = jnp.where(i == ntile - 1, 2, v)
        return v

    def scores(blk, kvh):
        kb = jnp.concatenate([band(blk, kp_ref, kc_ref, kn_ref, kvh, 0), kam_ref[kvh]], axis=0)
        q4 = q_ref[A_GROUP * kvh:A_GROUP * (kvh + 1), qrows(blk), :].reshape(A_GROUP * BLOCK, A_HEAD_DIM)
        bias = jnp.concatenate([bias_ref[variant(blk), kvh], bm_ref[blk, kvh]], axis=0)
        s = lax.dot_general(kb, q4, _NT, preferred_element_type=f32) + bias
        m = jnp.maximum(jnp.max(s, axis=0, keepdims=True), sink_ref[kvh])
        return s, m

    def attend(blk, kvh, s, m):
        vb = jnp.concatenate([band(blk, vp_ref, vc_ref, vn_ref, kvh, 1), vam_ref[0, kvh]], axis=1)
        p = jnp.exp2(s - m).astype(bf16)
        acc = jnp.dot(vb, p, preferred_element_type=f32)
        den = acc[A_HEAD_DIM:A_HEAD_DIM + 1] + jnp.exp2(sink_ref[kvh] - m)
        return acc[:A_HEAD_DIM] * (1.0 / den)

    units = [(blk, kvh) for blk in range(w) for kvh in range(A_KV_HEADS)]
    outs = {}
    pending = scores(*units[0])
    for u, unit in enumerate(units):
        nxt = scores(*units[u + 1]) if u + 1 < len(units) else None
        outs[unit] = attend(*unit, *pending)
        pending = nxt
        blk, kvh = unit
        if kvh == A_KV_HEADS - 1:
            ot = jnp.concatenate([outs[(blk, j)][:, g * BLOCK:(g + 1) * BLOCK]
                                  for j in range(A_KV_HEADS) for g in range(A_GROUP)], axis=0)
            o_ref[qrows(blk), :] = ot.T.astype(o_ref.dtype)


def _win(sink_rows, qa, ka, vat, kam, vatm, bias_band, bias_meta, batch, seq):
    tb = WIN_BLOCKS * BLOCK
    ntile = seq // tb
    prev_ = lambda b, i: b * ntile + jnp.maximum(i - 1, 0)
    cur_ = lambda b, i: b * ntile + i
    next_ = lambda b, i: b * ntile + jnp.minimum(i + 1, ntile - 1)
    kspec = lambda f: pl.BlockSpec((A_KV_HEADS, tb, A_HEAD_DIM), lambda b, i: (0, f(b, i), 0))
    vspec = lambda f: pl.BlockSpec((A_KV_HEADS, A_V_AUG, tb), lambda b, i: (0, 0, f(b, i)))
    return pl.pallas_call(
        _win_kernel,
        grid=(batch, ntile),
        in_specs=[
            pl.BlockSpec((A_KV_HEADS, 1, A_GROUP * BLOCK), lambda b, i: (0, 0, 0)),
            pl.BlockSpec((A_HEADS, tb, A_HEAD_DIM), lambda b, i: (0, cur_(b, i), 0)),
            kspec(prev_), kspec(cur_), kspec(next_),
            vspec(prev_), vspec(cur_), vspec(next_),
            pl.BlockSpec((A_KV_HEADS, N_META, A_HEAD_DIM), lambda b, i: (0, b, 0)),
            pl.BlockSpec((1, A_KV_HEADS, A_V_AUG, N_META), lambda b, i: (b, 0, 0, 0)),
            pl.BlockSpec((3, A_KV_HEADS, 3 * BLOCK, A_GROUP * BLOCK), lambda b, i: (0, 0, 0, 0)),
            pl.BlockSpec((WIN_BLOCKS, A_KV_HEADS, N_META, A_GROUP * BLOCK), lambda b, i: (i, 0, 0, 0)),
        ],
        out_specs=pl.BlockSpec((tb, A_WIDTH), lambda b, i: (cur_(b, i), 0)),
        out_shape=jax.ShapeDtypeStruct((batch * seq, A_WIDTH), bf16),
        compiler_params=_cparams("arbitrary", "arbitrary"),
        name="win",
    )(sink_rows, qa, ka, ka, ka, vat, vat, vat, kam, vatm, bias_band, bias_meta)


def _win_meta_kernel(sink_ref, q_ref, ka_ref, vat_ref, kam_ref, vatm_ref, biasm_ref, biasr_ref, o_ref):
    for kvh in range(A_KV_HEADS):
        q4 = q_ref[A_GROUP * kvh:A_GROUP * (kvh + 1)].reshape(A_GROUP * N_META, A_HEAD_DIM)
        sm = lax.dot_general(q4, kam_ref[kvh], _NT, preferred_element_type=f32)
        sr = lax.dot_general(q4, ka_ref[kvh], _NT, preferred_element_type=f32)
        for g in range(A_GROUP):
            h = A_GROUP * kvh + g
            rows = slice(g * N_META, (g + 1) * N_META)
            smg = sm[rows] + biasm_ref[h]
            srg = sr[rows] + biasr_ref[h]
            sink = sink_ref[h]
            m = jnp.maximum(jnp.maximum(jnp.max(smg, axis=-1, keepdims=True),
                                        jnp.max(srg, axis=-1, keepdims=True)), sink)
            pm = jnp.exp2(smg - m).astype(bf16)
            pr = jnp.exp2(srg - m).astype(bf16)
            o = (lax.dot_general(pm, vatm_ref[0, kvh], _NT, preferred_element_type=f32)
                 + lax.dot_general(pr, vat_ref[kvh], _NT, preferred_element_type=f32))
            den = o[:, A_HEAD_DIM:A_HEAD_DIM + 1] + jnp.exp2(sink - m)
            o_ref[:, h * A_HEAD_DIM:(h + 1) * A_HEAD_DIM] = (o[:, :A_HEAD_DIM] * (1.0 / den)).astype(o_ref.dtype)


def _win_meta(sink, qam, ka, vat, kam, vatm, bias_mm, bias_mr, batch, seq):
    nblk = seq // BLOCK
    const3 = lambda shape: pl.BlockSpec(shape, lambda b: (0, 0, 0))
    return pl.pallas_call(
        _win_meta_kernel,
        grid=(batch,),
        in_specs=[
            pl.BlockSpec(memory_space=pltpu.SMEM),
            pl.BlockSpec((A_HEADS, N_META, A_HEAD_DIM), lambda b: (0, b, 0)),
            pl.BlockSpec((A_KV_HEADS, BLOCK, A_HEAD_DIM), lambda b: (0, b * nblk, 0)),
            pl.BlockSpec((A_KV_HEADS, A_V_AUG, BLOCK), lambda b: (0, 0, b * nblk)),
            pl.BlockSpec((A_KV_HEADS, N_META, A_HEAD_DIM), lambda b: (0, b, 0)),
            pl.BlockSpec((1, A_KV_HEADS, A_V_AUG, N_META), lambda b: (b, 0, 0, 0)),
            const3((A_HEADS, N_META, N_META)),
            const3((A_HEADS, N_META, BLOCK)),
        ],
        out_specs=pl.BlockSpec((N_META, A_WIDTH), lambda b: (b, 0)),
        out_shape=jax.ShapeDtypeStruct((batch * N_META, A_WIDTH), bf16),
        compiler_params=_cparams("arbitrary"),
        name="win_meta",
    )(sink, qam, ka, vat, kam, vatm, bias_mm, bias_mr)


def _mla_kernel(q_ref, k_ref, km_ref, vt_ref, vtm_ref, o_ref, acc_ref, s0_ref, s1_ref, sm0_ref, sm1_ref):
    s_bufs, sm_bufs = (s0_ref, s1_ref), (sm0_ref, sm1_ref)
    nsub, _, _, tq = acc_ref.shape
    units = nsub * B_HEADS
    seq = k_ref.shape[1]
    chunks = [(c * MLA_KEY_CHUNK, (c + 1) * MLA_KEY_CHUNK) for c in range(seq // MLA_KEY_CHUNK)]

    def unit(u):
        return u % B_HEADS, u // B_HEADS

    def step(t, slot, m, *, score_next=True, attend_cur=True):
        if score_next:
            hn, subn = unit(t + 1)
            q = q_ref[hn, pl.ds(pl.multiple_of(subn * tq, tq), tq), :]
            sm = lax.dot_general(km_ref[hn], q, _NT, preferred_element_type=f32)
            sm_bufs[1 - slot][...] = sm
            m_next = jnp.max(sm, axis=0, keepdims=True)
        if attend_cur:
            h, sub = unit(t)
            pm = jnp.exp2(sm_bufs[slot][...] - m).astype(bf16)
            acc = jnp.dot(vtm_ref[0, h], pm, preferred_element_type=f32)
        for lo, hi in chunks:
            if score_next:
                s = lax.dot_general(k_ref[hn, lo:hi, :], q, _NT, preferred_element_type=f32)
                s_bufs[1 - slot][lo:hi, :] = s
                m_next = jnp.maximum(m_next, jnp.max(s, axis=0, keepdims=True))
            if attend_cur:
                p = jnp.exp2(s_bufs[slot][lo:hi, :] - m).astype(bf16)
                acc = acc + jnp.dot(vt_ref[h, :, lo:hi], p, preferred_element_type=f32)
        if attend_cur:
            acc_ref[sub, h] = acc[:B_V_DIM] * (1.0 / acc[B_V_DIM:B_V_DIM + 1])
        return m_next if score_next else None

    def steps(j, m):
        for i in range(MLA_STEPS_PER_ITER):
            m = step(MLA_STEPS_PER_ITER * j + i, i % 2, m)
        return m

    assert MLA_STEPS_PER_ITER % 2 == 0 and units % 2 == 0
    iters = (units - 1) // MLA_STEPS_PER_ITER
    m = step(-1, 1, None, attend_cur=False)
    m = lax.fori_loop(0, iters, steps, m)
    for t in range(iters * MLA_STEPS_PER_ITER, units - 1):
        m = step(t, t % 2, m)
    step(units - 1, 1, m, score_next=False)
    for sub in range(nsub):
        o_ref[sub * tq:(sub + 1) * tq, :] = acc_ref[sub].reshape(B_WIDTH, tq).T.astype(o_ref.dtype)


def _mla(q, k, km, vt, vtm, batch, seq):
    tq = MLA_Q_TILE
    tb = MLA_Q_TILE * MLA_SUBTILES
    nq = seq // tb
    return pl.pallas_call(
        _mla_kernel,
        grid=(batch, nq),
        in_specs=[
            pl.BlockSpec((B_HEADS, tb, B_QK_PAD), lambda b, i: (0, b * nq + i, 0)),
            pl.BlockSpec((B_HEADS, seq, B_QK_PAD), lambda b, i: (0, b, 0)),
            pl.BlockSpec((B_HEADS, N_META, B_QK_PAD), lambda b, i: (0, b, 0)),
            pl.BlockSpec((B_HEADS, B_V_AUG, seq), lambda b, i: (0, 0, b)),
            pl.BlockSpec((1, B_HEADS, B_V_AUG, N_META), lambda b, i: (b, 0, 0, 0)),
        ],
        out_specs=pl.BlockSpec((tb, B_WIDTH), lambda b, i: (b * nq + i, 0)),
        out_shape=jax.ShapeDtypeStruct((batch * seq, B_WIDTH), bf16),
        scratch_shapes=([pltpu.VMEM((MLA_SUBTILES, B_HEADS, B_V_DIM, tq), f32)]
                        + [pltpu.VMEM((seq, tq), f32)] * 2 + [pltpu.VMEM((N_META, tq), f32)] * 2),
        compiler_params=_cparams("arbitrary", "arbitrary"),
        name="mla",
    )(q, k, km, vt, vtm)


def _mla_meta_kernel(q_ref, kcat_ref, kcatm_ref, lat_t_ref, lat_tm_ref, wuk_ref, wuv_ref, o_ref):
    lane = lax.broadcasted_iota(jnp.int32, (N_META, B_QK_PAD), 1)
    rope_lanes = (lane >= B_NOPE_DIM) & (lane < B_NOPE_DIM + B_ROPE_DIM)
    q_rows = []
    for h in range(B_HEADS):
        q = q_ref[h]
        q_lat = jnp.dot(q, wuk_ref[h], preferred_element_type=f32).astype(bf16)
        q_rows.append(jnp.concatenate([q_lat, jnp.where(rope_lanes, q, jnp.zeros_like(q))], axis=1))
    qcat = jnp.concatenate(q_rows, axis=0)
    seq = kcat_ref.shape[0]
    chunks = [(c * MLA_KEY_CHUNK, (c + 1) * MLA_KEY_CHUNK) for c in range(seq // MLA_KEY_CHUNK)]
    sm = lax.dot_general(kcatm_ref[...], qcat, _NT, preferred_element_type=f32)
    ss = [lax.dot_general(kcat_ref[lo:hi, :], qcat, _NT, preferred_element_type=f32) for lo, hi in chunks]
    m = jnp.max(sm, axis=0, keepdims=True)
    for s in ss:
        m = jnp.maximum(m, jnp.max(s, axis=0, keepdims=True))
    acc = jnp.dot(lat_tm_ref[0], jnp.exp2(sm - m).astype(bf16), preferred_element_type=f32)
    for (lo, hi), s in zip(chunks, ss):
        acc = acc + jnp.dot(lat_t_ref[:, lo:hi], jnp.exp2(s - m).astype(bf16),
                            preferred_element_type=f32)
    o_lat = (acc[:KV_LORA_RANK] * (1.0 / acc[KV_LORA_RANK:KV_LORA_RANK + 1])).T.astype(bf16)
    for h in range(B_HEADS):
        o_ref[:, h * B_V_DIM:(h + 1) * B_V_DIM] = jnp.dot(
            o_lat[h * N_META:(h + 1) * N_META], wuv_ref[h], preferred_element_type=f32).astype(o_ref.dtype)


def _mla_meta(qm, kcat, kcatm, lat_t, lat_tm, lw, layer, batch, seq):
    const = functools.partial(_layer_spec, layer)
    return pl.pallas_call(
        _mla_meta_kernel,
        grid=(batch,),
        in_specs=[
            pl.BlockSpec((B_HEADS, N_META, B_QK_PAD), lambda b: (0, b, 0)),
            pl.BlockSpec((seq, KV_LORA_RANK + B_QK_PAD), lambda b: (b, 0)),
            pl.BlockSpec((N_META, KV_LORA_RANK + B_QK_PAD), lambda b: (b, 0)),
            pl.BlockSpec((LAT_T_ROWS, seq), lambda b: (0, b)),
            pl.BlockSpec((1, LAT_T_ROWS, N_META), lambda b: (b, 0, 0)),
            const((B_HEADS, B_QK_PAD, KV_LORA_RANK)),
            const((B_HEADS, KV_LORA_RANK, B_V_DIM)),
        ],
        out_specs=pl.BlockSpec((N_META, B_WIDTH), lambda b: (b, 0)),
        out_shape=jax.ShapeDtypeStruct((batch * N_META, B_WIDTH), bf16),
        compiler_params=_cparams("arbitrary"),
        name="mla_meta",
    )(qm, kcat, kcatm, lat_t, lat_tm, lw["wuk"], lw["wuv"])


def _rope_tables(pos):
    half = B_ROPE_DIM // 2
    freqs = ROPE_THETA ** (-jnp.arange(half, dtype=jnp.float32) / half)
    tail = B_QK_PAD - B_NOPE_DIM - B_ROPE_DIM
    lane_freq = jnp.concatenate([jnp.zeros((B_NOPE_DIM,), f32), freqs, freqs, jnp.zeros((tail,), f32)])
    ang = pos.astype(jnp.float32)[:, None] * lane_freq[None, :]
    cos, sin = jnp.cos(ang), jnp.sin(ang)
    lane = jnp.arange(B_QK_PAD)[None, :]
    nope = lane < B_NOPE_DIM
    rope = (lane >= B_NOPE_DIM) & (lane < B_NOPE_DIM + B_ROPE_DIM)
    c_mla = (B_NOPE_DIM + B_ROPE_DIM) ** -0.5 * LOG2E
    cos_q = jnp.where(nope, c_mla, jnp.where(rope, c_mla * cos, 0.0))
    sin_q = jnp.where(rope, c_mla * sin, 0.0)
    cos_k = jnp.where(rope, cos, 0.0)
    sin_k = jnp.where(rope, sin, 0.0)
    return jnp.concatenate([cos_q, sin_q, cos_k, sin_k], axis=1).astype(f32)


def _rot_cols(w):
    half = w.shape[-1] // 2
    return jnp.concatenate([-w[..., half:], w[..., :half]], axis=-1)


def _stacked_weights(norm_in, w_in, norm_q_lat, w_uq, norm_kv_lat, w_ukv, norm_out_a, norm_out_b, w_out):
    depth, d, _ = w_in.shape
    sizes = (A_WIDTH, A_KV_HEADS * A_HEAD_DIM, A_KV_HEADS * A_HEAD_DIM, A_WIDTH,
             Q_LORA_RANK, KV_LORA_RANK, B_ROPE_DIM, B_WIDTH)
    offs = [sum(sizes[:i]) for i in range(len(sizes) + 1)]
    qa, ka, va, ga, cq, ckv, kr, gb = (w_in[..., offs[i]:offs[i + 1]] for i in range(len(sizes)))
    tail = B_QK_PAD - B_NOPE_DIM - B_ROPE_DIM
    zeros = lambda *shape: jnp.zeros(shape, w_in.dtype)
    kr_groups = jnp.concatenate([zeros(depth, d, B_NOPE_DIM), kr, zeros(depth, d, tail + B_NOPE_DIM),
                                 _rot_cols(kr), zeros(depth, d, tail)], axis=-1)
    w1 = jnp.concatenate([qa, cq, ckv, ka, kr_groups], axis=-1).astype(bf16)

    r = w_uq.shape[1]
    uq = w_uq.reshape(depth, r, B_HEADS, B_NOPE_DIM + B_ROPE_DIM)
    nope, rope = uq[..., :B_NOPE_DIM], uq[..., B_NOPE_DIM:]
    wqa = jnp.concatenate([nope, rope, zeros(depth, r, B_HEADS, tail)], axis=-1)
    wqb = jnp.concatenate([jnp.zeros_like(nope), _rot_cols(rope), zeros(depth, r, B_HEADS, tail)], axis=-1)

    rk = w_ukv.shape[1]
    ukv = w_ukv.reshape(depth, rk, B_HEADS, B_NOPE_DIM + B_V_DIM)
    k_nope, v = ukv[..., :B_NOPE_DIM], ukv[..., B_NOPE_DIM:]
    wk = jnp.concatenate([k_nope, zeros(depth, rk, B_HEADS, B_QK_PAD - B_NOPE_DIM)], axis=-1)
    row = lambda g: g.astype(f32)[:, None, :]
    return {
        "gin": row(norm_in),
        "w1": w1,
        "wg": jnp.concatenate([ga, gb], axis=-1).astype(bf16),
        "wvat": jnp.swapaxes(va, 1, 2).astype(bf16),
        "gq": row(norm_q_lat),
        "wqa": wqa.reshape(depth, r, B_HEADS * B_QK_PAD).astype(bf16),
        "wqb": wqb.reshape(depth, r, B_HEADS * B_QK_PAD).astype(bf16),
        "gkv": row(norm_kv_lat),
        "wk": wk.reshape(depth, rk, B_HEADS * B_QK_PAD).astype(bf16),
        "wvt": jnp.swapaxes(v.reshape(depth, rk, B_WIDTH), 1, 2).astype(bf16),
        "wuk": jnp.concatenate([jnp.transpose(k_nope, (0, 2, 3, 1)),
                                zeros(depth, B_HEADS, B_QK_PAD - B_NOPE_DIM, rk)], axis=2).astype(bf16),
        "wuv": jnp.transpose(v, (0, 2, 1, 3)).astype(bf16),
        "na": row(norm_out_a),
        "nb": row(norm_out_b),
        "wout": w_out.astype(bf16),
    }


def kernel(x, meta_tokens, rel_bias_table, norm_in, w_in, sink_a, norm_q_lat, w_uq, norm_kv_lat, w_ukv,
           norm_out_a, norm_out_b, w_out, norm_final):
    batch, seq, d = x.shape
    depth = w_in.shape[0]
    assert d == D_MODEL and seq % (MLA_Q_TILE * MLA_SUBTILES) == 0 and seq % ROW_TILE == 0

    h_real = x.reshape(batch * seq, d).astype(f32)
    h_meta = jnp.tile(meta_tokens.astype(f32), (batch, 1))
    tab_real = _rope_tables(N_META + jnp.arange(seq))
    tab_meta = _rope_tables(jnp.arange(batch * N_META) % N_META)
    bias_band, bias_meta, bias_mm, bias_mr = _bias_tables(rel_bias_table, seq // BLOCK)

    lw = _stacked_weights(norm_in, w_in, norm_q_lat, w_uq, norm_kv_lat, w_ukv, norm_out_a, norm_out_b, w_out)
    sinks = sink_a.astype(f32) * LOG2E
    sinks_rows = jnp.repeat(sinks.reshape(depth, A_KV_HEADS, A_GROUP), BLOCK, axis=2)[:, :, None, :]

    out = None
    for i in range(depth):
        last = i == depth - 1
        sink, sink_rows = sinks[i], sinks_rows[i]
        qa, ka, vat, q, k, vt, *lat = _proj(h_real, tab_real, lw, i, latent=not last)
        qam, kam, vatm, qm, km, vtm, *latm = _proj(h_meta, tab_meta, lw, i, latent=not last)
        vtm_b = vtm.reshape(B_HEADS, B_V_AUG, batch, N_META).transpose(2, 0, 1, 3)
        vatm_b = vatm.reshape(A_KV_HEADS, A_V_AUG, batch, N_META).transpose(2, 0, 1, 3)

        ya = _win(sink_rows, qa, ka, vat, kam, vatm_b, bias_band, bias_meta, batch, seq)
        yb = _mla(q, k, km, vt, vtm_b, batch, seq)
        if last:
            out = _out(h_real, ya, yb, lw, i, norm_final.astype(f32)[None, :])
        else:
            yam = _win_meta(sink, qam, ka, vat, kam, vatm_b, bias_mm, bias_mr, batch, seq)
            (kcat, lat_t), (kcatm, lat_tm) = lat, latm
            lat_tm_b = lat_tm.reshape(LAT_T_ROWS, batch, N_META).transpose(1, 0, 2)
            ybm = _mla_meta(qm, kcat, kcatm, lat_t, lat_tm_b, lw, i, batch, seq)
            h_real = _out(h_real, ya, yb, lw, i)
            h_meta = _out(h_meta, yam, ybm, lw, i)
    return out.reshape(batch, seq, d).astype(x.dtype)
```

```python
import functools
import math

import jax
import jax.numpy as jnp
from jax import lax
from jax.experimental import pallas as pl
from jax.experimental.pallas import tpu as pltpu

D_MODEL = 1024
N_META = 16
BLOCK = 128
WINDOW = 128
A_HEADS = 8
A_KV_HEADS = 2
A_GROUP = A_HEADS // A_KV_HEADS
A_HEAD_DIM = 64
A_WIDTH = A_HEADS * A_HEAD_DIM
B_HEADS = 8
B_NOPE_DIM = 64
B_ROPE_DIM = 32
B_V_DIM = 64
B_WIDTH = B_HEADS * B_V_DIM
B_QK_PAD = 128
BF16_SUBLANES = 16
B_V_AUG = B_V_DIM + BF16_SUBLANES
A_V_AUG = A_HEAD_DIM + BF16_SUBLANES
Q_LORA_RANK = 256
KV_LORA_RANK = 128
LAT_T_ROWS = KV_LORA_RANK + BF16_SUBLANES
N_BUCKETS = 32
MAX_DISTANCE = 128
ROPE_THETA = 10000.0
EPS = 1e-6

LOG2E = math.log2(math.e)
NEG = -1e30
ROW_TILE = 1024
MLA_Q_TILE = 256
MLA_SUBTILES = 8
MLA_STEPS_PER_ITER = 8
MLA_KEY_CHUNK = 1024
WIN_BLOCKS = 8
WIN_AHEAD = 2
BIAS_ROW_CHUNK = 32
VMEM_LIMIT_BYTES = 56 * 1024 * 1024

_C_QA = 0
_C_CQ = _C_QA + A_WIDTH
_C_CKV = _C_CQ + Q_LORA_RANK
_C_KA = _C_CKV + KV_LORA_RANK
_C_KRA = _C_KA + A_KV_HEADS * A_HEAD_DIM
_C_KRB = _C_KRA + B_QK_PAD
_C_END = _C_KRB + B_QK_PAD

_NT = (((1,), (1,)), ((), ()))

bf16 = jnp.bfloat16
f32 = jnp.float32


def _cparams(*sem, flags=None):
    return pltpu.CompilerParams(dimension_semantics=sem, vmem_limit_bytes=VMEM_LIMIT_BYTES, flags=flags)


def _rms(x, gain):
    return x * lax.rsqrt(jnp.mean(x * x, axis=-1, keepdims=True) + EPS) * gain


def _silu(x):
    return x / (1.0 + jnp.exp(-x))


def _proj_kernel(h_ref, tab_ref, gin_ref, w1_ref, wvat_ref, gq_ref, wqa_ref, wqb_ref, gkv_ref, wk_ref, wvt_ref,
                 qa_ref, ka_ref, vat_ref, q_ref, k_ref, vt_ref, *latent_refs):
    u = _rms(h_ref[...], gin_ref[...]).astype(bf16)
    t = u.shape[0]

    def mm(lo, hi):
        return jnp.dot(u, w1_ref[:, lo:hi], preferred_element_type=f32)

    qa = mm(_C_QA, _C_CQ) * (A_HEAD_DIM ** -0.5 * LOG2E)
    for h in range(A_HEADS):
        qa_ref[h] = qa[:, h * A_HEAD_DIM:(h + 1) * A_HEAD_DIM].astype(bf16)
    vat = lax.dot_general(wvat_ref[...], u, _NT, preferred_element_type=f32)
    vat_ref[:, :A_HEAD_DIM, :] = vat.reshape(A_KV_HEADS, A_HEAD_DIM, t).astype(bf16)
    vat_ref[:, A_HEAD_DIM:, :] = jnp.ones((A_KV_HEADS, A_V_AUG - A_HEAD_DIM, t), bf16)
    mixed = mm(_C_CQ, _C_END)
    cq = mixed[:, :_C_CKV - _C_CQ]
    ckv = mixed[:, _C_CKV - _C_CQ:_C_KA - _C_CQ]
    ka = mixed[:, _C_KA - _C_CQ:_C_KRA - _C_CQ]
    kra = mixed[:, _C_KRA - _C_CQ:_C_KRB - _C_CQ]
    krb = mixed[:, _C_KRB - _C_CQ:]
    for j in range(A_KV_HEADS):
        ka_ref[j] = ka[:, j * A_HEAD_DIM:(j + 1) * A_HEAD_DIM].astype(bf16)
    tab_rows = pl.ds(pl.multiple_of((pl.program_id(0) % (tab_ref.shape[0] // t)) * t, t), t)
    cos_q = tab_ref[tab_rows, 0 * B_QK_PAD:1 * B_QK_PAD]
    sin_q = tab_ref[tab_rows, 1 * B_QK_PAD:2 * B_QK_PAD]
    cos_k = tab_ref[tab_rows, 2 * B_QK_PAD:3 * B_QK_PAD]
    sin_k = tab_ref[tab_rows, 3 * B_QK_PAD:4 * B_QK_PAD]

    cqn = _rms(cq, gq_ref[...]).astype(bf16)
    qa_part = jnp.dot(cqn, wqa_ref[...], preferred_element_type=f32)
    qb_part = jnp.dot(cqn, wqb_ref[...], preferred_element_type=f32)
    for h in range(B_HEADS):
        sl = slice(h * B_QK_PAD, (h + 1) * B_QK_PAD)
        q_ref[h] = (qa_part[:, sl] * cos_q + qb_part[:, sl] * sin_q).astype(bf16)

    ckvn_f32 = _rms(ckv, gkv_ref[...])
    ckvn = ckvn_f32.astype(bf16)
    k_nope = jnp.dot(ckvn, wk_ref[...], preferred_element_type=f32)
    k_rope = kra * cos_k + krb * sin_k
    for h in range(B_HEADS):
        k_ref[h] = (k_nope[:, h * B_QK_PAD:(h + 1) * B_QK_PAD] + k_rope).astype(bf16)
    vt = lax.dot_general(wvt_ref[...], ckvn, _NT, preferred_element_type=f32)
    vt_ref[:, :B_V_DIM, :] = vt.reshape(B_HEADS, B_V_DIM, t).astype(bf16)
    vt_ref[:, B_V_DIM:, :] = jnp.ones((B_HEADS, B_V_AUG - B_V_DIM, t), bf16)
    if latent_refs:
        kcat_ref, lat_t_ref = latent_refs
        kcat_ref[:, :KV_LORA_RANK] = ckvn
        kcat_ref[:, KV_LORA_RANK:] = k_rope.astype(bf16)
        lat_t_ref[:KV_LORA_RANK, :] = ckvn_f32.T.astype(bf16)
        lat_t_ref[KV_LORA_RANK:, :] = jnp.ones((LAT_T_ROWS - KV_LORA_RANK, t), bf16)


def _layer_spec(layer, shape):
    return pl.BlockSpec((None,) + tuple(shape), lambda i: (layer,) + (0,) * len(shape))


def _proj(h, tab, lw, layer, latent):
    rows = h.shape[0]
    t = min(ROW_TILE, rows)
    steps = rows // t
    assert tab.shape[0] % t == 0
    const = functools.partial(_layer_spec, layer)
    latent_specs = [pl.BlockSpec((t, KV_LORA_RANK + B_QK_PAD), lambda i: (i, 0)),
                    pl.BlockSpec((LAT_T_ROWS, t), lambda i: (0, i))] if latent else []
    latent_shapes = [jax.ShapeDtypeStruct((rows, KV_LORA_RANK + B_QK_PAD), bf16),
                     jax.ShapeDtypeStruct((LAT_T_ROWS, rows), bf16)] if latent else []
    return pl.pallas_call(
        _proj_kernel,
        grid=(steps,),
        in_specs=[
            pl.BlockSpec((t, D_MODEL), lambda i: (i, 0)),
            pl.BlockSpec(tab.shape, lambda i: (0, 0)),
            const((1, D_MODEL)),
            const((D_MODEL, _C_END)),
            const((A_KV_HEADS * A_HEAD_DIM, D_MODEL)),
            const((1, Q_LORA_RANK)),
            const((Q_LORA_RANK, B_HEADS * B_QK_PAD)),
            const((Q_LORA_RANK, B_HEADS * B_QK_PAD)),
            const((1, KV_LORA_RANK)),
            const((KV_LORA_RANK, B_HEADS * B_QK_PAD)),
            const((B_WIDTH, KV_LORA_RANK)),
        ],
        out_specs=[
            pl.BlockSpec((A_HEADS, t, A_HEAD_DIM), lambda i: (0, i, 0)),
            pl.BlockSpec((A_KV_HEADS, t, A_HEAD_DIM), lambda i: (0, i, 0)),
            pl.BlockSpec((A_KV_HEADS, A_V_AUG, t), lambda i: (0, 0, i)),
            pl.BlockSpec((B_HEADS, t, B_QK_PAD), lambda i: (0, i, 0)),
            pl.BlockSpec((B_HEADS, t, B_QK_PAD), lambda i: (0, i, 0)),
            pl.BlockSpec((B_HEADS, B_V_AUG, t), lambda i: (0, 0, i)),
        ] + latent_specs,
        out_shape=[
            jax.ShapeDtypeStruct((A_HEADS, rows, A_HEAD_DIM), bf16),
            jax.ShapeDtypeStruct((A_KV_HEADS, rows, A_HEAD_DIM), bf16),
            jax.ShapeDtypeStruct((A_KV_HEADS, A_V_AUG, rows), bf16),
            jax.ShapeDtypeStruct((B_HEADS, rows, B_QK_PAD), bf16),
            jax.ShapeDtypeStruct((B_HEADS, rows, B_QK_PAD), bf16),
            jax.ShapeDtypeStruct((B_HEADS, B_V_AUG, rows), bf16),
        ] + latent_shapes,
        compiler_params=_cparams("arbitrary"),
        name="proj",
    )(h, tab, lw["gin"], lw["w1"], lw["wvat"], lw["gq"], lw["wqa"], lw["wqb"], lw["gkv"], lw["wk"], lw["wvt"])


def _out_kernel(h_ref, ya_ref, yb_ref, gin_ref, wg_ref, na_ref, nb_ref, w_ref, *rest, final):
    if final:
        nf_ref, o_ref = rest
    else:
        (o_ref,) = rest
    hn = h_ref[...]
    u = _rms(hn, gin_ref[...]).astype(bf16)
    ga = _silu(jnp.dot(u, wg_ref[:, :A_WIDTH], preferred_element_type=f32))
    gb = _silu(jnp.dot(u, wg_ref[:, A_WIDTH:], preferred_element_type=f32))
    ya = _rms(ya_ref[...].astype(f32), na_ref[...]) * ga
    yb = _rms(yb_ref[...].astype(f32), nb_ref[...]) * gb
    hn = hn + jnp.dot(ya.astype(bf16), w_ref[:A_WIDTH, :], preferred_element_type=f32)
    hn = hn + jnp.dot(yb.astype(bf16), w_ref[A_WIDTH:, :], preferred_element_type=f32)
    if final:
        hn = _rms(hn, nf_ref[...])
    o_ref[...] = hn


def _out(h, ya, yb, lw, layer, norm_final=None):
    rows = h.shape[0]
    t = min(ROW_TILE, rows)
    row = lambda width: pl.BlockSpec((t, width), lambda i: (i, 0))
    const = functools.partial(_layer_spec, layer)
    final = norm_final is not None
    in_specs = [row(D_MODEL), row(A_WIDTH), row(B_WIDTH),
                const((1, D_MODEL)), const((D_MODEL, A_WIDTH + B_WIDTH)),
                const((1, A_WIDTH)), const((1, B_WIDTH)), const((A_WIDTH + B_WIDTH, D_MODEL))]
    args = [h, ya, yb, lw["gin"], lw["wg"], lw["na"], lw["nb"], lw["wout"]]
    if final:
        in_specs.append(pl.BlockSpec((1, D_MODEL), lambda i: (0, 0)))
        args.append(norm_final)
    return pl.pallas_call(
        functools.partial(_out_kernel, final=final),
        grid=(rows // t,),
        in_specs=in_specs,
        out_specs=row(D_MODEL),
        out_shape=jax.ShapeDtypeStruct((rows, D_MODEL), f32),
        compiler_params=_cparams("arbitrary"),
        name="outproj",
    )(*args)


def _bias_kernel(tab_ref, idx_ref, o_ref):
    rows = idx_ref.shape[0]
    chunk = math.gcd(rows, BIAS_ROW_CHUNK)

    def body(r, carry):
        sl = pl.ds(pl.multiple_of(r * chunk, chunk), chunk)
        idx = idx_ref[sl, :]
        accs = [jnp.where(idx < 0, NEG, 0.0).astype(f32)] * A_HEADS
        for b in range(N_BUCKETS):
            hit = idx == b
            accs = [jnp.where(hit, tab_ref[b * A_HEADS + h], accs[h]) for h in range(A_HEADS)]
        for h in range(A_HEADS):
            o_ref[h, sl, :] = accs[h]
        return carry

    lax.fori_loop(0, rows // chunk, body, 0)


def _bias_lookup(table_flat, idx):
    return pl.pallas_call(
        _bias_kernel,
        in_specs=[pl.BlockSpec(memory_space=pltpu.SMEM), pl.BlockSpec(idx.shape, lambda: (0, 0))],
        out_specs=pl.BlockSpec((A_HEADS,) + idx.shape, lambda: (0, 0, 0)),
        out_shape=jax.ShapeDtypeStruct((A_HEADS,) + idx.shape, f32),
        compiler_params=pltpu.CompilerParams(vmem_limit_bytes=VMEM_LIMIT_BYTES),
        name="relbias",
    )(table_flat, idx)


def _t5_bucket(rel):
    nb = N_BUCKETS // 2
    max_exact = nb // 2
    ret = jnp.where(rel > 0, nb, 0)
    n = jnp.abs(rel)
    nf = jnp.maximum(n, 1).astype(jnp.float32)
    large = max_exact + (jnp.log(nf / max_exact) / math.log(MAX_DISTANCE / max_exact)
                         * (nb - max_exact)).astype(jnp.int32)
    large = jnp.minimum(large, nb - 1)
    bucket = ret + jnp.where(n < max_exact, n, large)
    return jnp.bitwise_and(bucket, N_BUCKETS - 1)


def _bias_tables(rel_bias_table, nblk):
    table_flat = (rel_bias_table.astype(f32) * LOG2E).reshape(-1)
    i = jnp.arange(BLOCK, dtype=jnp.int32)[:, None]
    j = jnp.arange(3 * BLOCK, dtype=jnp.int32)[None, :]
    rel = j - i - BLOCK
    band = jnp.where(jnp.abs(rel) <= WINDOW, _t5_bucket(rel), -1)
    interior = (_bias_lookup(table_flat, band.T)
                .reshape(A_KV_HEADS, A_GROUP, 3 * BLOCK, BLOCK)
                .transpose(0, 2, 1, 3).reshape(A_KV_HEADS, 3 * BLOCK, A_GROUP * BLOCK))
    key = jnp.arange(3 * BLOCK, dtype=jnp.int32)[None, :, None]
    first = jnp.where(key < BLOCK, NEG, interior)
    last = jnp.where(key >= 2 * BLOCK, NEG, interior)
    bias_band = jnp.stack([first, interior, last])

    n = jnp.arange(nblk, dtype=jnp.int32)[:, None, None]
    k = jnp.arange(N_META, dtype=jnp.int32)[None, :, None]
    qi = jnp.arange(BLOCK, dtype=jnp.int32)[None, None, :]
    rel_m = k - (N_META + n * BLOCK + qi)
    idx_m = _t5_bucket(rel_m).reshape(nblk * N_META, BLOCK)
    bias_meta = (_bias_lookup(table_flat, idx_m)
                 .reshape(A_KV_HEADS, A_GROUP, nblk, N_META, BLOCK)
                 .transpose(2, 0, 3, 1, 4).reshape(nblk, A_KV_HEADS, N_META, A_GROUP * BLOCK))

    qp = jnp.arange(N_META, dtype=jnp.int32)[:, None]
    kp = jnp.arange(N_META + BLOCK, dtype=jnp.int32)[None, :]
    rel_q = kp - qp
    idx_q = jnp.where(jnp.abs(rel_q) <= WINDOW, _t5_bucket(rel_q), -1)
    bias_q = _bias_lookup(table_flat, idx_q)
    return bias_band, bias_meta, bias_q[:, :, :N_META], bias_q[:, :, N_META:]


def _win_kernel(sink_ref, q_ref, kp_ref, kc_ref, kn_ref, vp_ref, vc_ref, vn_ref, kam_ref, vam_ref,
                bias_ref, bm_ref, o_ref):
    w = WIN_BLOCKS
    i, ntile = pl.program_id(1), pl.num_programs(1)

    def qrows(blk):
        return slice(blk * BLOCK, (blk + 1) * BLOCK)

    def band(blk, prev_ref, cur_ref, next_ref, kvh, axis):
        take = lambda ref, lo, hi: ref[kvh, lo:hi, :] if axis == 0 else ref[kvh, :, lo:hi]
        parts = [take(cur_ref, max(blk - 1, 0) * BLOCK, min(blk + 2, w) * BLOCK)]
        if blk == 0:
            parts.insert(0, take(prev_ref, (w - 1) * BLOCK, w * BLOCK))
        if blk == w - 1:
            parts.append(take(next_ref, 0, BLOCK))
        return jnp.concatenate(parts, axis=axis)

    def variant(blk):
        v = 1
        if blk == 0:
            v = jnp.where(i == 0, 0, v)
        if blk == w - 1:
            v = jnp.where(i == ntile - 1, 2, v)
        return v

    def scores(blk, kvh):
        kb = jnp.concatenate([band(blk, kp_ref, kc_ref, kn_ref, kvh, 0), kam_ref[kvh]], axis=0)
        q4 = q_ref[A_GROUP * kvh:A_GROUP * (kvh + 1), qrows(blk), :].reshape(A_GROUP * BLOCK, A_HEAD_DIM)
        bias = jnp.concatenate([bias_ref[variant(blk), kvh], bm_ref[blk, kvh]], axis=0)
        s = lax.dot_general(kb, q4, _NT, preferred_element_type=f32) + bias
        m = jnp.maximum(jnp.max(s, axis=0, keepdims=True), sink_ref[kvh])
        return s, m

    def attend(blk, kvh, s, m):
        vb = jnp.concatenate([band(blk, vp_ref, vc_ref, vn_ref, kvh, 1), vam_ref[0, kvh]], axis=1)
        p = jnp.exp2(s - m).astype(bf16)
        acc = jnp.dot(vb, p, preferred_element_type=f32)
        den = acc[A_HEAD_DIM:A_HEAD_DIM + 1] + jnp.exp2(sink_ref[kvh] - m)
        return acc[:A_HEAD_DIM] * (1.0 / den)

    units = [(blk, kvh) for blk in range(w) for kvh in range(A_KV_HEADS)]
    outs = {}
    pending = [scores(*units[u]) for u in range(WIN_AHEAD)]
    for u, unit in enumerate(units):
        if u + WIN_AHEAD < len(units):
            pending.append(scores(*units[u + WIN_AHEAD]))
        outs[unit] = attend(*unit, *pending.pop(0))
        blk, kvh = unit
        if kvh == A_KV_HEADS - 1:
            ot = jnp.concatenate([outs[(blk, j)][:, g * BLOCK:(g + 1) * BLOCK]
                                  for j in range(A_KV_HEADS) for g in range(A_GROUP)], axis=0)
            o_ref[qrows(blk), :] = ot.T.astype(o_ref.dtype)


def _win(sink_rows, qa, ka, vat, kam, vatm, bias_band, bias_meta, batch, seq):
    tb = WIN_BLOCKS * BLOCK
    ntile = seq // tb
    prev_ = lambda b, i: b * ntile + jnp.maximum(i - 1, 0)
    cur_ = lambda b, i: b * ntile + i
    next_ = lambda b, i: b * ntile + jnp.minimum(i + 1, ntile - 1)
    kspec = lambda f: pl.BlockSpec((A_KV_HEADS, tb, A_HEAD_DIM), lambda b, i: (0, f(b, i), 0))
    vspec = lambda f: pl.BlockSpec((A_KV_HEADS, A_V_AUG, tb), lambda b, i: (0, 0, f(b, i)))
    return pl.pallas_call(
        _win_kernel,
        grid=(batch, ntile),
        in_specs=[
            pl.BlockSpec((A_KV_HEADS, 1, A_GROUP * BLOCK), lambda b, i: (0, 0, 0)),
            pl.BlockSpec((A_HEADS, tb, A_HEAD_DIM), lambda b, i: (0, cur_(b, i), 0)),
            kspec(prev_), kspec(cur_), kspec(next_),
            vspec(prev_), vspec(cur_), vspec(next_),
            pl.BlockSpec((A_KV_HEADS, N_META, A_HEAD_DIM), lambda b, i: (0, b, 0)),
            pl.BlockSpec((1, A_KV_HEADS, A_V_AUG, N_META), lambda b, i: (b, 0, 0, 0)),
            pl.BlockSpec((3, A_KV_HEADS, 3 * BLOCK, A_GROUP * BLOCK), lambda b, i: (0, 0, 0, 0)),
            pl.BlockSpec((WIN_BLOCKS, A_KV_HEADS, N_META, A_GROUP * BLOCK), lambda b, i: (i, 0, 0, 0)),
        ],
        out_specs=pl.BlockSpec((tb, A_WIDTH), lambda b, i: (cur_(b, i), 0)),
        out_shape=jax.ShapeDtypeStruct((batch * seq, A_WIDTH), bf16),
        compiler_params=_cparams("arbitrary", "arbitrary"),
        name="win",
    )(sink_rows, qa, ka, ka, ka, vat, vat, vat, kam, vatm, bias_band, bias_meta)


def _win_meta_kernel(sink_ref, q_ref, ka_ref, vat_ref, kam_ref, vatm_ref, biasm_ref, biasr_ref, o_ref):
    for kvh in range(A_KV_HEADS):
        q4 = q_ref[A_GROUP * kvh:A_GROUP * (kvh + 1)].reshape(A_GROUP * N_META, A_HEAD_DIM)
        sm = lax.dot_general(q4, kam_ref[kvh], _NT, preferred_element_type=f32)
        sr = lax.dot_general(q4, ka_ref[kvh], _NT, preferred_element_type=f32)
        for g in range(A_GROUP):
            h = A_GROUP * kvh + g
            rows = slice(g * N_META, (g + 1) * N_META)
            smg = sm[rows] + biasm_ref[h]
            srg = sr[rows] + biasr_ref[h]
            sink = sink_ref[h]
            m = jnp.maximum(jnp.maximum(jnp.max(smg, axis=-1, keepdims=True),
                                        jnp.max(srg, axis=-1, keepdims=True)), sink)
            pm = jnp.exp2(smg - m).astype(bf16)
            pr = jnp.exp2(srg - m).astype(bf16)
            o = (lax.dot_general(pm, vatm_ref[0, kvh], _NT, preferred_element_type=f32)
                 + lax.dot_general(pr, vat_ref[kvh], _NT, preferred_element_type=f32))
            den = o[:, A_HEAD_DIM:A_HEAD_DIM + 1] + jnp.exp2(sink - m)
            o_ref[:, h * A_HEAD_DIM:(h + 1) * A_HEAD_DIM] = (o[:, :A_HEAD_DIM] * (1.0 / den)).astype(o_ref.dtype)


def _win_meta(sink, qam, ka, vat, kam, vatm, bias_mm, bias_mr, batch, seq):
    nblk = seq // BLOCK
    const3 = lambda shape: pl.BlockSpec(shape, lambda b: (0, 0, 0))
    return pl.pallas_call(
        _win_meta_kernel,
        grid=(batch,),
        in_specs=[
            pl.BlockSpec(memory_space=pltpu.SMEM),
            pl.BlockSpec((A_HEADS, N_META, A_HEAD_DIM), lambda b: (0, b, 0)),
            pl.BlockSpec((A_KV_HEADS, BLOCK, A_HEAD_DIM), lambda b: (0, b * nblk, 0)),
            pl.BlockSpec((A_KV_HEADS, A_V_AUG, BLOCK), lambda b: (0, 0, b * nblk)),
            pl.BlockSpec((A_KV_HEADS, N_META, A_HEAD_DIM), lambda b: (0, b, 0)),
            pl.BlockSpec((1, A_KV_HEADS, A_V_AUG, N_META), lambda b: (b, 0, 0, 0)),
            const3((A_HEADS, N_META, N_META)),
            const3((A_HEADS, N_META, BLOCK)),
        ],
        out_specs=pl.BlockSpec((N_META, A_WIDTH), lambda b: (b, 0)),
        out_shape=jax.ShapeDtypeStruct((batch * N_META, A_WIDTH), bf16),
        compiler_params=_cparams("arbitrary"),
        name="win_meta",
    )(sink, qam, ka, vat, kam, vatm, bias_mm, bias_mr)


def _mla_kernel(q_ref, k_ref, km_ref, vt_ref, vtm_ref, o_ref, acc_ref, s0_ref, s1_ref, sm0_ref, sm1_ref):
    s_bufs, sm_bufs = (s0_ref, s1_ref), (sm0_ref, sm1_ref)
    nsub, _, _, tq = acc_ref.shape
    units = nsub * B_HEADS
    seq = k_ref.shape[1]
    chunks = [(c * MLA_KEY_CHUNK, (c + 1) * MLA_KEY_CHUNK) for c in range(seq // MLA_KEY_CHUNK)]

    def unit(u):
        return u % B_HEADS, u // B_HEADS

    def step(t, slot, m, *, score_next=True, attend_cur=True):
        if score_next:
            hn, subn = unit(t + 1)
            q = q_ref[hn, pl.ds(pl.multiple_of(subn * tq, tq), tq), :]
            sm = lax.dot_general(km_ref[hn], q, _NT, preferred_element_type=f32)
            sm_bufs[1 - slot][...] = sm
            m_next = jnp.max(sm, axis=0, keepdims=True)
        if attend_cur:
            h, sub = unit(t)
            pm = jnp.exp2(sm_bufs[slot][...] - m).astype(bf16)
            acc = jnp.dot(vtm_ref[0, h], pm, preferred_element_type=f32)
        for lo, hi in chunks:
            if score_next:
                s = lax.dot_general(k_ref[hn, lo:hi, :], q, _NT, preferred_element_type=f32)
                s_bufs[1 - slot][lo:hi, :] = s
                m_next = jnp.maximum(m_next, jnp.max(s, axis=0, keepdims=True))
            if attend_cur:
                p = jnp.exp2(s_bufs[slot][lo:hi, :] - m).astype(bf16)
                acc = acc + jnp.dot(vt_ref[h, :, lo:hi], p, preferred_element_type=f32)
        if attend_cur:
            acc_ref[sub, h] = acc[:B_V_DIM] * (1.0 / acc[B_V_DIM:B_V_DIM + 1])
        return m_next if score_next else None

    def steps(j, m):
        for i in range(MLA_STEPS_PER_ITER):
            m = step(MLA_STEPS_PER_ITER * j + i, i % 2, m)
        return m

    assert MLA_STEPS_PER_ITER % 2 == 0 and units % 2 == 0
    iters = (units - 1) // MLA_STEPS_PER_ITER
    m = step(-1, 1, None, attend_cur=False)
    m = lax.fori_loop(0, iters, steps, m)
    for t in range(iters * MLA_STEPS_PER_ITER, units - 1):
        m = step(t, t % 2, m)
    step(units - 1, 1, m, score_next=False)
    for sub in range(nsub):
        o_ref[sub * tq:(sub + 1) * tq, :] = acc_ref[sub].reshape(B_WIDTH, tq).T.astype(o_ref.dtype)


def _mla(q, k, km, vt, vtm, batch, seq):
    tq = MLA_Q_TILE
    tb = MLA_Q_TILE * MLA_SUBTILES
    nq = seq // tb
    return pl.pallas_call(
        _mla_kernel,
        grid=(batch, nq),
        in_specs=[
            pl.BlockSpec((B_HEADS, tb, B_QK_PAD), lambda b, i: (0, b * nq + i, 0)),
            pl.BlockSpec((B_HEADS, seq, B_QK_PAD), lambda b, i: (0, b, 0)),
            pl.BlockSpec((B_HEADS, N_META, B_QK_PAD), lambda b, i: (0, b, 0)),
            pl.BlockSpec((B_HEADS, B_V_AUG, seq), lambda b, i: (0, 0, b)),
            pl.BlockSpec((1, B_HEADS, B_V_AUG, N_META), lambda b, i: (b, 0, 0, 0)),
        ],
        out_specs=pl.BlockSpec((tb, B_WIDTH), lambda b, i: (b * nq + i, 0)),
        out_shape=jax.ShapeDtypeStruct((batch * seq, B_WIDTH), bf16),
        scratch_shapes=([pltpu.VMEM((MLA_SUBTILES, B_HEADS, B_V_DIM, tq), f32)]
                        + [pltpu.VMEM((seq, tq), f32)] * 2 + [pltpu.VMEM((N_META, tq), f32)] * 2),
        compiler_params=_cparams("arbitrary", "arbitrary"),
        name="mla",
    )(q, k, km, vt, vtm)


def _mla_meta_kernel(q_ref, kcat_ref, kcatm_ref, lat_t_ref, lat_tm_ref, wuk_ref, wuv_ref, o_ref):
    lane = lax.broadcasted_iota(jnp.int32, (N_META, B_QK_PAD), 1)
    rope_lanes = (lane >= B_NOPE_DIM) & (lane < B_NOPE_DIM + B_ROPE_DIM)
    q_rows = []
    for h in range(B_HEADS):
        q = q_ref[h]
        q_lat = jnp.dot(q, wuk_ref[h], preferred_element_type=f32).astype(bf16)
        q_rows.append(jnp.concatenate([q_lat, jnp.where(rope_lanes, q, jnp.zeros_like(q))], axis=1))
    qcat = jnp.concatenate(q_rows, axis=0)
    seq = kcat_ref.shape[0]
    chunks = [(c * MLA_KEY_CHUNK, (c + 1) * MLA_KEY_CHUNK) for c in range(seq // MLA_KEY_CHUNK)]
    sm = lax.dot_general(kcatm_ref[...], qcat, _NT, preferred_element_type=f32)
    ss = [lax.dot_general(kcat_ref[lo:hi, :], qcat, _NT, preferred_element_type=f32) for lo, hi in chunks]
    m = jnp.max(sm, axis=0, keepdims=True)
    for s in ss:
        m = jnp.maximum(m, jnp.max(s, axis=0, keepdims=True))
    acc = jnp.dot(lat_tm_ref[0], jnp.exp2(sm - m).astype(bf16), preferred_element_type=f32)
    for (lo, hi), s in zip(chunks, ss):
        acc = acc + jnp.dot(lat_t_ref[:, lo:hi], jnp.exp2(s - m).astype(bf16),
                            preferred_element_type=f32)
    o_lat = (acc[:KV_LORA_RANK] * (1.0 / acc[KV_LORA_RANK:KV_LORA_RANK + 1])).T.astype(bf16)
    for h in range(B_HEADS):
        o_ref[:, h * B_V_DIM:(h + 1) * B_V_DIM] = jnp.dot(
            o_lat[h * N_META:(h + 1) * N_META], wuv_ref[h], preferred_element_type=f32).astype(o_ref.dtype)


def _mla_meta(qm, kcat, kcatm, lat_t, lat_tm, lw, layer, batch, seq):
    const = functools.partial(_layer_spec, layer)
    return pl.pallas_call(
        _mla_meta_kernel,
        grid=(batch,),
        in_specs=[
            pl.BlockSpec((B_HEADS, N_META, B_QK_PAD), lambda b: (0, b, 0)),
            pl.BlockSpec((seq, KV_LORA_RANK + B_QK_PAD), lambda b: (b, 0)),
            pl.BlockSpec((N_META, KV_LORA_RANK + B_QK_PAD), lambda b: (b, 0)),
            pl.BlockSpec((LAT_T_ROWS, seq), lambda b: (0, b)),
            pl.BlockSpec((1, LAT_T_ROWS, N_META), lambda b: (b, 0, 0)),
            const((B_HEADS, B_QK_PAD, KV_LORA_RANK)),
            const((B_HEADS, KV_LORA_RANK, B_V_DIM)),
        ],
        out_specs=pl.BlockSpec((N_META, B_WIDTH), lambda b: (b, 0)),
        out_shape=jax.ShapeDtypeStruct((batch * N_META, B_WIDTH), bf16),
        compiler_params=_cparams("arbitrary"),
        name="mla_meta",
    )(qm, kcat, kcatm, lat_t, lat_tm, lw["wuk"], lw["wuv"])


def _rope_tables(pos):
    half = B_ROPE_DIM // 2
    freqs = ROPE_THETA ** (-jnp.arange(half, dtype=jnp.float32) / half)
    tail = B_QK_PAD - B_NOPE_DIM - B_ROPE_DIM
    lane_freq = jnp.concatenate([jnp.zeros((B_NOPE_DIM,), f32), freqs, freqs, jnp.zeros((tail,), f32)])
    ang = pos.astype(jnp.float32)[:, None] * lane_freq[None, :]
    cos, sin = jnp.cos(ang), jnp.sin(ang)
    lane = jnp.arange(B_QK_PAD)[None, :]
    nope = lane < B_NOPE_DIM
    rope = (lane >= B_NOPE_DIM) & (lane < B_NOPE_DIM + B_ROPE_DIM)
    c_mla = (B_NOPE_DIM + B_ROPE_DIM) ** -0.5 * LOG2E
    cos_q = jnp.where(nope, c_mla, jnp.where(rope, c_mla * cos, 0.0))
    sin_q = jnp.where(rope, c_mla * sin, 0.0)
    cos_k = jnp.where(rope, cos, 0.0)
    sin_k = jnp.where(rope, sin, 0.0)
    return jnp.concatenate([cos_q, sin_q, cos_k, sin_k], axis=1).astype(f32)


def _rot_cols(w):
    half = w.shape[-1] // 2
    return jnp.concatenate([-w[..., half:], w[..., :half]], axis=-1)


def _stacked_weights(norm_in, w_in, norm_q_lat, w_uq, norm_kv_lat, w_ukv, norm_out_a, norm_out_b, w_out):
    depth, d, _ = w_in.shape
    sizes = (A_WIDTH, A_KV_HEADS * A_HEAD_DIM, A_KV_HEADS * A_HEAD_DIM, A_WIDTH,
             Q_LORA_RANK, KV_LORA_RANK, B_ROPE_DIM, B_WIDTH)
    offs = [sum(sizes[:i]) for i in range(len(sizes) + 1)]
    qa, ka, va, ga, cq, ckv, kr, gb = (w_in[..., offs[i]:offs[i + 1]] for i in range(len(sizes)))
    tail = B_QK_PAD - B_NOPE_DIM - B_ROPE_DIM
    zeros = lambda *shape: jnp.zeros(shape, w_in.dtype)
    kr_groups = jnp.concatenate([zeros(depth, d, B_NOPE_DIM), kr, zeros(depth, d, tail + B_NOPE_DIM),
                                 _rot_cols(kr), zeros(depth, d, tail)], axis=-1)
    w1 = jnp.concatenate([qa, cq, ckv, ka, kr_groups], axis=-1).astype(bf16)

    r = w_uq.shape[1]
    uq = w_uq.reshape(depth, r, B_HEADS, B_NOPE_DIM + B_ROPE_DIM)
    nope, rope = uq[..., :B_NOPE_DIM], uq[..., B_NOPE_DIM:]
    wqa = jnp.concatenate([nope, rope, zeros(depth, r, B_HEADS, tail)], axis=-1)
    wqb = jnp.concatenate([jnp.zeros_like(nope), _rot_cols(rope), zeros(depth, r, B_HEADS, tail)], axis=-1)

    rk = w_ukv.shape[1]
    ukv = w_ukv.reshape(depth, rk, B_HEADS, B_NOPE_DIM + B_V_DIM)
    k_nope, v = ukv[..., :B_NOPE_DIM], ukv[..., B_NOPE_DIM:]
    wk = jnp.concatenate([k_nope, zeros(depth, rk, B_HEADS, B_QK_PAD - B_NOPE_DIM)], axis=-1)
    row = lambda g: g.astype(f32)[:, None, :]
    return {
        "gin": row(norm_in),
        "w1": w1,
        "wg": jnp.concatenate([ga, gb], axis=-1).astype(bf16),
        "wvat": jnp.swapaxes(va, 1, 2).astype(bf16),
        "gq": row(norm_q_lat),
        "wqa": wqa.reshape(depth, r, B_HEADS * B_QK_PAD).astype(bf16),
        "wqb": wqb.reshape(depth, r, B_HEADS * B_QK_PAD).astype(bf16),
        "gkv": row(norm_kv_lat),
        "wk": wk.reshape(depth, rk, B_HEADS * B_QK_PAD).astype(bf16),
        "wvt": jnp.swapaxes(v.reshape(depth, rk, B_WIDTH), 1, 2).astype(bf16),
        "wuk": jnp.concatenate([jnp.transpose(k_nope, (0, 2, 3, 1)),
                                zeros(depth, B_HEADS, B_QK_PAD - B_NOPE_DIM, rk)], axis=2).astype(bf16),
        "wuv": jnp.transpose(v, (0, 2, 1, 3)).astype(bf16),
        "na": row(norm_out_a),
        "nb": row(norm_out_b),
        "wout": w_out.astype(bf16),
    }


def kernel(x, meta_tokens, rel_bias_table, norm_in, w_in, sink_a, norm_q_lat, w_uq, norm_kv_lat, w_ukv,
           norm_out_a, norm_out_b, w_out, norm_final):
    batch, seq, d = x.shape
    depth = w_in.shape[0]
    assert d == D_MODEL and seq % (MLA_Q_TILE * MLA_SUBTILES) == 0 and seq % ROW_TILE == 0

    h_real = x.reshape(batch * seq, d).astype(f32)
    h_meta = jnp.tile(meta_tokens.astype(f32), (batch, 1))
    tab_real = _rope_tables(N_META + jnp.arange(seq))
    tab_meta = _rope_tables(jnp.arange(batch * N_META) % N_META)
    bias_band, bias_meta, bias_mm, bias_mr = _bias_tables(rel_bias_table, seq // BLOCK)

    lw = _stacked_weights(norm_in, w_in, norm_q_lat, w_uq, norm_kv_lat, w_ukv, norm_out_a, norm_out_b, w_out)
    sinks = sink_a.astype(f32) * LOG2E
    sinks_rows = jnp.repeat(sinks.reshape(depth, A_KV_HEADS, A_GROUP), BLOCK, axis=2)[:, :, None, :]

    out = None
    for i in range(depth):
        last = i == depth - 1
        sink, sink_rows = sinks[i], sinks_rows[i]
        qa, ka, vat, q, k, vt, *lat = _proj(h_real, tab_real, lw, i, latent=not last)
        qam, kam, vatm, qm, km, vtm, *latm = _proj(h_meta, tab_meta, lw, i, latent=not last)
        vtm_b = vtm.reshape(B_HEADS, B_V_AUG, batch, N_META).transpose(2, 0, 1, 3)
        vatm_b = vatm.reshape(A_KV_HEADS, A_V_AUG, batch, N_META).transpose(2, 0, 1, 3)

        ya = _win(sink_rows, qa, ka, vat, kam, vatm_b, bias_band, bias_meta, batch, seq)
        yb = _mla(q, k, km, vt, vtm_b, batch, seq)
        if last:
            out = _out(h_real, ya, yb, lw, i, norm_final.astype(f32)[None, :])
        else:
            yam = _win_meta(sink, qam, ka, vat, kam, vatm_b, bias_mm, bias_mr, batch, seq)
            (kcat, lat_t), (kcatm, lat_tm) = lat, latm
            lat_tm_b = lat_tm.reshape(LAT_T_ROWS, batch, N_META).transpose(1, 0, 2)
            ybm = _mla_meta(qm, kcat, kcatm, lat_t, lat_tm_b, lw, i, batch, seq)
            h_real = _out(h_real, ya, yb, lw, i)
            h_meta = _out(h_meta, yam, ybm, lw, i)
    return out.reshape(batch, seq, d).astype(x.dtype)
```

```python
import functools
import math

import jax
import jax.numpy as jnp
from jax import lax
from jax.experimental import pallas as pl
from jax.experimental.pallas import tpu as pltpu

D_MODEL = 1024
N_META = 16
BLOCK = 128
WINDOW = 128
A_HEADS = 8
A_KV_HEADS = 2
A_GROUP = A_HEADS // A_KV_HEADS
A_HEAD_DIM = 64
A_WIDTH = A_HEADS * A_HEAD_DIM
B_HEADS = 8
B_NOPE_DIM = 64
B_ROPE_DIM = 32
B_V_DIM = 64
B_WIDTH = B_HEADS * B_V_DIM
B_QK_PAD = 128
BF16_SUBLANES = 16
B_V_AUG = B_V_DIM + BF16_SUBLANES
A_V_AUG = A_HEAD_DIM + BF16_SUBLANES
Q_LORA_RANK = 256
KV_LORA_RANK = 128
LAT_T_ROWS = KV_LORA_RANK + BF16_SUBLANES
N_BUCKETS = 32
MAX_DISTANCE = 128
ROPE_THETA = 10000.0
EPS = 1e-6

LOG2E = math.log2(math.e)
NEG = -1e30
ROW_TILE = 1024
MLA_Q_TILE = 256
MLA_SUBTILES = 8
MLA_STEPS_PER_ITER = 20
MLA_KEY_CHUNK = 1024
WIN_BLOCKS = 16
WIN_AHEAD = 2
BIAS_ROW_CHUNK = 32
VMEM_LIMIT_BYTES = 56 * 1024 * 1024

_C_QA = 0
_C_CQ = _C_QA + A_WIDTH
_C_CKV = _C_CQ + Q_LORA_RANK
_C_KA = _C_CKV + KV_LORA_RANK
_C_KRA = _C_KA + A_KV_HEADS * A_HEAD_DIM
_C_KRB = _C_KRA + B_QK_PAD
_C_END = _C_KRB + B_QK_PAD

_NT = (((1,), (1,)), ((), ()))

bf16 = jnp.bfloat16
f32 = jnp.float32


def _cparams(*sem, flags=None):
    return pltpu.CompilerParams(dimension_semantics=sem, vmem_limit_bytes=VMEM_LIMIT_BYTES, flags=flags)


def _rms(x, gain):
    return x * lax.rsqrt(jnp.mean(x * x, axis=-1, keepdims=True) + EPS) * gain


def _silu(x):
    return x / (1.0 + jnp.exp(-x))


def _proj_kernel(h_ref, tab_ref, gin_ref, w1_ref, wvat_ref, gq_ref, wqa_ref, wqb_ref, gkv_ref, wk_ref, wvt_ref,
                 qa_ref, ka_ref, vat_ref, q_ref, k_ref, vt_ref, *latent_refs):
    u = _rms(h_ref[...], gin_ref[...]).astype(bf16)
    t = u.shape[0]

    def mm(lo, hi):
        return jnp.dot(u, w1_ref[:, lo:hi], preferred_element_type=f32)

    qa = mm(_C_QA, _C_CQ) * (A_HEAD_DIM ** -0.5 * LOG2E)
    for h in range(A_HEADS):
        qa_ref[h] = qa[:, h * A_HEAD_DIM:(h + 1) * A_HEAD_DIM].astype(bf16)
    vat = lax.dot_general(wvat_ref[...], u, _NT, preferred_element_type=f32)
    vat_ref[:, :A_HEAD_DIM, :] = vat.reshape(A_KV_HEADS, A_HEAD_DIM, t).astype(bf16)
    vat_ref[:, A_HEAD_DIM:, :] = jnp.ones((A_KV_HEADS, A_V_AUG - A_HEAD_DIM, t), bf16)
    mixed = mm(_C_CQ, _C_END)
    cq = mixed[:, :_C_CKV - _C_CQ]
    ckv = mixed[:, _C_CKV - _C_CQ:_C_KA - _C_CQ]
    ka = mixed[:, _C_KA - _C_CQ:_C_KRA - _C_CQ]
    kra = mixed[:, _C_KRA - _C_CQ:_C_KRB - _C_CQ]
    krb = mixed[:, _C_KRB - _C_CQ:]
    for j in range(A_KV_HEADS):
        ka_ref[j] = ka[:, j * A_HEAD_DIM:(j + 1) * A_HEAD_DIM].astype(bf16)
    tab_rows = pl.ds(pl.multiple_of((pl.program_id(0) % (tab_ref.shape[0] // t)) * t, t), t)
    cos_q = tab_ref[tab_rows, 0 * B_QK_PAD:1 * B_QK_PAD]
    sin_q = tab_ref[tab_rows, 1 * B_QK_PAD:2 * B_QK_PAD]
    cos_k = tab_ref[tab_rows, 2 * B_QK_PAD:3 * B_QK_PAD]
    sin_k = tab_ref[tab_rows, 3 * B_QK_PAD:4 * B_QK_PAD]

    cqn = _rms(cq, gq_ref[...]).astype(bf16)
    qa_part = jnp.dot(cqn, wqa_ref[...], preferred_element_type=f32)
    qb_part = jnp.dot(cqn, wqb_ref[...], preferred_element_type=f32)
    for h in range(B_HEADS):
        sl = slice(h * B_QK_PAD, (h + 1) * B_QK_PAD)
        q_ref[h] = (qa_part[:, sl] * cos_q + qb_part[:, sl] * sin_q).astype(bf16)

    ckvn_f32 = _rms(ckv, gkv_ref[...])
    ckvn = ckvn_f32.astype(bf16)
    k_nope = jnp.dot(ckvn, wk_ref[...], preferred_element_type=f32)
    k_rope = kra * cos_k + krb * sin_k
    for h in range(B_HEADS):
        k_ref[h] = (k_nope[:, h * B_QK_PAD:(h + 1) * B_QK_PAD] + k_rope).astype(bf16)
    vt = lax.dot_general(wvt_ref[...], ckvn, _NT, preferred_element_type=f32)
    vt_ref[:, :B_V_DIM, :] = vt.reshape(B_HEADS, B_V_DIM, t).astype(bf16)
    vt_ref[:, B_V_DIM:, :] = jnp.ones((B_HEADS, B_V_AUG - B_V_DIM, t), bf16)
    if latent_refs:
        kcat_ref, lat_t_ref = latent_refs
        kcat_ref[:, :KV_LORA_RANK] = ckvn
        kcat_ref[:, KV_LORA_RANK:] = k_rope.astype(bf16)
        lat_t_ref[:KV_LORA_RANK, :] = ckvn_f32.T.astype(bf16)
        lat_t_ref[KV_LORA_RANK:, :] = jnp.ones((LAT_T_ROWS - KV_LORA_RANK, t), bf16)


def _layer_spec(layer, shape):
    return pl.BlockSpec((None,) + tuple(shape), lambda i: (layer,) + (0,) * len(shape))


def _proj(h, tab, lw, layer, latent):
    rows = h.shape[0]
    t = min(ROW_TILE, rows)
    steps = rows // t
    assert tab.shape[0] % t == 0
    const = functools.partial(_layer_spec, layer)
    latent_specs = [pl.BlockSpec((t, KV_LORA_RANK + B_QK_PAD), lambda i: (i, 0)),
                    pl.BlockSpec((LAT_T_ROWS, t), lambda i: (0, i))] if latent else []
    latent_shapes = [jax.ShapeDtypeStruct((rows, KV_LORA_RANK + B_QK_PAD), bf16),
                     jax.ShapeDtypeStruct((LAT_T_ROWS, rows), bf16)] if latent else []
    return pl.pallas_call(
        _proj_kernel,
        grid=(steps,),
        in_specs=[
            pl.BlockSpec((t, D_MODEL), lambda i: (i, 0)),
            pl.BlockSpec(tab.shape, lambda i: (0, 0)),
            const((1, D_MODEL)),
            const((D_MODEL, _C_END)),
            const((A_KV_HEADS * A_HEAD_DIM, D_MODEL)),
            const((1, Q_LORA_RANK)),
            const((Q_LORA_RANK, B_HEADS * B_QK_PAD)),
            const((Q_LORA_RANK, B_HEADS * B_QK_PAD)),
            const((1, KV_LORA_RANK)),
            const((KV_LORA_RANK, B_HEADS * B_QK_PAD)),
            const((B_WIDTH, KV_LORA_RANK)),
        ],
        out_specs=[
            pl.BlockSpec((A_HEADS, t, A_HEAD_DIM), lambda i: (0, i, 0)),
            pl.BlockSpec((A_KV_HEADS, t, A_HEAD_DIM), lambda i: (0, i, 0)),
            pl.BlockSpec((A_KV_HEADS, A_V_AUG, t), lambda i: (0, 0, i)),
            pl.BlockSpec((B_HEADS, t, B_QK_PAD), lambda i: (0, i, 0)),
            pl.BlockSpec((B_HEADS, t, B_QK_PAD), lambda i: (0, i, 0)),
            pl.BlockSpec((B_HEADS, B_V_AUG, t), lambda i: (0, 0, i)),
        ] + latent_specs,
        out_shape=[
            jax.ShapeDtypeStruct((A_HEADS, rows, A_HEAD_DIM), bf16),
            jax.ShapeDtypeStruct((A_KV_HEADS, rows, A_HEAD_DIM), bf16),
            jax.ShapeDtypeStruct((A_KV_HEADS, A_V_AUG, rows), bf16),
            jax.ShapeDtypeStruct((B_HEADS, rows, B_QK_PAD), bf16),
            jax.ShapeDtypeStruct((B_HEADS, rows, B_QK_PAD), bf16),
            jax.ShapeDtypeStruct((B_HEADS, B_V_AUG, rows), bf16),
        ] + latent_shapes,
        compiler_params=_cparams("arbitrary"),
        name="proj",
    )(h, tab, lw["gin"], lw["w1"], lw["wvat"], lw["gq"], lw["wqa"], lw["wqb"], lw["gkv"], lw["wk"], lw["wvt"])


def _out_kernel(h_ref, ya_ref, yb_ref, gin_ref, wg_ref, na_ref, nb_ref, w_ref, *rest, final):
    if final:
        nf_ref, o_ref = rest
    else:
        (o_ref,) = rest
    hn = h_ref[...]
    u = _rms(hn, gin_ref[...]).astype(bf16)
    ga = _silu(jnp.dot(u, wg_ref[:, :A_WIDTH], preferred_element_type=f32))
    gb = _silu(jnp.dot(u, wg_ref[:, A_WIDTH:], preferred_element_type=f32))
    ya = _rms(ya_ref[...].astype(f32), na_ref[...]) * ga
    yb = _rms(yb_ref[...].astype(f32), nb_ref[...]) * gb
    hn = hn + jnp.dot(ya.astype(bf16), w_ref[:A_WIDTH, :], preferred_element_type=f32)
    hn = hn + jnp.dot(yb.astype(bf16), w_ref[A_WIDTH:, :], preferred_element_type=f32)
    if final:
        hn = _rms(hn, nf_ref[...])
    o_ref[...] = hn


def _out(h, ya, yb, lw, layer, norm_final=None):
    rows = h.shape[0]
    t = min(ROW_TILE, rows)
    row = lambda width: pl.BlockSpec((t, width), lambda i: (i, 0))
    const = functools.partial(_layer_spec, layer)
    final = norm_final is not None
    in_specs = [row(D_MODEL), row(A_WIDTH), row(B_WIDTH),
                const((1, D_MODEL)), const((D_MODEL, A_WIDTH + B_WIDTH)),
                const((1, A_WIDTH)), const((1, B_WIDTH)), const((A_WIDTH + B_WIDTH, D_MODEL))]
    args = [h, ya, yb, lw["gin"], lw["wg"], lw["na"], lw["nb"], lw["wout"]]
    if final:
        in_specs.append(pl.BlockSpec((1, D_MODEL), lambda i: (0, 0)))
        args.append(norm_final)
    return pl.pallas_call(
        functools.partial(_out_kernel, final=final),
        grid=(rows // t,),
        in_specs=in_specs,
        out_specs=row(D_MODEL),
        out_shape=jax.ShapeDtypeStruct((rows, D_MODEL), f32),
        compiler_params=_cparams("arbitrary"),
        name="outproj",
    )(*args)


def _bias_kernel(tab_ref, idx_ref, o_ref):
    rows = idx_ref.shape[0]
    chunk = math.gcd(rows, BIAS_ROW_CHUNK)

    def body(r, carry):
        sl = pl.ds(pl.multiple_of(r * chunk, chunk), chunk)
        idx = idx_ref[sl, :]
        accs = [jnp.where(idx < 0, NEG, 0.0).astype(f32)] * A_HEADS
        for b in range(N_BUCKETS):
            hit = idx == b
            accs = [jnp.where(hit, tab_ref[b * A_HEADS + h], accs[h]) for h in range(A_HEADS)]
        for h in range(A_HEADS):
            o_ref[h, sl, :] = accs[h]
        return carry

    lax.fori_loop(0, rows // chunk, body, 0)


def _bias_lookup(table_flat, idx):
    return pl.pallas_call(
        _bias_kernel,
        in_specs=[pl.BlockSpec(memory_space=pltpu.SMEM), pl.BlockSpec(idx.shape, lambda: (0, 0))],
        out_specs=pl.BlockSpec((A_HEADS,) + idx.shape, lambda: (0, 0, 0)),
        out_shape=jax.ShapeDtypeStruct((A_HEADS,) + idx.shape, f32),
        compiler_params=pltpu.CompilerParams(vmem_limit_bytes=VMEM_LIMIT_BYTES),
        name="relbias",
    )(table_flat, idx)


def _t5_bucket(rel):
    nb = N_BUCKETS // 2
    max_exact = nb // 2
    ret = jnp.where(rel > 0, nb, 0)
    n = jnp.abs(rel)
    nf = jnp.maximum(n, 1).astype(jnp.float32)
    large = max_exact + (jnp.log(nf / max_exact) / math.log(MAX_DISTANCE / max_exact)
                         * (nb - max_exact)).astype(jnp.int32)
    large = jnp.minimum(large, nb - 1)
    bucket = ret + jnp.where(n < max_exact, n, large)
    return jnp.bitwise_and(bucket, N_BUCKETS - 1)


def _bias_tables(rel_bias_table, nblk):
    table_flat = (rel_bias_table.astype(f32) * LOG2E).reshape(-1)
    i = jnp.arange(BLOCK, dtype=jnp.int32)[:, None]
    j = jnp.arange(3 * BLOCK, dtype=jnp.int32)[None, :]
    rel = j - i - BLOCK
    band = jnp.where(jnp.abs(rel) <= WINDOW, _t5_bucket(rel), -1)
    interior = (_bias_lookup(table_flat, band.T)
                .reshape(A_KV_HEADS, A_GROUP, 3 * BLOCK, BLOCK)
                .transpose(0, 2, 1, 3).reshape(A_KV_HEADS, 3 * BLOCK, A_GROUP * BLOCK))
    key = jnp.arange(3 * BLOCK, dtype=jnp.int32)[None, :, None]
    first = jnp.where(key < BLOCK, NEG, interior)
    last = jnp.where(key >= 2 * BLOCK, NEG, interior)
    bias_band = jnp.stack([first, interior, last])

    n = jnp.arange(nblk, dtype=jnp.int32)[:, None, None]
    k = jnp.arange(N_META, dtype=jnp.int32)[None, :, None]
    qi = jnp.arange(BLOCK, dtype=jnp.int32)[None, None, :]
    rel_m = k - (N_META + n * BLOCK + qi)
    idx_m = _t5_bucket(rel_m).reshape(nblk * N_META, BLOCK)
    bias_meta = (_bias_lookup(table_flat, idx_m)
                 .reshape(A_KV_HEADS, A_GROUP, nblk, N_META, BLOCK)
                 .transpose(2, 0, 3, 1, 4).reshape(nblk, A_KV_HEADS, N_META, A_GROUP * BLOCK))

    qp = jnp.arange(N_META, dtype=jnp.int32)[:, None]
    kp = jnp.arange(N_META + BLOCK, dtype=jnp.int32)[None, :]
    rel_q = kp - qp
    idx_q = jnp.where(jnp.abs(rel_q) <= WINDOW, _t5_bucket(rel_q), -1)
    bias_q = _bias_lookup(table_flat, idx_q)
    return bias_band, bias_meta, bias_q[:, :, :N_META], bias_q[:, :, N_META:]


def _win_kernel(sink_ref, q_ref, kp_ref, kc_ref, kn_ref, vp_ref, vc_ref, vn_ref, kam_ref, vam_ref,
                bias_ref, bm_ref, o_ref):
    w = WIN_BLOCKS
    i, ntile = pl.program_id(1), pl.num_programs(1)

    def qrows(blk):
        return slice(blk * BLOCK, (blk + 1) * BLOCK)

    def band(blk, prev_ref, cur_ref, next_ref, kvh, axis):
        take = lambda ref, lo, hi: ref[kvh, lo:hi, :] if axis == 0 else ref[kvh, :, lo:hi]
        parts = [take(cur_ref, max(blk - 1, 0) * BLOCK, min(blk + 2, w) * BLOCK)]
        if blk == 0:
            parts.insert(0, take(prev_ref, (w - 1) * BLOCK, w * BLOCK))
        if blk == w - 1:
            parts.append(take(next_ref, 0, BLOCK))
        return jnp.concatenate(parts, axis=axis)

    def variant(blk):
        v = 1
        if blk == 0:
            v = jnp.where(i == 0, 0, v)
        if blk == w - 1:
            v = jnp.where(i == ntile - 1, 2, v)
        return v

    def scores(blk, kvh):
        kb = jnp.concatenate([band(blk, kp_ref, kc_ref, kn_ref, kvh, 0), kam_ref[kvh]], axis=0)
        q4 = q_ref[A_GROUP * kvh:A_GROUP * (kvh + 1), qrows(blk), :].reshape(A_GROUP * BLOCK, A_HEAD_DIM)
        bias = jnp.concatenate([bias_ref[variant(blk), kvh], bm_ref[blk, kvh]], axis=0)
        s = lax.dot_general(kb, q4, _NT, preferred_element_type=f32) + bias
        m = jnp.maximum(jnp.max(s, axis=0, keepdims=True), sink_ref[kvh])
        return s, m

    def attend(blk, kvh, s, m):
        vb = jnp.concatenate([band(blk, vp_ref, vc_ref, vn_ref, kvh, 1), vam_ref[0, kvh]], axis=1)
        p = jnp.exp2(s - m).astype(bf16)
        acc = jnp.dot(vb, p, preferred_element_type=f32)
        den = acc[A_HEAD_DIM:A_HEAD_DIM + 1] + jnp.exp2(sink_ref[kvh] - m)
        return acc[:A_HEAD_DIM] * (1.0 / den)

    units = [(blk, kvh) for blk in range(w) for kvh in range(A_KV_HEADS)]
    outs = {}
    pending = [scores(*units[u]) for u in range(WIN_AHEAD)]
    for u, unit in enumerate(units):
        if u + WIN_AHEAD < len(units):
            pending.append(scores(*units[u + WIN_AHEAD]))
        outs[unit] = attend(*unit, *pending.pop(0))
        blk, kvh = unit
        if kvh == A_KV_HEADS - 1:
            ot = jnp.concatenate([outs[(blk, j)][:, g * BLOCK:(g + 1) * BLOCK]
                                  for j in range(A_KV_HEADS) for g in range(A_GROUP)], axis=0)
            o_ref[qrows(blk), :] = ot.T.astype(o_ref.dtype)


def _win(sink_rows, qa, ka, vat, kam, vatm, bias_band, bias_meta, batch, seq):
    tb = WIN_BLOCKS * BLOCK
    ntile = seq // tb
    prev_ = lambda b, i: b * ntile + jnp.maximum(i - 1, 0)
    cur_ = lambda b, i: b * ntile + i
    next_ = lambda b, i: b * ntile + jnp.minimum(i + 1, ntile - 1)
    kspec = lambda f: pl.BlockSpec((A_KV_HEADS, tb, A_HEAD_DIM), lambda b, i: (0, f(b, i), 0))
    vspec = lambda f: pl.BlockSpec((A_KV_HEADS, A_V_AUG, tb), lambda b, i: (0, 0, f(b, i)))
    return pl.pallas_call(
        _win_kernel,
        grid=(batch, ntile),
        in_specs=[
            pl.BlockSpec((A_KV_HEADS, 1, A_GROUP * BLOCK), lambda b, i: (0, 0, 0)),
            pl.BlockSpec((A_HEADS, tb, A_HEAD_DIM), lambda b, i: (0, cur_(b, i), 0)),
            kspec(prev_), kspec(cur_), kspec(next_),
            vspec(prev_), vspec(cur_), vspec(next_),
            pl.BlockSpec((A_KV_HEADS, N_META, A_HEAD_DIM), lambda b, i: (0, b, 0)),
            pl.BlockSpec((1, A_KV_HEADS, A_V_AUG, N_META), lambda b, i: (b, 0, 0, 0)),
            pl.BlockSpec((3, A_KV_HEADS, 3 * BLOCK, A_GROUP * BLOCK), lambda b, i: (0, 0, 0, 0)),
            pl.BlockSpec((WIN_BLOCKS, A_KV_HEADS, N_META, A_GROUP * BLOCK), lambda b, i: (i, 0, 0, 0)),
        ],
        out_specs=pl.BlockSpec((tb, A_WIDTH), lambda b, i: (cur_(b, i), 0)),
        out_shape=jax.ShapeDtypeStruct((batch * seq, A_WIDTH), bf16),
        compiler_params=_cparams("arbitrary", "arbitrary"),
        name="win",
    )(sink_rows, qa, ka, ka, ka, vat, vat, vat, kam, vatm, bias_band, bias_meta)


def _win_meta_kernel(sink_ref, q_ref, ka_ref, vat_ref, kam_ref, vatm_ref, biasm_ref, biasr_ref, o_ref):
    for kvh in range(A_KV_HEADS):
        q4 = q_ref[A_GROUP * kvh:A_GROUP * (kvh + 1)].reshape(A_GROUP * N_META, A_HEAD_DIM)
        sm = lax.dot_general(q4, kam_ref[kvh], _NT, preferred_element_type=f32)
        sr = lax.dot_general(q4, ka_ref[kvh], _NT, preferred_element_type=f32)
        for g in range(A_GROUP):
            h = A_GROUP * kvh + g
            rows = slice(g * N_META, (g + 1) * N_META)
            smg = sm[rows] + biasm_ref[h]
            srg = sr[rows] + biasr_ref[h]
            sink = sink_ref[h]
            m = jnp.maximum(jnp.maximum(jnp.max(smg, axis=-1, keepdims=True),
                                        jnp.max(srg, axis=-1, keepdims=True)), sink)
            pm = jnp.exp2(smg - m).astype(bf16)
            pr = jnp.exp2(srg - m).astype(bf16)
            o = (lax.dot_general(pm, vatm_ref[0, kvh], _NT, preferred_element_type=f32)
                 + lax.dot_general(pr, vat_ref[kvh], _NT, preferred_element_type=f32))
            den = o[:, A_HEAD_DIM:A_HEAD_DIM + 1] + jnp.exp2(sink - m)
            o_ref[:, h * A_HEAD_DIM:(h + 1) * A_HEAD_DIM] = (o[:, :A_HEAD_DIM] * (1.0 / den)).astype(o_ref.dtype)


def _win_meta(sink, qam, ka, vat, kam, vatm, bias_mm, bias_mr, batch, seq):
    nblk = seq // BLOCK
    const3 = lambda shape: pl.BlockSpec(shape, lambda b: (0, 0, 0))
    return pl.pallas_call(
        _win_meta_kernel,
        grid=(batch,),
        in_specs=[
            pl.BlockSpec(memory_space=pltpu.SMEM),
            pl.BlockSpec((A_HEADS, N_META, A_HEAD_DIM), lambda b: (0, b, 0)),
            pl.BlockSpec((A_KV_HEADS, BLOCK, A_HEAD_DIM), lambda b: (0, b * nblk, 0)),
            pl.BlockSpec((A_KV_HEADS, A_V_AUG, BLOCK), lambda b: (0, 0, b * nblk)),
            pl.BlockSpec((A_KV_HEADS, N_META, A_HEAD_DIM), lambda b: (0, b, 0)),
            pl.BlockSpec((1, A_KV_HEADS, A_V_AUG, N_META), lambda b: (b, 0, 0, 0)),
            const3((A_HEADS, N_META, N_META)),
            const3((A_HEADS, N_META, BLOCK)),
        ],
        out_specs=pl.BlockSpec((N_META, A_WIDTH), lambda b: (b, 0)),
        out_shape=jax.ShapeDtypeStruct((batch * N_META, A_WIDTH), bf16),
        compiler_params=_cparams("arbitrary"),
        name="win_meta",
    )(sink, qam, ka, vat, kam, vatm, bias_mm, bias_mr)


def _mla_kernel(q_ref, k_ref, km_ref, vt_ref, vtm_ref, o_ref, acc_ref, s0_ref, s1_ref, sm0_ref, sm1_ref):
    s_bufs, sm_bufs = (s0_ref, s1_ref), (sm0_ref, sm1_ref)
    nsub, _, _, tq = acc_ref.shape
    units = nsub * B_HEADS
    seq = k_ref.shape[1]
    chunks = [(c * MLA_KEY_CHUNK, (c + 1) * MLA_KEY_CHUNK) for c in range(seq // MLA_KEY_CHUNK)]

    def unit(u):
        return u % B_HEADS, u // B_HEADS

    def step(t, slot, m, *, score_next=True, attend_cur=True):
        if score_next:
            hn, subn = unit(t + 1)
            q = q_ref[hn, pl.ds(pl.multiple_of(subn * tq, tq), tq), :]
            sm = lax.dot_general(km_ref[hn], q, _NT, preferred_element_type=f32)
            sm_bufs[1 - slot][...] = sm
            m_next = jnp.max(sm, axis=0, keepdims=True)
        if attend_cur:
            h, sub = unit(t)
            pm = jnp.exp2(sm_bufs[slot][...] - m).astype(bf16)
            acc = jnp.dot(vtm_ref[0, h], pm, preferred_element_type=f32)
        for lo, hi in chunks:
            if score_next:
                s = lax.dot_general(k_ref[hn, lo:hi, :], q, _NT, preferred_element_type=f32)
                s_bufs[1 - slot][lo:hi, :] = s
                m_next = jnp.maximum(m_next, jnp.max(s, axis=0, keepdims=True))
            if attend_cur:
                p = jnp.exp2(s_bufs[slot][lo:hi, :] - m).astype(bf16)
                acc = acc + jnp.dot(vt_ref[h, :, lo:hi], p, preferred_element_type=f32)
        if attend_cur:
            acc_ref[sub, h] = acc[:B_V_DIM] * (1.0 / acc[B_V_DIM:B_V_DIM + 1])
        return m_next if score_next else None

    def steps(j, m):
        for i in range(MLA_STEPS_PER_ITER):
            m = step(MLA_STEPS_PER_ITER * j + i, i % 2, m)
        return m

    assert MLA_STEPS_PER_ITER % 2 == 0 and units % 2 == 0
    iters = (units - 1) // MLA_STEPS_PER_ITER
    m = step(-1, 1, None, attend_cur=False)
    m = lax.fori_loop(0, iters, steps, m)
    for t in range(iters * MLA_STEPS_PER_ITER, units - 1):
        m = step(t, t % 2, m)
    step(units - 1, 1, m, score_next=False)
    for sub in range(nsub):
        o_ref[sub * tq:(sub + 1) * tq, :] = acc_ref[sub].reshape(B_WIDTH, tq).T.astype(o_ref.dtype)


def _mla(q, k, km, vt, vtm, batch, seq):
    tq = MLA_Q_TILE
    tb = MLA_Q_TILE * MLA_SUBTILES
    nq = seq // tb
    return pl.pallas_call(
        _mla_kernel,
        grid=(batch, nq),
        in_specs=[
            pl.BlockSpec((B_HEADS, tb, B_QK_PAD), lambda b, i: (0, b * nq + i, 0)),
            pl.BlockSpec((B_HEADS, seq, B_QK_PAD), lambda b, i: (0, b, 0)),
            pl.BlockSpec((B_HEADS, N_META, B_QK_PAD), lambda b, i: (0, b, 0)),
            pl.BlockSpec((B_HEADS, B_V_AUG, seq), lambda b, i: (0, 0, b)),
            pl.BlockSpec((1, B_HEADS, B_V_AUG, N_META), lambda b, i: (b, 0, 0, 0)),
        ],
        out_specs=pl.BlockSpec((tb, B_WIDTH), lambda b, i: (b * nq + i, 0)),
        out_shape=jax.ShapeDtypeStruct((batch * seq, B_WIDTH), bf16),
        scratch_shapes=([pltpu.VMEM((MLA_SUBTILES, B_HEADS, B_V_DIM, tq), f32)]
                        + [pltpu.VMEM((seq, tq), f32)] * 2 + [pltpu.VMEM((N_META, tq), f32)] * 2),
        compiler_params=_cparams("arbitrary", "arbitrary"),
        name="mla",
    )(q, k, km, vt, vtm)


def _mla_meta_kernel(q_ref, kcat_ref, kcatm_ref, lat_t_ref, lat_tm_ref, wuk_ref, wuv_ref, o_ref):
    lane = lax.broadcasted_iota(jnp.int32, (N_META, B_QK_PAD), 1)
    rope_lanes = (lane >= B_NOPE_DIM) & (lane < B_NOPE_DIM + B_ROPE_DIM)
    q_rows = []
    for h in range(B_HEADS):
        q = q_ref[h]
        q_lat = jnp.dot(q, wuk_ref[h], preferred_element_type=f32).astype(bf16)
        q_rows.append(jnp.concatenate([q_lat, jnp.where(rope_lanes, q, jnp.zeros_like(q))], axis=1))
    qcat = jnp.concatenate(q_rows, axis=0)
    seq = kcat_ref.shape[0]
    chunks = [(c * MLA_KEY_CHUNK, (c + 1) * MLA_KEY_CHUNK) for c in range(seq // MLA_KEY_CHUNK)]
    sm = lax.dot_general(kcatm_ref[...], qcat, _NT, preferred_element_type=f32)
    ss = [lax.dot_general(kcat_ref[lo:hi, :], qcat, _NT, preferred_element_type=f32) for lo, hi in chunks]
    m = jnp.max(sm, axis=0, keepdims=True)
    for s in ss:
        m = jnp.maximum(m, jnp.max(s, axis=0, keepdims=True))
    acc = jnp.dot(lat_tm_ref[0], jnp.exp2(sm - m).astype(bf16), preferred_element_type=f32)
    for (lo, hi), s in zip(chunks, ss):
        acc = acc + jnp.dot(lat_t_ref[:, lo:hi], jnp.exp2(s - m).astype(bf16),
                            preferred_element_type=f32)
    o_lat = (acc[:KV_LORA_RANK] * (1.0 / acc[KV_LORA_RANK:KV_LORA_RANK + 1])).T.astype(bf16)
    for h in range(B_HEADS):
        o_ref[:, h * B_V_DIM:(h + 1) * B_V_DIM] = jnp.dot(
            o_lat[h * N_META:(h + 1) * N_META], wuv_ref[h], preferred_element_type=f32).astype(o_ref.dtype)


def _mla_meta(qm, kcat, kcatm, lat_t, lat_tm, lw, layer, batch, seq):
    const = functools.partial(_layer_spec, layer)
    return pl.pallas_call(
        _mla_meta_kernel,
        grid=(batch,),
        in_specs=[
            pl.BlockSpec((B_HEADS, N_META, B_QK_PAD), lambda b: (0, b, 0)),
            pl.BlockSpec((seq, KV_LORA_RANK + B_QK_PAD), lambda b: (b, 0)),
            pl.BlockSpec((N_META, KV_LORA_RANK + B_QK_PAD), lambda b: (b, 0)),
            pl.BlockSpec((LAT_T_ROWS, seq), lambda b: (0, b)),
            pl.BlockSpec((1, LAT_T_ROWS, N_META), lambda b: (b, 0, 0)),
            const((B_HEADS, B_QK_PAD, KV_LORA_RANK)),
            const((B_HEADS, KV_LORA_RANK, B_V_DIM)),
        ],
        out_specs=pl.BlockSpec((N_META, B_WIDTH), lambda b: (b, 0)),
        out_shape=jax.ShapeDtypeStruct((batch * N_META, B_WIDTH), bf16),
        compiler_params=_cparams("arbitrary"),
        name="mla_meta",
    )(qm, kcat, kcatm, lat_t, lat_tm, lw["wuk"], lw["wuv"])


def _rope_tables(pos):
    half = B_ROPE_DIM // 2
    freqs = ROPE_THETA ** (-jnp.arange(half, dtype=jnp.float32) / half)
    tail = B_QK_PAD - B_NOPE_DIM - B_ROPE_DIM
    lane_freq = jnp.concatenate([jnp.zeros((B_NOPE_DIM,), f32), freqs, freqs, jnp.zeros((tail,), f32)])
    ang = pos.astype(jnp.float32)[:, None] * lane_freq[None, :]
    cos, sin = jnp.cos(ang), jnp.sin(ang)
    lane = jnp.arange(B_QK_PAD)[None, :]
    nope = lane < B_NOPE_DIM
    rope = (lane >= B_NOPE_DIM) & (lane < B_NOPE_DIM + B_ROPE_DIM)
    c_mla = (B_NOPE_DIM + B_ROPE_DIM) ** -0.5 * LOG2E
    cos_q = jnp.where(nope, c_mla, jnp.where(rope, c_mla * cos, 0.0))
    sin_q = jnp.where(rope, c_mla * sin, 0.0)
    cos_k = jnp.where(rope, cos, 0.0)
    sin_k = jnp.where(rope, sin, 0.0)
    return jnp.concatenate([cos_q, sin_q, cos_k, sin_k], axis=1).astype(f32)


def _rot_cols(w):
    half = w.shape[-1] // 2
    return jnp.concatenate([-w[..., half:], w[..., :half]], axis=-1)


def _stacked_weights(norm_in, w_in, norm_q_lat, w_uq, norm_kv_lat, w_ukv, norm_out_a, norm_out_b, w_out):
    depth, d, _ = w_in.shape
    sizes = (A_WIDTH, A_KV_HEADS * A_HEAD_DIM, A_KV_HEADS * A_HEAD_DIM, A_WIDTH,
             Q_LORA_RANK, KV_LORA_RANK, B_ROPE_DIM, B_WIDTH)
    offs = [sum(sizes[:i]) for i in range(len(sizes) + 1)]
    qa, ka, va, ga, cq, ckv, kr, gb = (w_in[..., offs[i]:offs[i + 1]] for i in range(len(sizes)))
    tail = B_QK_PAD - B_NOPE_DIM - B_ROPE_DIM
    zeros = lambda *shape: jnp.zeros(shape, w_in.dtype)
    kr_groups = jnp.concatenate([zeros(depth, d, B_NOPE_DIM), kr, zeros(depth, d, tail + B_NOPE_DIM),
                                 _rot_cols(kr), zeros(depth, d, tail)], axis=-1)
    w1 = jnp.concatenate([qa, cq, ckv, ka, kr_groups], axis=-1).astype(bf16)

    r = w_uq.shape[1]
    uq = w_uq.reshape(depth, r, B_HEADS, B_NOPE_DIM + B_ROPE_DIM)
    nope, rope = uq[..., :B_NOPE_DIM], uq[..., B_NOPE_DIM:]
    wqa = jnp.concatenate([nope, rope, zeros(depth, r, B_HEADS, tail)], axis=-1)
    wqb = jnp.concatenate([jnp.zeros_like(nope), _rot_cols(rope), zeros(depth, r, B_HEADS, tail)], axis=-1)

    rk = w_ukv.shape[1]
    ukv = w_ukv.reshape(depth, rk, B_HEADS, B_NOPE_DIM + B_V_DIM)
    k_nope, v = ukv[..., :B_NOPE_DIM], ukv[..., B_NOPE_DIM:]
    wk = jnp.concatenate([k_nope, zeros(depth, rk, B_HEADS, B_QK_PAD - B_NOPE_DIM)], axis=-1)
    row = lambda g: g.astype(f32)[:, None, :]
    return {
        "gin": row(norm_in),
        "w1": w1,
        "wg": jnp.concatenate([ga, gb], axis=-1).astype(bf16),
        "wvat": jnp.swapaxes(va, 1, 2).astype(bf16),
        "gq": row(norm_q_lat),
        "wqa": wqa.reshape(depth, r, B_HEADS * B_QK_PAD).astype(bf16),
        "wqb": wqb.reshape(depth, r, B_HEADS * B_QK_PAD).astype(bf16),
        "gkv": row(norm_kv_lat),
        "wk": wk.reshape(depth, rk, B_HEADS * B_QK_PAD).astype(bf16),
        "wvt": jnp.swapaxes(v.reshape(depth, rk, B_WIDTH), 1, 2).astype(bf16),
        "wuk": jnp.concatenate([jnp.transpose(k_nope, (0, 2, 3, 1)),
                                zeros(depth, B_HEADS, B_QK_PAD - B_NOPE_DIM, rk)], axis=2).astype(bf16),
        "wuv": jnp.transpose(v, (0, 2, 1, 3)).astype(bf16),
        "na": row(norm_out_a),
        "nb": row(norm_out_b),
        "wout": w_out.astype(bf16),
    }


def kernel(x, meta_tokens, rel_bias_table, norm_in, w_in, sink_a, norm_q_lat, w_uq, norm_kv_lat, w_ukv,
           norm_out_a, norm_out_b, w_out, norm_final):
    batch, seq, d = x.shape
    depth = w_in.shape[0]
    assert d == D_MODEL and seq % (MLA_Q_TILE * MLA_SUBTILES) == 0 and seq % ROW_TILE == 0

    h_real = x.reshape(batch * seq, d).astype(f32)
    h_meta = jnp.tile(meta_tokens.astype(f32), (batch, 1))
    tab_real = _rope_tables(N_META + jnp.arange(seq))
    tab_meta = _rope_tables(jnp.arange(batch * N_META) % N_META)
    bias_band, bias_meta, bias_mm, bias_mr = _bias_tables(rel_bias_table, seq // BLOCK)

    lw = _stacked_weights(norm_in, w_in, norm_q_lat, w_uq, norm_kv_lat, w_ukv, norm_out_a, norm_out_b, w_out)
    sinks = sink_a.astype(f32) * LOG2E
    sinks_rows = jnp.repeat(sinks.reshape(depth, A_KV_HEADS, A_GROUP), BLOCK, axis=2)[:, :, None, :]

    out = None
    for i in range(depth):
        last = i == depth - 1
        sink, sink_rows = sinks[i], sinks_rows[i]
        qa, ka, vat, q, k, vt, *lat = _proj(h_real, tab_real, lw, i, latent=not last)
        qam, kam, vatm, qm, km, vtm, *latm = _proj(h_meta, tab_meta, lw, i, latent=not last)
        vtm_b = vtm.reshape(B_HEADS, B_V_AUG, batch, N_META).transpose(2, 0, 1, 3)
        vatm_b = vatm.reshape(A_KV_HEADS, A_V_AUG, batch, N_META).transpose(2, 0, 1, 3)

        ya = _win(sink_rows, qa, ka, vat, kam, vatm_b, bias_band, bias_meta, batch, seq)
        yb = _mla(q, k, km, vt, vtm_b, batch, seq)
        if last:
            out = _out(h_real, ya, yb, lw, i, norm_final.astype(f32)[None, :])
        else:
            yam = _win_meta(sink, qam, ka, vat, kam, vatm_b, bias_mm, bias_mr, batch, seq)
            (kcat, lat_t), (kcatm, lat_tm) = lat, latm
            lat_tm_b = lat_tm.reshape(LAT_T_ROWS, batch, N_META).transpose(1, 0, 2)
            ybm = _mla_meta(qm, kcat, kcatm, lat_t, lat_tm_b, lw, i, batch, seq)
            h_real = _out(h_real, ya, yb, lw, i)
            h_meta = _out(h_meta, yam, ybm, lw, i)
    return out.reshape(batch, seq, d).astype(x.dtype)
```

```python
import functools
import math

import jax
import jax.numpy as jnp
from jax import lax
from jax.experimental import pallas as pl
from jax.experimental.pallas import tpu as pltpu

D_MODEL = 1024
N_META = 16
BLOCK = 128
WINDOW = 128
A_HEADS = 8
A_KV_HEADS = 2
A_GROUP = A_HEADS // A_KV_HEADS
A_HEAD_DIM = 64
A_WIDTH = A_HEADS * A_HEAD_DIM
B_HEADS = 8
B_NOPE_DIM = 64
B_ROPE_DIM = 32
B_V_DIM = 64
B_WIDTH = B_HEADS * B_V_DIM
B_QK_PAD = B_NOPE_DIM + 2 * B_ROPE_DIM
BF16_SUBLANES = 16
B_V_AUG = B_V_DIM + BF16_SUBLANES
A_V_AUG = A_HEAD_DIM + BF16_SUBLANES
Q_LORA_RANK = 256
KV_LORA_RANK = 128
LAT_T_ROWS = KV_LORA_RANK + BF16_SUBLANES
N_BUCKETS = 32
MAX_DISTANCE = 128
ROPE_THETA = 10000.0
EPS = 1e-6

LOG2E = math.log2(math.e)
NEG = -1e30
ROW_TILE = 1024
MLA_Q_TILE = 256
MLA_SUBTILES = 8
MLA_STEPS_PER_ITER = 8
MLA_KEY_CHUNK = 1024
WIN_BLOCKS = 16
WIN_AHEAD = 2
BIAS_ROW_CHUNK = 32
VMEM_LIMIT_BYTES = 56 * 1024 * 1024

_C_QA = 0
_C_CQ = _C_QA + A_WIDTH
_C_CKV = _C_CQ + Q_LORA_RANK
_C_KA = _C_CKV + KV_LORA_RANK
_C_KRA = _C_KA + A_KV_HEADS * A_HEAD_DIM
_C_KRB = _C_KRA + B_QK_PAD
_C_END = _C_KRB + B_QK_PAD

_NT = (((1,), (1,)), ((), ()))

bf16 = jnp.bfloat16
f32 = jnp.float32


def _cparams(*sem, flags=None):
    return pltpu.CompilerParams(dimension_semantics=sem, vmem_limit_bytes=VMEM_LIMIT_BYTES, flags=flags)


def _rms(x, gain):
    return x * lax.rsqrt(jnp.mean(x * x, axis=-1, keepdims=True) + EPS) * gain


def _silu(x):
    return x / (1.0 + jnp.exp(-x))


def _proj_kernel(h_ref, tab_ref, gin_ref, w1_ref, wvat_ref, gq_ref, wq_ref, gkv_ref, wk_ref, wvt_ref,
                 qa_ref, ka_ref, vat_ref, q_ref, k_ref, vt_ref, *latent_refs):
    u = _rms(h_ref[...], gin_ref[...]).astype(bf16)
    t = u.shape[0]

    def mm(lo, hi):
        return jnp.dot(u, w1_ref[:, lo:hi], preferred_element_type=f32)

    qa = mm(_C_QA, _C_CQ) * (A_HEAD_DIM ** -0.5 * LOG2E)
    for h in range(A_HEADS):
        qa_ref[h] = qa[:, h * A_HEAD_DIM:(h + 1) * A_HEAD_DIM].astype(bf16)
    vat = lax.dot_general(wvat_ref[...], u, _NT, preferred_element_type=f32)
    vat_ref[:, :A_HEAD_DIM, :] = vat.reshape(A_KV_HEADS, A_HEAD_DIM, t).astype(bf16)
    vat_ref[:, A_HEAD_DIM:, :] = jnp.ones((A_KV_HEADS, A_V_AUG - A_HEAD_DIM, t), bf16)
    mixed = mm(_C_CQ, _C_END)
    cq = mixed[:, :_C_CKV - _C_CQ]
    ckv = mixed[:, _C_CKV - _C_CQ:_C_KA - _C_CQ]
    ka = mixed[:, _C_KA - _C_CQ:_C_KRA - _C_CQ]
    kra = mixed[:, _C_KRA - _C_CQ:_C_KRB - _C_CQ]
    krb = mixed[:, _C_KRB - _C_CQ:]
    for j in range(A_KV_HEADS):
        ka_ref[j] = ka[:, j * A_HEAD_DIM:(j + 1) * A_HEAD_DIM].astype(bf16)
    tab_rows = pl.ds(pl.multiple_of((pl.program_id(0) % (tab_ref.shape[0] // t)) * t, t), t)
    mul_q = tab_ref[tab_rows, 0 * B_QK_PAD:1 * B_QK_PAD]
    cos_k = tab_ref[tab_rows, 1 * B_QK_PAD:2 * B_QK_PAD]
    sin_k = tab_ref[tab_rows, 2 * B_QK_PAD:3 * B_QK_PAD]

    cqn = _rms(cq, gq_ref[...]).astype(bf16)
    qp = jnp.dot(cqn, wq_ref[...], preferred_element_type=f32)
    for h in range(B_HEADS):
        q_ref[h] = (qp[:, h * B_QK_PAD:(h + 1) * B_QK_PAD] * mul_q).astype(bf16)

    ckvn_f32 = _rms(ckv, gkv_ref[...])
    ckvn = ckvn_f32.astype(bf16)
    k_nope = jnp.dot(ckvn, wk_ref[...], preferred_element_type=f32)
    k_rope = kra * cos_k + krb * sin_k
    for h in range(B_HEADS):
        k_ref[h] = (k_nope[:, h * B_QK_PAD:(h + 1) * B_QK_PAD] + k_rope).astype(bf16)
    vt = lax.dot_general(wvt_ref[...], ckvn, _NT, preferred_element_type=f32)
    vt_ref[:, :B_V_DIM, :] = vt.reshape(B_HEADS, B_V_DIM, t).astype(bf16)
    vt_ref[:, B_V_DIM:, :] = jnp.ones((B_HEADS, B_V_AUG - B_V_DIM, t), bf16)
    if latent_refs:
        kcat_ref, lat_t_ref = latent_refs
        kcat_ref[:, :KV_LORA_RANK] = ckvn
        kcat_ref[:, KV_LORA_RANK:] = k_rope.astype(bf16)
        lat_t_ref[:KV_LORA_RANK, :] = ckvn_f32.T.astype(bf16)
        lat_t_ref[KV_LORA_RANK:, :] = jnp.ones((LAT_T_ROWS - KV_LORA_RANK, t), bf16)


def _layer_spec(layer, shape):
    return pl.BlockSpec((None,) + tuple(shape), lambda i: (layer,) + (0,) * len(shape))


def _proj(h, tab, lw, layer, latent):
    rows = h.shape[0]
    t = min(ROW_TILE, rows)
    steps = rows // t
    assert tab.shape[0] % t == 0
    const = functools.partial(_layer_spec, layer)
    latent_specs = [pl.BlockSpec((t, KV_LORA_RANK + B_QK_PAD), lambda i: (i, 0)),
                    pl.BlockSpec((LAT_T_ROWS, t), lambda i: (0, i))] if latent else []
    latent_shapes = [jax.ShapeDtypeStruct((rows, KV_LORA_RANK + B_QK_PAD), bf16),
                     jax.ShapeDtypeStruct((LAT_T_ROWS, rows), bf16)] if latent else []
    return pl.pallas_call(
        _proj_kernel,
        grid=(steps,),
        in_specs=[
            pl.BlockSpec((t, D_MODEL), lambda i: (i, 0)),
            pl.BlockSpec(tab.shape, lambda i: (0, 0)),
            const((1, D_MODEL)),
            const((D_MODEL, _C_END)),
            const((A_KV_HEADS * A_HEAD_DIM, D_MODEL)),
            const((1, Q_LORA_RANK)),
            const((Q_LORA_RANK, B_HEADS * B_QK_PAD)),
            const((1, KV_LORA_RANK)),
            const((KV_LORA_RANK, B_HEADS * B_QK_PAD)),
            const((B_WIDTH, KV_LORA_RANK)),
        ],
        out_specs=[
            pl.BlockSpec((A_HEADS, t, A_HEAD_DIM), lambda i: (0, i, 0)),
            pl.BlockSpec((A_KV_HEADS, t, A_HEAD_DIM), lambda i: (0, i, 0)),
            pl.BlockSpec((A_KV_HEADS, A_V_AUG, t), lambda i: (0, 0, i)),
            pl.BlockSpec((B_HEADS, t, B_QK_PAD), lambda i: (0, i, 0)),
            pl.BlockSpec((B_HEADS, t, B_QK_PAD), lambda i: (0, i, 0)),
            pl.BlockSpec((B_HEADS, B_V_AUG, t), lambda i: (0, 0, i)),
        ] + latent_specs,
        out_shape=[
            jax.ShapeDtypeStruct((A_HEADS, rows, A_HEAD_DIM), bf16),
            jax.ShapeDtypeStruct((A_KV_HEADS, rows, A_HEAD_DIM), bf16),
            jax.ShapeDtypeStruct((A_KV_HEADS, A_V_AUG, rows), bf16),
            jax.ShapeDtypeStruct((B_HEADS, rows, B_QK_PAD), bf16),
            jax.ShapeDtypeStruct((B_HEADS, rows, B_QK_PAD), bf16),
            jax.ShapeDtypeStruct((B_HEADS, B_V_AUG, rows), bf16),
        ] + latent_shapes,
        compiler_params=_cparams("arbitrary"),
        name="proj",
    )(h, tab, lw["gin"], lw["w1"], lw["wvat"], lw["gq"], lw["wq"], lw["gkv"], lw["wk"], lw["wvt"])


def _out_kernel(h_ref, ya_ref, yb_ref, gin_ref, wg_ref, na_ref, nb_ref, w_ref, *rest, final):
    if final:
        nf_ref, o_ref = rest
    else:
        (o_ref,) = rest
    hn = h_ref[...]
    u = _rms(hn, gin_ref[...]).astype(bf16)
    ga = _silu(jnp.dot(u, wg_ref[:, :A_WIDTH], preferred_element_type=f32))
    gb = _silu(jnp.dot(u, wg_ref[:, A_WIDTH:], preferred_element_type=f32))
    ya = _rms(ya_ref[...].astype(f32), na_ref[...]) * ga
    yb = _rms(yb_ref[...].astype(f32), nb_ref[...]) * gb
    hn = hn + jnp.dot(ya.astype(bf16), w_ref[:A_WIDTH, :], preferred_element_type=f32)
    hn = hn + jnp.dot(yb.astype(bf16), w_ref[A_WIDTH:, :], preferred_element_type=f32)
    if final:
        hn = _rms(hn, nf_ref[...])
    o_ref[...] = hn


def _out(h, ya, yb, lw, layer, norm_final=None):
    rows = h.shape[0]
    t = min(ROW_TILE, rows)
    row = lambda width: pl.BlockSpec((t, width), lambda i: (i, 0))
    const = functools.partial(_layer_spec, layer)
    final = norm_final is not None
    in_specs = [row(D_MODEL), row(A_WIDTH), row(B_WIDTH),
                const((1, D_MODEL)), const((D_MODEL, A_WIDTH + B_WIDTH)),
                const((1, A_WIDTH)), const((1, B_WIDTH)), const((A_WIDTH + B_WIDTH, D_MODEL))]
    args = [h, ya, yb, lw["gin"], lw["wg"], lw["na"], lw["nb"], lw["wout"]]
    if final:
        in_specs.append(pl.BlockSpec((1, D_MODEL), lambda i: (0, 0)))
        args.append(norm_final)
    return pl.pallas_call(
        functools.partial(_out_kernel, final=final),
        grid=(rows // t,),
        in_specs=in_specs,
        out_specs=row(D_MODEL),
        out_shape=jax.ShapeDtypeStruct((rows, D_MODEL), f32),
        compiler_params=_cparams("arbitrary"),
        name="outproj",
    )(*args)


def _bias_kernel(tab_ref, idx_ref, o_ref):
    rows = idx_ref.shape[0]
    chunk = math.gcd(rows, BIAS_ROW_CHUNK)

    def body(r, carry):
        sl = pl.ds(pl.multiple_of(r * chunk, chunk), chunk)
        idx = idx_ref[sl, :]
        accs = [jnp.where(idx < 0, NEG, 0.0).astype(f32)] * A_HEADS
        for b in range(N_BUCKETS):
            hit = idx == b
            accs = [jnp.where(hit, tab_ref[b * A_HEADS + h], accs[h]) for h in range(A_HEADS)]
        for h in range(A_HEADS):
            o_ref[h, sl, :] = accs[h]
        return carry

    lax.fori_loop(0, rows // chunk, body, 0)


def _bias_lookup(table_flat, idx):
    return pl.pallas_call(
        _bias_kernel,
        in_specs=[pl.BlockSpec(memory_space=pltpu.SMEM), pl.BlockSpec(idx.shape, lambda: (0, 0))],
        out_specs=pl.BlockSpec((A_HEADS,) + idx.shape, lambda: (0, 0, 0)),
        out_shape=jax.ShapeDtypeStruct((A_HEADS,) + idx.shape, f32),
        compiler_params=pltpu.CompilerParams(vmem_limit_bytes=VMEM_LIMIT_BYTES),
        name="relbias",
    )(table_flat, idx)


def _t5_bucket(rel):
    nb = N_BUCKETS // 2
    max_exact = nb // 2
    ret = jnp.where(rel > 0, nb, 0)
    n = jnp.abs(rel)
    nf = jnp.maximum(n, 1).astype(jnp.float32)
    large = max_exact + (jnp.log(nf / max_exact) / math.log(MAX_DISTANCE / max_exact)
                         * (nb - max_exact)).astype(jnp.int32)
    large = jnp.minimum(large, nb - 1)
    bucket = ret + jnp.where(n < max_exact, n, large)
    return jnp.bitwise_and(bucket, N_BUCKETS - 1)


def _bias_tables(rel_bias_table, nblk):
    table_flat = (rel_bias_table.astype(f32) * LOG2E).reshape(-1)
    i = jnp.arange(BLOCK, dtype=jnp.int32)[:, None]
    j = jnp.arange(3 * BLOCK, dtype=jnp.int32)[None, :]
    rel = j - i - BLOCK
    band = jnp.where(jnp.abs(rel) <= WINDOW, _t5_bucket(rel), -1)
    interior = (_bias_lookup(table_flat, band.T)
                .reshape(A_KV_HEADS, A_GROUP, 3 * BLOCK, BLOCK)
                .transpose(0, 2, 1, 3).reshape(A_KV_HEADS, 3 * BLOCK, A_GROUP * BLOCK))
    key = jnp.arange(3 * BLOCK, dtype=jnp.int32)[None, :, None]
    first = jnp.where(key < BLOCK, NEG, interior)
    last = jnp.where(key >= 2 * BLOCK, NEG, interior)
    bias_band = jnp.stack([first, interior, last])

    n = jnp.arange(nblk, dtype=jnp.int32)[:, None, None]
    k = jnp.arange(N_META, dtype=jnp.int32)[None, :, None]
    qi = jnp.arange(BLOCK, dtype=jnp.int32)[None, None, :]
    rel_m = k - (N_META + n * BLOCK + qi)
    idx_m = _t5_bucket(rel_m).reshape(nblk * N_META, BLOCK)
    bias_meta = (_bias_lookup(table_flat, idx_m)
                 .reshape(A_KV_HEADS, A_GROUP, nblk, N_META, BLOCK)
                 .transpose(2, 0, 3, 1, 4).reshape(nblk, A_KV_HEADS, N_META, A_GROUP * BLOCK))

    qp = jnp.arange(N_META, dtype=jnp.int32)[:, None]
    kp = jnp.arange(N_META + BLOCK, dtype=jnp.int32)[None, :]
    rel_q = kp - qp
    idx_q = jnp.where(jnp.abs(rel_q) <= WINDOW, _t5_bucket(rel_q), -1)
    bias_q = _bias_lookup(table_flat, idx_q)
    return bias_band, bias_meta, bias_q[:, :, :N_META], bias_q[:, :, N_META:]


def _win_kernel(sink_ref, q_ref, kp_ref, kc_ref, kn_ref, vp_ref, vc_ref, vn_ref, kam_ref, vam_ref,
                bias_ref, bm_ref, o_ref):
    w = WIN_BLOCKS
    i, ntile = pl.program_id(1), pl.num_programs(1)

    def qrows(blk):
        return slice(blk * BLOCK, (blk + 1) * BLOCK)

    def band(blk, prev_ref, cur_ref, next_ref, kvh, axis):
        take = lambda ref, lo, hi: ref[kvh, lo:hi, :] if axis == 0 else ref[kvh, :, lo:hi]
        parts = [take(cur_ref, max(blk - 1, 0) * BLOCK, min(blk + 2, w) * BLOCK)]
        if blk == 0:
            parts.insert(0, take(prev_ref, (w - 1) * BLOCK, w * BLOCK))
        if blk == w - 1:
            parts.append(take(next_ref, 0, BLOCK))
        return jnp.concatenate(parts, axis=axis)

    def variant(blk):
        v = 1
        if blk == 0:
            v = jnp.where(i == 0, 0, v)
        if blk == w - 1:
            v = jnp.where(i == ntile - 1, 2, v)
        return v

    def scores(blk, kvh):
        kb = jnp.concatenate([band(blk, kp_ref, kc_ref, kn_ref, kvh, 0), kam_ref[kvh]], axis=0)
        q4 = q_ref[A_GROUP * kvh:A_GROUP * (kvh + 1), qrows(blk), :].reshape(A_GROUP * BLOCK, A_HEAD_DIM)
        bias = jnp.concatenate([bias_ref[variant(blk), kvh], bm_ref[blk, kvh]], axis=0)
        s = lax.dot_general(kb, q4, _NT, preferred_element_type=f32) + bias
        m = jnp.maximum(jnp.max(s, axis=0, keepdims=True), sink_ref[kvh])
        return s, m

    def attend(blk, kvh, s, m):
        vb = jnp.concatenate([band(blk, vp_ref, vc_ref, vn_ref, kvh, 1), vam_ref[0, kvh]], axis=1)
        p = jnp.exp2(s - m).astype(bf16)
        acc = jnp.dot(vb, p, preferred_element_type=f32)
        den = acc[A_HEAD_DIM:A_HEAD_DIM + 1] + jnp.exp2(sink_ref[kvh] - m)
        return acc[:A_HEAD_DIM] * (1.0 / den)

    units = [(blk, kvh) for blk in range(w) for kvh in range(A_KV_HEADS)]
    outs = {}
    pending = [scores(*units[u]) for u in range(WIN_AHEAD)]
    for u, unit in enumerate(units):
        if u + WIN_AHEAD < len(units):
            pending.append(scores(*units[u + WIN_AHEAD]))
        outs[unit] = attend(*unit, *pending.pop(0))
        blk, kvh = unit
        if kvh == A_KV_HEADS - 1:
            ot = jnp.concatenate([outs[(blk, j)][:, g * BLOCK:(g + 1) * BLOCK]
                                  for j in range(A_KV_HEADS) for g in range(A_GROUP)], axis=0)
            o_ref[qrows(blk), :] = ot.T.astype(o_ref.dtype)


def _win(sink_rows, qa, ka, vat, kam, vatm, bias_band, bias_meta, batch, seq):
    tb = WIN_BLOCKS * BLOCK
    ntile = seq // tb
    prev_ = lambda b, i: b * ntile + jnp.maximum(i - 1, 0)
    cur_ = lambda b, i: b * ntile + i
    next_ = lambda b, i: b * ntile + jnp.minimum(i + 1, ntile - 1)
    kspec = lambda f: pl.BlockSpec((A_KV_HEADS, tb, A_HEAD_DIM), lambda b, i: (0, f(b, i), 0))
    vspec = lambda f: pl.BlockSpec((A_KV_HEADS, A_V_AUG, tb), lambda b, i: (0, 0, f(b, i)))
    return pl.pallas_call(
        _win_kernel,
        grid=(batch, ntile),
        in_specs=[
            pl.BlockSpec((A_KV_HEADS, 1, A_GROUP * BLOCK), lambda b, i: (0, 0, 0)),
            pl.BlockSpec((A_HEADS, tb, A_HEAD_DIM), lambda b, i: (0, cur_(b, i), 0)),
            kspec(prev_), kspec(cur_), kspec(next_),
            vspec(prev_), vspec(cur_), vspec(next_),
            pl.BlockSpec((A_KV_HEADS, N_META, A_HEAD_DIM), lambda b, i: (0, b, 0)),
            pl.BlockSpec((1, A_KV_HEADS, A_V_AUG, N_META), lambda b, i: (b, 0, 0, 0)),
            pl.BlockSpec((3, A_KV_HEADS, 3 * BLOCK, A_GROUP * BLOCK), lambda b, i: (0, 0, 0, 0)),
            pl.BlockSpec((WIN_BLOCKS, A_KV_HEADS, N_META, A_GROUP * BLOCK), lambda b, i: (i, 0, 0, 0)),
        ],
        out_specs=pl.BlockSpec((tb, A_WIDTH), lambda b, i: (cur_(b, i), 0)),
        out_shape=jax.ShapeDtypeStruct((batch * seq, A_WIDTH), bf16),
        compiler_params=_cparams("arbitrary", "arbitrary"),
        name="win",
    )(sink_rows, qa, ka, ka, ka, vat, vat, vat, kam, vatm, bias_band, bias_meta)


def _win_meta_kernel(sink_ref, q_ref, ka_ref, vat_ref, kam_ref, vatm_ref, biasm_ref, biasr_ref, o_ref):
    for kvh in range(A_KV_HEADS):
        q4 = q_ref[A_GROUP * kvh:A_GROUP * (kvh + 1)].reshape(A_GROUP * N_META, A_HEAD_DIM)
        sm = lax.dot_general(q4, kam_ref[kvh], _NT, preferred_element_type=f32)
        sr = lax.dot_general(q4, ka_ref[kvh], _NT, preferred_element_type=f32)
        for g in range(A_GROUP):
            h = A_GROUP * kvh + g
            rows = slice(g * N_META, (g + 1) * N_META)
            smg = sm[rows] + biasm_ref[h]
            srg = sr[rows] + biasr_ref[h]
            sink = sink_ref[h]
            m = jnp.maximum(jnp.maximum(jnp.max(smg, axis=-1, keepdims=True),
                                        jnp.max(srg, axis=-1, keepdims=True)), sink)
            pm = jnp.exp2(smg - m).astype(bf16)
            pr = jnp.exp2(srg - m).astype(bf16)
            o = (lax.dot_general(pm, vatm_ref[0, kvh], _NT, preferred_element_type=f32)
                 + lax.dot_general(pr, vat_ref[kvh], _NT, preferred_element_type=f32))
            den = o[:, A_HEAD_DIM:A_HEAD_DIM + 1] + jnp.exp2(sink - m)
            o_ref[:, h * A_HEAD_DIM:(h + 1) * A_HEAD_DIM] = (o[:, :A_HEAD_DIM] * (1.0 / den)).astype(o_ref.dtype)


def _win_meta(sink, qam, ka, vat, kam, vatm, bias_mm, bias_mr, batch, seq):
    nblk = seq // BLOCK
    const3 = lambda shape: pl.BlockSpec(shape, lambda b: (0, 0, 0))
    return pl.pallas_call(
        _win_meta_kernel,
        grid=(batch,),
        in_specs=[
            pl.BlockSpec(memory_space=pltpu.SMEM),
            pl.BlockSpec((A_HEADS, N_META, A_HEAD_DIM), lambda b: (0, b, 0)),
            pl.BlockSpec((A_KV_HEADS, BLOCK, A_HEAD_DIM), lambda b: (0, b * nblk, 0)),
            pl.BlockSpec((A_KV_HEADS, A_V_AUG, BLOCK), lambda b: (0, 0, b * nblk)),
            pl.BlockSpec((A_KV_HEADS, N_META, A_HEAD_DIM), lambda b: (0, b, 0)),
            pl.BlockSpec((1, A_KV_HEADS, A_V_AUG, N_META), lambda b: (b, 0, 0, 0)),
            const3((A_HEADS, N_META, N_META)),
            const3((A_HEADS, N_META, BLOCK)),
        ],
        out_specs=pl.BlockSpec((N_META, A_WIDTH), lambda b: (b, 0)),
        out_shape=jax.ShapeDtypeStruct((batch * N_META, A_WIDTH), bf16),
        compiler_params=_cparams("arbitrary"),
        name="win_meta",
    )(sink, qam, ka, vat, kam, vatm, bias_mm, bias_mr)


def _mla_kernel(q_ref, k_ref, km_ref, vt_ref, vtm_ref, o_ref, acc_ref, s0_ref, s1_ref, sm0_ref, sm1_ref):
    s_bufs, sm_bufs = (s0_ref, s1_ref), (sm0_ref, sm1_ref)
    nsub, _, _, tq = acc_ref.shape
    units = nsub * B_HEADS
    seq = k_ref.shape[1]
    chunks = [(c * MLA_KEY_CHUNK, (c + 1) * MLA_KEY_CHUNK) for c in range(seq // MLA_KEY_CHUNK)]

    def unit(u):
        return u % B_HEADS, u // B_HEADS

    def step(t, slot, m, *, score_next=True, attend_cur=True):
        if score_next:
            hn, subn = unit(t + 1)
            q = q_ref[hn, pl.ds(pl.multiple_of(subn * tq, tq), tq), :]
            sm = lax.dot_general(km_ref[hn], q, _NT, preferred_element_type=f32)
            sm_bufs[1 - slot][...] = sm
            m_next = jnp.max(sm, axis=0, keepdims=True)
        if attend_cur:
            h, sub = unit(t)
            pm = jnp.exp2(sm_bufs[slot][...] - m).astype(bf16)
            acc = jnp.dot(vtm_ref[0, h], pm, preferred_element_type=f32)
        for lo, hi in chunks:
            if score_next:
                s = lax.dot_general(k_ref[hn, lo:hi, :], q, _NT, preferred_element_type=f32)
                s_bufs[1 - slot][lo:hi, :] = s
                m_next = jnp.maximum(m_next, jnp.max(s, axis=0, keepdims=True))
            if attend_cur:
                p = jnp.exp2(s_bufs[slot][lo:hi, :] - m).astype(bf16)
                acc = acc + jnp.dot(vt_ref[h, :, lo:hi], p, preferred_element_type=f32)
        if attend_cur:
            acc_ref[sub, h] = acc[:B_V_DIM] * (1.0 / acc[B_V_DIM:B_V_DIM + 1])
        return m_next if score_next else None

    def steps(j, m):
        for i in range(MLA_STEPS_PER_ITER):
            m = step(MLA_STEPS_PER_ITER * j + i, i % 2, m)
        return m

    assert MLA_STEPS_PER_ITER % 2 == 0 and units % 2 == 0
    iters = (units - 1) // MLA_STEPS_PER_ITER
    m = step(-1, 1, None, attend_cur=False)
    m = lax.fori_loop(0, iters, steps, m)
    for t in range(iters * MLA_STEPS_PER_ITER, units - 1):
        m = step(t, t % 2, m)
    step(units - 1, 1, m, score_next=False)
    for sub in range(nsub):
        o_ref[sub * tq:(sub + 1) * tq, :] = acc_ref[sub].reshape(B_WIDTH, tq).T.astype(o_ref.dtype)


def _mla(q, k, km, vt, vtm, batch, seq):
    tq = MLA_Q_TILE
    tb = MLA_Q_TILE * MLA_SUBTILES
    nq = seq // tb
    return pl.pallas_call(
        _mla_kernel,
        grid=(batch, nq),
        in_specs=[
            pl.BlockSpec((B_HEADS, tb, B_QK_PAD), lambda b, i: (0, b * nq + i, 0)),
            pl.BlockSpec((B_HEADS, seq, B_QK_PAD), lambda b, i: (0, b, 0)),
            pl.BlockSpec((B_HEADS, N_META, B_QK_PAD), lambda b, i: (0, b, 0)),
            pl.BlockSpec((B_HEADS, B_V_AUG, seq), lambda b, i: (0, 0, b)),
            pl.BlockSpec((1, B_HEADS, B_V_AUG, N_META), lambda b, i: (b, 0, 0, 0)),
        ],
        out_specs=pl.BlockSpec((tb, B_WIDTH), lambda b, i: (b * nq + i, 0)),
        out_shape=jax.ShapeDtypeStruct((batch * seq, B_WIDTH), bf16),
        scratch_shapes=([pltpu.VMEM((MLA_SUBTILES, B_HEADS, B_V_DIM, tq), f32)]
                        + [pltpu.VMEM((seq, tq), f32)] * 2 + [pltpu.VMEM((N_META, tq), f32)] * 2),
        compiler_params=_cparams("arbitrary", "arbitrary"),
        name="mla",
    )(q, k, km, vt, vtm)


def _mla_meta_kernel(q_ref, kcat_ref, kcatm_ref, lat_t_ref, lat_tm_ref, wuk_ref, wuv_ref, o_ref):
    lane = lax.broadcasted_iota(jnp.int32, (N_META, B_QK_PAD), 1)
    rope_lanes = lane >= B_NOPE_DIM
    q_rows = []
    for h in range(B_HEADS):
        q = q_ref[h]
        q_lat = jnp.dot(q, wuk_ref[h], preferred_element_type=f32).astype(bf16)
        q_rows.append(jnp.concatenate([q_lat, jnp.where(rope_lanes, q, jnp.zeros_like(q))], axis=1))
    qcat = jnp.concatenate(q_rows, axis=0)
    seq = kcat_ref.shape[0]
    chunks = [(c * MLA_KEY_CHUNK, (c + 1) * MLA_KEY_CHUNK) for c in range(seq // MLA_KEY_CHUNK)]
    sm = lax.dot_general(kcatm_ref[...], qcat, _NT, preferred_element_type=f32)
    ss = [lax.dot_general(kcat_ref[lo:hi, :], qcat, _NT, preferred_element_type=f32) for lo, hi in chunks]
    m = jnp.max(sm, axis=0, keepdims=True)
    for s in ss:
        m = jnp.maximum(m, jnp.max(s, axis=0, keepdims=True))
    acc = jnp.dot(lat_tm_ref[0], jnp.exp2(sm - m).astype(bf16), preferred_element_type=f32)
    for (lo, hi), s in zip(chunks, ss):
        acc = acc + jnp.dot(lat_t_ref[:, lo:hi], jnp.exp2(s - m).astype(bf16),
                            preferred_element_type=f32)
    o_lat = (acc[:KV_LORA_RANK] * (1.0 / acc[KV_LORA_RANK:KV_LORA_RANK + 1])).T.astype(bf16)
    for h in range(B_HEADS):
        o_ref[:, h * B_V_DIM:(h + 1) * B_V_DIM] = jnp.dot(
            o_lat[h * N_META:(h + 1) * N_META], wuv_ref[h], preferred_element_type=f32).astype(o_ref.dtype)


def _mla_meta(qm, kcat, kcatm, lat_t, lat_tm, lw, layer, batch, seq):
    const = functools.partial(_layer_spec, layer)
    return pl.pallas_call(
        _mla_meta_kernel,
        grid=(batch,),
        in_specs=[
            pl.BlockSpec((B_HEADS, N_META, B_QK_PAD), lambda b: (0, b, 0)),
            pl.BlockSpec((seq, KV_LORA_RANK + B_QK_PAD), lambda b: (b, 0)),
            pl.BlockSpec((N_META, KV_LORA_RANK + B_QK_PAD), lambda b: (b, 0)),
            pl.BlockSpec((LAT_T_ROWS, seq), lambda b: (0, b)),
            pl.BlockSpec((1, LAT_T_ROWS, N_META), lambda b: (b, 0, 0)),
            const((B_HEADS, B_QK_PAD, KV_LORA_RANK)),
            const((B_HEADS, KV_LORA_RANK, B_V_DIM)),
        ],
        out_specs=pl.BlockSpec((N_META, B_WIDTH), lambda b: (b, 0)),
        out_shape=jax.ShapeDtypeStruct((batch * N_META, B_WIDTH), bf16),
        compiler_params=_cparams("arbitrary"),
        name="mla_meta",
    )(qm, kcat, kcatm, lat_t, lat_tm, lw["wuk"], lw["wuv"])


def _rope_tables(pos):
    half = B_ROPE_DIM // 2
    freqs = ROPE_THETA ** (-jnp.arange(half, dtype=jnp.float32) / half)
    lane_freq = jnp.concatenate([jnp.zeros((B_NOPE_DIM,), f32)] + [freqs] * (2 * B_ROPE_DIM // half))
    ang = pos.astype(jnp.float32)[:, None] * lane_freq[None, :]
    cos, sin = jnp.cos(ang), jnp.sin(ang)
    lane = jnp.arange(B_QK_PAD)[None, :]
    nope = lane < B_NOPE_DIM
    first_rope = lane < B_NOPE_DIM + B_ROPE_DIM
    c_mla = (B_NOPE_DIM + B_ROPE_DIM) ** -0.5 * LOG2E
    mul_q = jnp.where(nope, c_mla, jnp.where(first_rope, c_mla * cos, c_mla * sin))
    cos_k = jnp.where(nope, 0.0, cos)
    sin_k = jnp.where(nope, 0.0, sin)
    return jnp.concatenate([mul_q, cos_k, sin_k], axis=1).astype(f32)


def _rot_cols(w):
    half = w.shape[-1] // 2
    return jnp.concatenate([-w[..., half:], w[..., :half]], axis=-1)


def _stacked_weights(norm_in, w_in, norm_q_lat, w_uq, norm_kv_lat, w_ukv, norm_out_a, norm_out_b, w_out):
    depth, d, _ = w_in.shape
    sizes = (A_WIDTH, A_KV_HEADS * A_HEAD_DIM, A_KV_HEADS * A_HEAD_DIM, A_WIDTH,
             Q_LORA_RANK, KV_LORA_RANK, B_ROPE_DIM, B_WIDTH)
    offs = [sum(sizes[:i]) for i in range(len(sizes) + 1)]
    qa, ka, va, ga, cq, ckv, kr, gb = (w_in[..., offs[i]:offs[i + 1]] for i in range(len(sizes)))
    zeros = lambda *shape: jnp.zeros(shape, w_in.dtype)
    kr_rot = _rot_cols(kr)
    kr_groups = jnp.concatenate([zeros(depth, d, B_NOPE_DIM), kr, kr,
                                 zeros(depth, d, B_NOPE_DIM), kr_rot, kr_rot], axis=-1)
    w1 = jnp.concatenate([qa, cq, ckv, ka, kr_groups], axis=-1).astype(bf16)

    r = w_uq.shape[1]
    uq = w_uq.reshape(depth, r, B_HEADS, B_NOPE_DIM + B_ROPE_DIM)
    wq = jnp.concatenate([uq, _rot_cols(uq[..., B_NOPE_DIM:])], axis=-1)

    rk = w_ukv.shape[1]
    ukv = w_ukv.reshape(depth, rk, B_HEADS, B_NOPE_DIM + B_V_DIM)
    k_nope, v = ukv[..., :B_NOPE_DIM], ukv[..., B_NOPE_DIM:]
    wk = jnp.concatenate([k_nope, zeros(depth, rk, B_HEADS, B_QK_PAD - B_NOPE_DIM)], axis=-1)
    row = lambda g: g.astype(f32)[:, None, :]
    return {
        "gin": row(norm_in),
        "w1": w1,
        "wg": jnp.concatenate([ga, gb], axis=-1).astype(bf16),
        "wvat": jnp.swapaxes(va.astype(bf16), 1, 2),
        "gq": row(norm_q_lat),
        "wq": wq.reshape(depth, r, B_HEADS * B_QK_PAD).astype(bf16),
        "gkv": row(norm_kv_lat),
        "wk": wk.reshape(depth, rk, B_HEADS * B_QK_PAD).astype(bf16),
        "wvt": jnp.swapaxes(v.reshape(depth, rk, B_WIDTH).astype(bf16), 1, 2),
        "wuk": jnp.concatenate([jnp.transpose(k_nope, (0, 2, 3, 1)),
                                zeros(depth, B_HEADS, B_QK_PAD - B_NOPE_DIM, rk)], axis=2).astype(bf16),
        "wuv": jnp.transpose(v, (0, 2, 1, 3)).astype(bf16),
        "na": row(norm_out_a),
        "nb": row(norm_out_b),
        "wout": w_out.astype(bf16),
    }


def kernel(x, meta_tokens, rel_bias_table, norm_in, w_in, sink_a, norm_q_lat, w_uq, norm_kv_lat, w_ukv,
           norm_out_a, norm_out_b, w_out, norm_final):
    batch, seq, d = x.shape
    depth = w_in.shape[0]
    assert d == D_MODEL and seq % (MLA_Q_TILE * MLA_SUBTILES) == 0 and seq % ROW_TILE == 0

    h_real = x.reshape(batch * seq, d).astype(f32)
    h_meta = jnp.tile(meta_tokens.astype(f32), (batch, 1))
    tab_real = _rope_tables(N_META + jnp.arange(seq))
    tab_meta = _rope_tables(jnp.arange(batch * N_META) % N_META)
    bias_band, bias_meta, bias_mm, bias_mr = _bias_tables(rel_bias_table, seq // BLOCK)

    lw = _stacked_weights(norm_in, w_in, norm_q_lat, w_uq, norm_kv_lat, w_ukv, norm_out_a, norm_out_b, w_out)
    sinks = sink_a.astype(f32) * LOG2E
    sinks_rows = jnp.repeat(sinks.reshape(depth, A_KV_HEADS, A_GROUP), BLOCK, axis=2)[:, :, None, :]

    out = None
    for i in range(depth):
        last = i == depth - 1
        sink, sink_rows = sinks[i], sinks_rows[i]
        qa, ka, vat, q, k, vt, *lat = _proj(h_real, tab_real, lw, i, latent=not last)
        qam, kam, vatm, qm, km, vtm, *latm = _proj(h_meta, tab_meta, lw, i, latent=not last)
        vtm_b = vtm.reshape(B_HEADS, B_V_AUG, batch, N_META).transpose(2, 0, 1, 3)
        vatm_b = vatm.reshape(A_KV_HEADS, A_V_AUG, batch, N_META).transpose(2, 0, 1, 3)

        ya = _win(sink_rows, qa, ka, vat, kam, vatm_b, bias_band, bias_meta, batch, seq)
        yb = _mla(q, k, km, vt, vtm_b, batch, seq)
        if last:
            out = _out(h_real, ya, yb, lw, i, norm_final.astype(f32)[None, :])
        else:
            yam = _win_meta(sink, qam, ka, vat, kam, vatm_b, bias_mm, bias_mr, batch, seq)
            (kcat, lat_t), (kcatm, lat_tm) = lat, latm
            lat_tm_b = lat_tm.reshape(LAT_T_ROWS, batch, N_META).transpose(1, 0, 2)
            ybm = _mla_meta(qm, kcat, kcatm, lat_t, lat_tm_b, lw, i, batch, seq)
            h_real = _out(h_real, ya, yb, lw, i)
            h_meta = _out(h_meta, yam, ybm, lw, i)
    return out.reshape(batch, seq, d).astype(x.dtype)
```

```python
import functools
import math

import jax
import jax.numpy as jnp
from jax import lax
from jax.experimental import pallas as pl
from jax.experimental.pallas import tpu as pltpu

D_MODEL = 1024
N_META = 16
BLOCK = 128
WINDOW = 128
A_HEADS = 8
A_KV_HEADS = 2
A_GROUP = A_HEADS // A_KV_HEADS
A_HEAD_DIM = 64
A_WIDTH = A_HEADS * A_HEAD_DIM
B_HEADS = 8
B_NOPE_DIM = 64
B_ROPE_DIM = 32
B_V_DIM = 64
B_WIDTH = B_HEADS * B_V_DIM
B_QK_PAD = B_NOPE_DIM + 2 * B_ROPE_DIM
BF16_SUBLANES = 16
B_V_AUG = B_V_DIM + BF16_SUBLANES
A_V_AUG = A_HEAD_DIM + BF16_SUBLANES
Q_LORA_RANK = 256
KV_LORA_RANK = 128
LAT_T_ROWS = KV_LORA_RANK + BF16_SUBLANES
N_BUCKETS = 32
MAX_DISTANCE = 128
ROPE_THETA = 10000.0
EPS = 1e-6

LOG2E = math.log2(math.e)
NEG = -1e30
ROW_TILE = 1024
MLA_Q_TILE = 256
MLA_SUBTILES = 8
MLA_STEPS_PER_ITER = 8
MLA_KEY_CHUNK = 1024
WIN_BLOCKS = 16
WIN_AHEAD = 2
BIAS_ROW_CHUNK = 32
VMEM_LIMIT_BYTES = 56 * 1024 * 1024

_C_QA = 0
_C_CQ = _C_QA + A_WIDTH
_C_CKV = _C_CQ + Q_LORA_RANK
_C_KA = _C_CKV + KV_LORA_RANK
_C_KRA = _C_KA + A_KV_HEADS * A_HEAD_DIM
_C_KRB = _C_KRA + B_QK_PAD
_C_END = _C_KRB + B_QK_PAD

_NT = (((1,), (1,)), ((), ()))

bf16 = jnp.bfloat16
f32 = jnp.float32


def _cparams(*sem, flags=None):
    return pltpu.CompilerParams(dimension_semantics=sem, vmem_limit_bytes=VMEM_LIMIT_BYTES, flags=flags)


def _rms(x, gain):
    return x * lax.rsqrt(jnp.mean(x * x, axis=-1, keepdims=True) + EPS) * gain


def _silu(x):
    return x / (1.0 + jnp.exp(-x))


def _proj_kernel(h_ref, tab_ref, gin_ref, w1_ref, wvat_ref, gq_ref, wq_ref, gkv_ref, wk_ref, wvt_ref,
                 qa_ref, ka_ref, vat_ref, q_ref, k_ref, vt_ref, *latent_refs):
    u = _rms(h_ref[...], gin_ref[...]).astype(bf16)
    t = u.shape[0]

    def mm(lo, hi):
        return jnp.dot(u, w1_ref[:, lo:hi], preferred_element_type=f32)

    qa = mm(_C_QA, _C_CQ) * (A_HEAD_DIM ** -0.5 * LOG2E)
    for h in range(A_HEADS):
        qa_ref[h] = qa[:, h * A_HEAD_DIM:(h + 1) * A_HEAD_DIM].astype(bf16)
    vat = lax.dot_general(wvat_ref[...], u, _NT, preferred_element_type=f32)
    vat_ref[:, :A_HEAD_DIM, :] = vat.reshape(A_KV_HEADS, A_HEAD_DIM, t).astype(bf16)
    vat_ref[:, A_HEAD_DIM:, :] = jnp.ones((A_KV_HEADS, A_V_AUG - A_HEAD_DIM, t), bf16)
    mixed = mm(_C_CQ, _C_END)
    cq = mixed[:, :_C_CKV - _C_CQ]
    ckv = mixed[:, _C_CKV - _C_CQ:_C_KA - _C_CQ]
    ka = mixed[:, _C_KA - _C_CQ:_C_KRA - _C_CQ]
    kra = mixed[:, _C_KRA - _C_CQ:_C_KRB - _C_CQ]
    krb = mixed[:, _C_KRB - _C_CQ:]
    for j in range(A_KV_HEADS):
        ka_ref[j] = ka[:, j * A_HEAD_DIM:(j + 1) * A_HEAD_DIM].astype(bf16)
    tab_rows = pl.ds(pl.multiple_of((pl.program_id(0) % (tab_ref.shape[0] // t)) * t, t), t)
    mul_q = (tab_ref[tab_rows, 0 * B_QK_PAD:1 * B_QK_PAD],
             tab_ref[tab_rows, 1 * B_QK_PAD:2 * B_QK_PAD])
    cos_k = tab_ref[tab_rows, 2 * B_QK_PAD:3 * B_QK_PAD]
    sin_k = tab_ref[tab_rows, 3 * B_QK_PAD:4 * B_QK_PAD]

    cqn = _rms(cq, gq_ref[...]).astype(bf16)
    qp = jnp.dot(cqn, wq_ref[...], preferred_element_type=f32)
    for h in range(B_HEADS):
        q_ref[h] = (qp[:, h * B_QK_PAD:(h + 1) * B_QK_PAD] * mul_q[h % 2]).astype(bf16)

    ckvn_f32 = _rms(ckv, gkv_ref[...])
    ckvn = ckvn_f32.astype(bf16)
    k_nope = jnp.dot(ckvn, wk_ref[...], preferred_element_type=f32)
    k_rope = kra * cos_k + krb * sin_k
    lane = lax.broadcasted_iota(jnp.int32, k_rope.shape, 1)
    for h in range(B_HEADS):
        pair = k_nope[:, (h // 2) * B_QK_PAD:(h // 2 + 1) * B_QK_PAD]
        own_nope = (lane < B_NOPE_DIM) if h % 2 == 0 else (lane >= B_NOPE_DIM)
        k_ref[h] = jnp.where(own_nope, pair, k_rope).astype(bf16)
    vt = lax.dot_general(wvt_ref[...], ckvn, _NT, preferred_element_type=f32)
    vt_ref[:, :B_V_DIM, :] = vt.reshape(B_HEADS, B_V_DIM, t).astype(bf16)
    vt_ref[:, B_V_DIM:, :] = jnp.ones((B_HEADS, B_V_AUG - B_V_DIM, t), bf16)
    if latent_refs:
        kcat_ref, lat_t_ref = latent_refs
        kcat_ref[:, :KV_LORA_RANK] = ckvn
        kcat_ref[:, KV_LORA_RANK:] = k_rope.astype(bf16)
        lat_t_ref[:KV_LORA_RANK, :] = ckvn_f32.T.astype(bf16)
        lat_t_ref[KV_LORA_RANK:, :] = jnp.ones((LAT_T_ROWS - KV_LORA_RANK, t), bf16)


def _layer_spec(layer, shape):
    return pl.BlockSpec((None,) + tuple(shape), lambda i: (layer,) + (0,) * len(shape))


def _proj(h, tab, lw, layer, latent):
    rows = h.shape[0]
    t = min(ROW_TILE, rows)
    steps = rows // t
    assert tab.shape[0] % t == 0
    const = functools.partial(_layer_spec, layer)
    latent_specs = [pl.BlockSpec((t, KV_LORA_RANK + B_QK_PAD), lambda i: (i, 0)),
                    pl.BlockSpec((LAT_T_ROWS, t), lambda i: (0, i))] if latent else []
    latent_shapes = [jax.ShapeDtypeStruct((rows, KV_LORA_RANK + B_QK_PAD), bf16),
                     jax.ShapeDtypeStruct((LAT_T_ROWS, rows), bf16)] if latent else []
    return pl.pallas_call(
        _proj_kernel,
        grid=(steps,),
        in_specs=[
            pl.BlockSpec((t, D_MODEL), lambda i: (i, 0)),
            pl.BlockSpec(tab.shape, lambda i: (0, 0)),
            const((1, D_MODEL)),
            const((D_MODEL, _C_END)),
            const((A_KV_HEADS * A_HEAD_DIM, D_MODEL)),
            const((1, Q_LORA_RANK)),
            const((Q_LORA_RANK, B_HEADS * B_QK_PAD)),
            const((1, KV_LORA_RANK)),
            const((KV_LORA_RANK, B_HEADS * B_NOPE_DIM)),
            const((B_WIDTH, KV_LORA_RANK)),
        ],
        out_specs=[
            pl.BlockSpec((A_HEADS, t, A_HEAD_DIM), lambda i: (0, i, 0)),
            pl.BlockSpec((A_KV_HEADS, t, A_HEAD_DIM), lambda i: (0, i, 0)),
            pl.BlockSpec((A_KV_HEADS, A_V_AUG, t), lambda i: (0, 0, i)),
            pl.BlockSpec((B_HEADS, t, B_QK_PAD), lambda i: (0, i, 0)),
            pl.BlockSpec((B_HEADS, t, B_QK_PAD), lambda i: (0, i, 0)),
            pl.BlockSpec((B_HEADS, B_V_AUG, t), lambda i: (0, 0, i)),
        ] + latent_specs,
        out_shape=[
            jax.ShapeDtypeStruct((A_HEADS, rows, A_HEAD_DIM), bf16),
            jax.ShapeDtypeStruct((A_KV_HEADS, rows, A_HEAD_DIM), bf16),
            jax.ShapeDtypeStruct((A_KV_HEADS, A_V_AUG, rows), bf16),
            jax.ShapeDtypeStruct((B_HEADS, rows, B_QK_PAD), bf16),
            jax.ShapeDtypeStruct((B_HEADS, rows, B_QK_PAD), bf16),
            jax.ShapeDtypeStruct((B_HEADS, B_V_AUG, rows), bf16),
        ] + latent_shapes,
        compiler_params=_cparams("arbitrary"),
        name="proj",
    )(h, tab, lw["gin"], lw["w1"], lw["wvat"], lw["gq"], lw["wq"], lw["gkv"], lw["wk"], lw["wvt"])


def _out_kernel(h_ref, ya_ref, yb_ref, gin_ref, wg_ref, na_ref, nb_ref, w_ref, *rest, final):
    if final:
        nf_ref, o_ref = rest
    else:
        (o_ref,) = rest
    hn = h_ref[...]
    u = _rms(hn, gin_ref[...]).astype(bf16)
    ga = _silu(jnp.dot(u, wg_ref[:, :A_WIDTH], preferred_element_type=f32))
    gb = _silu(jnp.dot(u, wg_ref[:, A_WIDTH:], preferred_element_type=f32))
    ya = _rms(ya_ref[...].astype(f32), na_ref[...]) * ga
    yb = _rms(yb_ref[...].astype(f32), nb_ref[...]) * gb
    hn = hn + jnp.dot(ya.astype(bf16), w_ref[:A_WIDTH, :], preferred_element_type=f32)
    hn = hn + jnp.dot(yb.astype(bf16), w_ref[A_WIDTH:, :], preferred_element_type=f32)
    if final:
        hn = _rms(hn, nf_ref[...])
    o_ref[...] = hn


def _out(h, ya, yb, lw, layer, norm_final=None):
    rows = h.shape[0]
    t = min(ROW_TILE, rows)
    row = lambda width: pl.BlockSpec((t, width), lambda i: (i, 0))
    const = functools.partial(_layer_spec, layer)
    final = norm_final is not None
    in_specs = [row(D_MODEL), row(A_WIDTH), row(B_WIDTH),
                const((1, D_MODEL)), const((D_MODEL, A_WIDTH + B_WIDTH)),
                const((1, A_WIDTH)), const((1, B_WIDTH)), const((A_WIDTH + B_WIDTH, D_MODEL))]
    args = [h, ya, yb, lw["gin"], lw["wg"], lw["na"], lw["nb"], lw["wout"]]
    if final:
        in_specs.append(pl.BlockSpec((1, D_MODEL), lambda i: (0, 0)))
        args.append(norm_final)
    return pl.pallas_call(
        functools.partial(_out_kernel, final=final),
        grid=(rows // t,),
        in_specs=in_specs,
        out_specs=row(D_MODEL),
        out_shape=jax.ShapeDtypeStruct((rows, D_MODEL), f32),
        compiler_params=_cparams("arbitrary"),
        name="outproj",
    )(*args)


def _bias_kernel(tab_ref, idx_ref, o_ref):
    rows = idx_ref.shape[0]
    chunk = math.gcd(rows, BIAS_ROW_CHUNK)

    def body(r, carry):
        sl = pl.ds(pl.multiple_of(r * chunk, chunk), chunk)
        idx = idx_ref[sl, :]
        accs = [jnp.where(idx < 0, NEG, 0.0).astype(f32)] * A_HEADS
        for b in range(N_BUCKETS):
            hit = idx == b
            accs = [jnp.where(hit, tab_ref[b * A_HEADS + h], accs[h]) for h in range(A_HEADS)]
        for h in range(A_HEADS):
            o_ref[h, sl, :] = accs[h]
        return carry

    lax.fori_loop(0, rows // chunk, body, 0)


def _bias_lookup(table_flat, idx):
    return pl.pallas_call(
        _bias_kernel,
        in_specs=[pl.BlockSpec(memory_space=pltpu.SMEM), pl.BlockSpec(idx.shape, lambda: (0, 0))],
        out_specs=pl.BlockSpec((A_HEADS,) + idx.shape, lambda: (0, 0, 0)),
        out_shape=jax.ShapeDtypeStruct((A_HEADS,) + idx.shape, f32),
        compiler_params=pltpu.CompilerParams(vmem_limit_bytes=VMEM_LIMIT_BYTES),
        name="relbias",
    )(table_flat, idx)


def _t5_bucket(rel):
    nb = N_BUCKETS // 2
    max_exact = nb // 2
    ret = jnp.where(rel > 0, nb, 0)
    n = jnp.abs(rel)
    nf = jnp.maximum(n, 1).astype(jnp.float32)
    large = max_exact + (jnp.log(nf / max_exact) / math.log(MAX_DISTANCE / max_exact)
                         * (nb - max_exact)).astype(jnp.int32)
    large = jnp.minimum(large, nb - 1)
    bucket = ret + jnp.where(n < max_exact, n, large)
    return jnp.bitwise_and(bucket, N_BUCKETS - 1)


def _bias_tables(rel_bias_table, nblk):
    table_flat = (rel_bias_table.astype(f32) * LOG2E).reshape(-1)
    i = jnp.arange(BLOCK, dtype=jnp.int32)[:, None]
    j = jnp.arange(3 * BLOCK, dtype=jnp.int32)[None, :]
    rel = j - i - BLOCK
    band = jnp.where(jnp.abs(rel) <= WINDOW, _t5_bucket(rel), -1)
    interior = (_bias_lookup(table_flat, band.T)
                .reshape(A_KV_HEADS, A_GROUP, 3 * BLOCK, BLOCK)
                .transpose(0, 2, 1, 3).reshape(A_KV_HEADS, 3 * BLOCK, A_GROUP * BLOCK))
    key = jnp.arange(3 * BLOCK, dtype=jnp.int32)[None, :, None]
    first = jnp.where(key < BLOCK, NEG, interior)
    last = jnp.where(key >= 2 * BLOCK, NEG, interior)
    bias_band = jnp.stack([first, interior, last])

    n = jnp.arange(nblk, dtype=jnp.int32)[:, None, None]
    k = jnp.arange(N_META, dtype=jnp.int32)[None, :, None]
    qi = jnp.arange(BLOCK, dtype=jnp.int32)[None, None, :]
    rel_m = k - (N_META + n * BLOCK + qi)
    idx_m = _t5_bucket(rel_m).reshape(nblk * N_META, BLOCK)
    bias_meta = (_bias_lookup(table_flat, idx_m)
                 .reshape(A_KV_HEADS, A_GROUP, nblk, N_META, BLOCK)
                 .transpose(2, 0, 3, 1, 4).reshape(nblk, A_KV_HEADS, N_META, A_GROUP * BLOCK))

    qp = jnp.arange(N_META, dtype=jnp.int32)[:, None]
    kp = jnp.arange(N_META + BLOCK, dtype=jnp.int32)[None, :]
    rel_q = kp - qp
    idx_q = jnp.where(jnp.abs(rel_q) <= WINDOW, _t5_bucket(rel_q), -1)
    bias_q = _bias_lookup(table_flat, idx_q)
    return bias_band, bias_meta, bias_q[:, :, :N_META], bias_q[:, :, N_META:]


def _win_kernel(sink_ref, q_ref, kp_ref, kc_ref, kn_ref, vp_ref, vc_ref, vn_ref, kam_ref, vam_ref,
                bias_ref, bm_ref, o_ref):
    w = WIN_BLOCKS
    i, ntile = pl.program_id(1), pl.num_programs(1)

    def qrows(blk):
        return slice(blk * BLOCK, (blk + 1) * BLOCK)

    def band(blk, prev_ref, cur_ref, next_ref, kvh, axis):
        take = lambda ref, lo, hi: ref[kvh, lo:hi, :] if axis == 0 else ref[kvh, :, lo:hi]
        parts = [take(cur_ref, max(blk - 1, 0) * BLOCK, min(blk + 2, w) * BLOCK)]
        if blk == 0:
            parts.insert(0, take(prev_ref, (w - 1) * BLOCK, w * BLOCK))
        if blk == w - 1:
            parts.append(take(next_ref, 0, BLOCK))
        return jnp.concatenate(parts, axis=axis)

    def variant(blk):
        v = 1
        if blk == 0:
            v = jnp.where(i == 0, 0, v)
        if blk == w - 1:
            v = jnp.where(i == ntile - 1, 2, v)
        return v

    def scores(blk, kvh):
        kb = jnp.concatenate([band(blk, kp_ref, kc_ref, kn_ref, kvh, 0), kam_ref[kvh]], axis=0)
        q4 = q_ref[A_GROUP * kvh:A_GROUP * (kvh + 1), qrows(blk), :].reshape(A_GROUP * BLOCK, A_HEAD_DIM)
        bias = jnp.concatenate([bias_ref[variant(blk), kvh], bm_ref[blk, kvh]], axis=0)
        s = lax.dot_general(kb, q4, _NT, preferred_element_type=f32) + bias
        m = jnp.maximum(jnp.max(s, axis=0, keepdims=True), sink_ref[kvh])
        return s, m

    def attend(blk, kvh, s, m):
        vb = jnp.concatenate([band(blk, vp_ref, vc_ref, vn_ref, kvh, 1), vam_ref[0, kvh]], axis=1)
        p = jnp.exp2(s - m).astype(bf16)
        acc = jnp.dot(vb, p, preferred_element_type=f32)
        den = acc[A_HEAD_DIM:A_HEAD_DIM + 1] + jnp.exp2(sink_ref[kvh] - m)
        return acc[:A_HEAD_DIM] * (1.0 / den)

    units = [(blk, kvh) for blk in range(w) for kvh in range(A_KV_HEADS)]
    outs = {}
    pending = [scores(*units[u]) for u in range(WIN_AHEAD)]
    for u, unit in enumerate(units):
        if u + WIN_AHEAD < len(units):
            pending.append(scores(*units[u + WIN_AHEAD]))
        outs[unit] = attend(*unit, *pending.pop(0))
        blk, kvh = unit
        if kvh == A_KV_HEADS - 1:
            ot = jnp.concatenate([outs[(blk, j)][:, g * BLOCK:(g + 1) * BLOCK]
                                  for j in range(A_KV_HEADS) for g in range(A_GROUP)], axis=0)
            o_ref[qrows(blk), :] = ot.T.astype(o_ref.dtype)


def _win(sink_rows, qa, ka, vat, kam, vatm, bias_band, bias_meta, batch, seq):
    tb = WIN_BLOCKS * BLOCK
    ntile = seq // tb
    prev_ = lambda b, i: b * ntile + jnp.maximum(i - 1, 0)
    cur_ = lambda b, i: b * ntile + i
    next_ = lambda b, i: b * ntile + jnp.minimum(i + 1, ntile - 1)
    kspec = lambda f: pl.BlockSpec((A_KV_HEADS, tb, A_HEAD_DIM), lambda b, i: (0, f(b, i), 0))
    vspec = lambda f: pl.BlockSpec((A_KV_HEADS, A_V_AUG, tb), lambda b, i: (0, 0, f(b, i)))
    return pl.pallas_call(
        _win_kernel,
        grid=(batch, ntile),
        in_specs=[
            pl.BlockSpec((A_KV_HEADS, 1, A_GROUP * BLOCK), lambda b, i: (0, 0, 0)),
            pl.BlockSpec((A_HEADS, tb, A_HEAD_DIM), lambda b, i: (0, cur_(b, i), 0)),
            kspec(prev_), kspec(cur_), kspec(next_),
            vspec(prev_), vspec(cur_), vspec(next_),
            pl.BlockSpec((A_KV_HEADS, N_META, A_HEAD_DIM), lambda b, i: (0, b, 0)),
            pl.BlockSpec((1, A_KV_HEADS, A_V_AUG, N_META), lambda b, i: (b, 0, 0, 0)),
            pl.BlockSpec((3, A_KV_HEADS, 3 * BLOCK, A_GROUP * BLOCK), lambda b, i: (0, 0, 0, 0)),
            pl.BlockSpec((WIN_BLOCKS, A_KV_HEADS, N_META, A_GROUP * BLOCK), lambda b, i: (i, 0, 0, 0)),
        ],
        out_specs=pl.BlockSpec((tb, A_WIDTH), lambda b, i: (cur_(b, i), 0)),
        out_shape=jax.ShapeDtypeStruct((batch * seq, A_WIDTH), bf16),
        compiler_params=_cparams("arbitrary", "arbitrary"),
        name="win",
    )(sink_rows, qa, ka, ka, ka, vat, vat, vat, kam, vatm, bias_band, bias_meta)


def _win_meta_kernel(sink_ref, q_ref, ka_ref, vat_ref, kam_ref, vatm_ref, biasm_ref, biasr_ref, o_ref):
    for kvh in range(A_KV_HEADS):
        q4 = q_ref[A_GROUP * kvh:A_GROUP * (kvh + 1)].reshape(A_GROUP * N_META, A_HEAD_DIM)
        sm = lax.dot_general(q4, kam_ref[kvh], _NT, preferred_element_type=f32)
        sr = lax.dot_general(q4, ka_ref[kvh], _NT, preferred_element_type=f32)
        for g in range(A_GROUP):
            h = A_GROUP * kvh + g
            rows = slice(g * N_META, (g + 1) * N_META)
            smg = sm[rows] + biasm_ref[h]
            srg = sr[rows] + biasr_ref[h]
            sink = sink_ref[h]
            m = jnp.maximum(jnp.maximum(jnp.max(smg, axis=-1, keepdims=True),
                                        jnp.max(srg, axis=-1, keepdims=True)), sink)
            pm = jnp.exp2(smg - m).astype(bf16)
            pr = jnp.exp2(srg - m).astype(bf16)
            o = (lax.dot_general(pm, vatm_ref[0, kvh], _NT, preferred_element_type=f32)
                 + lax.dot_general(pr, vat_ref[kvh], _NT, preferred_element_type=f32))
            den = o[:, A_HEAD_DIM:A_HEAD_DIM + 1] + jnp.exp2(sink - m)
            o_ref[:, h * A_HEAD_DIM:(h + 1) * A_HEAD_DIM] = (o[:, :A_HEAD_DIM] * (1.0 / den)).astype(o_ref.dtype)


def _win_meta(sink, qam, ka, vat, kam, vatm, bias_mm, bias_mr, batch, seq):
    nblk = seq // BLOCK
    const3 = lambda shape: pl.BlockSpec(shape, lambda b: (0, 0, 0))
    return pl.pallas_call(
        _win_meta_kernel,
        grid=(batch,),
        in_specs=[
            pl.BlockSpec(memory_space=pltpu.SMEM),
            pl.BlockSpec((A_HEADS, N_META, A_HEAD_DIM), lambda b: (0, b, 0)),
            pl.BlockSpec((A_KV_HEADS, BLOCK, A_HEAD_DIM), lambda b: (0, b * nblk, 0)),
            pl.BlockSpec((A_KV_HEADS, A_V_AUG, BLOCK), lambda b: (0, 0, b * nblk)),
            pl.BlockSpec((A_KV_HEADS, N_META, A_HEAD_DIM), lambda b: (0, b, 0)),
            pl.BlockSpec((1, A_KV_HEADS, A_V_AUG, N_META), lambda b: (b, 0, 0, 0)),
            const3((A_HEADS, N_META, N_META)),
            const3((A_HEADS, N_META, BLOCK)),
        ],
        out_specs=pl.BlockSpec((N_META, A_WIDTH), lambda b: (b, 0)),
        out_shape=jax.ShapeDtypeStruct((batch * N_META, A_WIDTH), bf16),
        compiler_params=_cparams("arbitrary"),
        name="win_meta",
    )(sink, qam, ka, vat, kam, vatm, bias_mm, bias_mr)


def _mla_kernel(q_ref, k_ref, km_ref, vt_ref, vtm_ref, o_ref, acc_ref, s0_ref, s1_ref, sm0_ref, sm1_ref):
    s_bufs, sm_bufs = (s0_ref, s1_ref), (sm0_ref, sm1_ref)
    nsub, _, _, tq = acc_ref.shape
    units = nsub * B_HEADS
    seq = k_ref.shape[1]
    chunks = [(c * MLA_KEY_CHUNK, (c + 1) * MLA_KEY_CHUNK) for c in range(seq // MLA_KEY_CHUNK)]

    def unit(u):
        return u % B_HEADS, u // B_HEADS

    def step(t, slot, m, *, score_next=True, attend_cur=True):
        if score_next:
            hn, subn = unit(t + 1)
            q = q_ref[hn, pl.ds(pl.multiple_of(subn * tq, tq), tq), :]
            sm = lax.dot_general(km_ref[hn], q, _NT, preferred_element_type=f32)
            sm_bufs[1 - slot][...] = sm
            m_next = jnp.max(sm, axis=0, keepdims=True)
        if attend_cur:
            h, sub = unit(t)
            pm = jnp.exp2(sm_bufs[slot][...] - m).astype(bf16)
            acc = jnp.dot(vtm_ref[0, h], pm, preferred_element_type=f32)
        for lo, hi in chunks:
            if score_next:
                s = lax.dot_general(k_ref[hn, lo:hi, :], q, _NT, preferred_element_type=f32)
                s_bufs[1 - slot][lo:hi, :] = s
                m_next = jnp.maximum(m_next, jnp.max(s, axis=0, keepdims=True))
            if attend_cur:
                p = jnp.exp2(s_bufs[slot][lo:hi, :] - m).astype(bf16)
                acc = acc + jnp.dot(vt_ref[h, :, lo:hi], p, preferred_element_type=f32)
        if attend_cur:
            acc_ref[sub, h] = acc[:B_V_DIM] * (1.0 / acc[B_V_DIM:B_V_DIM + 1])
        return m_next if score_next else None

    def steps(j, m):
        for i in range(MLA_STEPS_PER_ITER):
            m = step(MLA_STEPS_PER_ITER * j + i, i % 2, m)
        return m

    assert MLA_STEPS_PER_ITER % 2 == 0 and units % 2 == 0
    iters = (units - 1) // MLA_STEPS_PER_ITER
    m = step(-1, 1, None, attend_cur=False)
    m = lax.fori_loop(0, iters, steps, m)
    for t in range(iters * MLA_STEPS_PER_ITER, units - 1):
        m = step(t, t % 2, m)
    step(units - 1, 1, m, score_next=False)
    for sub in range(nsub):
        o_ref[sub * tq:(sub + 1) * tq, :] = acc_ref[sub].reshape(B_WIDTH, tq).T.astype(o_ref.dtype)


def _mla(q, k, km, vt, vtm, batch, seq):
    tq = MLA_Q_TILE
    tb = MLA_Q_TILE * MLA_SUBTILES
    nq = seq // tb
    return pl.pallas_call(
        _mla_kernel,
        grid=(batch, nq),
        in_specs=[
            pl.BlockSpec((B_HEADS, tb, B_QK_PAD), lambda b, i: (0, b * nq + i, 0)),
            pl.BlockSpec((B_HEADS, seq, B_QK_PAD), lambda b, i: (0, b, 0)),
            pl.BlockSpec((B_HEADS, N_META, B_QK_PAD), lambda b, i: (0, b, 0)),
            pl.BlockSpec((B_HEADS, B_V_AUG, seq), lambda b, i: (0, 0, b)),
            pl.BlockSpec((1, B_HEADS, B_V_AUG, N_META), lambda b, i: (b, 0, 0, 0)),
        ],
        out_specs=pl.BlockSpec((tb, B_WIDTH), lambda b, i: (b * nq + i, 0)),
        out_shape=jax.ShapeDtypeStruct((batch * seq, B_WIDTH), bf16),
        scratch_shapes=([pltpu.VMEM((MLA_SUBTILES, B_HEADS, B_V_DIM, tq), f32)]
                        + [pltpu.VMEM((seq, tq), f32)] * 2 + [pltpu.VMEM((N_META, tq), f32)] * 2),
        compiler_params=_cparams("arbitrary", "arbitrary"),
        name="mla",
    )(q, k, km, vt, vtm)


def _mla_meta_kernel(q_ref, kcat_ref, kcatm_ref, lat_t_ref, lat_tm_ref, wuk_ref, wuv_ref, o_ref):
    lane = lax.broadcasted_iota(jnp.int32, (N_META, B_QK_PAD), 1)
    q_rows = []
    for h in range(B_HEADS):
        q = q_ref[h]
        rope_lanes = (lane >= B_NOPE_DIM) if h % 2 == 0 else (lane < B_NOPE_DIM)
        q_lat = jnp.dot(q, wuk_ref[h], preferred_element_type=f32).astype(bf16)
        q_rows.append(jnp.concatenate([q_lat, jnp.where(rope_lanes, q, jnp.zeros_like(q))], axis=1))
    qcat = jnp.concatenate(q_rows, axis=0)
    seq = kcat_ref.shape[0]
    chunks = [(c * MLA_KEY_CHUNK, (c + 1) * MLA_KEY_CHUNK) for c in range(seq // MLA_KEY_CHUNK)]
    sm = lax.dot_general(kcatm_ref[...], qcat, _NT, preferred_element_type=f32)
    ss = [lax.dot_general(kcat_ref[lo:hi, :], qcat, _NT, preferred_element_type=f32) for lo, hi in chunks]
    m = jnp.max(sm, axis=0, keepdims=True)
    for s in ss:
        m = jnp.maximum(m, jnp.max(s, axis=0, keepdims=True))
    acc = jnp.dot(lat_tm_ref[0], jnp.exp2(sm - m).astype(bf16), preferred_element_type=f32)
    for (lo, hi), s in zip(chunks, ss):
        acc = acc + jnp.dot(lat_t_ref[:, lo:hi], jnp.exp2(s - m).astype(bf16),
                            preferred_element_type=f32)
    o_lat = (acc[:KV_LORA_RANK] * (1.0 / acc[KV_LORA_RANK:KV_LORA_RANK + 1])).T.astype(bf16)
    for h in range(B_HEADS):
        o_ref[:, h * B_V_DIM:(h + 1) * B_V_DIM] = jnp.dot(
            o_lat[h * N_META:(h + 1) * N_META], wuv_ref[h], preferred_element_type=f32).astype(o_ref.dtype)


def _mla_meta(qm, kcat, kcatm, lat_t, lat_tm, lw, layer, batch, seq):
    const = functools.partial(_layer_spec, layer)
    return pl.pallas_call(
        _mla_meta_kernel,
        grid=(batch,),
        in_specs=[
            pl.BlockSpec((B_HEADS, N_META, B_QK_PAD), lambda b: (0, b, 0)),
            pl.BlockSpec((seq, KV_LORA_RANK + B_QK_PAD), lambda b: (b, 0)),
            pl.BlockSpec((N_META, KV_LORA_RANK + B_QK_PAD), lambda b: (b, 0)),
            pl.BlockSpec((LAT_T_ROWS, seq), lambda b: (0, b)),
            pl.BlockSpec((1, LAT_T_ROWS, N_META), lambda b: (b, 0, 0)),
            const((B_HEADS, B_QK_PAD, KV_LORA_RANK)),
            const((B_HEADS, KV_LORA_RANK, B_V_DIM)),
        ],
        out_specs=pl.BlockSpec((N_META, B_WIDTH), lambda b: (b, 0)),
        out_shape=jax.ShapeDtypeStruct((batch * N_META, B_WIDTH), bf16),
        compiler_params=_cparams("arbitrary"),
        name="mla_meta",
    )(qm, kcat, kcatm, lat_t, lat_tm, lw["wuk"], lw["wuv"])


def _rope_tables(pos):
    half = B_ROPE_DIM // 2
    freqs = ROPE_THETA ** (-jnp.arange(half, dtype=jnp.float32) / half)
    lane_freq = jnp.concatenate([freqs] * (B_QK_PAD // half))
    ang = pos.astype(jnp.float32)[:, None] * lane_freq[None, :]
    cos, sin = jnp.cos(ang), jnp.sin(ang)
    lane = jnp.arange(B_QK_PAD)[None, :]
    c_mla = (B_NOPE_DIM + B_ROPE_DIM) ** -0.5 * LOG2E
    rope_pair = jnp.where(lane % (2 * B_ROPE_DIM) < B_ROPE_DIM, c_mla * cos, c_mla * sin)
    mul_q_even = jnp.where(lane < B_NOPE_DIM, c_mla, rope_pair)
    mul_q_odd = jnp.where(lane < 2 * B_ROPE_DIM, rope_pair, c_mla)
    return jnp.concatenate([mul_q_even, mul_q_odd, cos, sin], axis=1).astype(f32)


def _rot_cols(w):
    half = w.shape[-1] // 2
    return jnp.concatenate([-w[..., half:], w[..., :half]], axis=-1)


def _stacked_weights(norm_in, w_in, norm_q_lat, w_uq, norm_kv_lat, w_ukv, norm_out_a, norm_out_b, w_out):
    depth, d, _ = w_in.shape
    sizes = (A_WIDTH, A_KV_HEADS * A_HEAD_DIM, A_KV_HEADS * A_HEAD_DIM, A_WIDTH,
             Q_LORA_RANK, KV_LORA_RANK, B_ROPE_DIM, B_WIDTH)
    offs = [sum(sizes[:i]) for i in range(len(sizes) + 1)]
    qa, ka, va, ga, cq, ckv, kr, gb = (w_in[..., offs[i]:offs[i + 1]] for i in range(len(sizes)))
    zeros = lambda *shape: jnp.zeros(shape, w_in.dtype)
    reps = B_QK_PAD // B_ROPE_DIM
    kr_groups = jnp.concatenate([kr] * reps + [_rot_cols(kr)] * reps, axis=-1)
    w1 = jnp.concatenate([qa, cq, ckv, ka, kr_groups], axis=-1).astype(bf16)

    r = w_uq.shape[1]
    uq = w_uq.reshape(depth, r, B_HEADS // 2, 2, B_NOPE_DIM + B_ROPE_DIM)
    nope, rope = uq[..., :B_NOPE_DIM], uq[..., B_NOPE_DIM:]
    rope2 = jnp.concatenate([rope, _rot_cols(rope)], axis=-1)
    wq = jnp.stack([jnp.concatenate([nope[..., 0, :], rope2[..., 0, :]], axis=-1),
                    jnp.concatenate([rope2[..., 1, :], nope[..., 1, :]], axis=-1)], axis=3)

    rk = w_ukv.shape[1]
    ukv = w_ukv.reshape(depth, rk, B_HEADS, B_NOPE_DIM + B_V_DIM)
    k_nope, v = ukv[..., :B_NOPE_DIM], ukv[..., B_NOPE_DIM:]
    uk_t = jnp.transpose(k_nope, (0, 2, 3, 1)).reshape(depth, B_HEADS // 2, 2, B_NOPE_DIM, rk)
    no_lanes = zeros(depth, B_HEADS // 2, B_QK_PAD - B_NOPE_DIM, rk)
    wuk = jnp.stack([jnp.concatenate([uk_t[:, :, 0], no_lanes], axis=2),
                     jnp.concatenate([no_lanes, uk_t[:, :, 1]], axis=2)], axis=2)
    row = lambda g: g.astype(f32)[:, None, :]
    return {
        "gin": row(norm_in),
        "w1": w1,
        "wg": jnp.concatenate([ga, gb], axis=-1).astype(bf16),
        "wvat": jnp.swapaxes(va.astype(bf16), 1, 2),
        "gq": row(norm_q_lat),
        "wq": wq.reshape(depth, r, B_HEADS * B_QK_PAD).astype(bf16),
        "gkv": row(norm_kv_lat),
        "wk": k_nope.reshape(depth, rk, B_HEADS * B_NOPE_DIM).astype(bf16),
        "wvt": jnp.swapaxes(v.reshape(depth, rk, B_WIDTH).astype(bf16), 1, 2),
        "wuk": wuk.reshape(depth, B_HEADS, B_QK_PAD, rk).astype(bf16),
        "wuv": jnp.transpose(v, (0, 2, 1, 3)).astype(bf16),
        "na": row(norm_out_a),
        "nb": row(norm_out_b),
        "wout": w_out.astype(bf16),
    }


def kernel(x, meta_tokens, rel_bias_table, norm_in, w_in, sink_a, norm_q_lat, w_uq, norm_kv_lat, w_ukv,
           norm_out_a, norm_out_b, w_out, norm_final):
    batch, seq, d = x.shape
    depth = w_in.shape[0]
    assert d == D_MODEL and seq % (MLA_Q_TILE * MLA_SUBTILES) == 0 and seq % ROW_TILE == 0
    assert B_QK_PAD == 2 * B_NOPE_DIM and B_HEADS % 2 == 0

    h_real = x.reshape(batch * seq, d).astype(f32)
    h_meta = jnp.tile(meta_tokens.astype(f32), (batch, 1))
    tab_real = _rope_tables(N_META + jnp.arange(seq))
    tab_meta = _rope_tables(jnp.arange(batch * N_META) % N_META)
    bias_band, bias_meta, bias_mm, bias_mr = _bias_tables(rel_bias_table, seq // BLOCK)

    lw = _stacked_weights(norm_in, w_in, norm_q_lat, w_uq, norm_kv_lat, w_ukv, norm_out_a, norm_out_b, w_out)
    sinks = sink_a.astype(f32) * LOG2E
    sinks_rows = jnp.repeat(sinks.reshape(depth, A_KV_HEADS, A_GROUP), BLOCK, axis=2)[:, :, None, :]

    out = None
    for i in range(depth):
        last = i == depth - 1
        sink, sink_rows = sinks[i], sinks_rows[i]
        qa, ka, vat, q, k, vt, *lat = _proj(h_real, tab_real, lw, i, latent=not last)
        qam, kam, vatm, qm, km, vtm, *latm = _proj(h_meta, tab_meta, lw, i, latent=not last)
        vtm_b = vtm.reshape(B_HEADS, B_V_AUG, batch, N_META).transpose(2, 0, 1, 3)
        vatm_b = vatm.reshape(A_KV_HEADS, A_V_AUG, batch, N_META).transpose(2, 0, 1, 3)

        ya = _win(sink_rows, qa, ka, vat, kam, vatm_b, bias_band, bias_meta, batch, seq)
        yb = _mla(q, k, km, vt, vtm_b, batch, seq)
        if last:
            out = _out(h_real, ya, yb, lw, i, norm_final.astype(f32)[None, :])
        else:
            yam = _win_meta(sink, qam, ka, vat, kam, vatm_b, bias_mm, bias_mr, batch, seq)
            (kcat, lat_t), (kcatm, lat_tm) = lat, latm
            lat_tm_b = lat_tm.reshape(LAT_T_ROWS, batch, N_META).transpose(1, 0, 2)
            ybm = _mla_meta(qm, kcat, kcatm, lat_t, lat_tm_b, lw, i, batch, seq)
            h_real = _out(h_real, ya, yb, lw, i)
            h_meta = _out(h_meta, yam, ybm, lw, i)
    return out.reshape(batch, seq, d).astype(x.dtype)
```

```python
import functools
import math

import jax
import jax.numpy as jnp
from jax import lax
from jax.experimental import pallas as pl
from jax.experimental.pallas import tpu as pltpu

D_MODEL = 1024
N_META = 16
BLOCK = 128
WINDOW = 128
A_HEADS = 8
A_KV_HEADS = 2
A_GROUP = A_HEADS // A_KV_HEADS
A_HEAD_DIM = 64
A_WIDTH = A_HEADS * A_HEAD_DIM
B_HEADS = 8
B_NOPE_DIM = 64
B_ROPE_DIM = 32
B_V_DIM = 64
B_WIDTH = B_HEADS * B_V_DIM
B_QK_PAD = B_NOPE_DIM + 2 * B_ROPE_DIM
BF16_SUBLANES = 16
B_V_AUG = B_V_DIM + BF16_SUBLANES
A_V_AUG = A_HEAD_DIM + BF16_SUBLANES
Q_LORA_RANK = 256
KV_LORA_RANK = 128
LAT_T_ROWS = KV_LORA_RANK + BF16_SUBLANES
N_BUCKETS = 32
MAX_DISTANCE = 128
ROPE_THETA = 10000.0
EPS = 1e-6

LOG2E = math.log2(math.e)
NEG = -1e30
ROW_TILE = 1024
OUT_ROW_SPLIT = 4
OUT_MIN_PIECE = 256
MLA_Q_TILE = 256
MLA_SUBTILES = 8
MLA_STEPS_PER_ITER = 8
MLA_KEY_CHUNK = 1024
WIN_BLOCKS = 16
WIN_AHEAD = 2
BIAS_ROW_CHUNK = 32
VMEM_LIMIT_BYTES = 56 * 1024 * 1024

_C_QA = 0
_C_CQ = _C_QA + A_WIDTH
_C_CKV = _C_CQ + Q_LORA_RANK
_C_KA = _C_CKV + KV_LORA_RANK
_C_KRA = _C_KA + A_KV_HEADS * A_HEAD_DIM
_C_KRB = _C_KRA + B_QK_PAD
_C_END = _C_KRB + B_QK_PAD

_NT = (((1,), (1,)), ((), ()))

bf16 = jnp.bfloat16
f32 = jnp.float32


def _cparams(*sem, flags=None):
    return pltpu.CompilerParams(dimension_semantics=sem, vmem_limit_bytes=VMEM_LIMIT_BYTES, flags=flags)


def _rms(x, gain):
    return x * lax.rsqrt(jnp.mean(x * x, axis=-1, keepdims=True) + EPS) * gain


def _silu(x):
    return x / (1.0 + jnp.exp(-x))


def _proj_kernel(h_ref, tab_ref, gin_ref, w1_ref, wvat_ref, gq_ref, wq_ref, gkv_ref, wk_ref, wvt_ref,
                 qa_ref, ka_ref, vat_ref, q_ref, k_ref, vt_ref, *latent_refs):
    t = h_ref.shape[0]
    rows = slice(0, t)
    tab_rows = pl.ds(pl.multiple_of((pl.program_id(0) % (tab_ref.shape[0] // t)) * t, t), t)
    u = _rms(h_ref[rows, :], gin_ref[...]).astype(bf16)

    def mm(lo, hi):
        return jnp.dot(u, w1_ref[:, lo:hi], preferred_element_type=f32)

    qa = mm(_C_QA, _C_CQ) * (A_HEAD_DIM ** -0.5 * LOG2E)
    for h in range(A_HEADS):
        qa_ref[h, rows, :] = qa[:, h * A_HEAD_DIM:(h + 1) * A_HEAD_DIM].astype(bf16)
    vat = lax.dot_general(wvat_ref[...], u, _NT, preferred_element_type=f32)
    vat_ref[:, :A_HEAD_DIM, rows] = vat.reshape(A_KV_HEADS, A_HEAD_DIM, t).astype(bf16)
    vat_ref[:, A_HEAD_DIM:, rows] = jnp.ones((A_KV_HEADS, A_V_AUG - A_HEAD_DIM, t), bf16)
    mixed = mm(_C_CQ, _C_END)
    cq = mixed[:, :_C_CKV - _C_CQ]
    ckv = mixed[:, _C_CKV - _C_CQ:_C_KA - _C_CQ]
    ka = mixed[:, _C_KA - _C_CQ:_C_KRA - _C_CQ]
    kra = mixed[:, _C_KRA - _C_CQ:_C_KRB - _C_CQ]
    krb = mixed[:, _C_KRB - _C_CQ:]
    for j in range(A_KV_HEADS):
        ka_ref[j, rows, :] = ka[:, j * A_HEAD_DIM:(j + 1) * A_HEAD_DIM].astype(bf16)
    mul_q = (tab_ref[tab_rows, 0 * B_QK_PAD:1 * B_QK_PAD],
             tab_ref[tab_rows, 1 * B_QK_PAD:2 * B_QK_PAD])
    cos_k = tab_ref[tab_rows, 2 * B_QK_PAD:3 * B_QK_PAD]
    sin_k = tab_ref[tab_rows, 3 * B_QK_PAD:4 * B_QK_PAD]

    cqn = _rms(cq, gq_ref[...]).astype(bf16)
    qp = jnp.dot(cqn, wq_ref[...], preferred_element_type=f32)
    for h in range(B_HEADS):
        q_ref[h, rows, :] = (qp[:, h * B_QK_PAD:(h + 1) * B_QK_PAD] * mul_q[h % 2]).astype(bf16)

    ckvn_f32 = _rms(ckv, gkv_ref[...])
    ckvn = ckvn_f32.astype(bf16)
    k_nope = jnp.dot(ckvn, wk_ref[...], preferred_element_type=f32)
    k_rope = kra * cos_k + krb * sin_k
    lane = lax.broadcasted_iota(jnp.int32, k_rope.shape, 1)
    for h in range(B_HEADS):
        pair = k_nope[:, (h // 2) * B_QK_PAD:(h // 2 + 1) * B_QK_PAD]
        own_nope = (lane < B_NOPE_DIM) if h % 2 == 0 else (lane >= B_NOPE_DIM)
        k_ref[h, rows, :] = jnp.where(own_nope, pair, k_rope).astype(bf16)
    vt = lax.dot_general(wvt_ref[...], ckvn, _NT, preferred_element_type=f32)
    vt_ref[:, :B_V_DIM, rows] = vt.reshape(B_HEADS, B_V_DIM, t).astype(bf16)
    vt_ref[:, B_V_DIM:, rows] = jnp.ones((B_HEADS, B_V_AUG - B_V_DIM, t), bf16)
    if latent_refs:
        kcat_ref, lat_t_ref = latent_refs
        kcat_ref[rows, :KV_LORA_RANK] = ckvn
        kcat_ref[rows, KV_LORA_RANK:] = k_rope.astype(bf16)
        lat_t_ref[:KV_LORA_RANK, rows] = ckvn_f32.T.astype(bf16)
        lat_t_ref[KV_LORA_RANK:, rows] = jnp.ones((LAT_T_ROWS - KV_LORA_RANK, t), bf16)


def _layer_spec(layer, shape):
    return pl.BlockSpec((None,) + tuple(shape), lambda i: (layer,) + (0,) * len(shape))


def _proj(h, tab, lw, layer, latent):
    rows = h.shape[0]
    t = min(ROW_TILE, rows)
    steps = rows // t
    assert tab.shape[0] % t == 0
    const = functools.partial(_layer_spec, layer)
    latent_specs = [pl.BlockSpec((t, KV_LORA_RANK + B_QK_PAD), lambda i: (i, 0)),
                    pl.BlockSpec((LAT_T_ROWS, t), lambda i: (0, i))] if latent else []
    latent_shapes = [jax.ShapeDtypeStruct((rows, KV_LORA_RANK + B_QK_PAD), bf16),
                     jax.ShapeDtypeStruct((LAT_T_ROWS, rows), bf16)] if latent else []
    return pl.pallas_call(
        _proj_kernel,
        grid=(steps,),
        in_specs=[
            pl.BlockSpec((t, D_MODEL), lambda i: (i, 0)),
            pl.BlockSpec(tab.shape, lambda i: (0, 0)),
            const((1, D_MODEL)),
            const((D_MODEL, _C_END)),
            const((A_KV_HEADS * A_HEAD_DIM, D_MODEL)),
            const((1, Q_LORA_RANK)),
            const((Q_LORA_RANK, B_HEADS * B_QK_PAD)),
            const((1, KV_LORA_RANK)),
            const((KV_LORA_RANK, B_HEADS * B_NOPE_DIM)),
            const((B_WIDTH, KV_LORA_RANK)),
        ],
        out_specs=[
            pl.BlockSpec((A_HEADS, t, A_HEAD_DIM), lambda i: (0, i, 0)),
            pl.BlockSpec((A_KV_HEADS, t, A_HEAD_DIM), lambda i: (0, i, 0)),
            pl.BlockSpec((A_KV_HEADS, A_V_AUG, t), lambda i: (0, 0, i)),
            pl.BlockSpec((B_HEADS, t, B_QK_PAD), lambda i: (0, i, 0)),
            pl.BlockSpec((B_HEADS, t, B_QK_PAD), lambda i: (0, i, 0)),
            pl.BlockSpec((B_HEADS, B_V_AUG, t), lambda i: (0, 0, i)),
        ] + latent_specs,
        out_shape=[
            jax.ShapeDtypeStruct((A_HEADS, rows, A_HEAD_DIM), bf16),
            jax.ShapeDtypeStruct((A_KV_HEADS, rows, A_HEAD_DIM), bf16),
            jax.ShapeDtypeStruct((A_KV_HEADS, A_V_AUG, rows), bf16),
            jax.ShapeDtypeStruct((B_HEADS, rows, B_QK_PAD), bf16),
            jax.ShapeDtypeStruct((B_HEADS, rows, B_QK_PAD), bf16),
            jax.ShapeDtypeStruct((B_HEADS, B_V_AUG, rows), bf16),
        ] + latent_shapes,
        compiler_params=_cparams("arbitrary"),
        name="proj",
    )(h, tab, lw["gin"], lw["w1"], lw["wvat"], lw["gq"], lw["wq"], lw["gkv"], lw["wk"], lw["wvt"])


def _out_kernel(h_ref, ya_ref, yb_ref, gin_ref, wg_ref, na_ref, nb_ref, w_ref, *rest, final):
    if final:
        nf_ref, o_ref = rest
    else:
        (o_ref,) = rest
    t = h_ref.shape[0]
    piece = max(t // OUT_ROW_SPLIT, min(t, OUT_MIN_PIECE))

    def body(rows):
        hn = h_ref[rows, :]
        u = _rms(hn, gin_ref[...]).astype(bf16)
        ga = _silu(jnp.dot(u, wg_ref[:, :A_WIDTH], preferred_element_type=f32))
        gb = _silu(jnp.dot(u, wg_ref[:, A_WIDTH:], preferred_element_type=f32))
        ya = _rms(ya_ref[rows, :].astype(f32), na_ref[...]) * ga
        yb = _rms(yb_ref[rows, :].astype(f32), nb_ref[...]) * gb
        hn = hn + jnp.dot(ya.astype(bf16), w_ref[:A_WIDTH, :], preferred_element_type=f32)
        return hn + jnp.dot(yb.astype(bf16), w_ref[A_WIDTH:, :], preferred_element_type=f32)

    def tail(rows, hn):
        o_ref[rows, :] = _rms(hn, nf_ref[...]) if final else hn

    pending = None
    for r in range(0, t, piece):
        rows = slice(r, r + piece)
        hn = body(rows)
        if pending is not None:
            tail(*pending)
        pending = (rows, hn)
    tail(*pending)


def _out(h, ya, yb, lw, layer, norm_final=None):
    rows = h.shape[0]
    t = min(ROW_TILE, rows)
    row = lambda width: pl.BlockSpec((t, width), lambda i: (i, 0))
    const = functools.partial(_layer_spec, layer)
    final = norm_final is not None
    in_specs = [row(D_MODEL), row(A_WIDTH), row(B_WIDTH),
                const((1, D_MODEL)), const((D_MODEL, A_WIDTH + B_WIDTH)),
                const((1, A_WIDTH)), const((1, B_WIDTH)), const((A_WIDTH + B_WIDTH, D_MODEL))]
    args = [h, ya, yb, lw["gin"], lw["wg"], lw["na"], lw["nb"], lw["wout"]]
    if final:
        in_specs.append(pl.BlockSpec((1, D_MODEL), lambda i: (0, 0)))
        args.append(norm_final)
    return pl.pallas_call(
        functools.partial(_out_kernel, final=final),
        grid=(rows // t,),
        in_specs=in_specs,
        out_specs=row(D_MODEL),
        out_shape=jax.ShapeDtypeStruct((rows, D_MODEL), f32),
        compiler_params=_cparams("arbitrary"),
        name="outproj",
    )(*args)


def _bias_kernel(tab_ref, idx_ref, o_ref):
    rows = idx_ref.shape[0]
    chunk = math.gcd(rows, BIAS_ROW_CHUNK)

    def body(r, carry):
        sl = pl.ds(pl.multiple_of(r * chunk, chunk), chunk)
        idx = idx_ref[sl, :]
        accs = [jnp.where(idx < 0, NEG, 0.0).astype(f32)] * A_HEADS
        for b in range(N_BUCKETS):
            hit = idx == b
            accs = [jnp.where(hit, tab_ref[b * A_HEADS + h], accs[h]) for h in range(A_HEADS)]
        for h in range(A_HEADS):
            o_ref[h, sl, :] = accs[h]
        return carry

    lax.fori_loop(0, rows // chunk, body, 0)


def _bias_lookup(table_flat, idx):
    return pl.pallas_call(
        _bias_kernel,
        in_specs=[pl.BlockSpec(memory_space=pltpu.SMEM), pl.BlockSpec(idx.shape, lambda: (0, 0))],
        out_specs=pl.BlockSpec((A_HEADS,) + idx.shape, lambda: (0, 0, 0)),
        out_shape=jax.ShapeDtypeStruct((A_HEADS,) + idx.shape, f32),
        compiler_params=pltpu.CompilerParams(vmem_limit_bytes=VMEM_LIMIT_BYTES),
        name="relbias",
    )(table_flat, idx)


def _t5_bucket(rel):
    nb = N_BUCKETS // 2
    max_exact = nb // 2
    ret = jnp.where(rel > 0, nb, 0)
    n = jnp.abs(rel)
    nf = jnp.maximum(n, 1).astype(jnp.float32)
    large = max_exact + (jnp.log(nf / max_exact) / math.log(MAX_DISTANCE / max_exact)
                         * (nb - max_exact)).astype(jnp.int32)
    large = jnp.minimum(large, nb - 1)
    bucket = ret + jnp.where(n < max_exact, n, large)
    return jnp.bitwise_and(bucket, N_BUCKETS - 1)


def _bias_tables(rel_bias_table, nblk):
    table_flat = (rel_bias_table.astype(f32) * LOG2E).reshape(-1)
    i = jnp.arange(BLOCK, dtype=jnp.int32)[:, None]
    j = jnp.arange(3 * BLOCK, dtype=jnp.int32)[None, :]
    rel = j - i - BLOCK
    band = jnp.where(jnp.abs(rel) <= WINDOW, _t5_bucket(rel), -1)
    interior = (_bias_lookup(table_flat, band.T)
                .reshape(A_KV_HEADS, A_GROUP, 3 * BLOCK, BLOCK)
                .transpose(0, 2, 1, 3).reshape(A_KV_HEADS, 3 * BLOCK, A_GROUP * BLOCK))
    key = jnp.arange(3 * BLOCK, dtype=jnp.int32)[None, :, None]
    first = jnp.where(key < BLOCK, NEG, interior)
    last = jnp.where(key >= 2 * BLOCK, NEG, interior)
    bias_band = jnp.stack([first, interior, last])

    n = jnp.arange(nblk, dtype=jnp.int32)[:, None, None]
    k = jnp.arange(N_META, dtype=jnp.int32)[None, :, None]
    qi = jnp.arange(BLOCK, dtype=jnp.int32)[None, None, :]
    rel_m = k - (N_META + n * BLOCK + qi)
    idx_m = _t5_bucket(rel_m).reshape(nblk * N_META, BLOCK)
    bias_meta = (_bias_lookup(table_flat, idx_m)
                 .reshape(A_KV_HEADS, A_GROUP, nblk, N_META, BLOCK)
                 .transpose(2, 0, 3, 1, 4).reshape(nblk, A_KV_HEADS, N_META, A_GROUP * BLOCK))

    qp = jnp.arange(N_META, dtype=jnp.int32)[:, None]
    kp = jnp.arange(N_META + BLOCK, dtype=jnp.int32)[None, :]
    rel_q = kp - qp
    idx_q = jnp.where(jnp.abs(rel_q) <= WINDOW, _t5_bucket(rel_q), -1)
    bias_q = _bias_lookup(table_flat, idx_q)
    return bias_band, bias_meta, bias_q[:, :, :N_META], bias_q[:, :, N_META:]


def _win_kernel(sink_ref, q_ref, kp_ref, kc_ref, kn_ref, vp_ref, vc_ref, vn_ref, kam_ref, vam_ref,
                bias_ref, bm_ref, o_ref):
    w = WIN_BLOCKS
    i, ntile = pl.program_id(1), pl.num_programs(1)

    def qrows(blk):
        return slice(blk * BLOCK, (blk + 1) * BLOCK)

    def band(blk, prev_ref, cur_ref, next_ref, kvh, axis):
        take = lambda ref, lo, hi: ref[kvh, lo:hi, :] if axis == 0 else ref[kvh, :, lo:hi]
        parts = [take(cur_ref, max(blk - 1, 0) * BLOCK, min(blk + 2, w) * BLOCK)]
        if blk == 0:
            parts.insert(0, take(prev_ref, (w - 1) * BLOCK, w * BLOCK))
        if blk == w - 1:
            parts.append(take(next_ref, 0, BLOCK))
        return jnp.concatenate(parts, axis=axis)

    def variant(blk):
        v = 1
        if blk == 0:
            v = jnp.where(i == 0, 0, v)
        if blk == w - 1:
            v = jnp.where(i == ntile - 1, 2, v)
        return v

    def scores(blk, kvh):
        kb = jnp.concatenate([band(blk, kp_ref, kc_ref, kn_ref, kvh, 0), kam_ref[kvh]], axis=0)
        q4 = q_ref[A_GROUP * kvh:A_GROUP * (kvh + 1), qrows(blk), :].reshape(A_GROUP * BLOCK, A_HEAD_DIM)
        bias = jnp.concatenate([bias_ref[variant(blk), kvh], bm_ref[blk, kvh]], axis=0)
        s = lax.dot_general(kb, q4, _NT, preferred_element_type=f32) + bias
        m = jnp.maximum(jnp.max(s, axis=0, keepdims=True), sink_ref[kvh])
        return s, m

    def attend(blk, kvh, s, m):
        vb = jnp.concatenate([band(blk, vp_ref, vc_ref, vn_ref, kvh, 1), vam_ref[0, kvh]], axis=1)
        p = jnp.exp2(s - m).astype(bf16)
        acc = jnp.dot(vb, p, preferred_element_type=f32)
        den = acc[A_HEAD_DIM:A_HEAD_DIM + 1] + jnp.exp2(sink_ref[kvh] - m)
        return acc[:A_HEAD_DIM] * (1.0 / den)

    units = [(blk, kvh) for blk in range(w) for kvh in range(A_KV_HEADS)]
    outs = {}
    pending = [scores(*units[u]) for u in range(WIN_AHEAD)]
    for u, unit in enumerate(units):
        if u + WIN_AHEAD < len(units):
            pending.append(scores(*units[u + WIN_AHEAD]))
        outs[unit] = attend(*unit, *pending.pop(0))
        blk, kvh = unit
        if kvh == A_KV_HEADS - 1:
            ot = jnp.concatenate([outs[(blk, j)][:, g * BLOCK:(g + 1) * BLOCK]
                                  for j in range(A_KV_HEADS) for g in range(A_GROUP)], axis=0)
            o_ref[qrows(blk), :] = ot.T.astype(o_ref.dtype)


def _win(sink_rows, qa, ka, vat, kam, vatm, bias_band, bias_meta, batch, seq):
    tb = WIN_BLOCKS * BLOCK
    ntile = seq // tb
    prev_ = lambda b, i: b * ntile + jnp.maximum(i - 1, 0)
    cur_ = lambda b, i: b * ntile + i
    next_ = lambda b, i: b * ntile + jnp.minimum(i + 1, ntile - 1)
    kspec = lambda f: pl.BlockSpec((A_KV_HEADS, tb, A_HEAD_DIM), lambda b, i: (0, f(b, i), 0))
    vspec = lambda f: pl.BlockSpec((A_KV_HEADS, A_V_AUG, tb), lambda b, i: (0, 0, f(b, i)))
    return pl.pallas_call(
        _win_kernel,
        grid=(batch, ntile),
        in_specs=[
            pl.BlockSpec((A_KV_HEADS, 1, A_GROUP * BLOCK), lambda b, i: (0, 0, 0)),
            pl.BlockSpec((A_HEADS, tb, A_HEAD_DIM), lambda b, i: (0, cur_(b, i), 0)),
            kspec(prev_), kspec(cur_), kspec(next_),
            vspec(prev_), vspec(cur_), vspec(next_),
            pl.BlockSpec((A_KV_HEADS, N_META, A_HEAD_DIM), lambda b, i: (0, b, 0)),
            pl.BlockSpec((1, A_KV_HEADS, A_V_AUG, N_META), lambda b, i: (b, 0, 0, 0)),
            pl.BlockSpec((3, A_KV_HEADS, 3 * BLOCK, A_GROUP * BLOCK), lambda b, i: (0, 0, 0, 0)),
            pl.BlockSpec((WIN_BLOCKS, A_KV_HEADS, N_META, A_GROUP * BLOCK), lambda b, i: (i, 0, 0, 0)),
        ],
        out_specs=pl.BlockSpec((tb, A_WIDTH), lambda b, i: (cur_(b, i), 0)),
        out_shape=jax.ShapeDtypeStruct((batch * seq, A_WIDTH), bf16),
        compiler_params=_cparams("arbitrary", "arbitrary"),
        name="win",
    )(sink_rows, qa, ka, ka, ka, vat, vat, vat, kam, vatm, bias_band, bias_meta)


def _win_meta_kernel(sink_ref, q_ref, ka_ref, vat_ref, kam_ref, vatm_ref, biasm_ref, biasr_ref, o_ref):
    for kvh in range(A_KV_HEADS):
        q4 = q_ref[A_GROUP * kvh:A_GROUP * (kvh + 1)].reshape(A_GROUP * N_META, A_HEAD_DIM)
        sm = lax.dot_general(q4, kam_ref[kvh], _NT, preferred_element_type=f32)
        sr = lax.dot_general(q4, ka_ref[kvh], _NT, preferred_element_type=f32)
        for g in range(A_GROUP):
            h = A_GROUP * kvh + g
            rows = slice(g * N_META, (g + 1) * N_META)
            smg = sm[rows] + biasm_ref[h]
            srg = sr[rows] + biasr_ref[h]
            sink = sink_ref[h]
            m = jnp.maximum(jnp.maximum(jnp.max(smg, axis=-1, keepdims=True),
                                        jnp.max(srg, axis=-1, keepdims=True)), sink)
            pm = jnp.exp2(smg - m).astype(bf16)
            pr = jnp.exp2(srg - m).astype(bf16)
            o = (lax.dot_general(pm, vatm_ref[0, kvh], _NT, preferred_element_type=f32)
                 + lax.dot_general(pr, vat_ref[kvh], _NT, preferred_element_type=f32))
            den = o[:, A_HEAD_DIM:A_HEAD_DIM + 1] + jnp.exp2(sink - m)
            o_ref[:, h * A_HEAD_DIM:(h + 1) * A_HEAD_DIM] = (o[:, :A_HEAD_DIM] * (1.0 / den)).astype(o_ref.dtype)


def _win_meta(sink, qam, ka, vat, kam, vatm, bias_mm, bias_mr, batch, seq):
    nblk = seq // BLOCK
    const3 = lambda shape: pl.BlockSpec(shape, lambda b: (0, 0, 0))
    return pl.pallas_call(
        _win_meta_kernel,
        grid=(batch,),
        in_specs=[
            pl.BlockSpec(memory_space=pltpu.SMEM),
            pl.BlockSpec((A_HEADS, N_META, A_HEAD_DIM), lambda b: (0, b, 0)),
            pl.BlockSpec((A_KV_HEADS, BLOCK, A_HEAD_DIM), lambda b: (0, b * nblk, 0)),
            pl.BlockSpec((A_KV_HEADS, A_V_AUG, BLOCK), lambda b: (0, 0, b * nblk)),
            pl.BlockSpec((A_KV_HEADS, N_META, A_HEAD_DIM), lambda b: (0, b, 0)),
            pl.BlockSpec((1, A_KV_HEADS, A_V_AUG, N_META), lambda b: (b, 0, 0, 0)),
            const3((A_HEADS, N_META, N_META)),
            const3((A_HEADS, N_META, BLOCK)),
        ],
        out_specs=pl.BlockSpec((N_META, A_WIDTH), lambda b: (b, 0)),
        out_shape=jax.ShapeDtypeStruct((batch * N_META, A_WIDTH), bf16),
        compiler_params=_cparams("arbitrary"),
        name="win_meta",
    )(sink, qam, ka, vat, kam, vatm, bias_mm, bias_mr)


def _mla_kernel(q_ref, k_ref, km_ref, vt_ref, vtm_ref, o_ref, acc_ref, s0_ref, s1_ref, sm0_ref, sm1_ref):
    s_bufs, sm_bufs = (s0_ref, s1_ref), (sm0_ref, sm1_ref)
    nsub, _, _, tq = acc_ref.shape
    units = nsub * B_HEADS
    seq = k_ref.shape[1]
    chunks = [(c * MLA_KEY_CHUNK, (c + 1) * MLA_KEY_CHUNK) for c in range(seq // MLA_KEY_CHUNK)]

    def unit(u):
        return u % B_HEADS, u // B_HEADS

    def step(t, slot, m, *, score_next=True, attend_cur=True):
        if score_next:
            hn, subn = unit(t + 1)
            q = q_ref[hn, pl.ds(pl.multiple_of(subn * tq, tq), tq), :]
            sm = lax.dot_general(km_ref[hn], q, _NT, preferred_element_type=f32)
            sm_bufs[1 - slot][...] = sm
            m_next = jnp.max(sm, axis=0, keepdims=True)
        if attend_cur:
            h, sub = unit(t)
            pm = jnp.exp2(sm_bufs[slot][...] - m).astype(bf16)
            acc = jnp.dot(vtm_ref[0, h], pm, preferred_element_type=f32)
        for lo, hi in chunks:
            if score_next:
                s = lax.dot_general(k_ref[hn, lo:hi, :], q, _NT, preferred_element_type=f32)
                s_bufs[1 - slot][lo:hi, :] = s
                m_next = jnp.maximum(m_next, jnp.max(s, axis=0, keepdims=True))
            if attend_cur:
                p = jnp.exp2(s_bufs[slot][lo:hi, :] - m).astype(bf16)
                acc = acc + jnp.dot(vt_ref[h, :, lo:hi], p, preferred_element_type=f32)
        if attend_cur:
            acc_ref[sub, h] = acc[:B_V_DIM] * (1.0 / acc[B_V_DIM:B_V_DIM + 1])
        return m_next if score_next else None

    def steps(j, m):
        for i in range(MLA_STEPS_PER_ITER):
            m = step(MLA_STEPS_PER_ITER * j + i, i % 2, m)
        return m

    assert MLA_STEPS_PER_ITER % 2 == 0 and units % 2 == 0
    iters = (units - 1) // MLA_STEPS_PER_ITER
    m = step(-1, 1, None, attend_cur=False)
    m = lax.fori_loop(0, iters, steps, m)
    for t in range(iters * MLA_STEPS_PER_ITER, units - 1):
        m = step(t, t % 2, m)
    step(units - 1, 1, m, score_next=False)
    for sub in range(nsub):
        o_ref[sub * tq:(sub + 1) * tq, :] = acc_ref[sub].reshape(B_WIDTH, tq).T.astype(o_ref.dtype)


def _mla(q, k, km, vt, vtm, batch, seq):
    tq = MLA_Q_TILE
    tb = MLA_Q_TILE * MLA_SUBTILES
    nq = seq // tb
    return pl.pallas_call(
        _mla_kernel,
        grid=(batch, nq),
        in_specs=[
            pl.BlockSpec((B_HEADS, tb, B_QK_PAD), lambda b, i: (0, b * nq + i, 0)),
            pl.BlockSpec((B_HEADS, seq, B_QK_PAD), lambda b, i: (0, b, 0)),
            pl.BlockSpec((B_HEADS, N_META, B_QK_PAD), lambda b, i: (0, b, 0)),
            pl.BlockSpec((B_HEADS, B_V_AUG, seq), lambda b, i: (0, 0, b)),
            pl.BlockSpec((1, B_HEADS, B_V_AUG, N_META), lambda b, i: (b, 0, 0, 0)),
        ],
        out_specs=pl.BlockSpec((tb, B_WIDTH), lambda b, i: (b * nq + i, 0)),
        out_shape=jax.ShapeDtypeStruct((batch * seq, B_WIDTH), bf16),
        scratch_shapes=([pltpu.VMEM((MLA_SUBTILES, B_HEADS, B_V_DIM, tq), f32)]
                        + [pltpu.VMEM((seq, tq), f32)] * 2 + [pltpu.VMEM((N_META, tq), f32)] * 2),
        compiler_params=_cparams("arbitrary", "arbitrary"),
        name="mla",
    )(q, k, km, vt, vtm)


def _mla_meta_kernel(q_ref, kcat_ref, kcatm_ref, lat_t_ref, lat_tm_ref, wuk_ref, wuv_ref, o_ref):
    lane = lax.broadcasted_iota(jnp.int32, (N_META, B_QK_PAD), 1)
    q_rows = []
    for h in range(B_HEADS):
        q = q_ref[h]
        rope_lanes = (lane >= B_NOPE_DIM) if h % 2 == 0 else (lane < B_NOPE_DIM)
        q_lat = jnp.dot(q, wuk_ref[h], preferred_element_type=f32).astype(bf16)
        q_rows.append(jnp.concatenate([q_lat, jnp.where(rope_lanes, q, jnp.zeros_like(q))], axis=1))
    qcat = jnp.concatenate(q_rows, axis=0)
    seq = kcat_ref.shape[0]
    chunks = [(c * MLA_KEY_CHUNK, (c + 1) * MLA_KEY_CHUNK) for c in range(seq // MLA_KEY_CHUNK)]
    sm = lax.dot_general(kcatm_ref[...], qcat, _NT, preferred_element_type=f32)
    ss = [lax.dot_general(kcat_ref[lo:hi, :], qcat, _NT, preferred_element_type=f32) for lo, hi in chunks]
    m = jnp.max(sm, axis=0, keepdims=True)
    for s in ss:
        m = jnp.maximum(m, jnp.max(s, axis=0, keepdims=True))
    acc = jnp.dot(lat_tm_ref[0], jnp.exp2(sm - m).astype(bf16), preferred_element_type=f32)
    for (lo, hi), s in zip(chunks, ss):
        acc = acc + jnp.dot(lat_t_ref[:, lo:hi], jnp.exp2(s - m).astype(bf16),
                            preferred_element_type=f32)
    o_lat = (acc[:KV_LORA_RANK] * (1.0 / acc[KV_LORA_RANK:KV_LORA_RANK + 1])).T.astype(bf16)
    for h in range(B_HEADS):
        o_ref[:, h * B_V_DIM:(h + 1) * B_V_DIM] = jnp.dot(
            o_lat[h * N_META:(h + 1) * N_META], wuv_ref[h], preferred_element_type=f32).astype(o_ref.dtype)


def _mla_meta(qm, kcat, kcatm, lat_t, lat_tm, lw, layer, batch, seq):
    const = functools.partial(_layer_spec, layer)
    return pl.pallas_call(
        _mla_meta_kernel,
        grid=(batch,),
        in_specs=[
            pl.BlockSpec((B_HEADS, N_META, B_QK_PAD), lambda b: (0, b, 0)),
            pl.BlockSpec((seq, KV_LORA_RANK + B_QK_PAD), lambda b: (b, 0)),
            pl.BlockSpec((N_META, KV_LORA_RANK + B_QK_PAD), lambda b: (b, 0)),
            pl.BlockSpec((LAT_T_ROWS, seq), lambda b: (0, b)),
            pl.BlockSpec((1, LAT_T_ROWS, N_META), lambda b: (b, 0, 0)),
            const((B_HEADS, B_QK_PAD, KV_LORA_RANK)),
            const((B_HEADS, KV_LORA_RANK, B_V_DIM)),
        ],
        out_specs=pl.BlockSpec((N_META, B_WIDTH), lambda b: (b, 0)),
        out_shape=jax.ShapeDtypeStruct((batch * N_META, B_WIDTH), bf16),
        compiler_params=_cparams("arbitrary"),
        name="mla_meta",
    )(qm, kcat, kcatm, lat_t, lat_tm, lw["wuk"], lw["wuv"])


def _rope_tables(pos):
    half = B_ROPE_DIM // 2
    freqs = ROPE_THETA ** (-jnp.arange(half, dtype=jnp.float32) / half)
    lane_freq = jnp.concatenate([freqs] * (B_QK_PAD // half))
    ang = pos.astype(jnp.float32)[:, None] * lane_freq[None, :]
    cos, sin = jnp.cos(ang), jnp.sin(ang)
    lane = jnp.arange(B_QK_PAD)[None, :]
    c_mla = (B_NOPE_DIM + B_ROPE_DIM) ** -0.5 * LOG2E
    rope_pair = jnp.where(lane % (2 * B_ROPE_DIM) < B_ROPE_DIM, c_mla * cos, c_mla * sin)
    mul_q_even = jnp.where(lane < B_NOPE_DIM, c_mla, rope_pair)
    mul_q_odd = jnp.where(lane < 2 * B_ROPE_DIM, rope_pair, c_mla)
    return jnp.concatenate([mul_q_even, mul_q_odd, cos, sin], axis=1).astype(f32)


def _rot_cols(w):
    half = w.shape[-1] // 2
    return jnp.concatenate([-w[..., half:], w[..., :half]], axis=-1)


def _stacked_weights(norm_in, w_in, norm_q_lat, w_uq, norm_kv_lat, w_ukv, norm_out_a, norm_out_b, w_out):
    depth, d, _ = w_in.shape
    sizes = (A_WIDTH, A_KV_HEADS * A_HEAD_DIM, A_KV_HEADS * A_HEAD_DIM, A_WIDTH,
             Q_LORA_RANK, KV_LORA_RANK, B_ROPE_DIM, B_WIDTH)
    offs = [sum(sizes[:i]) for i in range(len(sizes) + 1)]
    qa, ka, va, ga, cq, ckv, kr, gb = (w_in[..., offs[i]:offs[i + 1]] for i in range(len(sizes)))
    zeros = lambda *shape: jnp.zeros(shape, w_in.dtype)
    reps = B_QK_PAD // B_ROPE_DIM
    kr_groups = jnp.concatenate([kr] * reps + [_rot_cols(kr)] * reps, axis=-1)
    w1 = jnp.concatenate([qa, cq, ckv, ka, kr_groups], axis=-1).astype(bf16)

    r = w_uq.shape[1]
    uq = w_uq.reshape(depth, r, B_HEADS // 2, 2, B_NOPE_DIM + B_ROPE_DIM)
    nope, rope = uq[..., :B_NOPE_DIM], uq[..., B_NOPE_DIM:]
    rope2 = jnp.concatenate([rope, _rot_cols(rope)], axis=-1)
    wq = jnp.stack([jnp.concatenate([nope[..., 0, :], rope2[..., 0, :]], axis=-1),
                    jnp.concatenate([rope2[..., 1, :], nope[..., 1, :]], axis=-1)], axis=3)

    rk = w_ukv.shape[1]
    ukv = w_ukv.reshape(depth, rk, B_HEADS, B_NOPE_DIM + B_V_DIM)
    k_nope, v = ukv[..., :B_NOPE_DIM], ukv[..., B_NOPE_DIM:]
    uk_t = jnp.transpose(k_nope, (0, 2, 3, 1)).reshape(depth, B_HEADS // 2, 2, B_NOPE_DIM, rk)
    no_lanes = zeros(depth, B_HEADS // 2, B_QK_PAD - B_NOPE_DIM, rk)
    wuk = jnp.stack([jnp.concatenate([uk_t[:, :, 0], no_lanes], axis=2),
                     jnp.concatenate([no_lanes, uk_t[:, :, 1]], axis=2)], axis=2)
    row = lambda g: g.astype(f32)[:, None, :]
    return {
        "gin": row(norm_in),
        "w1": w1,
        "wg": jnp.concatenate([ga, gb], axis=-1).astype(bf16),
        "wvat": jnp.swapaxes(va.astype(bf16), 1, 2),
        "gq": row(norm_q_lat),
        "wq": wq.reshape(depth, r, B_HEADS * B_QK_PAD).astype(bf16),
        "gkv": row(norm_kv_lat),
        "wk": k_nope.reshape(depth, rk, B_HEADS * B_NOPE_DIM).astype(bf16),
        "wvt": jnp.swapaxes(v.reshape(depth, rk, B_WIDTH).astype(bf16), 1, 2),
        "wuk": wuk.reshape(depth, B_HEADS, B_QK_PAD, rk).astype(bf16),
        "wuv": jnp.transpose(v, (0, 2, 1, 3)).astype(bf16),
        "na": row(norm_out_a),
        "nb": row(norm_out_b),
        "wout": w_out.astype(bf16),
    }


def kernel(x, meta_tokens, rel_bias_table, norm_in, w_in, sink_a, norm_q_lat, w_uq, norm_kv_lat, w_ukv,
           norm_out_a, norm_out_b, w_out, norm_final):
    batch, seq, d = x.shape
    depth = w_in.shape[0]
    assert d == D_MODEL and seq % (MLA_Q_TILE * MLA_SUBTILES) == 0 and seq % ROW_TILE == 0
    assert B_QK_PAD == 2 * B_NOPE_DIM and B_HEADS % 2 == 0

    h_real = x.reshape(batch * seq, d).astype(f32)
    h_meta = jnp.tile(meta_tokens.astype(f32), (batch, 1))
    tab_real = _rope_tables(N_META + jnp.arange(seq))
    tab_meta = _rope_tables(jnp.arange(batch * N_META) % N_META)
    bias_band, bias_meta, bias_mm, bias_mr = _bias_tables(rel_bias_table, seq // BLOCK)

    lw = _stacked_weights(norm_in, w_in, norm_q_lat, w_uq, norm_kv_lat, w_ukv, norm_out_a, norm_out_b, w_out)
    sinks = sink_a.astype(f32) * LOG2E
    sinks_rows = jnp.repeat(sinks.reshape(depth, A_KV_HEADS, A_GROUP), BLOCK, axis=2)[:, :, None, :]

    out = None
    for i in range(depth):
        last = i == depth - 1
        sink, sink_rows = sinks[i], sinks_rows[i]
        qa, ka, vat, q, k, vt, *lat = _proj(h_real, tab_real, lw, i, latent=not last)
        qam, kam, vatm, qm, km, vtm, *latm = _proj(h_meta, tab_meta, lw, i, latent=not last)
        vtm_b = vtm.reshape(B_HEADS, B_V_AUG, batch, N_META).transpose(2, 0, 1, 3)
        vatm_b = vatm.reshape(A_KV_HEADS, A_V_AUG, batch, N_META).transpose(2, 0, 1, 3)

        ya = _win(sink_rows, qa, ka, vat, kam, vatm_b, bias_band, bias_meta, batch, seq)
        yb = _mla(q, k, km, vt, vtm_b, batch, seq)
        if last:
            out = _out(h_real, ya, yb, lw, i, norm_final.astype(f32)[None, :])
        else:
            yam = _win_meta(sink, qam, ka, vat, kam, vatm_b, bias_mm, bias_mr, batch, seq)
            (kcat, lat_t), (kcatm, lat_tm) = lat, latm
            lat_tm_b = lat_tm.reshape(LAT_T_ROWS, batch, N_META).transpose(1, 0, 2)
            ybm = _mla_meta(qm, kcat, kcatm, lat_t, lat_tm_b, lw, i, batch, seq)
            h_real = _out(h_real, ya, yb, lw, i)
            h_meta = _out(h_meta, yam, ybm, lw, i)
    return out.reshape(batch, seq, d).astype(x.dtype)
```

```python
import functools
import math

import jax
import jax.numpy as jnp
from jax import lax
from jax.experimental import pallas as pl
from jax.experimental.pallas import tpu as pltpu

D_MODEL = 1024
N_META = 16
BLOCK = 128
WINDOW = 128
A_HEADS = 8
A_KV_HEADS = 2
A_GROUP = A_HEADS // A_KV_HEADS
A_HEAD_DIM = 64
A_WIDTH = A_HEADS * A_HEAD_DIM
B_HEADS = 8
B_NOPE_DIM = 64
B_ROPE_DIM = 32
B_V_DIM = 64
B_WIDTH = B_HEADS * B_V_DIM
B_QK_PAD = B_NOPE_DIM + 2 * B_ROPE_DIM
BF16_SUBLANES = 16
B_V_AUG = B_V_DIM + BF16_SUBLANES
A_V_AUG = A_HEAD_DIM + BF16_SUBLANES
Q_LORA_RANK = 256
KV_LORA_RANK = 128
LAT_T_ROWS = KV_LORA_RANK + BF16_SUBLANES
N_BUCKETS = 32
MAX_DISTANCE = 128
ROPE_THETA = 10000.0
EPS = 1e-6

LOG2E = math.log2(math.e)
NEG = -1e30
ROW_TILE = 1024
OUT_ROW_SPLIT = 4
OUT_MIN_PIECE = 256
MLA_Q_TILE = 256
MLA_SUBTILES = 8
MLA_STEPS_PER_ITER = 8
MLA_KEY_CHUNK = 1024
WIN_BLOCKS = 16
WIN_AHEAD = 2
BIAS_ROW_CHUNK = 32
VMEM_LIMIT_BYTES = 56 * 1024 * 1024

_C_QA = 0
_C_CQ = _C_QA + A_WIDTH
_C_CKV = _C_CQ + Q_LORA_RANK
_C_KA = _C_CKV + KV_LORA_RANK
_C_KRA = _C_KA + A_KV_HEADS * A_HEAD_DIM
_C_KRB = _C_KRA + B_QK_PAD
_C_END = _C_KRB + B_QK_PAD

_NT = (((1,), (1,)), ((), ()))

bf16 = jnp.bfloat16
f32 = jnp.float32


def _cparams(*sem, flags=None):
    return pltpu.CompilerParams(dimension_semantics=sem, vmem_limit_bytes=VMEM_LIMIT_BYTES, flags=flags)


def _rms(x, gain):
    return x * lax.rsqrt(jnp.mean(x * x, axis=-1, keepdims=True) + EPS) * gain


def _silu(x):
    return x / (1.0 + jnp.exp(-x))


def _proj_kernel(h_ref, tab_ref, gin_ref, w1_ref, wvat_ref, gq_ref, wq_ref, gkv_ref, wk_ref, wvt_ref,
                 qa_ref, ka_ref, vat_ref, q_ref, k_ref, vt_ref, *latent_refs):
    t = h_ref.shape[0]
    rows = slice(0, t)
    tab_rows = pl.ds(pl.multiple_of((pl.program_id(0) % (tab_ref.shape[0] // t)) * t, t), t)
    u = _rms(h_ref[rows, :], gin_ref[...]).astype(bf16)

    def mm(lo, hi):
        return jnp.dot(u, w1_ref[:, lo:hi], preferred_element_type=f32)

    qa = mm(_C_QA, _C_CQ) * (A_HEAD_DIM ** -0.5 * LOG2E)
    for h in range(A_HEADS):
        qa_ref[h, rows, :] = qa[:, h * A_HEAD_DIM:(h + 1) * A_HEAD_DIM].astype(bf16)
    vat = lax.dot_general(wvat_ref[...], u, _NT, preferred_element_type=f32)
    vat_ref[:, :A_HEAD_DIM, rows] = vat.reshape(A_KV_HEADS, A_HEAD_DIM, t).astype(bf16)
    vat_ref[:, A_HEAD_DIM:, rows] = jnp.ones((A_KV_HEADS, A_V_AUG - A_HEAD_DIM, t), bf16)
    mixed = mm(_C_CQ, _C_END)
    cq = mixed[:, :_C_CKV - _C_CQ]
    ckv = mixed[:, _C_CKV - _C_CQ:_C_KA - _C_CQ]
    ka = mixed[:, _C_KA - _C_CQ:_C_KRA - _C_CQ]
    kra = mixed[:, _C_KRA - _C_CQ:_C_KRB - _C_CQ]
    krb = mixed[:, _C_KRB - _C_CQ:]
    for j in range(A_KV_HEADS):
        ka_ref[j, rows, :] = ka[:, j * A_HEAD_DIM:(j + 1) * A_HEAD_DIM].astype(bf16)
    mul_q = (tab_ref[tab_rows, 0 * B_QK_PAD:1 * B_QK_PAD],
             tab_ref[tab_rows, 1 * B_QK_PAD:2 * B_QK_PAD])
    cos_k = tab_ref[tab_rows, 2 * B_QK_PAD:3 * B_QK_PAD]
    sin_k = tab_ref[tab_rows, 3 * B_QK_PAD:4 * B_QK_PAD]

    cqn = _rms(cq, gq_ref[...]).astype(bf16)
    qp = jnp.dot(cqn, wq_ref[...], preferred_element_type=f32)
    for h in range(B_HEADS):
        q_ref[h, rows, :] = (qp[:, h * B_QK_PAD:(h + 1) * B_QK_PAD] * mul_q[h % 2]).astype(bf16)

    ckvn_f32 = _rms(ckv, gkv_ref[...])
    ckvn = ckvn_f32.astype(bf16)
    k_nope = jnp.dot(ckvn, wk_ref[...], preferred_element_type=f32)
    k_rope = kra * cos_k + krb * sin_k
    lane = lax.broadcasted_iota(jnp.int32, k_rope.shape, 1)
    for h in range(B_HEADS):
        pair = k_nope[:, (h // 2) * B_QK_PAD:(h // 2 + 1) * B_QK_PAD]
        own_nope = (lane < B_NOPE_DIM) if h % 2 == 0 else (lane >= B_NOPE_DIM)
        k_ref[h, rows, :] = jnp.where(own_nope, pair, k_rope).astype(bf16)
    vt = lax.dot_general(wvt_ref[...], ckvn, _NT, preferred_element_type=f32)
    vt_ref[:, :B_V_DIM, rows] = vt.reshape(B_HEADS, B_V_DIM, t).astype(bf16)
    vt_ref[:, B_V_DIM:, rows] = jnp.ones((B_HEADS, B_V_AUG - B_V_DIM, t), bf16)
    if latent_refs:
        kcat_ref, lat_t_ref = latent_refs
        kcat_ref[rows, :KV_LORA_RANK] = ckvn
        kcat_ref[rows, KV_LORA_RANK:] = k_rope.astype(bf16)
        lat_t_ref[:KV_LORA_RANK, rows] = ckvn_f32.T.astype(bf16)
        lat_t_ref[KV_LORA_RANK:, rows] = jnp.ones((LAT_T_ROWS - KV_LORA_RANK, t), bf16)


def _layer_spec(layer, shape):
    return pl.BlockSpec((None,) + tuple(shape), lambda i: (layer,) + (0,) * len(shape))


def _proj(h, tab, lw, layer, latent):
    rows = h.shape[0]
    t = min(ROW_TILE, rows)
    steps = rows // t
    assert tab.shape[0] % t == 0
    const = functools.partial(_layer_spec, layer)
    latent_specs = [pl.BlockSpec((t, KV_LORA_RANK + B_QK_PAD), lambda i: (i, 0)),
                    pl.BlockSpec((LAT_T_ROWS, t), lambda i: (0, i))] if latent else []
    latent_shapes = [jax.ShapeDtypeStruct((rows, KV_LORA_RANK + B_QK_PAD), bf16),
                     jax.ShapeDtypeStruct((LAT_T_ROWS, rows), bf16)] if latent else []
    return pl.pallas_call(
        _proj_kernel,
        grid=(steps,),
        in_specs=[
            pl.BlockSpec((t, D_MODEL), lambda i: (i, 0)),
            pl.BlockSpec(tab.shape, lambda i: (0, 0)),
            const((1, D_MODEL)),
            const((D_MODEL, _C_END)),
            const((A_KV_HEADS * A_HEAD_DIM, D_MODEL)),
            const((1, Q_LORA_RANK)),
            const((Q_LORA_RANK, B_HEADS * B_QK_PAD)),
            const((1, KV_LORA_RANK)),
            const((KV_LORA_RANK, B_HEADS * B_NOPE_DIM)),
            const((B_WIDTH, KV_LORA_RANK)),
        ],
        out_specs=[
            pl.BlockSpec((A_HEADS, t, A_HEAD_DIM), lambda i: (0, i, 0)),
            pl.BlockSpec((A_KV_HEADS, t, A_HEAD_DIM), lambda i: (0, i, 0)),
            pl.BlockSpec((A_KV_HEADS, A_V_AUG, t), lambda i: (0, 0, i)),
            pl.BlockSpec((B_HEADS, t, B_QK_PAD), lambda i: (0, i, 0)),
            pl.BlockSpec((B_HEADS, t, B_QK_PAD), lambda i: (0, i, 0)),
            pl.BlockSpec((B_HEADS, B_V_AUG, t), lambda i: (0, 0, i)),
        ] + latent_specs,
        out_shape=[
            jax.ShapeDtypeStruct((A_HEADS, rows, A_HEAD_DIM), bf16),
            jax.ShapeDtypeStruct((A_KV_HEADS, rows, A_HEAD_DIM), bf16),
            jax.ShapeDtypeStruct((A_KV_HEADS, A_V_AUG, rows), bf16),
            jax.ShapeDtypeStruct((B_HEADS, rows, B_QK_PAD), bf16),
            jax.ShapeDtypeStruct((B_HEADS, rows, B_QK_PAD), bf16),
            jax.ShapeDtypeStruct((B_HEADS, B_V_AUG, rows), bf16),
        ] + latent_shapes,
        compiler_params=_cparams("arbitrary"),
        name="proj",
    )(h, tab, lw["gin"], lw["w1"], lw["wvat"], lw["gq"], lw["wq"], lw["gkv"], lw["wk"], lw["wvt"])


def _out_kernel(h_ref, ya_ref, yb_ref, gin_ref, wg_ref, na_ref, nb_ref, w_ref, *rest, final):
    if final:
        nf_ref, o_ref = rest
    else:
        (o_ref,) = rest
    t = h_ref.shape[0]
    piece = t if final else max(t // OUT_ROW_SPLIT, min(t, OUT_MIN_PIECE))

    def body(rows):
        hn = h_ref[rows, :]
        u = _rms(hn, gin_ref[...]).astype(bf16)
        ga = _silu(jnp.dot(u, wg_ref[:, :A_WIDTH], preferred_element_type=f32))
        gb = _silu(jnp.dot(u, wg_ref[:, A_WIDTH:], preferred_element_type=f32))
        ya = _rms(ya_ref[rows, :].astype(f32), na_ref[...]) * ga
        yb = _rms(yb_ref[rows, :].astype(f32), nb_ref[...]) * gb
        hn = hn + jnp.dot(ya.astype(bf16), w_ref[:A_WIDTH, :], preferred_element_type=f32)
        return hn + jnp.dot(yb.astype(bf16), w_ref[A_WIDTH:, :], preferred_element_type=f32)

    def tail(rows, hn):
        o_ref[rows, :] = _rms(hn, nf_ref[...]) if final else hn

    pending = None
    for r in range(0, t, piece):
        rows = slice(r, r + piece)
        hn = body(rows)
        if pending is not None:
            tail(*pending)
        pending = (rows, hn)
    tail(*pending)


def _out(h, ya, yb, lw, layer, norm_final=None):
    rows = h.shape[0]
    t = min(ROW_TILE, rows)
    row = lambda width: pl.BlockSpec((t, width), lambda i: (i, 0))
    const = functools.partial(_layer_spec, layer)
    final = norm_final is not None
    in_specs = [row(D_MODEL), row(A_WIDTH), row(B_WIDTH),
                const((1, D_MODEL)), const((D_MODEL, A_WIDTH + B_WIDTH)),
                const((1, A_WIDTH)), const((1, B_WIDTH)), const((A_WIDTH + B_WIDTH, D_MODEL))]
    args = [h, ya, yb, lw["gin"], lw["wg"], lw["na"], lw["nb"], lw["wout"]]
    if final:
        in_specs.append(pl.BlockSpec((1, D_MODEL), lambda i: (0, 0)))
        args.append(norm_final)
    return pl.pallas_call(
        functools.partial(_out_kernel, final=final),
        grid=(rows // t,),
        in_specs=in_specs,
        out_specs=row(D_MODEL),
        out_shape=jax.ShapeDtypeStruct((rows, D_MODEL), f32),
        compiler_params=_cparams("arbitrary"),
        name="outproj",
    )(*args)


def _bias_kernel(tab_ref, idx_ref, o_ref):
    rows = idx_ref.shape[0]
    chunk = math.gcd(rows, BIAS_ROW_CHUNK)

    def body(r, carry):
        sl = pl.ds(pl.multiple_of(r * chunk, chunk), chunk)
        idx = idx_ref[sl, :]
        accs = [jnp.where(idx < 0, NEG, 0.0).astype(f32)] * A_HEADS
        for b in range(N_BUCKETS):
            hit = idx == b
            accs = [jnp.where(hit, tab_ref[b * A_HEADS + h], accs[h]) for h in range(A_HEADS)]
        for h in range(A_HEADS):
            o_ref[h, sl, :] = accs[h]
        return carry

    lax.fori_loop(0, rows // chunk, body, 0)


def _bias_lookup(table_flat, idx):
    return pl.pallas_call(
        _bias_kernel,
        in_specs=[pl.BlockSpec(memory_space=pltpu.SMEM), pl.BlockSpec(idx.shape, lambda: (0, 0))],
        out_specs=pl.BlockSpec((A_HEADS,) + idx.shape, lambda: (0, 0, 0)),
        out_shape=jax.ShapeDtypeStruct((A_HEADS,) + idx.shape, f32),
        compiler_params=pltpu.CompilerParams(vmem_limit_bytes=VMEM_LIMIT_BYTES),
        name="relbias",
    )(table_flat, idx)


def _t5_bucket(rel):
    nb = N_BUCKETS // 2
    max_exact = nb // 2
    ret = jnp.where(rel > 0, nb, 0)
    n = jnp.abs(rel)
    nf = jnp.maximum(n, 1).astype(jnp.float32)
    large = max_exact + (jnp.log(nf / max_exact) / math.log(MAX_DISTANCE / max_exact)
                         * (nb - max_exact)).astype(jnp.int32)
    large = jnp.minimum(large, nb - 1)
    bucket = ret + jnp.where(n < max_exact, n, large)
    return jnp.bitwise_and(bucket, N_BUCKETS - 1)


def _bias_tables(rel_bias_table, nblk):
    table_flat = (rel_bias_table.astype(f32) * LOG2E).reshape(-1)
    i = jnp.arange(BLOCK, dtype=jnp.int32)[:, None]
    j = jnp.arange(3 * BLOCK, dtype=jnp.int32)[None, :]
    rel = j - i - BLOCK
    band = jnp.where(jnp.abs(rel) <= WINDOW, _t5_bucket(rel), -1)
    interior = (_bias_lookup(table_flat, band.T)
                .reshape(A_KV_HEADS, A_GROUP, 3 * BLOCK, BLOCK)
                .transpose(0, 2, 1, 3).reshape(A_KV_HEADS, 3 * BLOCK, A_GROUP * BLOCK))
    key = jnp.arange(3 * BLOCK, dtype=jnp.int32)[None, :, None]
    first = jnp.where(key < BLOCK, NEG, interior)
    last = jnp.where(key >= 2 * BLOCK, NEG, interior)
    bias_band = jnp.stack([first, interior, last])

    n = jnp.arange(nblk, dtype=jnp.int32)[:, None, None]
    k = jnp.arange(N_META, dtype=jnp.int32)[None, :, None]
    qi = jnp.arange(BLOCK, dtype=jnp.int32)[None, None, :]
    rel_m = k - (N_META + n * BLOCK + qi)
    idx_m = _t5_bucket(rel_m).reshape(nblk * N_META, BLOCK)
    bias_meta = (_bias_lookup(table_flat, idx_m)
                 .reshape(A_KV_HEADS, A_GROUP, nblk, N_META, BLOCK)
                 .transpose(2, 0, 3, 1, 4).reshape(nblk, A_KV_HEADS, N_META, A_GROUP * BLOCK))

    qp = jnp.arange(N_META, dtype=jnp.int32)[:, None]
    kp = jnp.arange(N_META + BLOCK, dtype=jnp.int32)[None, :]
    rel_q = kp - qp
    idx_q = jnp.where(jnp.abs(rel_q) <= WINDOW, _t5_bucket(rel_q), -1)
    bias_q = _bias_lookup(table_flat, idx_q)
    return bias_band, bias_meta, bias_q[:, :, :N_META], bias_q[:, :, N_META:]


def _win_kernel(sink_ref, q_ref, kp_ref, kc_ref, kn_ref, vp_ref, vc_ref, vn_ref, kam_ref, vam_ref,
                bias_ref, bm_ref, o_ref):
    w = WIN_BLOCKS
    i, ntile = pl.program_id(1), pl.num_programs(1)

    def qrows(blk):
        return slice(blk * BLOCK, (blk + 1) * BLOCK)

    def band(blk, prev_ref, cur_ref, next_ref, kvh, axis):
        take = lambda ref, lo, hi: ref[kvh, lo:hi, :] if axis == 0 else ref[kvh, :, lo:hi]
        parts = [take(cur_ref, max(blk - 1, 0) * BLOCK, min(blk + 2, w) * BLOCK)]
        if blk == 0:
            parts.insert(0, take(prev_ref, (w - 1) * BLOCK, w * BLOCK))
        if blk == w - 1:
            parts.append(take(next_ref, 0, BLOCK))
        return jnp.concatenate(parts, axis=axis)

    def variant(blk):
        v = 1
        if blk == 0:
            v = jnp.where(i == 0, 0, v)
        if blk == w - 1:
            v = jnp.where(i == ntile - 1, 2, v)
        return v

    def scores(blk, kvh):
        kb = jnp.concatenate([band(blk, kp_ref, kc_ref, kn_ref, kvh, 0), kam_ref[kvh]], axis=0)
        q4 = q_ref[A_GROUP * kvh:A_GROUP * (kvh + 1), qrows(blk), :].reshape(A_GROUP * BLOCK, A_HEAD_DIM)
        bias = jnp.concatenate([bias_ref[variant(blk), kvh], bm_ref[blk, kvh]], axis=0)
        s = lax.dot_general(kb, q4, _NT, preferred_element_type=f32) + bias
        m = jnp.maximum(jnp.max(s, axis=0, keepdims=True), sink_ref[kvh])
        return s, m

    def attend(blk, kvh, s, m):
        vb = jnp.concatenate([band(blk, vp_ref, vc_ref, vn_ref, kvh, 1), vam_ref[0, kvh]], axis=1)
        p = jnp.exp2(s - m).astype(bf16)
        acc = jnp.dot(vb, p, preferred_element_type=f32)
        den = acc[A_HEAD_DIM:A_HEAD_DIM + 1] + jnp.exp2(sink_ref[kvh] - m)
        return acc[:A_HEAD_DIM] * (1.0 / den)

    units = [(blk, kvh) for blk in range(w) for kvh in range(A_KV_HEADS)]
    outs = {}
    pending = [scores(*units[u]) for u in range(WIN_AHEAD)]
    for u, unit in enumerate(units):
        if u + WIN_AHEAD < len(units):
            pending.append(scores(*units[u + WIN_AHEAD]))
        outs[unit] = attend(*unit, *pending.pop(0))
        blk, kvh = unit
        if kvh == A_KV_HEADS - 1:
            ot = jnp.concatenate([outs[(blk, j)][:, g * BLOCK:(g + 1) * BLOCK]
                                  for j in range(A_KV_HEADS) for g in range(A_GROUP)], axis=0)
            o_ref[qrows(blk), :] = ot.T.astype(o_ref.dtype)


def _win(sink_rows, qa, ka, vat, kam, vatm, bias_band, bias_meta, batch, seq):
    tb = WIN_BLOCKS * BLOCK
    ntile = seq // tb
    prev_ = lambda b, i: b * ntile + jnp.maximum(i - 1, 0)
    cur_ = lambda b, i: b * ntile + i
    next_ = lambda b, i: b * ntile + jnp.minimum(i + 1, ntile - 1)
    kspec = lambda f: pl.BlockSpec((A_KV_HEADS, tb, A_HEAD_DIM), lambda b, i: (0, f(b, i), 0))
    vspec = lambda f: pl.BlockSpec((A_KV_HEADS, A_V_AUG, tb), lambda b, i: (0, 0, f(b, i)))
    return pl.pallas_call(
        _win_kernel,
        grid=(batch, ntile),
        in_specs=[
            pl.BlockSpec((A_KV_HEADS, 1, A_GROUP * BLOCK), lambda b, i: (0, 0, 0)),
            pl.BlockSpec((A_HEADS, tb, A_HEAD_DIM), lambda b, i: (0, cur_(b, i), 0)),
            kspec(prev_), kspec(cur_), kspec(next_),
            vspec(prev_), vspec(cur_), vspec(next_),
            pl.BlockSpec((A_KV_HEADS, N_META, A_HEAD_DIM), lambda b, i: (0, b, 0)),
            pl.BlockSpec((1, A_KV_HEADS, A_V_AUG, N_META), lambda b, i: (b, 0, 0, 0)),
            pl.BlockSpec((3, A_KV_HEADS, 3 * BLOCK, A_GROUP * BLOCK), lambda b, i: (0, 0, 0, 0)),
            pl.BlockSpec((WIN_BLOCKS, A_KV_HEADS, N_META, A_GROUP * BLOCK), lambda b, i: (i, 0, 0, 0)),
        ],
        out_specs=pl.BlockSpec((tb, A_WIDTH), lambda b, i: (cur_(b, i), 0)),
        out_shape=jax.ShapeDtypeStruct((batch * seq, A_WIDTH), bf16),
        compiler_params=_cparams("arbitrary", "arbitrary"),
        name="win",
    )(sink_rows, qa, ka, ka, ka, vat, vat, vat, kam, vatm, bias_band, bias_meta)


def _win_meta_kernel(sink_ref, q_ref, ka_ref, vat_ref, kam_ref, vatm_ref, biasm_ref, biasr_ref, o_ref):
    for kvh in range(A_KV_HEADS):
        q4 = q_ref[A_GROUP * kvh:A_GROUP * (kvh + 1)].reshape(A_GROUP * N_META, A_HEAD_DIM)
        sm = lax.dot_general(q4, kam_ref[kvh], _NT, preferred_element_type=f32)
        sr = lax.dot_general(q4, ka_ref[kvh], _NT, preferred_element_type=f32)
        for g in range(A_GROUP):
            h = A_GROUP * kvh + g
            rows = slice(g * N_META, (g + 1) * N_META)
            smg = sm[rows] + biasm_ref[h]
            srg = sr[rows] + biasr_ref[h]
            sink = sink_ref[h]
            m = jnp.maximum(jnp.maximum(jnp.max(smg, axis=-1, keepdims=True),
                                        jnp.max(srg, axis=-1, keepdims=True)), sink)
            pm = jnp.exp2(smg - m).astype(bf16)
            pr = jnp.exp2(srg - m).astype(bf16)
            o = (lax.dot_general(pm, vatm_ref[0, kvh], _NT, preferred_element_type=f32)
                 + lax.dot_general(pr, vat_ref[kvh], _NT, preferred_element_type=f32))
            den = o[:, A_HEAD_DIM:A_HEAD_DIM + 1] + jnp.exp2(sink - m)
            o_ref[:, h * A_HEAD_DIM:(h + 1) * A_HEAD_DIM] = (o[:, :A_HEAD_DIM] * (1.0 / den)).astype(o_ref.dtype)


def _win_meta(sink, qam, ka, vat, kam, vatm, bias_mm, bias_mr, batch, seq):
    nblk = seq // BLOCK
    const3 = lambda shape: pl.BlockSpec(shape, lambda b: (0, 0, 0))
    return pl.pallas_call(
        _win_meta_kernel,
        grid=(batch,),
        in_specs=[
            pl.BlockSpec(memory_space=pltpu.SMEM),
            pl.BlockSpec((A_HEADS, N_META, A_HEAD_DIM), lambda b: (0, b, 0)),
            pl.BlockSpec((A_KV_HEADS, BLOCK, A_HEAD_DIM), lambda b: (0, b * nblk, 0)),
            pl.BlockSpec((A_KV_HEADS, A_V_AUG, BLOCK), lambda b: (0, 0, b * nblk)),
            pl.BlockSpec((A_KV_HEADS, N_META, A_HEAD_DIM), lambda b: (0, b, 0)),
            pl.BlockSpec((1, A_KV_HEADS, A_V_AUG, N_META), lambda b: (b, 0, 0, 0)),
            const3((A_HEADS, N_META, N_META)),
            const3((A_HEADS, N_META, BLOCK)),
        ],
        out_specs=pl.BlockSpec((N_META, A_WIDTH), lambda b: (b, 0)),
        out_shape=jax.ShapeDtypeStruct((batch * N_META, A_WIDTH), bf16),
        compiler_params=_cparams("arbitrary"),
        name="win_meta",
    )(sink, qam, ka, vat, kam, vatm, bias_mm, bias_mr)


def _mla_kernel(q_ref, k_ref, km_ref, vt_ref, vtm_ref, o_ref, acc_ref, s0_ref, s1_ref, sm0_ref, sm1_ref):
    s_bufs, sm_bufs = (s0_ref, s1_ref), (sm0_ref, sm1_ref)
    nsub, _, _, tq = acc_ref.shape
    units = nsub * B_HEADS
    seq = k_ref.shape[1]
    chunks = [(c * MLA_KEY_CHUNK, (c + 1) * MLA_KEY_CHUNK) for c in range(seq // MLA_KEY_CHUNK)]

    def unit(u):
        return u % B_HEADS, u // B_HEADS

    def step(t, slot, m, *, score_next=True, attend_cur=True):
        if score_next:
            hn, subn = unit(t + 1)
            q = q_ref[hn, pl.ds(pl.multiple_of(subn * tq, tq), tq), :]
            sm = lax.dot_general(km_ref[hn], q, _NT, preferred_element_type=f32)
            sm_bufs[1 - slot][...] = sm
            m_next = jnp.max(sm, axis=0, keepdims=True)
        if attend_cur:
            h, sub = unit(t)
            pm = jnp.exp2(sm_bufs[slot][...] - m).astype(bf16)
            acc = jnp.dot(vtm_ref[0, h], pm, preferred_element_type=f32)
        for lo, hi in chunks:
            if score_next:
                s = lax.dot_general(k_ref[hn, lo:hi, :], q, _NT, preferred_element_type=f32)
                s_bufs[1 - slot][lo:hi, :] = s
                m_next = jnp.maximum(m_next, jnp.max(s, axis=0, keepdims=True))
            if attend_cur:
                p = jnp.exp2(s_bufs[slot][lo:hi, :] - m).astype(bf16)
                acc = acc + jnp.dot(vt_ref[h, :, lo:hi], p, preferred_element_type=f32)
        if attend_cur:
            acc_ref[sub, h] = acc[:B_V_DIM] * (1.0 / acc[B_V_DIM:B_V_DIM + 1])
        return m_next if score_next else None

    def steps(j, m):
        for i in range(MLA_STEPS_PER_ITER):
            m = step(MLA_STEPS_PER_ITER * j + i, i % 2, m)
        return m

    assert MLA_STEPS_PER_ITER % 2 == 0 and units % 2 == 0
    iters = (units - 1) // MLA_STEPS_PER_ITER
    m = step(-1, 1, None, attend_cur=False)
    m = lax.fori_loop(0, iters, steps, m)
    for t in range(iters * MLA_STEPS_PER_ITER, units - 1):
        m = step(t, t % 2, m)
    step(units - 1, 1, m, score_next=False)
    for sub in range(nsub):
        o_ref[sub * tq:(sub + 1) * tq, :] = acc_ref[sub].reshape(B_WIDTH, tq).T.astype(o_ref.dtype)


def _mla(q, k, km, vt, vtm, batch, seq):
    tq = MLA_Q_TILE
    tb = MLA_Q_TILE * MLA_SUBTILES
    nq = seq // tb
    return pl.pallas_call(
        _mla_kernel,
        grid=(batch, nq),
        in_specs=[
            pl.BlockSpec((B_HEADS, tb, B_QK_PAD), lambda b, i: (0, b * nq + i, 0)),
            pl.BlockSpec((B_HEADS, seq, B_QK_PAD), lambda b, i: (0, b, 0)),
            pl.BlockSpec((B_HEADS, N_META, B_QK_PAD), lambda b, i: (0, b, 0)),
            pl.BlockSpec((B_HEADS, B_V_AUG, seq), lambda b, i: (0, 0, b)),
            pl.BlockSpec((1, B_HEADS, B_V_AUG, N_META), lambda b, i: (b, 0, 0, 0)),
        ],
        out_specs=pl.BlockSpec((tb, B_WIDTH), lambda b, i: (b * nq + i, 0)),
        out_shape=jax.ShapeDtypeStruct((batch * seq, B_WIDTH), bf16),
        scratch_shapes=([pltpu.VMEM((MLA_SUBTILES, B_HEADS, B_V_DIM, tq), f32)]
                        + [pltpu.VMEM((seq, tq), f32)] * 2 + [pltpu.VMEM((N_META, tq), f32)] * 2),
        compiler_params=_cparams("arbitrary", "arbitrary"),
        name="mla",
    )(q, k, km, vt, vtm)


def _mla_meta_kernel(q_ref, kcat_ref, kcatm_ref, lat_t_ref, lat_tm_ref, wuk_ref, wuv_ref, o_ref):
    lane = lax.broadcasted_iota(jnp.int32, (N_META, B_QK_PAD), 1)
    q_rows = []
    for h in range(B_HEADS):
        q = q_ref[h]
        rope_lanes = (lane >= B_NOPE_DIM) if h % 2 == 0 else (lane < B_NOPE_DIM)
        q_lat = jnp.dot(q, wuk_ref[h], preferred_element_type=f32).astype(bf16)
        q_rows.append(jnp.concatenate([q_lat, jnp.where(rope_lanes, q, jnp.zeros_like(q))], axis=1))
    qcat = jnp.concatenate(q_rows, axis=0)
    seq = kcat_ref.shape[0]
    chunks = [(c * MLA_KEY_CHUNK, (c + 1) * MLA_KEY_CHUNK) for c in range(seq // MLA_KEY_CHUNK)]
    sm = lax.dot_general(kcatm_ref[...], qcat, _NT, preferred_element_type=f32)
    ss = [lax.dot_general(kcat_ref[lo:hi, :], qcat, _NT, preferred_element_type=f32) for lo, hi in chunks]
    m = jnp.max(sm, axis=0, keepdims=True)
    for s in ss:
        m = jnp.maximum(m, jnp.max(s, axis=0, keepdims=True))
    acc = jnp.dot(lat_tm_ref[0], jnp.exp2(sm - m).astype(bf16), preferred_element_type=f32)
    for (lo, hi), s in zip(chunks, ss):
        acc = acc + jnp.dot(lat_t_ref[:, lo:hi], jnp.exp2(s - m).astype(bf16),
                            preferred_element_type=f32)
    o_lat = (acc[:KV_LORA_RANK] * (1.0 / acc[KV_LORA_RANK:KV_LORA_RANK + 1])).T.astype(bf16)
    for h in range(B_HEADS):
        o_ref[:, h * B_V_DIM:(h + 1) * B_V_DIM] = jnp.dot(
            o_lat[h * N_META:(h + 1) * N_META], wuv_ref[h], preferred_element_type=f32).astype(o_ref.dtype)


def _mla_meta(qm, kcat, kcatm, lat_t, lat_tm, lw, layer, batch, seq):
    const = functools.partial(_layer_spec, layer)
    return pl.pallas_call(
        _mla_meta_kernel,
        grid=(batch,),
        in_specs=[
            pl.BlockSpec((B_HEADS, N_META, B_QK_PAD), lambda b: (0, b, 0)),
            pl.BlockSpec((seq, KV_LORA_RANK + B_QK_PAD), lambda b: (b, 0)),
            pl.BlockSpec((N_META, KV_LORA_RANK + B_QK_PAD), lambda b: (b, 0)),
            pl.BlockSpec((LAT_T_ROWS, seq), lambda b: (0, b)),
            pl.BlockSpec((1, LAT_T_ROWS, N_META), lambda b: (b, 0, 0)),
            const((B_HEADS, B_QK_PAD, KV_LORA_RANK)),
            const((B_HEADS, KV_LORA_RANK, B_V_DIM)),
        ],
        out_specs=pl.BlockSpec((N_META, B_WIDTH), lambda b: (b, 0)),
        out_shape=jax.ShapeDtypeStruct((batch * N_META, B_WIDTH), bf16),
        compiler_params=_cparams("arbitrary"),
        name="mla_meta",
    )(qm, kcat, kcatm, lat_t, lat_tm, lw["wuk"], lw["wuv"])


def _rope_tables(pos):
    half = B_ROPE_DIM // 2
    freqs = ROPE_THETA ** (-jnp.arange(half, dtype=jnp.float32) / half)
    lane_freq = jnp.concatenate([freqs] * (B_QK_PAD // half))
    ang = pos.astype(jnp.float32)[:, None] * lane_freq[None, :]
    cos, sin = jnp.cos(ang), jnp.sin(ang)
    lane = jnp.arange(B_QK_PAD)[None, :]
    c_mla = (B_NOPE_DIM + B_ROPE_DIM) ** -0.5 * LOG2E
    rope_pair = jnp.where(lane % (2 * B_ROPE_DIM) < B_ROPE_DIM, c_mla * cos, c_mla * sin)
    mul_q_even = jnp.where(lane < B_NOPE_DIM, c_mla, rope_pair)
    mul_q_odd = jnp.where(lane < 2 * B_ROPE_DIM, rope_pair, c_mla)
    return jnp.concatenate([mul_q_even, mul_q_odd, cos, sin], axis=1).astype(f32)


def _rot_cols(w):
    half = w.shape[-1] // 2
    return jnp.concatenate([-w[..., half:], w[..., :half]], axis=-1)


def _stacked_weights(norm_in, w_in, norm_q_lat, w_uq, norm_kv_lat, w_ukv, norm_out_a, norm_out_b, w_out):
    depth, d, _ = w_in.shape
    sizes = (A_WIDTH, A_KV_HEADS * A_HEAD_DIM, A_KV_HEADS * A_HEAD_DIM, A_WIDTH,
             Q_LORA_RANK, KV_LORA_RANK, B_ROPE_DIM, B_WIDTH)
    offs = [sum(sizes[:i]) for i in range(len(sizes) + 1)]
    qa, ka, va, ga, cq, ckv, kr, gb = (w_in[..., offs[i]:offs[i + 1]] for i in range(len(sizes)))
    zeros = lambda *shape: jnp.zeros(shape, w_in.dtype)
    reps = B_QK_PAD // B_ROPE_DIM
    kr_groups = jnp.concatenate([kr] * reps + [_rot_cols(kr)] * reps, axis=-1)
    w1 = jnp.concatenate([qa, cq, ckv, ka, kr_groups], axis=-1).astype(bf16)

    r = w_uq.shape[1]
    uq = w_uq.reshape(depth, r, B_HEADS // 2, 2, B_NOPE_DIM + B_ROPE_DIM)
    nope, rope = uq[..., :B_NOPE_DIM], uq[..., B_NOPE_DIM:]
    rope2 = jnp.concatenate([rope, _rot_cols(rope)], axis=-1)
    wq = jnp.stack([jnp.concatenate([nope[..., 0, :], rope2[..., 0, :]], axis=-1),
                    jnp.concatenate([rope2[..., 1, :], nope[..., 1, :]], axis=-1)], axis=3)

    rk = w_ukv.shape[1]
    ukv = w_ukv.reshape(depth, rk, B_HEADS, B_NOPE_DIM + B_V_DIM)
    k_nope, v = ukv[..., :B_NOPE_DIM], ukv[..., B_NOPE_DIM:]
    uk_t = jnp.transpose(k_nope, (0, 2, 3, 1)).reshape(depth, B_HEADS // 2, 2, B_NOPE_DIM, rk)
    no_lanes = zeros(depth, B_HEADS // 2, B_QK_PAD - B_NOPE_DIM, rk)
    wuk = jnp.stack([jnp.concatenate([uk_t[:, :, 0], no_lanes], axis=2),
                     jnp.concatenate([no_lanes, uk_t[:, :, 1]], axis=2)], axis=2)
    row = lambda g: g.astype(f32)[:, None, :]
    return {
        "gin": row(norm_in),
        "w1": w1,
        "wg": jnp.concatenate([ga, gb], axis=-1).astype(bf16),
        "wvat": jnp.swapaxes(va.astype(bf16), 1, 2),
        "gq": row(norm_q_lat),
        "wq": wq.reshape(depth, r, B_HEADS * B_QK_PAD).astype(bf16),
        "gkv": row(norm_kv_lat),
        "wk": k_nope.reshape(depth, rk, B_HEADS * B_NOPE_DIM).astype(bf16),
        "wvt": jnp.swapaxes(v.reshape(depth, rk, B_WIDTH).astype(bf16), 1, 2),
        "wuk": wuk.reshape(depth, B_HEADS, B_QK_PAD, rk).astype(bf16),
        "wuv": jnp.transpose(v, (0, 2, 1, 3)).astype(bf16),
        "na": row(norm_out_a),
        "nb": row(norm_out_b),
        "wout": w_out.astype(bf16),
    }


def kernel(x, meta_tokens, rel_bias_table, norm_in, w_in, sink_a, norm_q_lat, w_uq, norm_kv_lat, w_ukv,
           norm_out_a, norm_out_b, w_out, norm_final):
    batch, seq, d = x.shape
    depth = w_in.shape[0]
    assert d == D_MODEL and seq % (MLA_Q_TILE * MLA_SUBTILES) == 0 and seq % ROW_TILE == 0
    assert B_QK_PAD == 2 * B_NOPE_DIM and B_HEADS % 2 == 0

    h_real = x.reshape(batch * seq, d).astype(f32)
    h_meta = jnp.tile(meta_tokens.astype(f32), (batch, 1))
    tab_real = _rope_tables(N_META + jnp.arange(seq))
    tab_meta = _rope_tables(jnp.arange(batch * N_META) % N_META)
    bias_band, bias_meta, bias_mm, bias_mr = _bias_tables(rel_bias_table, seq // BLOCK)

    lw = _stacked_weights(norm_in, w_in, norm_q_lat, w_uq, norm_kv_lat, w_ukv, norm_out_a, norm_out_b, w_out)
    sinks = sink_a.astype(f32) * LOG2E
    sinks_rows = jnp.repeat(sinks.reshape(depth, A_KV_HEADS, A_GROUP), BLOCK, axis=2)[:, :, None, :]

    out = None
    for i in range(depth):
        last = i == depth - 1
        sink, sink_rows = sinks[i], sinks_rows[i]
        qa, ka, vat, q, k, vt, *lat = _proj(h_real, tab_real, lw, i, latent=not last)
        qam, kam, vatm, qm, km, vtm, *latm = _proj(h_meta, tab_meta, lw, i, latent=not last)
        vtm_b = vtm.reshape(B_HEADS, B_V_AUG, batch, N_META).transpose(2, 0, 1, 3)
        vatm_b = vatm.reshape(A_KV_HEADS, A_V_AUG, batch, N_META).transpose(2, 0, 1, 3)

        ya = _win(sink_rows, qa, ka, vat, kam, vatm_b, bias_band, bias_meta, batch, seq)
        yb = _mla(q, k, km, vt, vtm_b, batch, seq)
        if last:
            out = _out(h_real, ya, yb, lw, i, norm_final.astype(f32)[None, :])
        else:
            yam = _win_meta(sink, qam, ka, vat, kam, vatm_b, bias_mm, bias_mr, batch, seq)
            (kcat, lat_t), (kcatm, lat_tm) = lat, latm
            lat_tm_b = lat_tm.reshape(LAT_T_ROWS, batch, N_META).transpose(1, 0, 2)
            ybm = _mla_meta(qm, kcat, kcatm, lat_t, lat_tm_b, lw, i, batch, seq)
            h_real = _out(h_real, ya, yb, lw, i)
            h_meta = _out(h_meta, yam, ybm, lw, i)
    return out.reshape(batch, seq, d).astype(x.dtype)
```

```python
import functools
import math

import jax
import jax.numpy as jnp
from jax import lax
from jax.experimental import pallas as pl
from jax.experimental.pallas import tpu as pltpu

D_MODEL = 1024
N_META = 16
BLOCK = 128
WINDOW = 128
A_HEADS = 8
A_KV_HEADS = 2
A_GROUP = A_HEADS // A_KV_HEADS
A_HEAD_DIM = 64
A_WIDTH = A_HEADS * A_HEAD_DIM
B_HEADS = 8
B_NOPE_DIM = 64
B_ROPE_DIM = 32
B_V_DIM = 64
B_WIDTH = B_HEADS * B_V_DIM
B_QK_PAD = B_NOPE_DIM + 2 * B_ROPE_DIM
BF16_SUBLANES = 16
B_V_AUG = B_V_DIM + BF16_SUBLANES
A_V_AUG = A_HEAD_DIM + BF16_SUBLANES
Q_LORA_RANK = 256
KV_LORA_RANK = 128
LAT_T_ROWS = KV_LORA_RANK + BF16_SUBLANES
N_BUCKETS = 32
MAX_DISTANCE = 128
ROPE_THETA = 10000.0
EPS = 1e-6

LOG2E = math.log2(math.e)
NEG = -1e30
ROW_TILE = 1024
OUT_ROW_SPLIT = 4
OUT_MIN_PIECE = 256
MLA_Q_TILE = 256
MLA_SUBTILES = 8
MLA_STEPS_PER_ITER = 8
MLA_KEY_CHUNK = 1024
WIN_BLOCKS = 16
WIN_AHEAD = 2
BIAS_ROW_CHUNK = 32
VMEM_LIMIT_BYTES = 56 * 1024 * 1024

_C_QA = 0
_C_CQ = _C_QA + A_WIDTH
_C_CKV = _C_CQ + Q_LORA_RANK
_C_KA = _C_CKV + KV_LORA_RANK
_C_KRA = _C_KA + A_KV_HEADS * A_HEAD_DIM
_C_KRB = _C_KRA + B_QK_PAD
_C_END = _C_KRB + B_QK_PAD

_NT = (((1,), (1,)), ((), ()))

bf16 = jnp.bfloat16
f32 = jnp.float32


def _cparams(*sem, flags=None):
    return pltpu.CompilerParams(dimension_semantics=sem, vmem_limit_bytes=VMEM_LIMIT_BYTES, flags=flags)


def _rms(x, gain):
    return x * lax.rsqrt(jnp.mean(x * x, axis=-1, keepdims=True) + EPS) * gain


def _silu(x):
    return x / (1.0 + jnp.exp(-x))


def _proj_kernel(h_ref, tab_ref, gin_ref, w1_ref, wvat_ref, gq_ref, wq_ref, gkv_ref, wk_ref, wvt_ref,
                 qa_ref, ka_ref, vat_ref, q_ref, k_ref, vt_ref, *latent_refs):
    t = h_ref.shape[0]
    rows = slice(0, t)
    tab_rows = pl.ds(pl.multiple_of((pl.program_id(0) % (tab_ref.shape[0] // t)) * t, t), t)
    u = _rms(h_ref[rows, :], gin_ref[...]).astype(bf16)

    def mm(lo, hi):
        return jnp.dot(u, w1_ref[:, lo:hi], preferred_element_type=f32)

    qa = mm(_C_QA, _C_CQ) * (A_HEAD_DIM ** -0.5 * LOG2E)
    for h in range(A_HEADS):
        qa_ref[h, rows, :] = qa[:, h * A_HEAD_DIM:(h + 1) * A_HEAD_DIM].astype(bf16)
    vat = lax.dot_general(wvat_ref[...], u, _NT, preferred_element_type=f32)
    vat_ref[:, :A_HEAD_DIM, rows] = vat.reshape(A_KV_HEADS, A_HEAD_DIM, t).astype(bf16)
    vat_ref[:, A_HEAD_DIM:, rows] = jnp.ones((A_KV_HEADS, A_V_AUG - A_HEAD_DIM, t), bf16)
    mixed = mm(_C_CQ, _C_END)
    cq = mixed[:, :_C_CKV - _C_CQ]
    ckv = mixed[:, _C_CKV - _C_CQ:_C_KA - _C_CQ]
    ka = mixed[:, _C_KA - _C_CQ:_C_KRA - _C_CQ]
    kra = mixed[:, _C_KRA - _C_CQ:_C_KRB - _C_CQ]
    krb = mixed[:, _C_KRB - _C_CQ:]
    for j in range(A_KV_HEADS):
        ka_ref[j, rows, :] = ka[:, j * A_HEAD_DIM:(j + 1) * A_HEAD_DIM].astype(bf16)
    mul_q = (tab_ref[tab_rows, 0 * B_QK_PAD:1 * B_QK_PAD],
             tab_ref[tab_rows, 1 * B_QK_PAD:2 * B_QK_PAD])
    cos_k = tab_ref[tab_rows, 2 * B_QK_PAD:3 * B_QK_PAD]
    sin_k = tab_ref[tab_rows, 3 * B_QK_PAD:4 * B_QK_PAD]

    cqn = _rms(cq, gq_ref[...]).astype(bf16)
    qp = jnp.dot(cqn, wq_ref[...], preferred_element_type=f32)
    for h in range(B_HEADS):
        q_ref[h, rows, :] = (qp[:, h * B_QK_PAD:(h + 1) * B_QK_PAD] * mul_q[h % 2]).astype(bf16)

    ckvn_f32 = _rms(ckv, gkv_ref[...])
    ckvn = ckvn_f32.astype(bf16)
    k_nope = jnp.dot(ckvn, wk_ref[...], preferred_element_type=f32)
    k_rope = kra * cos_k + krb * sin_k
    lane = lax.broadcasted_iota(jnp.int32, k_rope.shape, 1)
    for h in range(B_HEADS):
        pair = k_nope[:, (h // 2) * B_QK_PAD:(h // 2 + 1) * B_QK_PAD]
        own_nope = (lane < B_NOPE_DIM) if h % 2 == 0 else (lane >= B_NOPE_DIM)
        k_ref[h, rows, :] = jnp.where(own_nope, pair, k_rope).astype(bf16)
    vt = lax.dot_general(wvt_ref[...], ckvn, _NT, preferred_element_type=f32)
    vt_ref[:, :B_V_DIM, rows] = vt.reshape(B_HEADS, B_V_DIM, t).astype(bf16)
    vt_ref[:, B_V_DIM:, rows] = jnp.ones((B_HEADS, B_V_AUG - B_V_DIM, t), bf16)
    if latent_refs:
        kcat_ref, lat_t_ref = latent_refs
        kcat_ref[rows, :KV_LORA_RANK] = ckvn
        kcat_ref[rows, KV_LORA_RANK:] = k_rope.astype(bf16)
        lat_t_ref[:KV_LORA_RANK, rows] = ckvn_f32.T.astype(bf16)
        lat_t_ref[KV_LORA_RANK:, rows] = jnp.ones((LAT_T_ROWS - KV_LORA_RANK, t), bf16)


def _layer_spec(layer, shape):
    return pl.BlockSpec((None,) + tuple(shape), lambda i: (layer,) + (0,) * len(shape))


def _proj(h, tab, lw, layer, latent):
    rows = h.shape[0]
    t = min(ROW_TILE, rows)
    steps = rows // t
    assert tab.shape[0] % t == 0
    const = functools.partial(_layer_spec, layer)
    latent_specs = [pl.BlockSpec((t, KV_LORA_RANK + B_QK_PAD), lambda i: (i, 0)),
                    pl.BlockSpec((LAT_T_ROWS, t), lambda i: (0, i))] if latent else []
    latent_shapes = [jax.ShapeDtypeStruct((rows, KV_LORA_RANK + B_QK_PAD), bf16),
                     jax.ShapeDtypeStruct((LAT_T_ROWS, rows), bf16)] if latent else []
    return pl.pallas_call(
        _proj_kernel,
        grid=(steps,),
        in_specs=[
            pl.BlockSpec((t, D_MODEL), lambda i: (i, 0)),
            pl.BlockSpec(tab.shape, lambda i: (0, 0)),
            const((1, D_MODEL)),
            const((D_MODEL, _C_END)),
            const((A_KV_HEADS * A_HEAD_DIM, D_MODEL)),
            const((1, Q_LORA_RANK)),
            const((Q_LORA_RANK, B_HEADS * B_QK_PAD)),
            const((1, KV_LORA_RANK)),
            const((KV_LORA_RANK, B_HEADS * B_NOPE_DIM)),
            const((B_WIDTH, KV_LORA_RANK)),
        ],
        out_specs=[
            pl.BlockSpec((A_HEADS, t, A_HEAD_DIM), lambda i: (0, i, 0)),
            pl.BlockSpec((A_KV_HEADS, t, A_HEAD_DIM), lambda i: (0, i, 0)),
            pl.BlockSpec((A_KV_HEADS, A_V_AUG, t), lambda i: (0, 0, i)),
            pl.BlockSpec((B_HEADS, t, B_QK_PAD), lambda i: (0, i, 0)),
            pl.BlockSpec((B_HEADS, t, B_QK_PAD), lambda i: (0, i, 0)),
            pl.BlockSpec((B_HEADS, B_V_AUG, t), lambda i: (0, 0, i)),
        ] + latent_specs,
        out_shape=[
            jax.ShapeDtypeStruct((A_HEADS, rows, A_HEAD_DIM), bf16),
            jax.ShapeDtypeStruct((A_KV_HEADS, rows, A_HEAD_DIM), bf16),
            jax.ShapeDtypeStruct((A_KV_HEADS, A_V_AUG, rows), bf16),
            jax.ShapeDtypeStruct((B_HEADS, rows, B_QK_PAD), bf16),
            jax.ShapeDtypeStruct((B_HEADS, rows, B_QK_PAD), bf16),
            jax.ShapeDtypeStruct((B_HEADS, B_V_AUG, rows), bf16),
        ] + latent_shapes,
        compiler_params=_cparams("arbitrary"),
        name="proj",
    )(h, tab, lw["gin"], lw["w1"], lw["wvat"], lw["gq"], lw["wq"], lw["gkv"], lw["wk"], lw["wvt"])


def _out_kernel(h_ref, ya_ref, yb_ref, gin_ref, wg_ref, na_ref, nb_ref, w_ref, *rest, final):
    if final:
        nf_ref, o_ref = rest
    else:
        (o_ref,) = rest
    t = h_ref.shape[0]
    piece = t if final else max(t // OUT_ROW_SPLIT, min(t, OUT_MIN_PIECE))

    def body(rows):
        hn = h_ref[rows, :]
        u = _rms(hn, gin_ref[...]).astype(bf16)
        ga = _silu(jnp.dot(u, wg_ref[:, :A_WIDTH], preferred_element_type=f32))
        gb = _silu(jnp.dot(u, wg_ref[:, A_WIDTH:], preferred_element_type=f32))
        ya = _rms(ya_ref[rows, :].astype(f32), na_ref[...]) * ga
        yb = _rms(yb_ref[rows, :].astype(f32), nb_ref[...]) * gb
        hn = hn + jnp.dot(ya.astype(bf16), w_ref[:A_WIDTH, :], preferred_element_type=f32)
        return hn + jnp.dot(yb.astype(bf16), w_ref[A_WIDTH:, :], preferred_element_type=f32)

    def tail(rows, hn):
        o_ref[rows, :] = _rms(hn, nf_ref[...]) if final else hn

    pending = None
    for r in range(0, t, piece):
        rows = slice(r, r + piece)
        hn = body(rows)
        if pending is not None:
            tail(*pending)
        pending = (rows, hn)
    tail(*pending)


def _out(h, ya, yb, lw, layer, norm_final=None):
    rows = h.shape[0]
    t = min(ROW_TILE, rows)
    row = lambda width: pl.BlockSpec((t, width), lambda i: (i, 0))
    const = functools.partial(_layer_spec, layer)
    final = norm_final is not None
    in_specs = [row(D_MODEL), row(A_WIDTH), row(B_WIDTH),
                const((1, D_MODEL)), const((D_MODEL, A_WIDTH + B_WIDTH)),
                const((1, A_WIDTH)), const((1, B_WIDTH)), const((A_WIDTH + B_WIDTH, D_MODEL))]
    args = [h, ya, yb, lw["gin"], lw["wg"], lw["na"], lw["nb"], lw["wout"]]
    if final:
        in_specs.append(pl.BlockSpec((1, D_MODEL), lambda i: (0, 0)))
        args.append(norm_final)
    return pl.pallas_call(
        functools.partial(_out_kernel, final=final),
        grid=(rows // t,),
        in_specs=in_specs,
        out_specs=row(D_MODEL),
        out_shape=jax.ShapeDtypeStruct((rows, D_MODEL), f32),
        compiler_params=_cparams("arbitrary"),
        name="outproj",
    )(*args)


def _bias_kernel(tab_ref, idx_ref, o_ref):
    rows = idx_ref.shape[0]
    chunk = math.gcd(rows, BIAS_ROW_CHUNK)

    def body(r, carry):
        sl = pl.ds(pl.multiple_of(r * chunk, chunk), chunk)
        idx = idx_ref[sl, :]
        accs = [jnp.where(idx < 0, NEG, 0.0).astype(f32)] * A_HEADS
        for b in range(N_BUCKETS):
            hit = idx == b
            accs = [jnp.where(hit, tab_ref[b * A_HEADS + h], accs[h]) for h in range(A_HEADS)]
        for h in range(A_HEADS):
            o_ref[h, sl, :] = accs[h]
        return carry

    lax.fori_loop(0, rows // chunk, body, 0)


def _bias_lookup(table_flat, idx):
    return pl.pallas_call(
        _bias_kernel,
        in_specs=[pl.BlockSpec(memory_space=pltpu.SMEM), pl.BlockSpec(idx.shape, lambda: (0, 0))],
        out_specs=pl.BlockSpec((A_HEADS,) + idx.shape, lambda: (0, 0, 0)),
        out_shape=jax.ShapeDtypeStruct((A_HEADS,) + idx.shape, f32),
        compiler_params=pltpu.CompilerParams(vmem_limit_bytes=VMEM_LIMIT_BYTES),
        name="relbias",
    )(table_flat, idx)


def _t5_bucket(rel):
    nb = N_BUCKETS // 2
    max_exact = nb // 2
    ret = jnp.where(rel > 0, nb, 0)
    n = jnp.abs(rel)
    nf = jnp.maximum(n, 1).astype(jnp.float32)
    large = max_exact + (jnp.log(nf / max_exact) / math.log(MAX_DISTANCE / max_exact)
                         * (nb - max_exact)).astype(jnp.int32)
    large = jnp.minimum(large, nb - 1)
    bucket = ret + jnp.where(n < max_exact, n, large)
    return jnp.bitwise_and(bucket, N_BUCKETS - 1)


def _bias_tables(rel_bias_table, nblk):
    table_flat = (rel_bias_table.astype(f32) * LOG2E).reshape(-1)
    i = jnp.arange(BLOCK, dtype=jnp.int32)[:, None]
    j = jnp.arange(3 * BLOCK, dtype=jnp.int32)[None, :]
    rel = j - i - BLOCK
    band = jnp.where(jnp.abs(rel) <= WINDOW, _t5_bucket(rel), -1)
    interior = (_bias_lookup(table_flat, band.T)
                .reshape(A_KV_HEADS, A_GROUP, 3 * BLOCK, BLOCK)
                .transpose(0, 2, 1, 3).reshape(A_KV_HEADS, 3 * BLOCK, A_GROUP * BLOCK))
    key = jnp.arange(3 * BLOCK, dtype=jnp.int32)[None, :, None]
    first = jnp.where(key < BLOCK, NEG, interior)
    last = jnp.where(key >= 2 * BLOCK, NEG, interior)
    bias_band = jnp.stack([first, interior, last])

    n = jnp.arange(nblk, dtype=jnp.int32)[:, None, None]
    k = jnp.arange(N_META, dtype=jnp.int32)[None, :, None]
    qi = jnp.arange(BLOCK, dtype=jnp.int32)[None, None, :]
    rel_m = k - (N_META + n * BLOCK + qi)
    idx_m = _t5_bucket(rel_m).reshape(nblk * N_META, BLOCK)
    bias_meta = (_bias_lookup(table_flat, idx_m)
                 .reshape(A_KV_HEADS, A_GROUP, nblk, N_META, BLOCK)
                 .transpose(2, 0, 3, 1, 4).reshape(nblk, A_KV_HEADS, N_META, A_GROUP * BLOCK))

    qp = jnp.arange(N_META, dtype=jnp.int32)[:, None]
    kp = jnp.arange(N_META + BLOCK, dtype=jnp.int32)[None, :]
    rel_q = kp - qp
    idx_q = jnp.where(jnp.abs(rel_q) <= WINDOW, _t5_bucket(rel_q), -1)
    bias_q = _bias_lookup(table_flat, idx_q)
    return bias_band, bias_meta, bias_q[:, :, :N_META], bias_q[:, :, N_META:]


def _win_kernel(sink_ref, q_ref, kp_ref, kc_ref, kn_ref, vp_ref, vc_ref, vn_ref, kam_ref, vam_ref,
                bias_ref, bm_ref, o_ref):
    w = WIN_BLOCKS
    i, ntile = pl.program_id(1), pl.num_programs(1)

    def qrows(blk):
        return slice(blk * BLOCK, (blk + 1) * BLOCK)

    def band(blk, prev_ref, cur_ref, next_ref, kvh, axis):
        take = lambda ref, lo, hi: ref[kvh, lo:hi, :] if axis == 0 else ref[kvh, :, lo:hi]
        parts = [take(cur_ref, max(blk - 1, 0) * BLOCK, min(blk + 2, w) * BLOCK)]
        if blk == 0:
            parts.insert(0, take(prev_ref, (w - 1) * BLOCK, w * BLOCK))
        if blk == w - 1:
            parts.append(take(next_ref, 0, BLOCK))
        return jnp.concatenate(parts, axis=axis)

    def variant(blk):
        v = 1
        if blk == 0:
            v = jnp.where(i == 0, 0, v)
        if blk == w - 1:
            v = jnp.where(i == ntile - 1, 2, v)
        return v

    def scores(blk, kvh):
        kb = jnp.concatenate([band(blk, kp_ref, kc_ref, kn_ref, kvh, 0), kam_ref[kvh]], axis=0)
        q4 = q_ref[A_GROUP * kvh:A_GROUP * (kvh + 1), qrows(blk), :].reshape(A_GROUP * BLOCK, A_HEAD_DIM)
        bias = jnp.concatenate([bias_ref[variant(blk), kvh], bm_ref[blk, kvh]], axis=0)
        s = lax.dot_general(kb, q4, _NT, preferred_element_type=f32) + bias
        m = jnp.maximum(jnp.max(s, axis=0, keepdims=True), sink_ref[kvh])
        return s, m

    def attend(blk, kvh, s, m):
        vb = jnp.concatenate([band(blk, vp_ref, vc_ref, vn_ref, kvh, 1), vam_ref[0, kvh]], axis=1)
        p = jnp.exp2(s - m).astype(bf16)
        acc = jnp.dot(vb, p, preferred_element_type=f32)
        den = acc[A_HEAD_DIM:A_HEAD_DIM + 1] + jnp.exp2(sink_ref[kvh] - m)
        return acc[:A_HEAD_DIM] * (1.0 / den)

    units = [(blk, kvh) for blk in range(w) for kvh in range(A_KV_HEADS)]
    outs = {}
    pending = [scores(*units[u]) for u in range(WIN_AHEAD)]
    for u, unit in enumerate(units):
        if u + WIN_AHEAD < len(units):
            pending.append(scores(*units[u + WIN_AHEAD]))
        outs[unit] = attend(*unit, *pending.pop(0))
        blk, kvh = unit
        if kvh == A_KV_HEADS - 1:
            ot = jnp.concatenate([outs[(blk, j)][:, g * BLOCK:(g + 1) * BLOCK]
                                  for j in range(A_KV_HEADS) for g in range(A_GROUP)], axis=0)
            o_ref[qrows(blk), :] = ot.T.astype(o_ref.dtype)


def _win(sink_rows, qa, ka, vat, kam, vatm, bias_band, bias_meta, batch, seq):
    tb = WIN_BLOCKS * BLOCK
    ntile = seq // tb
    prev_ = lambda b, i: b * ntile + jnp.maximum(i - 1, 0)
    cur_ = lambda b, i: b * ntile + i
    next_ = lambda b, i: b * ntile + jnp.minimum(i + 1, ntile - 1)
    kspec = lambda f: pl.BlockSpec((A_KV_HEADS, tb, A_HEAD_DIM), lambda b, i: (0, f(b, i), 0))
    vspec = lambda f: pl.BlockSpec((A_KV_HEADS, A_V_AUG, tb), lambda b, i: (0, 0, f(b, i)))
    return pl.pallas_call(
        _win_kernel,
        grid=(batch, ntile),
        in_specs=[
            pl.BlockSpec((A_KV_HEADS, 1, A_GROUP * BLOCK), lambda b, i: (0, 0, 0)),
            pl.BlockSpec((A_HEADS, tb, A_HEAD_DIM), lambda b, i: (0, cur_(b, i), 0)),
            kspec(prev_), kspec(cur_), kspec(next_),
            vspec(prev_), vspec(cur_), vspec(next_),
            pl.BlockSpec((A_KV_HEADS, N_META, A_HEAD_DIM), lambda b, i: (0, b, 0)),
            pl.BlockSpec((1, A_KV_HEADS, A_V_AUG, N_META), lambda b, i: (b, 0, 0, 0)),
            pl.BlockSpec((3, A_KV_HEADS, 3 * BLOCK, A_GROUP * BLOCK), lambda b, i: (0, 0, 0, 0)),
            pl.BlockSpec((WIN_BLOCKS, A_KV_HEADS, N_META, A_GROUP * BLOCK), lambda b, i: (i, 0, 0, 0)),
        ],
        out_specs=pl.BlockSpec((tb, A_WIDTH), lambda b, i: (cur_(b, i), 0)),
        out_shape=jax.ShapeDtypeStruct((batch * seq, A_WIDTH), bf16),
        compiler_params=_cparams("arbitrary", "arbitrary"),
        name="win",
    )(sink_rows, qa, ka, ka, ka, vat, vat, vat, kam, vatm, bias_band, bias_meta)


def _win_meta_kernel(sink_ref, q_ref, ka_ref, vat_ref, kam_ref, vatm_ref, biasm_ref, biasr_ref, o_ref):
    for kvh in range(A_KV_HEADS):
        q4 = q_ref[A_GROUP * kvh:A_GROUP * (kvh + 1)].reshape(A_GROUP * N_META, A_HEAD_DIM)
        sm = lax.dot_general(q4, kam_ref[kvh], _NT, preferred_element_type=f32)
        sr = lax.dot_general(q4, ka_ref[kvh], _NT, preferred_element_type=f32)
        for g in range(A_GROUP):
            h = A_GROUP * kvh + g
            rows = slice(g * N_META, (g + 1) * N_META)
            smg = sm[rows] + biasm_ref[h]
            srg = sr[rows] + biasr_ref[h]
            sink = sink_ref[h]
            m = jnp.maximum(jnp.maximum(jnp.max(smg, axis=-1, keepdims=True),
                                        jnp.max(srg, axis=-1, keepdims=True)), sink)
            pm = jnp.exp2(smg - m).astype(bf16)
            pr = jnp.exp2(srg - m).astype(bf16)
            o = (lax.dot_general(pm, vatm_ref[0, kvh], _NT, preferred_element_type=f32)
                 + lax.dot_general(pr, vat_ref[kvh], _NT, preferred_element_type=f32))
            den = o[:, A_HEAD_DIM:A_HEAD_DIM + 1] + jnp.exp2(sink - m)
            o_ref[:, h * A_HEAD_DIM:(h + 1) * A_HEAD_DIM] = (o[:, :A_HEAD_DIM] * (1.0 / den)).astype(o_ref.dtype)


def _win_meta(sink, qam, ka, vat, kam, vatm, bias_mm, bias_mr, batch, seq):
    nblk = seq // BLOCK
    const3 = lambda shape: pl.BlockSpec(shape, lambda b: (0, 0, 0))
    return pl.pallas_call(
        _win_meta_kernel,
        grid=(batch,),
        in_specs=[
            pl.BlockSpec(memory_space=pltpu.SMEM),
            pl.BlockSpec((A_HEADS, N_META, A_HEAD_DIM), lambda b: (0, b, 0)),
            pl.BlockSpec((A_KV_HEADS, BLOCK, A_HEAD_DIM), lambda b: (0, b * nblk, 0)),
            pl.BlockSpec((A_KV_HEADS, A_V_AUG, BLOCK), lambda b: (0, 0, b * nblk)),
            pl.BlockSpec((A_KV_HEADS, N_META, A_HEAD_DIM), lambda b: (0, b, 0)),
            pl.BlockSpec((1, A_KV_HEADS, A_V_AUG, N_META), lambda b: (b, 0, 0, 0)),
            const3((A_HEADS, N_META, N_META)),
            const3((A_HEADS, N_META, BLOCK)),
        ],
        out_specs=pl.BlockSpec((N_META, A_WIDTH), lambda b: (b, 0)),
        out_shape=jax.ShapeDtypeStruct((batch * N_META, A_WIDTH), bf16),
        compiler_params=_cparams("arbitrary"),
        name="win_meta",
    )(sink, qam, ka, vat, kam, vatm, bias_mm, bias_mr)


def _mla_kernel(q_ref, k_ref, km_ref, vt_ref, vtm_ref, o_ref, acc_ref, s0_ref, s1_ref, sm0_ref, sm1_ref):
    s_bufs, sm_bufs = (s0_ref, s1_ref), (sm0_ref, sm1_ref)
    nsub, _, _, tq = acc_ref.shape
    units = nsub * B_HEADS
    seq = k_ref.shape[1]
    chunks = [(c * MLA_KEY_CHUNK, (c + 1) * MLA_KEY_CHUNK) for c in range(seq // MLA_KEY_CHUNK)]

    def unit(u):
        return u % B_HEADS, u // B_HEADS

    def step(t, slot, m, *, score_next=True, attend_cur=True):
        if score_next:
            hn, subn = unit(t + 1)
            q = q_ref[hn, pl.ds(pl.multiple_of(subn * tq, tq), tq), :]
            m_next = None
        if attend_cur:
            h, sub = unit(t)
            acc = None
        for lo, hi in chunks:
            if score_next:
                s = lax.dot_general(k_ref[hn, lo:hi, :], q, _NT, preferred_element_type=f32)
                s_bufs[1 - slot][lo:hi, :] = s
                m_c = jnp.max(s, axis=0, keepdims=True)
                m_next = m_c if m_next is None else jnp.maximum(m_next, m_c)
            if attend_cur:
                p = jnp.exp2(s_bufs[slot][lo:hi, :] - m).astype(bf16)
                d = jnp.dot(vt_ref[h, :, lo:hi], p, preferred_element_type=f32)
                acc = d if acc is None else acc + d
        if score_next:
            sm = lax.dot_general(km_ref[hn], q, _NT, preferred_element_type=f32)
            sm_bufs[1 - slot][...] = sm
            m_next = jnp.maximum(m_next, jnp.max(sm, axis=0, keepdims=True))
        if attend_cur:
            pm = jnp.exp2(sm_bufs[slot][...] - m).astype(bf16)
            acc = acc + jnp.dot(vtm_ref[0, h], pm, preferred_element_type=f32)
        if attend_cur:
            acc_ref[sub, h] = acc[:B_V_DIM] * (1.0 / acc[B_V_DIM:B_V_DIM + 1])
        return m_next if score_next else None

    def steps(j, m):
        for i in range(MLA_STEPS_PER_ITER):
            m = step(MLA_STEPS_PER_ITER * j + i, i % 2, m)
        return m

    assert MLA_STEPS_PER_ITER % 2 == 0 and units % 2 == 0
    iters = (units - 1) // MLA_STEPS_PER_ITER
    m = step(-1, 1, None, attend_cur=False)
    m = lax.fori_loop(0, iters, steps, m)
    for t in range(iters * MLA_STEPS_PER_ITER, units - 1):
        m = step(t, t % 2, m)
    step(units - 1, 1, m, score_next=False)
    for sub in range(nsub):
        o_ref[sub * tq:(sub + 1) * tq, :] = acc_ref[sub].reshape(B_WIDTH, tq).T.astype(o_ref.dtype)


def _mla(q, k, km, vt, vtm, batch, seq):
    tq = MLA_Q_TILE
    tb = MLA_Q_TILE * MLA_SUBTILES
    nq = seq // tb
    return pl.pallas_call(
        _mla_kernel,
        grid=(batch, nq),
        in_specs=[
            pl.BlockSpec((B_HEADS, tb, B_QK_PAD), lambda b, i: (0, b * nq + i, 0)),
            pl.BlockSpec((B_HEADS, seq, B_QK_PAD), lambda b, i: (0, b, 0)),
            pl.BlockSpec((B_HEADS, N_META, B_QK_PAD), lambda b, i: (0, b, 0)),
            pl.BlockSpec((B_HEADS, B_V_AUG, seq), lambda b, i: (0, 0, b)),
            pl.BlockSpec((1, B_HEADS, B_V_AUG, N_META), lambda b, i: (b, 0, 0, 0)),
        ],
        out_specs=pl.BlockSpec((tb, B_WIDTH), lambda b, i: (b * nq + i, 0)),
        out_shape=jax.ShapeDtypeStruct((batch * seq, B_WIDTH), bf16),
        scratch_shapes=([pltpu.VMEM((MLA_SUBTILES, B_HEADS, B_V_DIM, tq), f32)]
                        + [pltpu.VMEM((seq, tq), f32)] * 2 + [pltpu.VMEM((N_META, tq), f32)] * 2),
        compiler_params=_cparams("arbitrary", "arbitrary"),
        name="mla",
    )(q, k, km, vt, vtm)


def _mla_meta_kernel(q_ref, kcat_ref, kcatm_ref, lat_t_ref, lat_tm_ref, wuk_ref, wuv_ref, o_ref):
    lane = lax.broadcasted_iota(jnp.int32, (N_META, B_QK_PAD), 1)
    q_rows = []
    for h in range(B_HEADS):
        q = q_ref[h]
        rope_lanes = (lane >= B_NOPE_DIM) if h % 2 == 0 else (lane < B_NOPE_DIM)
        q_lat = jnp.dot(q, wuk_ref[h], preferred_element_type=f32).astype(bf16)
        q_rows.append(jnp.concatenate([q_lat, jnp.where(rope_lanes, q, jnp.zeros_like(q))], axis=1))
    qcat = jnp.concatenate(q_rows, axis=0)
    seq = kcat_ref.shape[0]
    chunks = [(c * MLA_KEY_CHUNK, (c + 1) * MLA_KEY_CHUNK) for c in range(seq // MLA_KEY_CHUNK)]
    sm = lax.dot_general(kcatm_ref[...], qcat, _NT, preferred_element_type=f32)
    ss = [lax.dot_general(kcat_ref[lo:hi, :], qcat, _NT, preferred_element_type=f32) for lo, hi in chunks]
    m = jnp.max(sm, axis=0, keepdims=True)
    for s in ss:
        m = jnp.maximum(m, jnp.max(s, axis=0, keepdims=True))
    acc = jnp.dot(lat_tm_ref[0], jnp.exp2(sm - m).astype(bf16), preferred_element_type=f32)
    for (lo, hi), s in zip(chunks, ss):
        acc = acc + jnp.dot(lat_t_ref[:, lo:hi], jnp.exp2(s - m).astype(bf16),
                            preferred_element_type=f32)
    o_lat = (acc[:KV_LORA_RANK] * (1.0 / acc[KV_LORA_RANK:KV_LORA_RANK + 1])).T.astype(bf16)
    for h in range(B_HEADS):
        o_ref[:, h * B_V_DIM:(h + 1) * B_V_DIM] = jnp.dot(
            o_lat[h * N_META:(h + 1) * N_META], wuv_ref[h], preferred_element_type=f32).astype(o_ref.dtype)


def _mla_meta(qm, kcat, kcatm, lat_t, lat_tm, lw, layer, batch, seq):
    const = functools.partial(_layer_spec, layer)
    return pl.pallas_call(
        _mla_meta_kernel,
        grid=(batch,),
        in_specs=[
            pl.BlockSpec((B_HEADS, N_META, B_QK_PAD), lambda b: (0, b, 0)),
            pl.BlockSpec((seq, KV_LORA_RANK + B_QK_PAD), lambda b: (b, 0)),
            pl.BlockSpec((N_META, KV_LORA_RANK + B_QK_PAD), lambda b: (b, 0)),
            pl.BlockSpec((LAT_T_ROWS, seq), lambda b: (0, b)),
            pl.BlockSpec((1, LAT_T_ROWS, N_META), lambda b: (b, 0, 0)),
            const((B_HEADS, B_QK_PAD, KV_LORA_RANK)),
            const((B_HEADS, KV_LORA_RANK, B_V_DIM)),
        ],
        out_specs=pl.BlockSpec((N_META, B_WIDTH), lambda b: (b, 0)),
        out_shape=jax.ShapeDtypeStruct((batch * N_META, B_WIDTH), bf16),
        compiler_params=_cparams("arbitrary"),
        name="mla_meta",
    )(qm, kcat, kcatm, lat_t, lat_tm, lw["wuk"], lw["wuv"])


def _rope_tables(pos):
    half = B_ROPE_DIM // 2
    freqs = ROPE_THETA ** (-jnp.arange(half, dtype=jnp.float32) / half)
    lane_freq = jnp.concatenate([freqs] * (B_QK_PAD // half))
    ang = pos.astype(jnp.float32)[:, None] * lane_freq[None, :]
    cos, sin = jnp.cos(ang), jnp.sin(ang)
    lane = jnp.arange(B_QK_PAD)[None, :]
    c_mla = (B_NOPE_DIM + B_ROPE_DIM) ** -0.5 * LOG2E
    rope_pair = jnp.where(lane % (2 * B_ROPE_DIM) < B_ROPE_DIM, c_mla * cos, c_mla * sin)
    mul_q_even = jnp.where(lane < B_NOPE_DIM, c_mla, rope_pair)
    mul_q_odd = jnp.where(lane < 2 * B_ROPE_DIM, rope_pair, c_mla)
    return jnp.concatenate([mul_q_even, mul_q_odd, cos, sin], axis=1).astype(f32)


def _rot_cols(w):
    half = w.shape[-1] // 2
    return jnp.concatenate([-w[..., half:], w[..., :half]], axis=-1)


def _stacked_weights(norm_in, w_in, norm_q_lat, w_uq, norm_kv_lat, w_ukv, norm_out_a, norm_out_b, w_out):
    depth, d, _ = w_in.shape
    sizes = (A_WIDTH, A_KV_HEADS * A_HEAD_DIM, A_KV_HEADS * A_HEAD_DIM, A_WIDTH,
             Q_LORA_RANK, KV_LORA_RANK, B_ROPE_DIM, B_WIDTH)
    offs = [sum(sizes[:i]) for i in range(len(sizes) + 1)]
    qa, ka, va, ga, cq, ckv, kr, gb = (w_in[..., offs[i]:offs[i + 1]] for i in range(len(sizes)))
    zeros = lambda *shape: jnp.zeros(shape, w_in.dtype)
    reps = B_QK_PAD // B_ROPE_DIM
    kr_groups = jnp.concatenate([kr] * reps + [_rot_cols(kr)] * reps, axis=-1)
    w1 = jnp.concatenate([qa, cq, ckv, ka, kr_groups], axis=-1).astype(bf16)

    r = w_uq.shape[1]
    uq = w_uq.reshape(depth, r, B_HEADS // 2, 2, B_NOPE_DIM + B_ROPE_DIM)
    nope, rope = uq[..., :B_NOPE_DIM], uq[..., B_NOPE_DIM:]
    rope2 = jnp.concatenate([rope, _rot_cols(rope)], axis=-1)
    wq = jnp.stack([jnp.concatenate([nope[..., 0, :], rope2[..., 0, :]], axis=-1),
                    jnp.concatenate([rope2[..., 1, :], nope[..., 1, :]], axis=-1)], axis=3)

    rk = w_ukv.shape[1]
    ukv = w_ukv.reshape(depth, rk, B_HEADS, B_NOPE_DIM + B_V_DIM)
    k_nope, v = ukv[..., :B_NOPE_DIM], ukv[..., B_NOPE_DIM:]
    uk_t = jnp.transpose(k_nope, (0, 2, 3, 1)).reshape(depth, B_HEADS // 2, 2, B_NOPE_DIM, rk)
    no_lanes = zeros(depth, B_HEADS // 2, B_QK_PAD - B_NOPE_DIM, rk)
    wuk = jnp.stack([jnp.concatenate([uk_t[:, :, 0], no_lanes], axis=2),
                     jnp.concatenate([no_lanes, uk_t[:, :, 1]], axis=2)], axis=2)
    row = lambda g: g.astype(f32)[:, None, :]
    return {
        "gin": row(norm_in),
        "w1": w1,
        "wg": jnp.concatenate([ga, gb], axis=-1).astype(bf16),
        "wvat": jnp.swapaxes(va.astype(bf16), 1, 2),
        "gq": row(norm_q_lat),
        "wq": wq.reshape(depth, r, B_HEADS * B_QK_PAD).astype(bf16),
        "gkv": row(norm_kv_lat),
        "wk": k_nope.reshape(depth, rk, B_HEADS * B_NOPE_DIM).astype(bf16),
        "wvt": jnp.swapaxes(v.reshape(depth, rk, B_WIDTH).astype(bf16), 1, 2),
        "wuk": wuk.reshape(depth, B_HEADS, B_QK_PAD, rk).astype(bf16),
        "wuv": jnp.transpose(v, (0, 2, 1, 3)).astype(bf16),
        "na": row(norm_out_a),
        "nb": row(norm_out_b),
        "wout": w_out.astype(bf16),
    }


def kernel(x, meta_tokens, rel_bias_table, norm_in, w_in, sink_a, norm_q_lat, w_uq, norm_kv_lat, w_ukv,
           norm_out_a, norm_out_b, w_out, norm_final):
    batch, seq, d = x.shape
    depth = w_in.shape[0]
    assert d == D_MODEL and seq % (MLA_Q_TILE * MLA_SUBTILES) == 0 and seq % ROW_TILE == 0
    assert B_QK_PAD == 2 * B_NOPE_DIM and B_HEADS % 2 == 0

    h_real = x.reshape(batch * seq, d).astype(f32)
    h_meta = jnp.tile(meta_tokens.astype(f32), (batch, 1))
    tab_real = _rope_tables(N_META + jnp.arange(seq))
    tab_meta = _rope_tables(jnp.arange(batch * N_META) % N_META)
    bias_band, bias_meta, bias_mm, bias_mr = _bias_tables(rel_bias_table, seq // BLOCK)

    lw = _stacked_weights(norm_in, w_in, norm_q_lat, w_uq, norm_kv_lat, w_ukv, norm_out_a, norm_out_b, w_out)
    sinks = sink_a.astype(f32) * LOG2E
    sinks_rows = jnp.repeat(sinks.reshape(depth, A_KV_HEADS, A_GROUP), BLOCK, axis=2)[:, :, None, :]

    out = None
    for i in range(depth):
        last = i == depth - 1
        sink, sink_rows = sinks[i], sinks_rows[i]
        qa, ka, vat, q, k, vt, *lat = _proj(h_real, tab_real, lw, i, latent=not last)
        qam, kam, vatm, qm, km, vtm, *latm = _proj(h_meta, tab_meta, lw, i, latent=not last)
        vtm_b = vtm.reshape(B_HEADS, B_V_AUG, batch, N_META).transpose(2, 0, 1, 3)
        vatm_b = vatm.reshape(A_KV_HEADS, A_V_AUG, batch, N_META).transpose(2, 0, 1, 3)

        ya = _win(sink_rows, qa, ka, vat, kam, vatm_b, bias_band, bias_meta, batch, seq)
        yb = _mla(q, k, km, vt, vtm_b, batch, seq)
        if last:
            out = _out(h_real, ya, yb, lw, i, norm_final.astype(f32)[None, :])
        else:
            yam = _win_meta(sink, qam, ka, vat, kam, vatm_b, bias_mm, bias_mr, batch, seq)
            (kcat, lat_t), (kcatm, lat_tm) = lat, latm
            lat_tm_b = lat_tm.reshape(LAT_T_ROWS, batch, N_META).transpose(1, 0, 2)
            ybm = _mla_meta(qm, kcat, kcatm, lat_t, lat_tm_b, lw, i, batch, seq)
            h_real = _out(h_real, ya, yb, lw, i)
            h_meta = _out(h_meta, yam, ybm, lw, i)
    return out.reshape(batch, seq, d).astype(x.dtype)
```

```python
import functools
import math

import jax
import jax.numpy as jnp
from jax import lax
from jax.experimental import pallas as pl
from jax.experimental.pallas import tpu as pltpu

D_MODEL = 1024
N_META = 16
BLOCK = 128
WINDOW = 128
A_HEADS = 8
A_KV_HEADS = 2
A_GROUP = A_HEADS // A_KV_HEADS
A_HEAD_DIM = 64
A_WIDTH = A_HEADS * A_HEAD_DIM
B_HEADS = 8
B_NOPE_DIM = 64
B_ROPE_DIM = 32
B_V_DIM = 64
B_WIDTH = B_HEADS * B_V_DIM
B_QK_PAD = B_NOPE_DIM + 2 * B_ROPE_DIM
BF16_SUBLANES = 16
B_V_AUG = B_V_DIM + BF16_SUBLANES
A_V_AUG = A_HEAD_DIM + BF16_SUBLANES
Q_LORA_RANK = 256
KV_LORA_RANK = 128
LAT_T_ROWS = KV_LORA_RANK + BF16_SUBLANES
N_BUCKETS = 32
MAX_DISTANCE = 128
ROPE_THETA = 10000.0
EPS = 1e-6

LOG2E = math.log2(math.e)
NEG = -1e30
ROW_TILE = 1024
OUT_ROW_SPLIT = 4
OUT_MIN_PIECE = 256
MLA_Q_TILE = 256
MLA_SUBTILES = 8
MLA_STEPS_PER_ITER = 8
MLA_KEY_CHUNK = 1024
WIN_BLOCKS = 16
WIN_AHEAD = 2
BIAS_ROW_CHUNK = 32
VMEM_LIMIT_BYTES = 56 * 1024 * 1024

_C_QA = 0
_C_CQ = _C_QA + A_WIDTH
_C_CKV = _C_CQ + Q_LORA_RANK
_C_KA = _C_CKV + KV_LORA_RANK
_C_KRA = _C_KA + A_KV_HEADS * A_HEAD_DIM
_C_KRB = _C_KRA + B_QK_PAD
_C_END = _C_KRB + B_QK_PAD

_NT = (((1,), (1,)), ((), ()))

bf16 = jnp.bfloat16
f32 = jnp.float32


def _cparams(*sem, flags=None):
    return pltpu.CompilerParams(dimension_semantics=sem, vmem_limit_bytes=VMEM_LIMIT_BYTES, flags=flags)


def _rms(x, gain):
    return x * lax.rsqrt(jnp.mean(x * x, axis=-1, keepdims=True) + EPS) * gain


def _silu(x):
    return x / (1.0 + jnp.exp(-x))


def _proj_kernel(h_ref, tab_ref, gin_ref, w1_ref, wvat_ref, gq_ref, wq_ref, gkv_ref, wk_ref, wvt_ref,
                 qa_ref, ka_ref, vat_ref, q_ref, k_ref, vt_ref, *latent_refs):
    t = h_ref.shape[0]
    rows = slice(0, t)
    tab_rows = pl.ds(pl.multiple_of((pl.program_id(0) % (tab_ref.shape[0] // t)) * t, t), t)
    u = _rms(h_ref[rows, :], gin_ref[...]).astype(bf16)

    def mm(lo, hi):
        return jnp.dot(u, w1_ref[:, lo:hi], preferred_element_type=f32)

    qa = mm(_C_QA, _C_CQ) * (A_HEAD_DIM ** -0.5 * LOG2E)
    for h in range(A_HEADS):
        qa_ref[h, rows, :] = qa[:, h * A_HEAD_DIM:(h + 1) * A_HEAD_DIM].astype(bf16)
    vat = lax.dot_general(wvat_ref[...], u, _NT, preferred_element_type=f32)
    vat_ref[:, :A_HEAD_DIM, rows] = vat.reshape(A_KV_HEADS, A_HEAD_DIM, t).astype(bf16)
    vat_ref[:, A_HEAD_DIM:, rows] = jnp.ones((A_KV_HEADS, A_V_AUG - A_HEAD_DIM, t), bf16)
    mixed = mm(_C_CQ, _C_END)
    cq = mixed[:, :_C_CKV - _C_CQ]
    ckv = mixed[:, _C_CKV - _C_CQ:_C_KA - _C_CQ]
    ka = mixed[:, _C_KA - _C_CQ:_C_KRA - _C_CQ]
    kra = mixed[:, _C_KRA - _C_CQ:_C_KRB - _C_CQ]
    krb = mixed[:, _C_KRB - _C_CQ:]
    for j in range(A_KV_HEADS):
        ka_ref[j, rows, :] = ka[:, j * A_HEAD_DIM:(j + 1) * A_HEAD_DIM].astype(bf16)
    mul_q = (tab_ref[tab_rows, 0 * B_QK_PAD:1 * B_QK_PAD],
             tab_ref[tab_rows, 1 * B_QK_PAD:2 * B_QK_PAD])
    cos_k = tab_ref[tab_rows, 2 * B_QK_PAD:3 * B_QK_PAD]
    sin_k = tab_ref[tab_rows, 3 * B_QK_PAD:4 * B_QK_PAD]

    cqn = _rms(cq, gq_ref[...]).astype(bf16)
    qp = jnp.dot(cqn, wq_ref[...], preferred_element_type=f32)
    for h in range(B_HEADS):
        q_ref[h, rows, :] = (qp[:, h * B_QK_PAD:(h + 1) * B_QK_PAD] * mul_q[h % 2]).astype(bf16)

    ckvn_f32 = _rms(ckv, gkv_ref[...])
    ckvn = ckvn_f32.astype(bf16)
    k_nope = jnp.dot(ckvn, wk_ref[...], preferred_element_type=f32)
    k_rope = kra * cos_k + krb * sin_k
    lane = lax.broadcasted_iota(jnp.int32, k_rope.shape, 1)
    for h in range(B_HEADS):
        pair = k_nope[:, (h // 2) * B_QK_PAD:(h // 2 + 1) * B_QK_PAD]
        own_nope = (lane < B_NOPE_DIM) if h % 2 == 0 else (lane >= B_NOPE_DIM)
        k_ref[h, rows, :] = jnp.where(own_nope, pair, k_rope).astype(bf16)
    vt = lax.dot_general(wvt_ref[...], ckvn, _NT, preferred_element_type=f32)
    vt_ref[:, :B_V_DIM, rows] = vt.reshape(B_HEADS, B_V_DIM, t).astype(bf16)
    vt_ref[:, B_V_DIM:, rows] = jnp.ones((B_HEADS, B_V_AUG - B_V_DIM, t), bf16)
    if latent_refs:
        kcat_ref, lat_t_ref = latent_refs
        kcat_ref[rows, :KV_LORA_RANK] = ckvn
        kcat_ref[rows, KV_LORA_RANK:] = k_rope.astype(bf16)
        lat_t_ref[:KV_LORA_RANK, rows] = ckvn_f32.T.astype(bf16)
        lat_t_ref[KV_LORA_RANK:, rows] = jnp.ones((LAT_T_ROWS - KV_LORA_RANK, t), bf16)


def _layer_spec(layer, shape):
    return pl.BlockSpec((None,) + tuple(shape), lambda i: (layer,) + (0,) * len(shape))


def _proj(h, tab, lw, layer, latent):
    rows = h.shape[0]
    t = min(ROW_TILE, rows)
    steps = rows // t
    assert tab.shape[0] % t == 0
    const = functools.partial(_layer_spec, layer)
    latent_specs = [pl.BlockSpec((t, KV_LORA_RANK + B_QK_PAD), lambda i: (i, 0)),
                    pl.BlockSpec((LAT_T_ROWS, t), lambda i: (0, i))] if latent else []
    latent_shapes = [jax.ShapeDtypeStruct((rows, KV_LORA_RANK + B_QK_PAD), bf16),
                     jax.ShapeDtypeStruct((LAT_T_ROWS, rows), bf16)] if latent else []
    return pl.pallas_call(
        _proj_kernel,
        grid=(steps,),
        in_specs=[
            pl.BlockSpec((t, D_MODEL), lambda i: (i, 0)),
            pl.BlockSpec(tab.shape, lambda i: (0, 0)),
            const((1, D_MODEL)),
            const((D_MODEL, _C_END)),
            const((A_KV_HEADS * A_HEAD_DIM, D_MODEL)),
            const((1, Q_LORA_RANK)),
            const((Q_LORA_RANK, B_HEADS * B_QK_PAD)),
            const((1, KV_LORA_RANK)),
            const((KV_LORA_RANK, B_HEADS * B_NOPE_DIM)),
            const((B_WIDTH, KV_LORA_RANK)),
        ],
        out_specs=[
            pl.BlockSpec((A_HEADS, t, A_HEAD_DIM), lambda i: (0, i, 0)),
            pl.BlockSpec((A_KV_HEADS, t, A_HEAD_DIM), lambda i: (0, i, 0)),
            pl.BlockSpec((A_KV_HEADS, A_V_AUG, t), lambda i: (0, 0, i)),
            pl.BlockSpec((B_HEADS, t, B_QK_PAD), lambda i: (0, i, 0)),
            pl.BlockSpec((B_HEADS, t, B_QK_PAD), lambda i: (0, i, 0)),
            pl.BlockSpec((B_HEADS, B_V_AUG, t), lambda i: (0, 0, i)),
        ] + latent_specs,
        out_shape=[
            jax.ShapeDtypeStruct((A_HEADS, rows, A_HEAD_DIM), bf16),
            jax.ShapeDtypeStruct((A_KV_HEADS, rows, A_HEAD_DIM), bf16),
            jax.ShapeDtypeStruct((A_KV_HEADS, A_V_AUG, rows), bf16),
            jax.ShapeDtypeStruct((B_HEADS, rows, B_QK_PAD), bf16),
            jax.ShapeDtypeStruct((B_HEADS, rows, B_QK_PAD), bf16),
            jax.ShapeDtypeStruct((B_HEADS, B_V_AUG, rows), bf16),
        ] + latent_shapes,
        compiler_params=_cparams("arbitrary"),
        name="proj",
    )(h, tab, lw["gin"], lw["w1"], lw["wvat"], lw["gq"], lw["wq"], lw["gkv"], lw["wk"], lw["wvt"])


def _out_kernel(h_ref, ya_ref, yb_ref, gin_ref, wg_ref, na_ref, nb_ref, w_ref, *rest, final):
    if final:
        nf_ref, o_ref = rest
    else:
        (o_ref,) = rest
    t = h_ref.shape[0]
    piece = t if final else max(t // OUT_ROW_SPLIT, min(t, OUT_MIN_PIECE))

    def body(rows):
        hn = h_ref[rows, :]
        u = _rms(hn, gin_ref[...]).astype(bf16)
        ga = _silu(jnp.dot(u, wg_ref[:, :A_WIDTH], preferred_element_type=f32))
        gb = _silu(jnp.dot(u, wg_ref[:, A_WIDTH:], preferred_element_type=f32))
        ya = _rms(ya_ref[rows, :].astype(f32), na_ref[...]) * ga
        yb = _rms(yb_ref[rows, :].astype(f32), nb_ref[...]) * gb
        hn = hn + jnp.dot(ya.astype(bf16), w_ref[:A_WIDTH, :], preferred_element_type=f32)
        return hn + jnp.dot(yb.astype(bf16), w_ref[A_WIDTH:, :], preferred_element_type=f32)

    def tail(rows, hn):
        o_ref[rows, :] = _rms(hn, nf_ref[...]) if final else hn

    pending = None
    for r in range(0, t, piece):
        rows = slice(r, r + piece)
        hn = body(rows)
        if pending is not None:
            tail(*pending)
        pending = (rows, hn)
    tail(*pending)


def _out(h, ya, yb, lw, layer, norm_final=None):
    rows = h.shape[0]
    t = min(ROW_TILE, rows)
    row = lambda width: pl.BlockSpec((t, width), lambda i: (i, 0))
    const = functools.partial(_layer_spec, layer)
    final = norm_final is not None
    in_specs = [row(D_MODEL), row(A_WIDTH), row(B_WIDTH),
                const((1, D_MODEL)), const((D_MODEL, A_WIDTH + B_WIDTH)),
                const((1, A_WIDTH)), const((1, B_WIDTH)), const((A_WIDTH + B_WIDTH, D_MODEL))]
    args = [h, ya, yb, lw["gin"], lw["wg"], lw["na"], lw["nb"], lw["wout"]]
    if final:
        in_specs.append(pl.BlockSpec((1, D_MODEL), lambda i: (0, 0)))
        args.append(norm_final)
    return pl.pallas_call(
        functools.partial(_out_kernel, final=final),
        grid=(rows // t,),
        in_specs=in_specs,
        out_specs=row(D_MODEL),
        out_shape=jax.ShapeDtypeStruct((rows, D_MODEL), f32),
        compiler_params=_cparams("arbitrary"),
        name="outproj",
    )(*args)


def _bias_kernel(tab_ref, idx_ref, o_ref):
    rows = idx_ref.shape[0]
    chunk = math.gcd(rows, BIAS_ROW_CHUNK)

    def body(r, carry):
        sl = pl.ds(pl.multiple_of(r * chunk, chunk), chunk)
        idx = idx_ref[sl, :]
        accs = [jnp.where(idx < 0, NEG, 0.0).astype(f32)] * A_HEADS
        for b in range(N_BUCKETS):
            hit = idx == b
            accs = [jnp.where(hit, tab_ref[b * A_HEADS + h], accs[h]) for h in range(A_HEADS)]
        for h in range(A_HEADS):
            o_ref[h, sl, :] = accs[h]
        return carry

    lax.fori_loop(0, rows // chunk, body, 0)


def _bias_lookup(table_flat, idx):
    return pl.pallas_call(
        _bias_kernel,
        in_specs=[pl.BlockSpec(memory_space=pltpu.SMEM), pl.BlockSpec(idx.shape, lambda: (0, 0))],
        out_specs=pl.BlockSpec((A_HEADS,) + idx.shape, lambda: (0, 0, 0)),
        out_shape=jax.ShapeDtypeStruct((A_HEADS,) + idx.shape, f32),
        compiler_params=pltpu.CompilerParams(vmem_limit_bytes=VMEM_LIMIT_BYTES),
        name="relbias",
    )(table_flat, idx)


def _t5_bucket(rel):
    nb = N_BUCKETS // 2
    max_exact = nb // 2
    ret = jnp.where(rel > 0, nb, 0)
    n = jnp.abs(rel)
    nf = jnp.maximum(n, 1).astype(jnp.float32)
    large = max_exact + (jnp.log(nf / max_exact) / math.log(MAX_DISTANCE / max_exact)
                         * (nb - max_exact)).astype(jnp.int32)
    large = jnp.minimum(large, nb - 1)
    bucket = ret + jnp.where(n < max_exact, n, large)
    return jnp.bitwise_and(bucket, N_BUCKETS - 1)


def _bias_tables(rel_bias_table, nblk):
    table_flat = (rel_bias_table.astype(f32) * LOG2E).reshape(-1)
    i = jnp.arange(BLOCK, dtype=jnp.int32)[:, None]
    j = jnp.arange(3 * BLOCK, dtype=jnp.int32)[None, :]
    rel = j - i - BLOCK
    band = jnp.where(jnp.abs(rel) <= WINDOW, _t5_bucket(rel), -1)
    interior = (_bias_lookup(table_flat, band.T)
                .reshape(A_KV_HEADS, A_GROUP, 3 * BLOCK, BLOCK)
                .transpose(0, 2, 1, 3).reshape(A_KV_HEADS, 3 * BLOCK, A_GROUP * BLOCK))
    key = jnp.arange(3 * BLOCK, dtype=jnp.int32)[None, :, None]
    first = jnp.where(key < BLOCK, NEG, interior)
    last = jnp.where(key >= 2 * BLOCK, NEG, interior)
    bias_band = jnp.stack([first, interior, last])

    n = jnp.arange(nblk, dtype=jnp.int32)[:, None, None]
    k = jnp.arange(N_META, dtype=jnp.int32)[None, :, None]
    qi = jnp.arange(BLOCK, dtype=jnp.int32)[None, None, :]
    rel_m = k - (N_META + n * BLOCK + qi)
    idx_m = _t5_bucket(rel_m).reshape(nblk * N_META, BLOCK)
    bias_meta = (_bias_lookup(table_flat, idx_m)
                 .reshape(A_KV_HEADS, A_GROUP, nblk, N_META, BLOCK)
                 .transpose(2, 0, 3, 1, 4).reshape(nblk, A_KV_HEADS, N_META, A_GROUP * BLOCK))

    qp = jnp.arange(N_META, dtype=jnp.int32)[:, None]
    kp = jnp.arange(N_META + BLOCK, dtype=jnp.int32)[None, :]
    rel_q = kp - qp
    idx_q = jnp.where(jnp.abs(rel_q) <= WINDOW, _t5_bucket(rel_q), -1)
    bias_q = _bias_lookup(table_flat, idx_q)
    return bias_band, bias_meta, bias_q[:, :, :N_META], bias_q[:, :, N_META:]


def _win_kernel(sink_ref, q_ref, kp_ref, kc_ref, kn_ref, vp_ref, vc_ref, vn_ref, kam_ref, vam_ref,
                bias_ref, bm_ref, o_ref):
    w = WIN_BLOCKS
    i, ntile = pl.program_id(1), pl.num_programs(1)

    def qrows(blk):
        return slice(blk * BLOCK, (blk + 1) * BLOCK)

    def band(blk, prev_ref, cur_ref, next_ref, kvh, axis):
        take = lambda ref, lo, hi: ref[kvh, lo:hi, :] if axis == 0 else ref[kvh, :, lo:hi]
        parts = [take(cur_ref, max(blk - 1, 0) * BLOCK, min(blk + 2, w) * BLOCK)]
        if blk == 0:
            parts.insert(0, take(prev_ref, (w - 1) * BLOCK, w * BLOCK))
        if blk == w - 1:
            parts.append(take(next_ref, 0, BLOCK))
        return jnp.concatenate(parts, axis=axis)

    def variant(blk):
        v = 1
        if blk == 0:
            v = jnp.where(i == 0, 0, v)
        if blk == w - 1:
            v = jnp.where(i == ntile - 1, 2, v)
        return v

    def scores(blk, kvh):
        kb = jnp.concatenate([band(blk, kp_ref, kc_ref, kn_ref, kvh, 0), kam_ref[kvh]], axis=0)
        q4 = q_ref[A_GROUP * kvh:A_GROUP * (kvh + 1), qrows(blk), :].reshape(A_GROUP * BLOCK, A_HEAD_DIM)
        bias = jnp.concatenate([bias_ref[variant(blk), kvh], bm_ref[blk, kvh]], axis=0)
        s = lax.dot_general(kb, q4, _NT, preferred_element_type=f32) + bias
        m = jnp.maximum(jnp.max(s, axis=0, keepdims=True), sink_ref[kvh])
        return s, m

    def attend(blk, kvh, s, m):
        vb = jnp.concatenate([band(blk, vp_ref, vc_ref, vn_ref, kvh, 1), vam_ref[0, kvh]], axis=1)
        p = jnp.exp2(s - m).astype(bf16)
        acc = jnp.dot(vb, p, preferred_element_type=f32)
        den = acc[A_HEAD_DIM:A_HEAD_DIM + 1] + jnp.exp2(sink_ref[kvh] - m)
        return acc[:A_HEAD_DIM] * (1.0 / den)

    units = [(blk, kvh) for blk in range(w) for kvh in range(A_KV_HEADS)]
    outs = {}
    pending = [scores(*units[u]) for u in range(WIN_AHEAD)]
    for u, unit in enumerate(units):
        if u + WIN_AHEAD < len(units):
            pending.append(scores(*units[u + WIN_AHEAD]))
        outs[unit] = attend(*unit, *pending.pop(0))
        blk, kvh = unit
        if kvh == A_KV_HEADS - 1:
            ot = jnp.concatenate([outs[(blk, j)][:, g * BLOCK:(g + 1) * BLOCK]
                                  for j in range(A_KV_HEADS) for g in range(A_GROUP)], axis=0)
            o_ref[qrows(blk), :] = ot.T.astype(o_ref.dtype)


def _win(sink_rows, qa, ka, vat, kam, vatm, bias_band, bias_meta, batch, seq):
    tb = WIN_BLOCKS * BLOCK
    ntile = seq // tb
    prev_ = lambda b, i: b * ntile + jnp.maximum(i - 1, 0)
    cur_ = lambda b, i: b * ntile + i
    next_ = lambda b, i: b * ntile + jnp.minimum(i + 1, ntile - 1)
    kspec = lambda f: pl.BlockSpec((A_KV_HEADS, tb, A_HEAD_DIM), lambda b, i: (0, f(b, i), 0))
    vspec = lambda f: pl.BlockSpec((A_KV_HEADS, A_V_AUG, tb), lambda b, i: (0, 0, f(b, i)))
    return pl.pallas_call(
        _win_kernel,
        grid=(batch, ntile),
        in_specs=[
            pl.BlockSpec((A_KV_HEADS, 1, A_GROUP * BLOCK), lambda b, i: (0, 0, 0)),
            pl.BlockSpec((A_HEADS, tb, A_HEAD_DIM), lambda b, i: (0, cur_(b, i), 0)),
            kspec(prev_), kspec(cur_), kspec(next_),
            vspec(prev_), vspec(cur_), vspec(next_),
            pl.BlockSpec((A_KV_HEADS, N_META, A_HEAD_DIM), lambda b, i: (0, b, 0)),
            pl.BlockSpec((1, A_KV_HEADS, A_V_AUG, N_META), lambda b, i: (b, 0, 0, 0)),
            pl.BlockSpec((3, A_KV_HEADS, 3 * BLOCK, A_GROUP * BLOCK), lambda b, i: (0, 0, 0, 0)),
            pl.BlockSpec((WIN_BLOCKS, A_KV_HEADS, N_META, A_GROUP * BLOCK), lambda b, i: (i, 0, 0, 0)),
        ],
        out_specs=pl.BlockSpec((tb, A_WIDTH), lambda b, i: (cur_(b, i), 0)),
        out_shape=jax.ShapeDtypeStruct((batch * seq, A_WIDTH), bf16),
        compiler_params=_cparams("arbitrary", "arbitrary"),
        name="win",
    )(sink_rows, qa, ka, ka, ka, vat, vat, vat, kam, vatm, bias_band, bias_meta)


def _win_meta_kernel(sink_ref, q_ref, ka_ref, vat_ref, kam_ref, vatm_ref, biasm_ref, biasr_ref, o_ref):
    for kvh in range(A_KV_HEADS):
        q4 = q_ref[A_GROUP * kvh:A_GROUP * (kvh + 1)].reshape(A_GROUP * N_META, A_HEAD_DIM)
        sm = lax.dot_general(q4, kam_ref[kvh], _NT, preferred_element_type=f32)
        sr = lax.dot_general(q4, ka_ref[kvh], _NT, preferred_element_type=f32)
        for g in range(A_GROUP):
            h = A_GROUP * kvh + g
            rows = slice(g * N_META, (g + 1) * N_META)
            smg = sm[rows] + biasm_ref[h]
            srg = sr[rows] + biasr_ref[h]
            sink = sink_ref[h]
            m = jnp.maximum(jnp.maximum(jnp.max(smg, axis=-1, keepdims=True),
                                        jnp.max(srg, axis=-1, keepdims=True)), sink)
            pm = jnp.exp2(smg - m).astype(bf16)
            pr = jnp.exp2(srg - m).astype(bf16)
            o = (lax.dot_general(pm, vatm_ref[0, kvh], _NT, preferred_element_type=f32)
                 + lax.dot_general(pr, vat_ref[kvh], _NT, preferred_element_type=f32))
            den = o[:, A_HEAD_DIM:A_HEAD_DIM + 1] + jnp.exp2(sink - m)
            o_ref[:, h * A_HEAD_DIM:(h + 1) * A_HEAD_DIM] = (o[:, :A_HEAD_DIM] * (1.0 / den)).astype(o_ref.dtype)


def _win_meta(sink, qam, ka, vat, kam, vatm, bias_mm, bias_mr, batch, seq):
    nblk = seq // BLOCK
    const3 = lambda shape: pl.BlockSpec(shape, lambda b: (0, 0, 0))
    return pl.pallas_call(
        _win_meta_kernel,
        grid=(batch,),
        in_specs=[
            pl.BlockSpec(memory_space=pltpu.SMEM),
            pl.BlockSpec((A_HEADS, N_META, A_HEAD_DIM), lambda b: (0, b, 0)),
            pl.BlockSpec((A_KV_HEADS, BLOCK, A_HEAD_DIM), lambda b: (0, b * nblk, 0)),
            pl.BlockSpec((A_KV_HEADS, A_V_AUG, BLOCK), lambda b: (0, 0, b * nblk)),
            pl.BlockSpec((A_KV_HEADS, N_META, A_HEAD_DIM), lambda b: (0, b, 0)),
            pl.BlockSpec((1, A_KV_HEADS, A_V_AUG, N_META), lambda b: (b, 0, 0, 0)),
            const3((A_HEADS, N_META, N_META)),
            const3((A_HEADS, N_META, BLOCK)),
        ],
        out_specs=pl.BlockSpec((N_META, A_WIDTH), lambda b: (b, 0)),
        out_shape=jax.ShapeDtypeStruct((batch * N_META, A_WIDTH), bf16),
        compiler_params=_cparams("arbitrary"),
        name="win_meta",
    )(sink, qam, ka, vat, kam, vatm, bias_mm, bias_mr)


def _mla_kernel(q_ref, k_ref, km_ref, vt_ref, vtm_ref, o_ref, acc_ref, s0_ref, s1_ref):
    s_bufs = (s0_ref, s1_ref)
    nsub, _, _, tq = acc_ref.shape
    units = nsub * B_HEADS
    seq = k_ref.shape[1]
    chunks = [(c * MLA_KEY_CHUNK, (c + 1) * MLA_KEY_CHUNK) for c in range(seq // MLA_KEY_CHUNK)]

    def unit(u):
        return u % B_HEADS, u // B_HEADS

    def step(t, slot, m, *, score_next=True, attend_cur=True):
        if score_next:
            hn, subn = unit(t + 1)
            q = q_ref[hn, pl.ds(pl.multiple_of(subn * tq, tq), tq), :]
            m_next = None
        if attend_cur:
            h, sub = unit(t)
            acc = None
        for c, (lo, hi) in enumerate(chunks):
            last = c == len(chunks) - 1
            hi_s = hi + N_META if last else hi
            if score_next:
                kc = k_ref[hn, lo:hi, :]
                if last:
                    kc = jnp.concatenate([kc, km_ref[hn]], axis=0)
                s = lax.dot_general(kc, q, _NT, preferred_element_type=f32)
                s_bufs[1 - slot][lo:hi_s, :] = s
                m_c = jnp.max(s, axis=0, keepdims=True)
                m_next = m_c if m_next is None else jnp.maximum(m_next, m_c)
            if attend_cur:
                p = jnp.exp2(s_bufs[slot][lo:hi_s, :] - m).astype(bf16)
                vc = vt_ref[h, :, lo:hi]
                if last:
                    vc = jnp.concatenate([vc, vtm_ref[0, h]], axis=1)
                d = jnp.dot(vc, p, preferred_element_type=f32)
                acc = d if acc is None else acc + d
        if attend_cur:
            acc_ref[sub, h] = acc[:B_V_DIM] * (1.0 / acc[B_V_DIM:B_V_DIM + 1])
        return m_next if score_next else None

    def steps(j, m):
        for i in range(MLA_STEPS_PER_ITER):
            m = step(MLA_STEPS_PER_ITER * j + i, i % 2, m)
        return m

    assert MLA_STEPS_PER_ITER % 2 == 0 and units % 2 == 0
    iters = (units - 1) // MLA_STEPS_PER_ITER
    m = step(-1, 1, None, attend_cur=False)
    m = lax.fori_loop(0, iters, steps, m)
    for t in range(iters * MLA_STEPS_PER_ITER, units - 1):
        m = step(t, t % 2, m)
    step(units - 1, 1, m, score_next=False)
    for sub in range(nsub):
        o_ref[sub * tq:(sub + 1) * tq, :] = acc_ref[sub].reshape(B_WIDTH, tq).T.astype(o_ref.dtype)


def _mla(q, k, km, vt, vtm, batch, seq):
    tq = MLA_Q_TILE
    tb = MLA_Q_TILE * MLA_SUBTILES
    nq = seq // tb
    return pl.pallas_call(
        _mla_kernel,
        grid=(batch, nq),
        in_specs=[
            pl.BlockSpec((B_HEADS, tb, B_QK_PAD), lambda b, i: (0, b * nq + i, 0)),
            pl.BlockSpec((B_HEADS, seq, B_QK_PAD), lambda b, i: (0, b, 0)),
            pl.BlockSpec((B_HEADS, N_META, B_QK_PAD), lambda b, i: (0, b, 0)),
            pl.BlockSpec((B_HEADS, B_V_AUG, seq), lambda b, i: (0, 0, b)),
            pl.BlockSpec((1, B_HEADS, B_V_AUG, N_META), lambda b, i: (b, 0, 0, 0)),
        ],
        out_specs=pl.BlockSpec((tb, B_WIDTH), lambda b, i: (b * nq + i, 0)),
        out_shape=jax.ShapeDtypeStruct((batch * seq, B_WIDTH), bf16),
        scratch_shapes=([pltpu.VMEM((MLA_SUBTILES, B_HEADS, B_V_DIM, tq), f32)]
                        + [pltpu.VMEM((seq + N_META, tq), f32)] * 2),
        compiler_params=_cparams("arbitrary", "arbitrary"),
        name="mla",
    )(q, k, km, vt, vtm)


def _mla_meta_kernel(q_ref, kcat_ref, kcatm_ref, lat_t_ref, lat_tm_ref, wuk_ref, wuv_ref, o_ref):
    lane = lax.broadcasted_iota(jnp.int32, (N_META, B_QK_PAD), 1)
    q_rows = []
    for h in range(B_HEADS):
        q = q_ref[h]
        rope_lanes = (lane >= B_NOPE_DIM) if h % 2 == 0 else (lane < B_NOPE_DIM)
        q_lat = jnp.dot(q, wuk_ref[h], preferred_element_type=f32).astype(bf16)
        q_rows.append(jnp.concatenate([q_lat, jnp.where(rope_lanes, q, jnp.zeros_like(q))], axis=1))
    qcat = jnp.concatenate(q_rows, axis=0)
    seq = kcat_ref.shape[0]
    chunks = [(c * MLA_KEY_CHUNK, (c + 1) * MLA_KEY_CHUNK) for c in range(seq // MLA_KEY_CHUNK)]
    sm = lax.dot_general(kcatm_ref[...], qcat, _NT, preferred_element_type=f32)
    ss = [lax.dot_general(kcat_ref[lo:hi, :], qcat, _NT, preferred_element_type=f32) for lo, hi in chunks]
    m = jnp.max(sm, axis=0, keepdims=True)
    for s in ss:
        m = jnp.maximum(m, jnp.max(s, axis=0, keepdims=True))
    acc = jnp.dot(lat_tm_ref[0], jnp.exp2(sm - m).astype(bf16), preferred_element_type=f32)
    for (lo, hi), s in zip(chunks, ss):
        acc = acc + jnp.dot(lat_t_ref[:, lo:hi], jnp.exp2(s - m).astype(bf16),
                            preferred_element_type=f32)
    o_lat = (acc[:KV_LORA_RANK] * (1.0 / acc[KV_LORA_RANK:KV_LORA_RANK + 1])).T.astype(bf16)
    for h in range(B_HEADS):
        o_ref[:, h * B_V_DIM:(h + 1) * B_V_DIM] = jnp.dot(
            o_lat[h * N_META:(h + 1) * N_META], wuv_ref[h], preferred_element_type=f32).astype(o_ref.dtype)


def _mla_meta(qm, kcat, kcatm, lat_t, lat_tm, lw, layer, batch, seq):
    const = functools.partial(_layer_spec, layer)
    return pl.pallas_call(
        _mla_meta_kernel,
        grid=(batch,),
        in_specs=[
            pl.BlockSpec((B_HEADS, N_META, B_QK_PAD), lambda b: (0, b, 0)),
            pl.BlockSpec((seq, KV_LORA_RANK + B_QK_PAD), lambda b: (b, 0)),
            pl.BlockSpec((N_META, KV_LORA_RANK + B_QK_PAD), lambda b: (b, 0)),
            pl.BlockSpec((LAT_T_ROWS, seq), lambda b: (0, b)),
            pl.BlockSpec((1, LAT_T_ROWS, N_META), lambda b: (b, 0, 0)),
            const((B_HEADS, B_QK_PAD, KV_LORA_RANK)),
            const((B_HEADS, KV_LORA_RANK, B_V_DIM)),
        ],
        out_specs=pl.BlockSpec((N_META, B_WIDTH), lambda b: (b, 0)),
        out_shape=jax.ShapeDtypeStruct((batch * N_META, B_WIDTH), bf16),
        compiler_params=_cparams("arbitrary"),
        name="mla_meta",
    )(qm, kcat, kcatm, lat_t, lat_tm, lw["wuk"], lw["wuv"])


def _rope_tables(pos):
    half = B_ROPE_DIM // 2
    freqs = ROPE_THETA ** (-jnp.arange(half, dtype=jnp.float32) / half)
    lane_freq = jnp.concatenate([freqs] * (B_QK_PAD // half))
    ang = pos.astype(jnp.float32)[:, None] * lane_freq[None, :]
    cos, sin = jnp.cos(ang), jnp.sin(ang)
    lane = jnp.arange(B_QK_PAD)[None, :]
    c_mla = (B_NOPE_DIM + B_ROPE_DIM) ** -0.5 * LOG2E
    rope_pair = jnp.where(lane % (2 * B_ROPE_DIM) < B_ROPE_DIM, c_mla * cos, c_mla * sin)
    mul_q_even = jnp.where(lane < B_NOPE_DIM, c_mla, rope_pair)
    mul_q_odd = jnp.where(lane < 2 * B_ROPE_DIM, rope_pair, c_mla)
    return jnp.concatenate([mul_q_even, mul_q_odd, cos, sin], axis=1).astype(f32)


def _rot_cols(w):
    half = w.shape[-1] // 2
    return jnp.concatenate([-w[..., half:], w[..., :half]], axis=-1)


def _stacked_weights(norm_in, w_in, norm_q_lat, w_uq, norm_kv_lat, w_ukv, norm_out_a, norm_out_b, w_out):
    depth, d, _ = w_in.shape
    sizes = (A_WIDTH, A_KV_HEADS * A_HEAD_DIM, A_KV_HEADS * A_HEAD_DIM, A_WIDTH,
             Q_LORA_RANK, KV_LORA_RANK, B_ROPE_DIM, B_WIDTH)
    offs = [sum(sizes[:i]) for i in range(len(sizes) + 1)]
    qa, ka, va, ga, cq, ckv, kr, gb = (w_in[..., offs[i]:offs[i + 1]] for i in range(len(sizes)))
    zeros = lambda *shape: jnp.zeros(shape, w_in.dtype)
    reps = B_QK_PAD // B_ROPE_DIM
    kr_groups = jnp.concatenate([kr] * reps + [_rot_cols(kr)] * reps, axis=-1)
    w1 = jnp.concatenate([qa, cq, ckv, ka, kr_groups], axis=-1).astype(bf16)

    r = w_uq.shape[1]
    uq = w_uq.reshape(depth, r, B_HEADS // 2, 2, B_NOPE_DIM + B_ROPE_DIM)
    nope, rope = uq[..., :B_NOPE_DIM], uq[..., B_NOPE_DIM:]
    rope2 = jnp.concatenate([rope, _rot_cols(rope)], axis=-1)
    wq = jnp.stack([jnp.concatenate([nope[..., 0, :], rope2[..., 0, :]], axis=-1),
                    jnp.concatenate([rope2[..., 1, :], nope[..., 1, :]], axis=-1)], axis=3)

    rk = w_ukv.shape[1]
    ukv = w_ukv.reshape(depth, rk, B_HEADS, B_NOPE_DIM + B_V_DIM)
    k_nope, v = ukv[..., :B_NOPE_DIM], ukv[..., B_NOPE_DIM:]
    uk_t = jnp.transpose(k_nope, (0, 2, 3, 1)).reshape(depth, B_HEADS // 2, 2, B_NOPE_DIM, rk)
    no_lanes = zeros(depth, B_HEADS // 2, B_QK_PAD - B_NOPE_DIM, rk)
    wuk = jnp.stack([jnp.concatenate([uk_t[:, :, 0], no_lanes], axis=2),
                     jnp.concatenate([no_lanes, uk_t[:, :, 1]], axis=2)], axis=2)
    row = lambda g: g.astype(f32)[:, None, :]
    return {
        "gin": row(norm_in),
        "w1": w1,
        "wg": jnp.concatenate([ga, gb], axis=-1).astype(bf16),
        "wvat": jnp.swapaxes(va.astype(bf16), 1, 2),
        "gq": row(norm_q_lat),
        "wq": wq.reshape(depth, r, B_HEADS * B_QK_PAD).astype(bf16),
        "gkv": row(norm_kv_lat),
        "wk": k_nope.reshape(depth, rk, B_HEADS * B_NOPE_DIM).astype(bf16),
        "wvt": jnp.swapaxes(v.reshape(depth, rk, B_WIDTH).astype(bf16), 1, 2),
        "wuk": wuk.reshape(depth, B_HEADS, B_QK_PAD, rk).astype(bf16),
        "wuv": jnp.transpose(v, (0, 2, 1, 3)).astype(bf16),
        "na": row(norm_out_a),
        "nb": row(norm_out_b),
        "wout": w_out.astype(bf16),
    }


def kernel(x, meta_tokens, rel_bias_table, norm_in, w_in, sink_a, norm_q_lat, w_uq, norm_kv_lat, w_ukv,
           norm_out_a, norm_out_b, w_out, norm_final):
    batch, seq, d = x.shape
    depth = w_in.shape[0]
    assert d == D_MODEL and seq % (MLA_Q_TILE * MLA_SUBTILES) == 0 and seq % ROW_TILE == 0
    assert B_QK_PAD == 2 * B_NOPE_DIM and B_HEADS % 2 == 0

    h_real = x.reshape(batch * seq, d).astype(f32)
    h_meta = jnp.tile(meta_tokens.astype(f32), (batch, 1))
    tab_real = _rope_tables(N_META + jnp.arange(seq))
    tab_meta = _rope_tables(jnp.arange(batch * N_META) % N_META)
    bias_band, bias_meta, bias_mm, bias_mr = _bias_tables(rel_bias_table, seq // BLOCK)

    lw = _stacked_weights(norm_in, w_in, norm_q_lat, w_uq, norm_kv_lat, w_ukv, norm_out_a, norm_out_b, w_out)
    sinks = sink_a.astype(f32) * LOG2E
    sinks_rows = jnp.repeat(sinks.reshape(depth, A_KV_HEADS, A_GROUP), BLOCK, axis=2)[:, :, None, :]

    out = None
    for i in range(depth):
        last = i == depth - 1
        sink, sink_rows = sinks[i], sinks_rows[i]
        qa, ka, vat, q, k, vt, *lat = _proj(h_real, tab_real, lw, i, latent=not last)
        qam, kam, vatm, qm, km, vtm, *latm = _proj(h_meta, tab_meta, lw, i, latent=not last)
        vtm_b = vtm.reshape(B_HEADS, B_V_AUG, batch, N_META).transpose(2, 0, 1, 3)
        vatm_b = vatm.reshape(A_KV_HEADS, A_V_AUG, batch, N_META).transpose(2, 0, 1, 3)

        ya = _win(sink_rows, qa, ka, vat, kam, vatm_b, bias_band, bias_meta, batch, seq)
        yb = _mla(q, k, km, vt, vtm_b, batch, seq)
        if last:
            out = _out(h_real, ya, yb, lw, i, norm_final.astype(f32)[None, :])
        else:
            yam = _win_meta(sink, qam, ka, vat, kam, vatm_b, bias_mm, bias_mr, batch, seq)
            (kcat, lat_t), (kcatm, lat_tm) = lat, latm
            lat_tm_b = lat_tm.reshape(LAT_T_ROWS, batch, N_META).transpose(1, 0, 2)
            ybm = _mla_meta(qm, kcat, kcatm, lat_t, lat_tm_b, lw, i, batch, seq)
            h_real = _out(h_real, ya, yb, lw, i)
            h_meta = _out(h_meta, yam, ybm, lw, i)
    return out.reshape(batch, seq, d).astype(x.dtype)
```

```python
import functools
import math

import jax
import jax.numpy as jnp
from jax import lax
from jax.experimental import pallas as pl
from jax.experimental.pallas import tpu as pltpu

D_MODEL = 1024
N_META = 16
BLOCK = 128
WINDOW = 128
A_HEADS = 8
A_KV_HEADS = 2
A_GROUP = A_HEADS // A_KV_HEADS
A_HEAD_DIM = 64
A_WIDTH = A_HEADS * A_HEAD_DIM
B_HEADS = 8
B_NOPE_DIM = 64
B_ROPE_DIM = 32
B_V_DIM = 64
B_WIDTH = B_HEADS * B_V_DIM
B_QK_PAD = B_NOPE_DIM + 2 * B_ROPE_DIM
BF16_SUBLANES = 16
B_V_AUG = B_V_DIM + BF16_SUBLANES
A_V_AUG = A_HEAD_DIM + BF16_SUBLANES
Q_LORA_RANK = 256
KV_LORA_RANK = 128
LAT_T_ROWS = KV_LORA_RANK + BF16_SUBLANES
N_BUCKETS = 32
MAX_DISTANCE = 128
ROPE_THETA = 10000.0
EPS = 1e-6

LOG2E = math.log2(math.e)
NEG = -1e30
ROW_TILE = 1024
OUT_ROW_SPLIT = 4
OUT_MIN_PIECE = 256
MLA_Q_TILE = 256
MLA_SUBTILES = 8
MLA_STEPS_PER_ITER = 8
MLA_KEY_CHUNK = 1024
WIN_BLOCKS = 16
WIN_AHEAD = 2
BIAS_ROW_CHUNK = 32
VMEM_LIMIT_BYTES = 56 * 1024 * 1024

_C_QA = 0
_C_CQ = _C_QA + A_WIDTH
_C_CKV = _C_CQ + Q_LORA_RANK
_C_KA = _C_CKV + KV_LORA_RANK
_C_KRA = _C_KA + A_KV_HEADS * A_HEAD_DIM
_C_KRB = _C_KRA + B_QK_PAD
_C_END = _C_KRB + B_QK_PAD

_NT = (((1,), (1,)), ((), ()))

bf16 = jnp.bfloat16
f32 = jnp.float32


def _cparams(*sem, flags=None):
    return pltpu.CompilerParams(dimension_semantics=sem, vmem_limit_bytes=VMEM_LIMIT_BYTES, flags=flags)


def _rms(x, gain):
    return x * lax.rsqrt(jnp.mean(x * x, axis=-1, keepdims=True) + EPS) * gain


def _silu(x):
    return x / (1.0 + jnp.exp(-x))


def _proj_kernel(h_ref, tab_ref, gin_ref, w1_ref, wvat_ref, gq_ref, wq_ref, gkv_ref, wk_ref, wvt_ref,
                 qa_ref, ka_ref, vat_ref, q_ref, k_ref, vt_ref, *latent_refs):
    t = h_ref.shape[0]
    rows = slice(0, t)
    tab_rows = pl.ds(pl.multiple_of((pl.program_id(0) % (tab_ref.shape[0] // t)) * t, t), t)
    u = _rms(h_ref[rows, :], gin_ref[...]).astype(bf16)

    def mm(lo, hi):
        return jnp.dot(u, w1_ref[:, lo:hi], preferred_element_type=f32)

    qa = mm(_C_QA, _C_CQ) * (A_HEAD_DIM ** -0.5 * LOG2E)
    for h in range(A_HEADS):
        qa_ref[h, rows, :] = qa[:, h * A_HEAD_DIM:(h + 1) * A_HEAD_DIM].astype(bf16)
    vat = lax.dot_general(wvat_ref[...], u, _NT, preferred_element_type=f32)
    vat_ref[:, :A_HEAD_DIM, rows] = vat.reshape(A_KV_HEADS, A_HEAD_DIM, t).astype(bf16)
    vat_ref[:, A_HEAD_DIM:, rows] = jnp.ones((A_KV_HEADS, A_V_AUG - A_HEAD_DIM, t), bf16)
    mixed = mm(_C_CQ, _C_END)
    cq = mixed[:, :_C_CKV - _C_CQ]
    ckv = mixed[:, _C_CKV - _C_CQ:_C_KA - _C_CQ]
    ka = mixed[:, _C_KA - _C_CQ:_C_KRA - _C_CQ]
    kra = mixed[:, _C_KRA - _C_CQ:_C_KRB - _C_CQ]
    krb = mixed[:, _C_KRB - _C_CQ:]
    for j in range(A_KV_HEADS):
        ka_ref[j, rows, :] = ka[:, j * A_HEAD_DIM:(j + 1) * A_HEAD_DIM].astype(bf16)
    mul_q = (tab_ref[tab_rows, 0 * B_QK_PAD:1 * B_QK_PAD],
             tab_ref[tab_rows, 1 * B_QK_PAD:2 * B_QK_PAD])
    cos_k = tab_ref[tab_rows, 2 * B_QK_PAD:3 * B_QK_PAD]
    sin_k = tab_ref[tab_rows, 3 * B_QK_PAD:4 * B_QK_PAD]

    cqn = _rms(cq, gq_ref[...]).astype(bf16)
    qp = jnp.dot(cqn, wq_ref[...], preferred_element_type=f32)
    for h in range(B_HEADS):
        q_ref[h, rows, :] = (qp[:, h * B_QK_PAD:(h + 1) * B_QK_PAD] * mul_q[h % 2]).astype(bf16)

    ckvn_f32 = _rms(ckv, gkv_ref[...])
    ckvn = ckvn_f32.astype(bf16)
    k_nope = jnp.dot(ckvn, wk_ref[...], preferred_element_type=f32)
    k_rope = kra * cos_k + krb * sin_k
    lane = lax.broadcasted_iota(jnp.int32, k_rope.shape, 1)
    for h in range(B_HEADS):
        pair = k_nope[:, (h // 2) * B_QK_PAD:(h // 2 + 1) * B_QK_PAD]
        own_nope = (lane < B_NOPE_DIM) if h % 2 == 0 else (lane >= B_NOPE_DIM)
        k_ref[h, rows, :] = jnp.where(own_nope, pair, k_rope).astype(bf16)
    vt = lax.dot_general(wvt_ref[...], ckvn, _NT, preferred_element_type=f32)
    vt_ref[:, :B_V_DIM, rows] = vt.reshape(B_HEADS, B_V_DIM, t).astype(bf16)
    vt_ref[:, B_V_DIM:, rows] = jnp.ones((B_HEADS, B_V_AUG - B_V_DIM, t), bf16)
    if latent_refs:
        kcat_ref, lat_t_ref = latent_refs
        kcat_ref[rows, :KV_LORA_RANK] = ckvn
        kcat_ref[rows, KV_LORA_RANK:] = k_rope.astype(bf16)
        lat_t_ref[:KV_LORA_RANK, rows] = ckvn_f32.T.astype(bf16)
        lat_t_ref[KV_LORA_RANK:, rows] = jnp.ones((LAT_T_ROWS - KV_LORA_RANK, t), bf16)


def _layer_spec(layer, shape):
    return pl.BlockSpec((None,) + tuple(shape), lambda i: (layer,) + (0,) * len(shape))


def _proj(h, tab, lw, layer, latent):
    rows = h.shape[0]
    t = min(ROW_TILE, rows)
    steps = rows // t
    assert tab.shape[0] % t == 0
    const = functools.partial(_layer_spec, layer)
    latent_specs = [pl.BlockSpec((t, KV_LORA_RANK + B_QK_PAD), lambda i: (i, 0)),
                    pl.BlockSpec((LAT_T_ROWS, t), lambda i: (0, i))] if latent else []
    latent_shapes = [jax.ShapeDtypeStruct((rows, KV_LORA_RANK + B_QK_PAD), bf16),
                     jax.ShapeDtypeStruct((LAT_T_ROWS, rows), bf16)] if latent else []
    return pl.pallas_call(
        _proj_kernel,
        grid=(steps,),
        in_specs=[
            pl.BlockSpec((t, D_MODEL), lambda i: (i, 0)),
            pl.BlockSpec(tab.shape, lambda i: (0, 0)),
            const((1, D_MODEL)),
            const((D_MODEL, _C_END)),
            const((A_KV_HEADS * A_HEAD_DIM, D_MODEL)),
            const((1, Q_LORA_RANK)),
            const((Q_LORA_RANK, B_HEADS * B_QK_PAD)),
            const((1, KV_LORA_RANK)),
            const((KV_LORA_RANK, B_HEADS * B_NOPE_DIM)),
            const((B_WIDTH, KV_LORA_RANK)),
        ],
        out_specs=[
            pl.BlockSpec((A_HEADS, t, A_HEAD_DIM), lambda i: (0, i, 0)),
            pl.BlockSpec((A_KV_HEADS, t, A_HEAD_DIM), lambda i: (0, i, 0)),
            pl.BlockSpec((A_KV_HEADS, A_V_AUG, t), lambda i: (0, 0, i)),
            pl.BlockSpec((B_HEADS, t, B_QK_PAD), lambda i: (0, i, 0)),
            pl.BlockSpec((B_HEADS, t, B_QK_PAD), lambda i: (0, i, 0)),
            pl.BlockSpec((B_HEADS, B_V_AUG, t), lambda i: (0, 0, i)),
        ] + latent_specs,
        out_shape=[
            jax.ShapeDtypeStruct((A_HEADS, rows, A_HEAD_DIM), bf16),
            jax.ShapeDtypeStruct((A_KV_HEADS, rows, A_HEAD_DIM), bf16),
            jax.ShapeDtypeStruct((A_KV_HEADS, A_V_AUG, rows), bf16),
            jax.ShapeDtypeStruct((B_HEADS, rows, B_QK_PAD), bf16),
            jax.ShapeDtypeStruct((B_HEADS, rows, B_QK_PAD), bf16),
            jax.ShapeDtypeStruct((B_HEADS, B_V_AUG, rows), bf16),
        ] + latent_shapes,
        compiler_params=_cparams("arbitrary"),
        name="proj",
    )(h, tab, lw["gin"], lw["w1"], lw["wvat"], lw["gq"], lw["wq"], lw["gkv"], lw["wk"], lw["wvt"])


def _out_kernel(h_ref, ya_ref, yb_ref, gin_ref, wg_ref, na_ref, nb_ref, w_ref, *rest, final):
    if final:
        nf_ref, o_ref = rest
    else:
        (o_ref,) = rest
    t = h_ref.shape[0]
    split = OUT_ROW_SPLIT // 2 if final else OUT_ROW_SPLIT
    piece = max(t // split, min(t, OUT_MIN_PIECE))

    def body(rows):
        hn = h_ref[rows, :]
        u = _rms(hn, gin_ref[...]).astype(bf16)
        ga = _silu(jnp.dot(u, wg_ref[:, :A_WIDTH], preferred_element_type=f32))
        gb = _silu(jnp.dot(u, wg_ref[:, A_WIDTH:], preferred_element_type=f32))
        ya = _rms(ya_ref[rows, :].astype(f32), na_ref[...]) * ga
        yb = _rms(yb_ref[rows, :].astype(f32), nb_ref[...]) * gb
        hn = hn + jnp.dot(ya.astype(bf16), w_ref[:A_WIDTH, :], preferred_element_type=f32)
        return hn + jnp.dot(yb.astype(bf16), w_ref[A_WIDTH:, :], preferred_element_type=f32)

    def tail(rows, hn):
        o_ref[rows, :] = _rms(hn, nf_ref[...]) if final else hn

    pending = None
    for r in range(0, t, piece):
        rows = slice(r, r + piece)
        hn = body(rows)
        if pending is not None:
            tail(*pending)
        pending = (rows, hn)
    tail(*pending)


def _out(h, ya, yb, lw, layer, norm_final=None):
    rows = h.shape[0]
    t = min(ROW_TILE, rows)
    row = lambda width: pl.BlockSpec((t, width), lambda i: (i, 0))
    const = functools.partial(_layer_spec, layer)
    final = norm_final is not None
    in_specs = [row(D_MODEL), row(A_WIDTH), row(B_WIDTH),
                const((1, D_MODEL)), const((D_MODEL, A_WIDTH + B_WIDTH)),
                const((1, A_WIDTH)), const((1, B_WIDTH)), const((A_WIDTH + B_WIDTH, D_MODEL))]
    args = [h, ya, yb, lw["gin"], lw["wg"], lw["na"], lw["nb"], lw["wout"]]
    if final:
        in_specs.append(pl.BlockSpec((1, D_MODEL), lambda i: (0, 0)))
        args.append(norm_final)
    return pl.pallas_call(
        functools.partial(_out_kernel, final=final),
        grid=(rows // t,),
        in_specs=in_specs,
        out_specs=row(D_MODEL),
        out_shape=jax.ShapeDtypeStruct((rows, D_MODEL), f32),
        compiler_params=_cparams("arbitrary"),
        name="outproj",
    )(*args)


def _bias_kernel(tab_ref, idx_ref, o_ref):
    rows = idx_ref.shape[0]
    chunk = math.gcd(rows, BIAS_ROW_CHUNK)

    def body(r, carry):
        sl = pl.ds(pl.multiple_of(r * chunk, chunk), chunk)
        idx = idx_ref[sl, :]
        accs = [jnp.where(idx < 0, NEG, 0.0).astype(f32)] * A_HEADS
        for b in range(N_BUCKETS):
            hit = idx == b
            accs = [jnp.where(hit, tab_ref[b * A_HEADS + h], accs[h]) for h in range(A_HEADS)]
        for h in range(A_HEADS):
            o_ref[h, sl, :] = accs[h]
        return carry

    lax.fori_loop(0, rows // chunk, body, 0)


def _bias_lookup(table_flat, idx):
    return pl.pallas_call(
        _bias_kernel,
        in_specs=[pl.BlockSpec(memory_space=pltpu.SMEM), pl.BlockSpec(idx.shape, lambda: (0, 0))],
        out_specs=pl.BlockSpec((A_HEADS,) + idx.shape, lambda: (0, 0, 0)),
        out_shape=jax.ShapeDtypeStruct((A_HEADS,) + idx.shape, f32),
        compiler_params=pltpu.CompilerParams(vmem_limit_bytes=VMEM_LIMIT_BYTES),
        name="relbias",
    )(table_flat, idx)


def _t5_bucket(rel):
    nb = N_BUCKETS // 2
    max_exact = nb // 2
    ret = jnp.where(rel > 0, nb, 0)
    n = jnp.abs(rel)
    nf = jnp.maximum(n, 1).astype(jnp.float32)
    large = max_exact + (jnp.log(nf / max_exact) / math.log(MAX_DISTANCE / max_exact)
                         * (nb - max_exact)).astype(jnp.int32)
    large = jnp.minimum(large, nb - 1)
    bucket = ret + jnp.where(n < max_exact, n, large)
    return jnp.bitwise_and(bucket, N_BUCKETS - 1)


def _bias_tables(rel_bias_table, nblk):
    table_flat = (rel_bias_table.astype(f32) * LOG2E).reshape(-1)
    i = jnp.arange(BLOCK, dtype=jnp.int32)[:, None]
    j = jnp.arange(3 * BLOCK, dtype=jnp.int32)[None, :]
    rel = j - i - BLOCK
    band = jnp.where(jnp.abs(rel) <= WINDOW, _t5_bucket(rel), -1)
    interior = (_bias_lookup(table_flat, band.T)
                .reshape(A_KV_HEADS, A_GROUP, 3 * BLOCK, BLOCK)
                .transpose(0, 2, 1, 3).reshape(A_KV_HEADS, 3 * BLOCK, A_GROUP * BLOCK))
    key = jnp.arange(3 * BLOCK, dtype=jnp.int32)[None, :, None]
    first = jnp.where(key < BLOCK, NEG, interior)
    last = jnp.where(key >= 2 * BLOCK, NEG, interior)
    bias_band = jnp.stack([first, interior, last])

    n = jnp.arange(nblk, dtype=jnp.int32)[:, None, None]
    k = jnp.arange(N_META, dtype=jnp.int32)[None, :, None]
    qi = jnp.arange(BLOCK, dtype=jnp.int32)[None, None, :]
    rel_m = k - (N_META + n * BLOCK + qi)
    idx_m = _t5_bucket(rel_m).reshape(nblk * N_META, BLOCK)
    bias_meta = (_bias_lookup(table_flat, idx_m)
                 .reshape(A_KV_HEADS, A_GROUP, nblk, N_META, BLOCK)
                 .transpose(2, 0, 3, 1, 4).reshape(nblk, A_KV_HEADS, N_META, A_GROUP * BLOCK))

    qp = jnp.arange(N_META, dtype=jnp.int32)[:, None]
    kp = jnp.arange(N_META + BLOCK, dtype=jnp.int32)[None, :]
    rel_q = kp - qp
    idx_q = jnp.where(jnp.abs(rel_q) <= WINDOW, _t5_bucket(rel_q), -1)
    bias_q = _bias_lookup(table_flat, idx_q)
    return bias_band, bias_meta, bias_q[:, :, :N_META], bias_q[:, :, N_META:]


def _win_kernel(sink_ref, q_ref, kp_ref, kc_ref, kn_ref, vp_ref, vc_ref, vn_ref, kam_ref, vam_ref,
                bias_ref, bm_ref, o_ref):
    w = WIN_BLOCKS
    i, ntile = pl.program_id(1), pl.num_programs(1)

    def qrows(blk):
        return slice(blk * BLOCK, (blk + 1) * BLOCK)

    def band(blk, prev_ref, cur_ref, next_ref, kvh, axis):
        take = lambda ref, lo, hi: ref[kvh, lo:hi, :] if axis == 0 else ref[kvh, :, lo:hi]
        parts = [take(cur_ref, max(blk - 1, 0) * BLOCK, min(blk + 2, w) * BLOCK)]
        if blk == 0:
            parts.insert(0, take(prev_ref, (w - 1) * BLOCK, w * BLOCK))
        if blk == w - 1:
            parts.append(take(next_ref, 0, BLOCK))
        return jnp.concatenate(parts, axis=axis)

    def variant(blk):
        v = 1
        if blk == 0:
            v = jnp.where(i == 0, 0, v)
        if blk == w - 1:
            v = jnp.where(i == ntile - 1, 2, v)
        return v

    def scores(blk, kvh):
        kb = jnp.concatenate([band(blk, kp_ref, kc_ref, kn_ref, kvh, 0), kam_ref[kvh]], axis=0)
        q4 = q_ref[A_GROUP * kvh:A_GROUP * (kvh + 1), qrows(blk), :].reshape(A_GROUP * BLOCK, A_HEAD_DIM)
        bias = jnp.concatenate([bias_ref[variant(blk), kvh], bm_ref[blk, kvh]], axis=0)
        s = lax.dot_general(kb, q4, _NT, preferred_element_type=f32) + bias
        m = jnp.maximum(jnp.max(s, axis=0, keepdims=True), sink_ref[kvh])
        return s, m

    def attend(blk, kvh, s, m):
        vb = jnp.concatenate([band(blk, vp_ref, vc_ref, vn_ref, kvh, 1), vam_ref[0, kvh]], axis=1)
        p = jnp.exp2(s - m).astype(bf16)
        acc = jnp.dot(vb, p, preferred_element_type=f32)
        den = acc[A_HEAD_DIM:A_HEAD_DIM + 1] + jnp.exp2(sink_ref[kvh] - m)
        return acc[:A_HEAD_DIM] * (1.0 / den)

    units = [(blk, kvh) for blk in range(w) for kvh in range(A_KV_HEADS)]
    outs = {}
    pending = [scores(*units[u]) for u in range(WIN_AHEAD)]
    for u, unit in enumerate(units):
        if u + WIN_AHEAD < len(units):
            pending.append(scores(*units[u + WIN_AHEAD]))
        outs[unit] = attend(*unit, *pending.pop(0))
        blk, kvh = unit
        if kvh == A_KV_HEADS - 1:
            ot = jnp.concatenate([outs[(blk, j)][:, g * BLOCK:(g + 1) * BLOCK]
                                  for j in range(A_KV_HEADS) for g in range(A_GROUP)], axis=0)
            o_ref[qrows(blk), :] = ot.T.astype(o_ref.dtype)


def _win(sink_rows, qa, ka, vat, kam, vatm, bias_band, bias_meta, batch, seq):
    tb = WIN_BLOCKS * BLOCK
    ntile = seq // tb
    prev_ = lambda b, i: b * ntile + jnp.maximum(i - 1, 0)
    cur_ = lambda b, i: b * ntile + i
    next_ = lambda b, i: b * ntile + jnp.minimum(i + 1, ntile - 1)
    kspec = lambda f: pl.BlockSpec((A_KV_HEADS, tb, A_HEAD_DIM), lambda b, i: (0, f(b, i), 0))
    vspec = lambda f: pl.BlockSpec((A_KV_HEADS, A_V_AUG, tb), lambda b, i: (0, 0, f(b, i)))
    return pl.pallas_call(
        _win_kernel,
        grid=(batch, ntile),
        in_specs=[
            pl.BlockSpec((A_KV_HEADS, 1, A_GROUP * BLOCK), lambda b, i: (0, 0, 0)),
            pl.BlockSpec((A_HEADS, tb, A_HEAD_DIM), lambda b, i: (0, cur_(b, i), 0)),
            kspec(prev_), kspec(cur_), kspec(next_),
            vspec(prev_), vspec(cur_), vspec(next_),
            pl.BlockSpec((A_KV_HEADS, N_META, A_HEAD_DIM), lambda b, i: (0, b, 0)),
            pl.BlockSpec((1, A_KV_HEADS, A_V_AUG, N_META), lambda b, i: (b, 0, 0, 0)),
            pl.BlockSpec((3, A_KV_HEADS, 3 * BLOCK, A_GROUP * BLOCK), lambda b, i: (0, 0, 0, 0)),
            pl.BlockSpec((WIN_BLOCKS, A_KV_HEADS, N_META, A_GROUP * BLOCK), lambda b, i: (i, 0, 0, 0)),
        ],
        out_specs=pl.BlockSpec((tb, A_WIDTH), lambda b, i: (cur_(b, i), 0)),
        out_shape=jax.ShapeDtypeStruct((batch * seq, A_WIDTH), bf16),
        compiler_params=_cparams("arbitrary", "arbitrary"),
        name="win",
    )(sink_rows, qa, ka, ka, ka, vat, vat, vat, kam, vatm, bias_band, bias_meta)


def _win_meta_kernel(sink_ref, q_ref, ka_ref, vat_ref, kam_ref, vatm_ref, biasm_ref, biasr_ref, o_ref):
    for kvh in range(A_KV_HEADS):
        q4 = q_ref[A_GROUP * kvh:A_GROUP * (kvh + 1)].reshape(A_GROUP * N_META, A_HEAD_DIM)
        sm = lax.dot_general(q4, kam_ref[kvh], _NT, preferred_element_type=f32)
        sr = lax.dot_general(q4, ka_ref[kvh], _NT, preferred_element_type=f32)
        for g in range(A_GROUP):
            h = A_GROUP * kvh + g
            rows = slice(g * N_META, (g + 1) * N_META)
            smg = sm[rows] + biasm_ref[h]
            srg = sr[rows] + biasr_ref[h]
            sink = sink_ref[h]
            m = jnp.maximum(jnp.maximum(jnp.max(smg, axis=-1, keepdims=True),
                                        jnp.max(srg, axis=-1, keepdims=True)), sink)
            pm = jnp.exp2(smg - m).astype(bf16)
            pr = jnp.exp2(srg - m).astype(bf16)
            o = (lax.dot_general(pm, vatm_ref[0, kvh], _NT, preferred_element_type=f32)
                 + lax.dot_general(pr, vat_ref[kvh], _NT, preferred_element_type=f32))
            den = o[:, A_HEAD_DIM:A_HEAD_DIM + 1] + jnp.exp2(sink - m)
            o_ref[:, h * A_HEAD_DIM:(h + 1) * A_HEAD_DIM] = (o[:, :A_HEAD_DIM] * (1.0 / den)).astype(o_ref.dtype)


def _win_meta(sink, qam, ka, vat, kam, vatm, bias_mm, bias_mr, batch, seq):
    nblk = seq // BLOCK
    const3 = lambda shape: pl.BlockSpec(shape, lambda b: (0, 0, 0))
    return pl.pallas_call(
        _win_meta_kernel,
        grid=(batch,),
        in_specs=[
            pl.BlockSpec(memory_space=pltpu.SMEM),
            pl.BlockSpec((A_HEADS, N_META, A_HEAD_DIM), lambda b: (0, b, 0)),
            pl.BlockSpec((A_KV_HEADS, BLOCK, A_HEAD_DIM), lambda b: (0, b * nblk, 0)),
            pl.BlockSpec((A_KV_HEADS, A_V_AUG, BLOCK), lambda b: (0, 0, b * nblk)),
            pl.BlockSpec((A_KV_HEADS, N_META, A_HEAD_DIM), lambda b: (0, b, 0)),
            pl.BlockSpec((1, A_KV_HEADS, A_V_AUG, N_META), lambda b: (b, 0, 0, 0)),
            const3((A_HEADS, N_META, N_META)),
            const3((A_HEADS, N_META, BLOCK)),
        ],
        out_specs=pl.BlockSpec((N_META, A_WIDTH), lambda b: (b, 0)),
        out_shape=jax.ShapeDtypeStruct((batch * N_META, A_WIDTH), bf16),
        compiler_params=_cparams("arbitrary"),
        name="win_meta",
    )(sink, qam, ka, vat, kam, vatm, bias_mm, bias_mr)


def _mla_kernel(q_ref, k_ref, km_ref, vt_ref, vtm_ref, o_ref, acc_ref, s0_ref, s1_ref):
    s_bufs = (s0_ref, s1_ref)
    nsub, _, _, tq = acc_ref.shape
    units = nsub * B_HEADS
    seq = k_ref.shape[1]
    chunks = [(c * MLA_KEY_CHUNK, (c + 1) * MLA_KEY_CHUNK) for c in range(seq // MLA_KEY_CHUNK)]

    def unit(u):
        return u % B_HEADS, u // B_HEADS

    def step(t, slot, m, *, score_next=True, attend_cur=True):
        if score_next:
            hn, subn = unit(t + 1)
            q = q_ref[hn, pl.ds(pl.multiple_of(subn * tq, tq), tq), :]
            m_next = None
        if attend_cur:
            h, sub = unit(t)
            acc = None
        for c, (lo, hi) in enumerate(chunks):
            last = c == len(chunks) - 1
            hi_s = hi + N_META if last else hi
            if score_next:
                kc = k_ref[hn, lo:hi, :]
                if last:
                    kc = jnp.concatenate([kc, km_ref[hn]], axis=0)
                s = lax.dot_general(kc, q, _NT, preferred_element_type=f32)
                s_bufs[1 - slot][lo:hi_s, :] = s
                m_c = jnp.max(s, axis=0, keepdims=True)
                m_next = m_c if m_next is None else jnp.maximum(m_next, m_c)
            if attend_cur:
                p = jnp.exp2(s_bufs[slot][lo:hi_s, :] - m).astype(bf16)
                vc = vt_ref[h, :, lo:hi]
                if last:
                    vc = jnp.concatenate([vc, vtm_ref[0, h]], axis=1)
                d = jnp.dot(vc, p, preferred_element_type=f32)
                acc = d if acc is None else acc + d
        if attend_cur:
            acc_ref[sub, h] = acc[:B_V_DIM] * (1.0 / acc[B_V_DIM:B_V_DIM + 1])
        return m_next if score_next else None

    def steps(j, m):
        for i in range(MLA_STEPS_PER_ITER):
            m = step(MLA_STEPS_PER_ITER * j + i, i % 2, m)
        return m

    assert MLA_STEPS_PER_ITER % 2 == 0 and units % 2 == 0
    iters = (units - 1) // MLA_STEPS_PER_ITER
    m = step(-1, 1, None, attend_cur=False)
    m = lax.fori_loop(0, iters, steps, m)
    for t in range(iters * MLA_STEPS_PER_ITER, units - 1):
        m = step(t, t % 2, m)
    step(units - 1, 1, m, score_next=False)
    for sub in range(nsub):
        o_ref[sub * tq:(sub + 1) * tq, :] = acc_ref[sub].reshape(B_WIDTH, tq).T.astype(o_ref.dtype)


def _mla(q, k, km, vt, vtm, batch, seq):
    tq = MLA_Q_TILE
    tb = MLA_Q_TILE * MLA_SUBTILES
    nq = seq // tb
    return pl.pallas_call(
        _mla_kernel,
        grid=(batch, nq),
        in_specs=[
            pl.BlockSpec((B_HEADS, tb, B_QK_PAD), lambda b, i: (0, b * nq + i, 0)),
            pl.BlockSpec((B_HEADS, seq, B_QK_PAD), lambda b, i: (0, b, 0)),
            pl.BlockSpec((B_HEADS, N_META, B_QK_PAD), lambda b, i: (0, b, 0)),
            pl.BlockSpec((B_HEADS, B_V_AUG, seq), lambda b, i: (0, 0, b)),
            pl.BlockSpec((1, B_HEADS, B_V_AUG, N_META), lambda b, i: (b, 0, 0, 0)),
        ],
        out_specs=pl.BlockSpec((tb, B_WIDTH), lambda b, i: (b * nq + i, 0)),
        out_shape=jax.ShapeDtypeStruct((batch * seq, B_WIDTH), bf16),
        scratch_shapes=([pltpu.VMEM((MLA_SUBTILES, B_HEADS, B_V_DIM, tq), f32)]
                        + [pltpu.VMEM((seq + N_META, tq), f32)] * 2),
        compiler_params=_cparams("arbitrary", "arbitrary"),
        name="mla",
    )(q, k, km, vt, vtm)


def _mla_meta_kernel(q_ref, kcat_ref, kcatm_ref, lat_t_ref, lat_tm_ref, wuk_ref, wuv_ref, o_ref):
    lane = lax.broadcasted_iota(jnp.int32, (N_META, B_QK_PAD), 1)
    q_rows = []
    for h in range(B_HEADS):
        q = q_ref[h]
        rope_lanes = (lane >= B_NOPE_DIM) if h % 2 == 0 else (lane < B_NOPE_DIM)
        q_lat = jnp.dot(q, wuk_ref[h], preferred_element_type=f32).astype(bf16)
        q_rows.append(jnp.concatenate([q_lat, jnp.where(rope_lanes, q, jnp.zeros_like(q))], axis=1))
    qcat = jnp.concatenate(q_rows, axis=0)
    seq = kcat_ref.shape[0]
    chunks = [(c * MLA_KEY_CHUNK, (c + 1) * MLA_KEY_CHUNK) for c in range(seq // MLA_KEY_CHUNK)]
    keys = [kcat_ref[lo:hi, :] for lo, hi in chunks]
    vals = [lat_t_ref[:, lo:hi] for lo, hi in chunks]
    keys[-1] = jnp.concatenate([keys[-1], kcatm_ref[...]], axis=0)
    vals[-1] = jnp.concatenate([vals[-1], lat_tm_ref[0]], axis=1)
    ss = [lax.dot_general(k, qcat, _NT, preferred_element_type=f32) for k in keys]
    m = jnp.max(ss[0], axis=0, keepdims=True)
    for s in ss[1:]:
        m = jnp.maximum(m, jnp.max(s, axis=0, keepdims=True))
    acc = None
    for v, s in zip(vals, ss):
        d = jnp.dot(v, jnp.exp2(s - m).astype(bf16), preferred_element_type=f32)
        acc = d if acc is None else acc + d
    o_lat = (acc[:KV_LORA_RANK] * (1.0 / acc[KV_LORA_RANK:KV_LORA_RANK + 1])).T.astype(bf16)
    for h in range(B_HEADS):
        o_ref[:, h * B_V_DIM:(h + 1) * B_V_DIM] = jnp.dot(
            o_lat[h * N_META:(h + 1) * N_META], wuv_ref[h], preferred_element_type=f32).astype(o_ref.dtype)


def _mla_meta(qm, kcat, kcatm, lat_t, lat_tm, lw, layer, batch, seq):
    const = functools.partial(_layer_spec, layer)
    return pl.pallas_call(
        _mla_meta_kernel,
        grid=(batch,),
        in_specs=[
            pl.BlockSpec((B_HEADS, N_META, B_QK_PAD), lambda b: (0, b, 0)),
            pl.BlockSpec((seq, KV_LORA_RANK + B_QK_PAD), lambda b: (b, 0)),
            pl.BlockSpec((N_META, KV_LORA_RANK + B_QK_PAD), lambda b: (b, 0)),
            pl.BlockSpec((LAT_T_ROWS, seq), lambda b: (0, b)),
            pl.BlockSpec((1, LAT_T_ROWS, N_META), lambda b: (b, 0, 0)),
            const((B_HEADS, B_QK_PAD, KV_LORA_RANK)),
            const((B_HEADS, KV_LORA_RANK, B_V_DIM)),
        ],
        out_specs=pl.BlockSpec((N_META, B_WIDTH), lambda b: (b, 0)),
        out_shape=jax.ShapeDtypeStruct((batch * N_META, B_WIDTH), bf16),
        compiler_params=_cparams("arbitrary"),
        name="mla_meta",
    )(qm, kcat, kcatm, lat_t, lat_tm, lw["wuk"], lw["wuv"])


def _rope_tables(pos):
    half = B_ROPE_DIM // 2
    freqs = ROPE_THETA ** (-jnp.arange(half, dtype=jnp.float32) / half)
    lane_freq = jnp.concatenate([freqs] * (B_QK_PAD // half))
    ang = pos.astype(jnp.float32)[:, None] * lane_freq[None, :]
    cos, sin = jnp.cos(ang), jnp.sin(ang)
    lane = jnp.arange(B_QK_PAD)[None, :]
    c_mla = (B_NOPE_DIM + B_ROPE_DIM) ** -0.5 * LOG2E
    rope_pair = jnp.where(lane % (2 * B_ROPE_DIM) < B_ROPE_DIM, c_mla * cos, c_mla * sin)
    mul_q_even = jnp.where(lane < B_NOPE_DIM, c_mla, rope_pair)
    mul_q_odd = jnp.where(lane < 2 * B_ROPE_DIM, rope_pair, c_mla)
    return jnp.concatenate([mul_q_even, mul_q_odd, cos, sin], axis=1).astype(f32)


def _rot_cols(w):
    half = w.shape[-1] // 2
    return jnp.concatenate([-w[..., half:], w[..., :half]], axis=-1)


def _stacked_weights(norm_in, w_in, norm_q_lat, w_uq, norm_kv_lat, w_ukv, norm_out_a, norm_out_b, w_out):
    depth, d, _ = w_in.shape
    sizes = (A_WIDTH, A_KV_HEADS * A_HEAD_DIM, A_KV_HEADS * A_HEAD_DIM, A_WIDTH,
             Q_LORA_RANK, KV_LORA_RANK, B_ROPE_DIM, B_WIDTH)
    offs = [sum(sizes[:i]) for i in range(len(sizes) + 1)]
    qa, ka, va, ga, cq, ckv, kr, gb = (w_in[..., offs[i]:offs[i + 1]] for i in range(len(sizes)))
    zeros = lambda *shape: jnp.zeros(shape, w_in.dtype)
    reps = B_QK_PAD // B_ROPE_DIM
    kr_groups = jnp.concatenate([kr] * reps + [_rot_cols(kr)] * reps, axis=-1)
    w1 = jnp.concatenate([qa, cq, ckv, ka, kr_groups], axis=-1).astype(bf16)

    r = w_uq.shape[1]
    uq = w_uq.reshape(depth, r, B_HEADS // 2, 2, B_NOPE_DIM + B_ROPE_DIM)
    nope, rope = uq[..., :B_NOPE_DIM], uq[..., B_NOPE_DIM:]
    rope2 = jnp.concatenate([rope, _rot_cols(rope)], axis=-1)
    wq = jnp.stack([jnp.concatenate([nope[..., 0, :], rope2[..., 0, :]], axis=-1),
                    jnp.concatenate([rope2[..., 1, :], nope[..., 1, :]], axis=-1)], axis=3)

    rk = w_ukv.shape[1]
    ukv = w_ukv.reshape(depth, rk, B_HEADS, B_NOPE_DIM + B_V_DIM)
    k_nope, v = ukv[..., :B_NOPE_DIM], ukv[..., B_NOPE_DIM:]
    uk_t = jnp.transpose(k_nope, (0, 2, 3, 1)).reshape(depth, B_HEADS // 2, 2, B_NOPE_DIM, rk)
    no_lanes = zeros(depth, B_HEADS // 2, B_QK_PAD - B_NOPE_DIM, rk)
    wuk = jnp.stack([jnp.concatenate([uk_t[:, :, 0], no_lanes], axis=2),
                     jnp.concatenate([no_lanes, uk_t[:, :, 1]], axis=2)], axis=2)
    row = lambda g: g.astype(f32)[:, None, :]
    return {
        "gin": row(norm_in),
        "w1": w1,
        "wg": jnp.concatenate([ga, gb], axis=-1).astype(bf16),
        "wvat": jnp.swapaxes(va.astype(bf16), 1, 2),
        "gq": row(norm_q_lat),
        "wq": wq.reshape(depth, r, B_HEADS * B_QK_PAD).astype(bf16),
        "gkv": row(norm_kv_lat),
        "wk": k_nope.reshape(depth, rk, B_HEADS * B_NOPE_DIM).astype(bf16),
        "wvt": jnp.swapaxes(v.reshape(depth, rk, B_WIDTH).astype(bf16), 1, 2),
        "wuk": wuk.reshape(depth, B_HEADS, B_QK_PAD, rk).astype(bf16),
        "wuv": jnp.transpose(v, (0, 2, 1, 3)).astype(bf16),
        "na": row(norm_out_a),
        "nb": row(norm_out_b),
        "wout": w_out.astype(bf16),
    }


def kernel(x, meta_tokens, rel_bias_table, norm_in, w_in, sink_a, norm_q_lat, w_uq, norm_kv_lat, w_ukv,
           norm_out_a, norm_out_b, w_out, norm_final):
    batch, seq, d = x.shape
    depth = w_in.shape[0]
    assert d == D_MODEL and seq % (MLA_Q_TILE * MLA_SUBTILES) == 0 and seq % ROW_TILE == 0
    assert B_QK_PAD == 2 * B_NOPE_DIM and B_HEADS % 2 == 0

    h_real = x.reshape(batch * seq, d).astype(f32)
    h_meta = jnp.tile(meta_tokens.astype(f32), (batch, 1))
    tab_real = _rope_tables(N_META + jnp.arange(seq))
    tab_meta = _rope_tables(jnp.arange(batch * N_META) % N_META)
    bias_band, bias_meta, bias_mm, bias_mr = _bias_tables(rel_bias_table, seq // BLOCK)

    lw = _stacked_weights(norm_in, w_in, norm_q_lat, w_uq, norm_kv_lat, w_ukv, norm_out_a, norm_out_b, w_out)
    sinks = sink_a.astype(f32) * LOG2E
    sinks_rows = jnp.repeat(sinks.reshape(depth, A_KV_HEADS, A_GROUP), BLOCK, axis=2)[:, :, None, :]

    out = None
    for i in range(depth):
        last = i == depth - 1
        sink, sink_rows = sinks[i], sinks_rows[i]
        qa, ka, vat, q, k, vt, *lat = _proj(h_real, tab_real, lw, i, latent=not last)
        qam, kam, vatm, qm, km, vtm, *latm = _proj(h_meta, tab_meta, lw, i, latent=not last)
        vtm_b = vtm.reshape(B_HEADS, B_V_AUG, batch, N_META).transpose(2, 0, 1, 3)
        vatm_b = vatm.reshape(A_KV_HEADS, A_V_AUG, batch, N_META).transpose(2, 0, 1, 3)

        ya = _win(sink_rows, qa, ka, vat, kam, vatm_b, bias_band, bias_meta, batch, seq)
        yb = _mla(q, k, km, vt, vtm_b, batch, seq)
        if last:
            out = _out(h_real, ya, yb, lw, i, norm_final.astype(f32)[None, :])
        else:
            yam = _win_meta(sink, qam, ka, vat, kam, vatm_b, bias_mm, bias_mr, batch, seq)
            (kcat, lat_t), (kcatm, lat_tm) = lat, latm
            lat_tm_b = lat_tm.reshape(LAT_T_ROWS, batch, N_META).transpose(1, 0, 2)
            ybm = _mla_meta(qm, kcat, kcatm, lat_t, lat_tm_b, lw, i, batch, seq)
            h_real = _out(h_real, ya, yb, lw, i)
            h_meta = _out(h_meta, yam, ybm, lw, i)
    return out.reshape(batch, seq, d).astype(x.dtype)
```
